```python
import math
import jax
import jax.numpy as jnp
from jax import lax
import numpy as np

D_MODEL = 1024
BATCH = 2
SEQ = 8192
DEPTH = 1
DEC_BATCH = 32
DEC_SEQ = 1
PAST_LEN = 16384
PAGE_SIZE = 128

HEAD_DIM = 64
A_HEADS = 8
KV_HEADS = 2
GROUP = A_HEADS // KV_HEADS
CMP_STRIDE = 16
CMP_LEN = 2 * CMP_STRIDE
SEL_BLOCK = 64
N_SELECT = 16
WINDOW = 512
Q_BLOCK = 128
ATTN_SCALE = HEAD_DIM ** -0.5
B_HEADS = 8
B_DK = 128
B_DV = 64
HGRN_CHUNK = 32
D_FF = 4 * D_MODEL
DEEPNORM_ALPHA = (2.0 * DEPTH) ** 0.25
DEEPNORM_BETA = (8.0 * DEPTH) ** -0.25
LN_EPS = 1e-5
NEG = -1e30
FORCE = 1e6
Q_A = A_HEADS * HEAD_DIM
KV_A = 2 * KV_HEADS * HEAD_DIM
GATE_A = 3 * A_HEADS
Q_B = B_HEADS * B_DK
F_B = B_HEADS * B_DK
I_B = B_HEADS * B_DV
G_B = B_HEADS * B_DV
MERGE = 2 * D_MODEL
SPLITS = (Q_A, KV_A, KV_A, KV_A, GATE_A, Q_B, F_B, I_B, G_B, MERGE)
D_IN = Q_A + 3 * KV_A + GATE_A + Q_B + F_B + I_B + G_B + MERGE

kernel_name = 'nsa_hgrn2_deepnorm_hybrid_step'


def layer_norm(x, g, b):
    xf = x.astype(jnp.float32)
    mu = jnp.mean(xf, axis=-1, keepdims=True)
    var = jnp.mean(jnp.square(xf - mu), axis=-1, keepdims=True)
    y = (xf - mu) * lax.rsqrt(var + LN_EPS) * g.astype(jnp.float32) + b.astype(jnp.float32)
    return y.astype(x.dtype)


def alibi_slopes():
    h = jnp.arange(1, A_HEADS + 1, dtype=jnp.float32)
    return jnp.exp2(-8.0 * h / A_HEADS).reshape(KV_HEADS, GROUP)


def masked_softmax(s, mask):
    s = jnp.where(mask, s.astype(jnp.float32), NEG)
    m = jnp.max(s, axis=-1, keepdims=True)
    p = jnp.where(mask, jnp.exp(s - m), 0.0)
    return p / jnp.maximum(jnp.sum(p, axis=-1, keepdims=True), 1e-30)


def split_projection(x, w_in):
    b, t = x.shape[:2]
    h = jnp.einsum('bsd,de->bse', x, w_in)
    parts = []
    off = 0
    for n in SPLITS:
        parts.append(h[..., off:off + n])
        off += n
    qa, kvc, kvs, kvw, gla, qb, fb, ib, gb, mg = parts
    kv = lambda a: a.reshape(b, t, 2, KV_HEADS, HEAD_DIM)
    return (qa.reshape(b, t, KV_HEADS, GROUP, HEAD_DIM), kv(kvc), kv(kvs), kv(kvw),
            gla.reshape(b, t, KV_HEADS, GROUP, 3), qb, fb, ib, gb, mg)


def chunk_parts(kv, cmp_w):
    b, L = kv.shape[:2]
    n_ch = L // CMP_STRIDE
    ch = kv[:, :n_ch * CMP_STRIDE].reshape(b, n_ch, CMP_STRIDE, 2, KV_HEADS, HEAD_DIM)
    w = cmp_w.reshape(2, CMP_LEN // CMP_STRIDE, CMP_STRIDE, HEAD_DIM, HEAD_DIM)
    return jnp.einsum('bnlcgd,cplde->bnpcge', ch, w)


def blocks_from_parts(parts, cmp_w, cmp_pos):
    bias = jnp.einsum('cld,clde->ce', cmp_pos, cmp_w)
    return parts[:, :-1, 0] + parts[:, 1:, 1] + bias[None, None, :, None, :]


def compressed_branch(qg, pos_q, kv_cmp, slopes):
    n = kv_cmp.shape[1]
    end = jnp.arange(n) * CMP_STRIDE + (CMP_LEN - 1)
    dist = (pos_q[:, None] - end[None, :]).astype(jnp.float32)
    s = jnp.einsum('bqgrd,bngd->bqgrn', qg, kv_cmp[:, :, 0]).astype(jnp.float32) * ATTN_SCALE
    s = s - slopes[:, :, None] * dist[None, :, None, None, :]
    p = masked_softmax(s, (dist >= 0)[None, :, None, None, :])
    o = jnp.einsum('bqgrn,bngd->bqgrd', p.astype(kv_cmp.dtype), kv_cmp[:, :, 1])
    return o, p


def select_blocks(p_cmp, pos_q, n_sel):
    imp = jnp.sum(p_cmp, axis=3)
    ratio = SEL_BLOCK // CMP_STRIDE
    imp = jnp.pad(imp, ((0, 0), (0, 0), (0, 0), (0, n_sel * ratio - imp.shape[-1])))
    imp = jnp.sum(imp.reshape(*imp.shape[:-1], n_sel, ratio), axis=-1)
    blk = jnp.arange(n_sel)
    cur = (pos_q // SEL_BLOCK)[:, None]
    forced = (blk[None, :] == 0) | (blk[None, :] == cur) | (blk[None, :] == cur - 1)
    allowed = blk[None, :] * SEL_BLOCK <= pos_q[:, None]
    score = jnp.where(forced[None, :, None, :], FORCE, imp)
    score = jnp.where(allowed[None, :, None, :], score, -FORCE)
    vals, idx = lax.top_k(score, min(N_SELECT, n_sel))
    return idx, vals > -0.5 * FORCE


def selected_branch(qg, pos_q, kv_g, idx, valid, slopes):
    b, q, g = qg.shape[:3]
    nk = idx.shape[-1] * SEL_BLOCK
    pos_k = (idx[..., None] * SEL_BLOCK + jnp.arange(SEL_BLOCK)).reshape(b, q, g, nk)
    dist = (pos_q[None, :, None, None] - pos_k).astype(jnp.float32)
    mask = jnp.repeat(valid, SEL_BLOCK, axis=-1) & (dist >= 0)
    k = kv_g[..., 0, :].reshape(b, q, g, nk, HEAD_DIM)
    v = kv_g[..., 1, :].reshape(b, q, g, nk, HEAD_DIM)
    s = jnp.einsum('bqgrd,bqgmd->bqgrm', qg, k).astype(jnp.float32) * ATTN_SCALE
    s = s - slopes[:, :, None] * dist[:, :, :, None, :]
    p = masked_softmax(s, mask[:, :, :, None, :])
    return jnp.einsum('bqgrm,bqgmd->bqgrd', p.astype(v.dtype), v)


def window_branch(qg, pos_q, kv_w, pos_k, slopes):
    dist = pos_q[:, None] - pos_k[None, :]
    mask = (dist >= 0) & (dist < WINDOW) & (pos_k[None, :] >= 0)
    s = jnp.einsum('bqgrd,bkgd->bqgrk', qg, kv_w[:, :, 0]).astype(jnp.float32) * ATTN_SCALE
    s = s - slopes[:, :, None] * dist.astype(jnp.float32)[None, :, None, None, :]
    p = masked_softmax(s, mask[None, :, None, None, :])
    return jnp.einsum('bqgrk,bkgd->bqgrd', p.astype(kv_w.dtype), kv_w[:, :, 1])


def gate_branches(gl, o_c, o_s, o_w):
    g = jax.nn.sigmoid(gl.astype(jnp.float32))
    o = g[..., 0:1] * o_c + g[..., 1:2] * o_s + g[..., 2:3] * o_w
    return o.astype(o_c.dtype)


def nsa_prompt(qg, kvc, kvs, kvw, gl, cmp_w, cmp_pos):
    b, t = qg.shape[:2]
    slopes = alibi_slopes()
    kv_cmp = blocks_from_parts(chunk_parts(kvc, cmp_w), cmp_w, cmp_pos)
    n_sel = -(-t // SEL_BLOCK)
    kv_sel_blk = kvs.reshape(b, n_sel, SEL_BLOCK, 2, KV_HEADS, HEAD_DIM)
    kv_w_pad = jnp.pad(kvw, ((0, 0), (WINDOW, 0), (0, 0), (0, 0), (0, 0)))
    b_ix = jnp.arange(b)[:, None, None, None]
    g_ix = jnp.arange(KV_HEADS)[None, None, :, None]

    def query_block(c):
        t0 = c * Q_BLOCK
        pos_q = t0 + jnp.arange(Q_BLOCK)
        qb = lax.dynamic_slice_in_dim(qg, t0, Q_BLOCK, axis=1)
        gb = lax.dynamic_slice_in_dim(gl, t0, Q_BLOCK, axis=1)
        o_c, p_c = compressed_branch(qb, pos_q, kv_cmp, slopes)
        idx, valid = select_blocks(p_c, pos_q, n_sel)
        kv_g = kv_sel_blk[b_ix, idx, :, :, g_ix]
        o_s = selected_branch(qb, pos_q, kv_g, idx, valid, slopes)
        kv_wb = lax.dynamic_slice_in_dim(kv_w_pad, t0, WINDOW + Q_BLOCK, axis=1)
        pos_k = t0 - WINDOW + jnp.arange(WINDOW + Q_BLOCK)
        o_w = window_branch(qb, pos_q, kv_wb, pos_k, slopes)
        return gate_branches(gb, o_c, o_s, o_w)

    out = lax.map(query_block, jnp.arange(t // Q_BLOCK))
    out = jnp.swapaxes(out, 0, 1).reshape(b, t, Q_A)
    new_win = kvw[:, -min(WINDOW, t):]
    return out, new_win


def nsa_sample(qg, kvc, kvs, kvw, gl, cache_cmp, cache_sel, cache_win, page_table, cmp_w, cmp_pos):
    b, t = qg.shape[:2]
    slopes = alibi_slopes()
    n_pages = page_table.shape[1]
    past = n_pages * PAGE_SIZE
    pos_q = past + jnp.arange(t)
    b_ix = jnp.arange(b)[:, None, None, None]
    g_ix = jnp.arange(KV_HEADS)[None, None, :, None]
    past_c = cache_cmp[page_table].reshape(b, past, 2, KV_HEADS, HEAD_DIM)
    parts = jnp.concatenate([chunk_parts(past_c, cmp_w), chunk_parts(kvc, cmp_w)], axis=1)
    kv_cmp = blocks_from_parts(parts, cmp_w, cmp_pos)
    o_c, p_c = compressed_branch(qg, pos_q, kv_cmp, slopes)
    n_sel = -(-(past + t) // SEL_BLOCK)
    idx, valid = select_blocks(p_c, pos_q, n_sel)
    n_past_blk = past // SEL_BLOCK
    blk_per_page = PAGE_SIZE // SEL_BLOCK
    in_past = idx < n_past_blk
    pblk = jnp.minimum(idx, n_past_blk - 1)
    phys = page_table[b_ix, pblk // blk_per_page]
    rows = (pblk % blk_per_page)[..., None] * SEL_BLOCK + jnp.arange(SEL_BLOCK)
    from_pool = cache_sel[phys[..., None], rows, :, g_ix[..., None]]
    n_new_blk = -(-t // SEL_BLOCK)
    tail = jnp.pad(kvs, ((0, 0), (0, n_new_blk * SEL_BLOCK - t), (0, 0), (0, 0), (0, 0)))
    tail = tail.reshape(b, n_new_blk, SEL_BLOCK, 2, KV_HEADS, HEAD_DIM)
    nblk = jnp.clip(idx - n_past_blk, 0, n_new_blk - 1)
    from_tail = tail[b_ix, nblk, :, :, g_ix]
    kv_g = jnp.where(in_past[..., None, None, None], from_pool, from_tail)
    o_s = selected_branch(qg, pos_q, kv_g, idx, valid, slopes)
    w_buf = cache_win.shape[1]
    kv_w = jnp.concatenate([cache_win.astype(kvw.dtype), kvw], axis=1)
    pos_k = past - w_buf + jnp.arange(w_buf + t)
    o_w = window_branch(qg, pos_q, kv_w, pos_k, slopes)
    out = gate_branches(gl, o_c, o_s, o_w).reshape(b, t, Q_A)
    new_win = kv_w[:, -min(WINDOW, w_buf + t):]
    return out, new_win


def hgrn_chunked(q, k, v, log_f, s0):
    b, t = q.shape[:2]
    nc = t // HGRN_CHUNK
    to_chunks = lambda a: jnp.swapaxes(a.reshape(b, nc, HGRN_CHUNK, *a.shape[2:]), 0, 1)
    tril = jnp.tril(jnp.ones((HGRN_CHUNK, HGRN_CHUNK), dtype=bool))

    def step(s, inp):
        qc, kc, vc, lf = inp
        lc = jnp.cumsum(lf, axis=1)
        qt = qc * jnp.exp(lc)
        kt = kc * jnp.exp(-lc)
        a = jnp.where(tril, jnp.einsum('bchk,bshk->bhcs', qt, kt), 0.0)
        o = jnp.einsum('bhcs,bshv->bchv', a, vc) + jnp.einsum('bchk,bhkv->bchv', qt, s)
        lend = lc[:, -1]
        s = jnp.exp(lend)[..., None] * s + jnp.einsum('bshk,bshv->bhkv', kc * jnp.exp(lend[:, None] - lc), vc)
        return s, o

    s, o = lax.scan(step, s0, (to_chunks(q), to_chunks(k), to_chunks(v), to_chunks(log_f)))
    return jnp.swapaxes(o, 0, 1).reshape(b, t, B_HEADS, B_DV), s


def hgrn_recurrent(q, k, v, log_f, s0):
    def step(s, inp):
        qt, kt, vt, lft = inp
        s = jnp.exp(lft)[..., None] * s + kt[..., None] * vt[..., None, :]
        return s, jnp.einsum('bhk,bhkv->bhv', qt, s)

    s, o = lax.scan(step, s0, (jnp.swapaxes(q, 0, 1), jnp.swapaxes(k, 0, 1),
                              jnp.swapaxes(v, 0, 1), jnp.swapaxes(log_f, 0, 1)))
    return jnp.swapaxes(o, 0, 1), s


def hgrn_mixer(qb, fb, ib, gb, lb, norm_g, s0, chunked):
    b, t = qb.shape[:2]
    f32 = jnp.float32
    q = qb.reshape(b, t, B_HEADS, B_DK).astype(f32)
    lbh = lb.reshape(B_HEADS, B_DK)
    f = lbh + (1.0 - lbh) * jax.nn.sigmoid(fb.reshape(b, t, B_HEADS, B_DK).astype(f32))
    log_f = jnp.log(f)
    k = 1.0 - f
    v = ib.reshape(b, t, B_HEADS, B_DV).astype(f32)
    if chunked:
        o, s = hgrn_chunked(q, k, v, log_f, s0.astype(f32))
    else:
        o, s = hgrn_recurrent(q, k, v, log_f, s0.astype(f32))
    o = o * lax.rsqrt(jnp.mean(jnp.square(o), axis=-1, keepdims=True) + LN_EPS) * norm_g.astype(f32)
    o = o.reshape(b, t, I_B) * jax.nn.silu(gb.astype(f32))
    return o.astype(qb.dtype), s


def merge_and_mlp(x, o_a, o_b, mg, w_br_a, w_br_b, w_out, ln1_g, ln1_b, w_up, w_down, ln2_g, ln2_b):
    br_a = jnp.einsum('bse,ed->bsd', o_a, w_br_a)
    br_b = jnp.einsum('bse,ed->bsd', o_b, w_br_b)
    merged = jax.nn.sigmoid(mg[..., :D_MODEL]) * br_a + jax.nn.sigmoid(mg[..., D_MODEL:]) * br_b
    mix = jnp.einsum('bsd,de->bse', merged, w_out)
    x1 = layer_norm(DEEPNORM_ALPHA * x + mix, ln1_g, ln1_b)
    h = jnp.einsum('bsf,fd->bsd', jnp.square(jax.nn.relu(jnp.einsum('bsd,df->bsf', x1, w_up))), w_down)
    return layer_norm(DEEPNORM_ALPHA * x1 + h, ln2_g, ln2_b)


def setup_inputs(seed: int = 0) -> dict:
    key = jax.random.key(seed)
    ks = jax.random.split(key, 24)
    f32 = jnp.float32
    nrm = lambda k, shape, scale: scale * jax.random.normal(k, shape, f32)
    n_pages = PAST_LEN // PAGE_SIZE
    n_pool = (5 * DEC_BATCH * n_pages) // 4
    win_buf = min(WINDOW, PAST_LEN)
    page_table = jax.random.permutation(ks[0], n_pool)[:DEC_BATCH * n_pages]
    page_table = page_table.reshape(DEC_BATCH, n_pages).astype(jnp.int32)
    return {
        'x_prompt': nrm(ks[1], (BATCH, SEQ, D_MODEL), 1.0),
        'x_sample': nrm(ks[2], (DEC_BATCH, DEC_SEQ, D_MODEL), 1.0),
        'cache_cmp_kv': nrm(ks[3], (DEPTH, n_pool, PAGE_SIZE, 2, KV_HEADS, HEAD_DIM), 1.0),
        'cache_sel_kv': nrm(ks[4], (DEPTH, n_pool, PAGE_SIZE, 2, KV_HEADS, HEAD_DIM), 1.0),
        'cache_win_kv': nrm(ks[5], (DEPTH, DEC_BATCH, win_buf, 2, KV_HEADS, HEAD_DIM), 1.0),
        'state_hgrn': nrm(ks[6], (DEPTH, DEC_BATCH, B_HEADS, B_DK, B_DV), 0.1),
        'page_table': page_table,
        'w_in': nrm(ks[7], (DEPTH, D_MODEL, D_IN), D_MODEL ** -0.5),
        'cmp_w': nrm(ks[8], (DEPTH, 2, CMP_LEN, HEAD_DIM, HEAD_DIM), (CMP_LEN * HEAD_DIM) ** -0.5),
        'cmp_pos': nrm(ks[9], (DEPTH, 2, CMP_LEN, HEAD_DIM), 0.5),
        'hgrn_lb_logits': nrm(ks[10], (DEPTH + 1, B_HEADS * B_DK), 0.5),
        'hgrn_norm_g': 1.0 + nrm(ks[11], (DEPTH, B_DV), 0.02),
        'w_br_a': nrm(ks[12], (DEPTH, Q_A, D_MODEL), DEEPNORM_BETA * Q_A ** -0.5),
        'w_br_b': nrm(ks[13], (DEPTH, I_B, D_MODEL), DEEPNORM_BETA * I_B ** -0.5),
        'w_out': nrm(ks[14], (DEPTH, D_MODEL, D_MODEL), DEEPNORM_BETA * D_MODEL ** -0.5),
        'ln1_g': 1.0 + nrm(ks[15], (DEPTH, D_MODEL), 0.02),
        'ln1_b': nrm(ks[16], (DEPTH, D_MODEL), 0.02),
        'w_up': nrm(ks[17], (DEPTH, D_MODEL, D_FF), DEEPNORM_BETA * D_MODEL ** -0.5),
        'w_down': nrm(ks[18], (DEPTH, D_FF, D_MODEL), DEEPNORM_BETA * D_FF ** -0.5),
        'ln2_g': 1.0 + nrm(ks[19], (DEPTH, D_MODEL), 0.02),
        'ln2_b': nrm(ks[20], (DEPTH, D_MODEL), 0.02),
    }


def reference(x_prompt, x_sample, cache_cmp_kv, cache_sel_kv, cache_win_kv, state_hgrn, page_table,
              w_in, cmp_w, cmp_pos, hgrn_lb_logits, hgrn_norm_g, w_br_a, w_br_b, w_out,
              ln1_g, ln1_b, w_up, w_down, ln2_g, ln2_b):
    lb_all = jnp.cumsum(jax.nn.softmax(hgrn_lb_logits.astype(jnp.float32), axis=0), axis=0)
    xp, xs = x_prompt, x_sample
    cmp_p, sel_p, win_p, hg_p = [], [], [], []
    cmp_s, sel_s, win_s, hg_s = [], [], [], []
    for l in range(DEPTH):
        qa, kvc, kvs, kvw, gla, qb, fb, ib, gb, mg = split_projection(xp, w_in[l])
        o_a, nw = nsa_prompt(qa, kvc, kvs, kvw, gla, cmp_w[l], cmp_pos[l])
        s0 = jnp.zeros((xp.shape[0], B_HEADS, B_DK, B_DV), jnp.float32)
        o_b, s_end = hgrn_mixer(qb, fb, ib, gb, lb_all[l], hgrn_norm_g[l], s0, True)
        xp = merge_and_mlp(xp, o_a, o_b, mg, w_br_a[l], w_br_b[l], w_out[l], ln1_g[l], ln1_b[l],
                           w_up[l], w_down[l], ln2_g[l], ln2_b[l])
        cmp_p.append(kvc)
        sel_p.append(kvs)
        win_p.append(nw)
        hg_p.append(s_end.astype(x_prompt.dtype))
        qa, kvc, kvs, kvw, gla, qb, fb, ib, gb, mg = split_projection(xs, w_in[l])
        o_a, nw = nsa_sample(qa, kvc, kvs, kvw, gla, cache_cmp_kv[l], cache_sel_kv[l], cache_win_kv[l],
                             page_table, cmp_w[l], cmp_pos[l])
        o_b, s_end = hgrn_mixer(qb, fb, ib, gb, lb_all[l], hgrn_norm_g[l], state_hgrn[l], False)
        xs = merge_and_mlp(xs, o_a, o_b, mg, w_br_a[l], w_br_b[l], w_out[l], ln1_g[l], ln1_b[l],
                           w_up[l], w_down[l], ln2_g[l], ln2_b[l])
        cmp_s.append(kvc)
        sel_s.append(kvs)
        win_s.append(nw)
        hg_s.append(s_end.astype(x_sample.dtype))
    return (xp, xs,
            jnp.stack(cmp_p), jnp.stack(sel_p), jnp.stack(win_p), jnp.stack(hg_p),
            jnp.stack(cmp_s), jnp.stack(sel_s), jnp.stack(win_s), jnp.stack(hg_s))
```

```python
import functools

import numpy as np
import jax
import jax.numpy as jnp
from jax import lax
from jax.experimental import pallas as pl
from jax.experimental.pallas import tpu as pltpu

F32 = jnp.float32
BF16 = jnp.bfloat16

D_MODEL = 1024
HEAD_DIM = 64
A_HEADS = 8
KV_HEADS = 2
GROUP = A_HEADS // KV_HEADS
CMP_STRIDE = 16
CMP_LEN = 32
SEL_BLOCK = 64
N_SELECT = 16
WINDOW = 512
PAGE_SIZE = 128
B_HEADS = 8
B_DK = 128
B_DV = 64
HGRN_CHUNK = 32
D_FF = 4 * D_MODEL
DEEPNORM_ALPHA = 2.0 ** 0.25
LN_EPS = 1e-5
NEG = -1e30
FORCE = 1e6
ATTN_SCALE = HEAD_DIM ** -0.5
Q_A = A_HEADS * HEAD_DIM
KV_A = 2 * KV_HEADS * HEAD_DIM
GATE_A = 3 * A_HEADS
Q_B = B_HEADS * B_DK
I_B = B_HEADS * B_DV

C_MG = 0
C_QB = 2048
C_FB = 3072
C_QA = 4096
C_IB = 4608
C_GB = 5120
C_KVC = 5632
C_KVS = 5888
C_KVW = 6144
C_GL = 6400
D_PAD = 6656
PROJ_TN = 1664

VMEM_LIMIT = 56 * 1024 * 1024
UNSEL = float(2.0 ** 100)


def _cparams(sem):
    return pltpu.CompilerParams(dimension_semantics=sem, vmem_limit_bytes=VMEM_LIMIT)


def _proj_perm():
    perm = np.full((D_PAD,), -1, np.int64)
    o_qa, o_kvc, o_kvs, o_kvw = 0, Q_A, Q_A + KV_A, Q_A + 2 * KV_A
    o_gl = Q_A + 3 * KV_A
    o_qb = o_gl + GATE_A
    o_fb = o_qb + Q_B
    o_ib = o_fb + Q_B
    o_gb = o_ib + I_B
    o_mg = o_gb + I_B
    perm[C_MG:C_MG + 2 * D_MODEL] = o_mg + np.arange(2 * D_MODEL)
    perm[C_QB:C_QB + Q_B] = o_qb + np.arange(Q_B)
    perm[C_FB:C_FB + Q_B] = o_fb + np.arange(Q_B)
    perm[C_QA:C_QA + Q_A] = o_qa + np.arange(Q_A)
    perm[C_IB:C_IB + I_B] = o_ib + np.arange(I_B)
    perm[C_GB:C_GB + I_B] = o_gb + np.arange(I_B)
    for new, old in ((C_KVC, o_kvc), (C_KVS, o_kvs), (C_KVW, o_kvw)):
        for g in range(KV_HEADS):
            for c in range(2):
                dst = new + g * 128 + c * 64
                src = old + c * 128 + g * 64
                perm[dst:dst + 64] = src + np.arange(64)
    for g in range(KV_HEADS):
        perm[C_GL + g * 128:C_GL + g * 128 + 12] = o_gl + g * 12 + np.arange(12)
    return perm


_PERM = _proj_perm()


def _proj_kernel(x_ref, w_ref, o_ref):
    o_ref[...] = jnp.dot(x_ref[...].astype(BF16), w_ref[...], preferred_element_type=F32)


def _project(x2d, w_pad):
    n = x2d.shape[0]
    tm = min(512, n)
    return pl.pallas_call(
        _proj_kernel,
        grid=(D_PAD // PROJ_TN, n // tm),
        in_specs=[pl.BlockSpec((tm, D_MODEL), lambda j, i: (i, 0)),
                  pl.BlockSpec((D_MODEL, PROJ_TN), lambda j, i: (0, j))],
        out_specs=pl.BlockSpec((tm, PROJ_TN), lambda j, i: (i, j)),
        out_shape=jax.ShapeDtypeStruct((n, D_PAD), F32),
        compiler_params=_cparams(("arbitrary", "arbitrary")),
        name="proj",
    )(x2d, w_pad)


def _layer_norm(v, g, b):
    mu = jnp.mean(v, axis=-1, keepdims=True)
    d = v - mu
    var = jnp.mean(d * d, axis=-1, keepdims=True)
    return d * lax.rsqrt(var + LN_EPS) * g + b


def _merge_kernel(x_ref, oa_ref, ob_ref, mga_ref, mgb_ref, wa_ref, wb_ref, wo_ref, g_ref, b_ref, o_ref):
    br_a = jnp.dot(oa_ref[...].astype(BF16), wa_ref[...], preferred_element_type=F32)
    br_b = jnp.dot(ob_ref[...].astype(BF16), wb_ref[...], preferred_element_type=F32)
    merged = jax.nn.sigmoid(mga_ref[...]) * br_a + jax.nn.sigmoid(mgb_ref[...]) * br_b
    mix = jnp.dot(merged.astype(BF16), wo_ref[...], preferred_element_type=F32)
    o_ref[...] = _layer_norm(DEEPNORM_ALPHA * x_ref[...] + mix, g_ref[...], b_ref[...])


def _merge(x2d, o_a, o_b, h, w_a, w_b, w_o, ln_g, ln_b):
    n = x2d.shape[0]
    tm = min(256, n)
    const = lambda i: (0, 0)
    return pl.pallas_call(
        _merge_kernel,
        grid=(n // tm,),
        in_specs=[pl.BlockSpec((tm, D_MODEL), lambda i: (i, 0)),
                  pl.BlockSpec((tm, Q_A), lambda i: (i, 0)),
                  pl.BlockSpec((tm, I_B), lambda i: (i, 0)),
                  pl.BlockSpec((tm, D_MODEL), lambda i: (i, C_MG // D_MODEL)),
                  pl.BlockSpec((tm, D_MODEL), lambda i: (i, C_MG // D_MODEL + 1)),
                  pl.BlockSpec((Q_A, D_MODEL), const),
                  pl.BlockSpec((I_B, D_MODEL), const),
                  pl.BlockSpec((D_MODEL, D_MODEL), const),
                  pl.BlockSpec((1, D_MODEL), const),
                  pl.BlockSpec((1, D_MODEL), const)],
        out_specs=pl.BlockSpec((tm, D_MODEL), lambda i: (i, 0)),
        out_shape=jax.ShapeDtypeStruct((n, D_MODEL), F32),
        compiler_params=_cparams(("arbitrary",)),
        name="merge_ln1",
    )(x2d, o_a, o_b, h, h, w_a, w_b, w_o, ln_g, ln_b)


def _mlp_kernel(x_ref, wu_ref, wd_ref, g_ref, b_ref, o_ref, xb_ref, acc_ref):
    j = pl.program_id(1)

    @pl.when(j == 0)
    def _():
        xb_ref[...] = x_ref[...].astype(BF16)
        acc_ref[...] = jnp.zeros_like(acc_ref)

    u = jnp.dot(xb_ref[...], wu_ref[...], preferred_element_type=F32)
    u = jnp.maximum(u, 0.0)
    acc_ref[...] += jnp.dot((u * u).astype(BF16), wd_ref[...], preferred_element_type=F32)

    @pl.when(j == pl.num_programs(1) - 1)
    def _():
        o_ref[...] = _layer_norm(DEEPNORM_ALPHA * x_ref[...] + acc_ref[...], g_ref[...], b_ref[...])


def _mlp(x1, w_up, w_down, ln_g, ln_b):
    n = x1.shape[0]
    tm = min(1024, n)
    tf = 1024
    return pl.pallas_call(
        _mlp_kernel,
        grid=(n // tm, D_FF // tf),
        in_specs=[pl.BlockSpec((tm, D_MODEL), lambda i, j: (i, 0)),
                  pl.BlockSpec((D_MODEL, tf), lambda i, j: (0, j)),
                  pl.BlockSpec((tf, D_MODEL), lambda i, j: (j, 0)),
                  pl.BlockSpec((1, D_MODEL), lambda i, j: (0, 0)),
                  pl.BlockSpec((1, D_MODEL), lambda i, j: (0, 0))],
        out_specs=pl.BlockSpec((tm, D_MODEL), lambda i, j: (i, 0)),
        out_shape=jax.ShapeDtypeStruct((n, D_MODEL), F32),
        scratch_shapes=[pltpu.VMEM((tm, D_MODEL), BF16), pltpu.VMEM((tm, D_MODEL), F32)],
        compiler_params=_cparams(("arbitrary", "arbitrary")),
        name="mlp_ln2",
    )(x1, w_up, w_down, ln_g, ln_b)


def _dot01(a01, x):
    hi = x.astype(BF16)
    r1 = x - hi.astype(F32)
    mid = r1.astype(BF16)
    lo = (r1 - mid.astype(F32)).astype(BF16)
    out = jnp.dot(a01, hi, preferred_element_type=F32)
    out += jnp.dot(a01, mid, preferred_element_type=F32)
    out += jnp.dot(a01, lo, preferred_element_type=F32)
    return out


def _x_dot01(x, b01):
    hi = x.astype(BF16)
    r1 = x - hi.astype(F32)
    mid = r1.astype(BF16)
    lo = (r1 - mid.astype(F32)).astype(BF16)
    out = jnp.dot(hi, b01, preferred_element_type=F32)
    out += jnp.dot(mid, b01, preferred_element_type=F32)
    out += jnp.dot(lo, b01, preferred_element_type=F32)
    return out


_NT = (((1,), (1,)), ((), ()))
_TN = (((0,), (0,)), ((), ()))


def _top_k_mask(score, k):
    rows, lanes = score.shape
    lane = lax.broadcasted_iota(jnp.int32, (rows, lanes), 1)
    sel = jnp.zeros((rows, lanes), jnp.bool_)
    work = score
    for _ in range(k):
        m = jnp.max(work, axis=-1, keepdims=True)
        first = jnp.min(jnp.where(work == m, lane, lanes), axis=-1, keepdims=True)
        pick = lane == first
        sel = jnp.logical_or(sel, pick)
        work = jnp.where(pick, -jnp.inf, work)
    return sel


def _softmax_tile(s, valid, carry, kv_b):
    m, l, acc = carry
    m_new = jnp.maximum(m, jnp.max(s, axis=-1, keepdims=True))
    alpha = jnp.exp(m - m_new)
    p = jnp.exp(s - m_new)
    if valid is not None:
        p = jnp.where(valid, p, 0.0)
    l = alpha * l + jnp.sum(p, axis=-1, keepdims=True)
    acc = alpha * acc + jnp.dot(p.astype(BF16), kv_b, preferred_element_type=F32)
    return m_new, l, acc


def _cmp_prompt_kernel(kv_ref, w_ref, posf_ref, wb_ref, o_ref, *, n_ch):
    acc = jnp.zeros((n_ch, 256), F32)
    for l in range(CMP_STRIDE):
        xl = kv_ref[pl.ds(l, n_ch, stride=CMP_STRIDE), :].astype(BF16)
        acc += jnp.dot(xl, w_ref[l], preferred_element_type=F32)
    bias = jnp.dot(posf_ref[...].astype(BF16), wb_ref[...], preferred_element_type=F32)[0:1]
    nxt = pltpu.roll(acc[:, 128:], n_ch - 1, 0)
    o_ref[0] = (acc[:, :128] + nxt + bias).astype(BF16)


def _cmp_weights(cmp_w, cmp_pos):
    w = cmp_w.reshape(2, 2, CMP_STRIDE, HEAD_DIM, HEAD_DIM)
    wl = jnp.zeros((CMP_STRIDE, 2, HEAD_DIM, 2, 2, HEAD_DIM), F32)
    wb = jnp.zeros((2, CMP_LEN * HEAD_DIM, 2, HEAD_DIM), F32)
    for c in range(2):
        wl = wl.at[:, c, :, :, c, :].set(jnp.transpose(w[c], (1, 2, 0, 3)))
        wb = wb.at[c, :, c, :].set(cmp_w[c].reshape(CMP_LEN * HEAD_DIM, HEAD_DIM))
    wl = wl.reshape(CMP_STRIDE, 128, 256).astype(BF16)
    wb = wb.reshape(2 * CMP_LEN * HEAD_DIM, 128).astype(BF16)
    posf = jnp.broadcast_to(cmp_pos.reshape(1, 2 * CMP_LEN * HEAD_DIM), (8, 2 * CMP_LEN * HEAD_DIM))
    return wl, wb, posf


def _cmp_prompt(h, b, t, wl, wb, posf):
    n_ch = t // CMP_STRIDE
    return pl.pallas_call(
        functools.partial(_cmp_prompt_kernel, n_ch=n_ch),
        grid=(b, KV_HEADS),
        in_specs=[pl.BlockSpec((t, 128), lambda i, g: (i, C_KVC // 128 + g)),
                  pl.BlockSpec((CMP_STRIDE, 128, 256), lambda i, g: (0, 0, 0)),
                  pl.BlockSpec((8, 4096), lambda i, g: (0, 0)),
                  pl.BlockSpec((4096, 128), lambda i, g: (0, 0))],
        out_specs=pl.BlockSpec((1, n_ch, 128), lambda i, g: (i, 0, g)),
        out_shape=jax.ShapeDtypeStruct((b, n_ch, 256), BF16),
        compiler_params=_cparams(("arbitrary", "arbitrary")),
        name="cmp_prompt",
    )(h, wl, posf, wb)


QB = 128
KT = 256
WT = 128


def _nsa_prompt_kernel(q_ref, gl_ref, kc_ref, ks_ref, kw_ref, o_ref, *, t, n_ch):
    g = pl.program_id(1)
    c = pl.program_id(2)
    t0 = c * QB
    rows = GROUP * QB
    q = q_ref[...]
    qs = jnp.concatenate([q[:, r * 64:(r + 1) * 64] for r in range(GROUP)], axis=0) * ATTN_SCALE
    qpad = jnp.concatenate([qs, jnp.zeros((rows, 64), F32)], axis=1).astype(BF16)
    pos_q = t0 + lax.broadcasted_iota(jnp.int32, (QB, 1), 0)
    slopes = [jnp.where(g == 0, 2.0 ** -(r + 1), 2.0 ** -(r + 1 + GROUP)).astype(F32) for r in range(GROUP)]

    kc = kc_ref[0]
    s = lax.dot_general(qpad, kc, _NT, preferred_element_type=F32)
    n_ix = lax.broadcasted_iota(jnp.int32, (1, n_ch), 1)
    dist_i = pos_q - (n_ix * CMP_STRIDE + (CMP_LEN - 1))
    valid = jnp.logical_and(dist_i >= 0, n_ix < n_ch - 1)
    dist = dist_i.astype(F32)
    p_rows = []
    imp = jnp.zeros((QB, n_ch), F32)
    for r in range(GROUP):
        sr = jnp.where(valid, s[r * QB:(r + 1) * QB] - slopes[r] * dist, NEG)
        m = jnp.max(sr, axis=-1, keepdims=True)
        p = jnp.where(valid, jnp.exp(sr - m), 0.0)
        p = p / jnp.maximum(jnp.sum(p, axis=-1, keepdims=True), 1e-30)
        p_rows.append(p)
        imp = imp + p
    pc = jnp.concatenate(p_rows, axis=0)
    o_c = jnp.dot(pc.astype(BF16), kc, preferred_element_type=F32)

    ratio = SEL_BLOCK // CMP_STRIDE
    pool = (lax.broadcasted_iota(jnp.int32, (n_ch, 128), 0) // ratio
            == lax.broadcasted_iota(jnp.int32, (n_ch, 128), 1)).astype(BF16)
    imp_blk = _x_dot01(imp, pool)
    blk = lax.broadcasted_iota(jnp.int32, (1, 128), 1)
    cur = pos_q // SEL_BLOCK
    forced = jnp.logical_or(jnp.logical_or(blk == 0, blk == cur), blk == cur - 1)
    allowed = blk * SEL_BLOCK <= pos_q
    score = jnp.where(allowed, jnp.where(forced, FORCE, imp_blk), -FORCE)
    sel = jnp.logical_and(_top_k_mask(score, N_SELECT), allowed)
    unsel = jnp.where(sel, 0.0, -UNSEL).astype(BF16)
    q_aug = jnp.concatenate([qpad, jnp.concatenate([unsel] * GROUP, axis=0)], axis=1)

    def attend(s, dist_i, valid, carry, kv_b):
        dist = dist_i.astype(F32)
        parts = [jnp.where(valid, s[r * QB:(r + 1) * QB] - slopes[r] * dist, NEG) for r in range(GROUP)]
        return _softmax_tile(jnp.concatenate(parts, axis=0), None, carry, kv_b)

    init = (jnp.full((rows, 1), NEG, F32), jnp.zeros((rows, 1), F32), jnp.zeros((rows, 128), F32))

    def sel_body(j, carry):
        k0 = pl.multiple_of(j * KT, KT)
        kv_b = ks_ref[pl.ds(k0, KT), :].astype(BF16)
        key = k0 + lax.broadcasted_iota(jnp.int32, (KT, 128), 0)
        onehot = (key // SEL_BLOCK == lax.broadcasted_iota(jnp.int32, (KT, 128), 1)).astype(BF16)
        s = lax.dot_general(q_aug, jnp.concatenate([kv_b, onehot], axis=1), _NT, preferred_element_type=F32)
        dist_i = pos_q - (k0 + lax.broadcasted_iota(jnp.int32, (1, KT), 1))
        return attend(s, dist_i, dist_i >= 0, carry, kv_b)

    n_sel_tiles = (t0 + QB + KT - 1) // KT
    _, l_s, acc_s = lax.fori_loop(0, n_sel_tiles, sel_body, init)
    o_s = acc_s / l_s

    n_wt = WINDOW // WT + 1

    def win_body(i, carry):
        k0 = pl.multiple_of(t0 - i * WT, WT)
        kv_b = kw_ref[pl.ds(k0, WT), :].astype(BF16)
        s = lax.dot_general(qpad, kv_b, _NT, preferred_element_type=F32)
        dist_i = pos_q - (k0 + lax.broadcasted_iota(jnp.int32, (1, WT), 1))
        valid = jnp.logical_and(dist_i >= 0, dist_i < WINDOW)
        return attend(s, dist_i, valid, carry, kv_b)

    _, l_w, acc_w = lax.fori_loop(0, jnp.minimum(n_wt, c + 1), win_body, init)
    o_w = acc_w / l_w

    gates = jax.nn.sigmoid(gl_ref[...])
    for r in range(GROUP):
        sl = slice(r * QB, (r + 1) * QB)
        o = (gates[:, 3 * r:3 * r + 1] * o_c[sl] + gates[:, 3 * r + 1:3 * r + 2] * o_s[sl]
             + gates[:, 3 * r + 2:3 * r + 3] * o_w[sl])
        o_ref[:, r * 64:(r + 1) * 64] = o[:, 64:128]


def _nsa_prompt(h, kcmp, b, t):
    n_ch = t // CMP_STRIDE
    nq = t // QB
    return pl.pallas_call(
        functools.partial(_nsa_prompt_kernel, t=t, n_ch=n_ch),
        grid=(b, KV_HEADS, nq),
        in_specs=[pl.BlockSpec((QB, 256), lambda i, g, c: (i * nq + c, C_QA // 256 + g)),
                  pl.BlockSpec((QB, 128), lambda i, g, c: (i * nq + c, C_GL // 128 + g)),
                  pl.BlockSpec((1, n_ch, 128), lambda i, g, c: (i, 0, g)),
                  pl.BlockSpec((t, 128), lambda i, g, c: (i, C_KVS // 128 + g)),
                  pl.BlockSpec((t, 128), lambda i, g, c: (i, C_KVW // 128 + g))],
        out_specs=pl.BlockSpec((QB, 256), lambda i, g, c: (i * nq + c, g)),
        out_shape=jax.ShapeDtypeStruct((b * t, Q_A), F32),
        compiler_params=_cparams(("arbitrary", "arbitrary", "arbitrary")),
        name="nsa_prompt",
    )(h, h, kcmp, h, h)


HB = 256


def _lower_bound(lb_ref):
    z = lb_ref[...]
    e = jnp.exp(z - jnp.max(z, axis=0, keepdims=True))
    return e[0:1] / jnp.sum(e, axis=0, keepdims=True)


def _hgrn_prompt_kernel(q_ref, f_ref, i_ref, g_ref, lb_ref, ng_ref, o_ref, s_out_ref, s_ref):
    tb = pl.program_id(1)

    @pl.when(tb == 0)
    def _():
        s_ref[...] = jnp.zeros_like(s_ref)

    n_c = HB // HGRN_CHUNK
    lb = _lower_bound(lb_ref)
    f = lb + (1.0 - lb) * jax.nn.sigmoid(f_ref[...])
    log_f = jnp.log(f)
    row = lax.broadcasted_iota(jnp.int32, (HB, HB), 0)
    col = lax.broadcasted_iota(jnp.int32, (HB, HB), 1)
    tril = jnp.logical_and(col <= row, row // HGRN_CHUNK == col // HGRN_CHUNK)
    lc = _dot01(tril.astype(BF16), log_f)
    lc3 = lc.reshape(n_c, HGRN_CHUNK, Q_B)
    lend = jnp.broadcast_to(lc3[:, HGRN_CHUNK - 1:HGRN_CHUNK, :], lc3.shape).reshape(HB, Q_B)
    q_t = (q_ref[...] * jnp.exp(lc)).astype(BF16)
    k = 1.0 - f
    k_t = (k * jnp.exp(-lc)).astype(BF16)
    k_e = (k * jnp.exp(lend - lc)).astype(BF16)
    dec = jnp.exp(lend)
    v_all = i_ref[...]
    gate = g_ref[...]
    ng = ng_ref[...]
    for hh in range(B_HEADS):
        ks = slice(hh * B_DK, (hh + 1) * B_DK)
        vs = slice(hh * B_DV, (hh + 1) * B_DV)
        v = v_all[:, vs].astype(BF16)
        a = lax.dot_general(q_t[:, ks], k_t[:, ks], _NT, preferred_element_type=F32)
        a = jnp.where(tril, a, 0.0).astype(BF16)
        o = jnp.dot(a, v, preferred_element_type=F32)
        dec_t = jnp.transpose(dec[:, ks])
        st = s_ref[hh]
        inter = []
        for cc in range(n_c):
            rs = slice(cc * HGRN_CHUNK, (cc + 1) * HGRN_CHUNK)
            inter.append(jnp.dot(q_t[rs, ks], st.astype(BF16), preferred_element_type=F32))
            u = lax.dot_general(k_e[rs, ks], v[rs], _TN, preferred_element_type=F32)
            last = (cc + 1) * HGRN_CHUNK - 1
            st = dec_t[:, last:last + 1] * st + u
        s_ref[hh] = st
        o = o + jnp.concatenate(inter, axis=0)
        o = o * lax.rsqrt(jnp.mean(o * o, axis=-1, keepdims=True) + LN_EPS) * ng
        gt = gate[:, vs]
        o_ref[:, vs] = o * (gt * jax.nn.sigmoid(gt))

    @pl.when(tb == pl.num_programs(1) - 1)
    def _():
        s_out_ref[0] = s_ref[...]


def _hgrn_prompt(h, b, t, lb_logits, norm_g):
    nt = t // HB
    return pl.pallas_call(
        _hgrn_prompt_kernel,
        grid=(b, nt),
        in_specs=[pl.BlockSpec((HB, Q_B), lambda i, j: (i * nt + j, C_QB // Q_B)),
                  pl.BlockSpec((HB, Q_B), lambda i, j: (i * nt + j, C_FB // Q_B)),
                  pl.BlockSpec((HB, I_B), lambda i, j: (i * nt + j, C_IB // I_B)),
                  pl.BlockSpec((HB, I_B), lambda i, j: (i * nt + j, C_GB // I_B)),
                  pl.BlockSpec((2, Q_B), lambda i, j: (0, 0)),
                  pl.BlockSpec((1, B_DV), lambda i, j: (0, 0))],
        out_specs=[pl.BlockSpec((HB, I_B), lambda i, j: (i * nt + j, 0)),
                   pl.BlockSpec((1, B_HEADS, B_DK, B_DV), lambda i, j: (i, 0, 0, 0))],
        out_shape=[jax.ShapeDtypeStruct((b * t, I_B), F32),
                   jax.ShapeDtypeStruct((b, B_HEADS, B_DK, B_DV), F32)],
        scratch_shapes=[pltpu.VMEM((B_HEADS, B_DK, B_DV), F32)],
        compiler_params=_cparams(("arbitrary", "arbitrary")),
        name="hgrn_prompt",
    )(h, h, h, h, lb_logits, norm_g)


def _prompt_layer(x, w_pad, cmpw, lb_logits, norm_g, w_a, w_b, w_o, ln1_g, ln1_b, w_up, w_down, ln2_g, ln2_b):
    b, t, _ = x.shape
    x2d = x.reshape(b * t, D_MODEL)
    h = _project(x2d, w_pad)
    kcmp = _cmp_prompt(h, b, t, *cmpw)
    o_a = _nsa_prompt(h, kcmp, b, t)
    o_b, s_end = _hgrn_prompt(h, b, t, lb_logits, norm_g)
    x1 = _merge(x2d, o_a, o_b, h, w_a, w_b, w_o, ln1_g, ln1_b)
    y = _mlp(x1, w_up, w_down, ln2_g, ln2_b)
    return y.reshape(b, t, D_MODEL), h, s_end


def _kv_out(h, col, b, t):
    kv = h[:, col:col + 256].reshape(b, t, KV_HEADS, 2, HEAD_DIM)
    return jnp.swapaxes(kv, 2, 3)


CMP_ROWS = 256
CH_PER_PAGE = PAGE_SIZE // CMP_STRIDE
BLK_PER_PAGE = PAGE_SIZE // SEL_BLOCK
N_POOL_SEL = N_SELECT - 1


def _head_rows(q_row, g):
    rows = [q_row[:, g * 256 + r * 64:g * 256 + (r + 1) * 64] for r in range(GROUP)]
    return jnp.concatenate(rows + [jnp.zeros((8 - GROUP, 64), F32)], axis=0) * ATTN_SCALE


def _slope_col(g):
    row = lax.broadcasted_iota(jnp.int32, (8, 1), 0)
    col = jnp.zeros((8, 1), F32)
    for r in range(GROUP):
        col = jnp.where(row == r, 2.0 ** -(g * GROUP + r + 1), col)
    return col


def _cmp_sample_kernel(pt_ref, cache_ref, w_ref, posf_ref, wb_ref, q_ref, oc_ref, idx_ref,
                       buf_ref, parts_ref, sem, *, n_pages, past):
    b = pl.program_id(0)
    n_ch = n_pages * CH_PER_PAGE
    n_blk = past // SEL_BLOCK
    lanes = idx_ref.shape[-1]

    def page_copy(pg, page):
        return pltpu.make_async_copy(cache_ref.at[page], buf_ref.at[pl.ds(pg * CH_PER_PAGE, CH_PER_PAGE)], sem)

    def issue(pg, carry):
        page_copy(pg, pt_ref[b, pg]).start()
        return carry

    def wait(pg, carry):
        page_copy(pg, 0).wait()
        return carry

    lax.fori_loop(0, n_pages, issue, 0)
    lax.fori_loop(0, n_pages, wait, 0)

    for i in range(n_ch // CMP_ROWS):
        rs = slice(i * CMP_ROWS, (i + 1) * CMP_ROWS)
        parts_ref[rs, :] = jnp.dot(buf_ref[rs, :].astype(BF16), w_ref[...], preferred_element_type=F32)
    bias = jnp.dot(posf_ref[...].astype(BF16), wb_ref[...], preferred_element_type=F32)[0:1]
    bias = jnp.concatenate([bias] * KV_HEADS, axis=1)
    kc = (parts_ref[:, :256] + pltpu.roll(parts_ref[:, 256:], n_ch - 1, 0) + bias).astype(BF16)

    q_row = q_ref[0]
    n_ix = lax.broadcasted_iota(jnp.int32, (1, n_ch), 1)
    dist_i = past - (n_ix * CMP_STRIDE + (CMP_LEN - 1))
    valid = jnp.logical_and(dist_i >= 0, n_ix < n_ch - 1)
    dist = dist_i.astype(F32)
    ratio = SEL_BLOCK // CMP_STRIDE
    pool = (lax.broadcasted_iota(jnp.int32, (n_ch, lanes), 0) // ratio
            == lax.broadcasted_iota(jnp.int32, (n_ch, lanes), 1)).astype(BF16)
    row8 = lax.broadcasted_iota(jnp.int32, (8, lanes), 0)
    score = jnp.full((8, lanes), -FORCE, F32)
    for g in range(KV_HEADS):
        kc_g = kc[:, g * 128:(g + 1) * 128]
        qpad = jnp.concatenate([_head_rows(q_row, g), jnp.zeros((8, 64), F32)], axis=1).astype(BF16)
        s = lax.dot_general(qpad, kc_g, _NT, preferred_element_type=F32)
        s = jnp.where(valid, s - _slope_col(g) * dist, NEG)
        m = jnp.max(s, axis=-1, keepdims=True)
        p = jnp.where(valid, jnp.exp(s - m), 0.0)
        p = p / jnp.maximum(jnp.sum(p, axis=-1, keepdims=True), 1e-30)
        oc_ref[0, g] = jnp.dot(p.astype(BF16), kc_g, preferred_element_type=F32)
        imp = p[0:1] + p[1:2] + p[2:3] + p[3:4]
        imp_blk = _x_dot01(jnp.broadcast_to(imp, (8, n_ch)), pool)
        score = jnp.where(row8 == g, imp_blk, score)
    lane = lax.broadcasted_iota(jnp.int32, (8, lanes), 1)
    forced = jnp.logical_or(lane == 0, lane == n_blk - 1)
    score = jnp.where(lane < n_blk, jnp.where(forced, FORCE, score), -jnp.inf)
    out = jnp.zeros((8, lanes), jnp.int32)
    for r in range(N_POOL_SEL):
        m = jnp.max(score, axis=-1, keepdims=True)
        first = jnp.min(jnp.where(score == m, lane, lanes), axis=-1, keepdims=True)
        out = jnp.where(lane == r, first, out)
        score = jnp.where(lane == first, -jnp.inf, score)
    idx_ref[0] = out


def _cmp_sample(page_table, cache3, w_s, posf, wb, q3, past):
    bsz, n_pages = page_table.shape
    n_ch = n_pages * CH_PER_PAGE
    lanes = -(-(past // SEL_BLOCK) // 128) * 128
    grid_spec = pltpu.PrefetchScalarGridSpec(
        num_scalar_prefetch=1,
        grid=(bsz,),
        in_specs=[pl.BlockSpec(memory_space=pl.ANY),
                  pl.BlockSpec((CMP_STRIDE * 256, 512), lambda i, pt: (0, 0)),
                  pl.BlockSpec((8, 4096), lambda i, pt: (0, 0)),
                  pl.BlockSpec((4096, 128), lambda i, pt: (0, 0)),
                  pl.BlockSpec((1, 1, Q_A), lambda i, pt: (i, 0, 0))],
        out_specs=[pl.BlockSpec((1, KV_HEADS, 8, 128), lambda i, pt: (i, 0, 0, 0)),
                   pl.BlockSpec((1, 8, lanes), lambda i, pt: (i, 0, 0))],
        scratch_shapes=[pltpu.VMEM((n_ch, CMP_STRIDE * 256), F32),
                        pltpu.VMEM((n_ch, 512), F32),
                        pltpu.SemaphoreType.DMA(())],
    )
    return pl.pallas_call(
        functools.partial(_cmp_sample_kernel, n_pages=n_pages, past=past),
        grid_spec=grid_spec,
        out_shape=[jax.ShapeDtypeStruct((bsz, KV_HEADS, 8, 128), F32),
                   jax.ShapeDtypeStruct((bsz, 8, lanes), jnp.int32)],
        compiler_params=_cparams(("arbitrary",)),
        name="cmp_sample",
    )(page_table, cache3, w_s, posf, wb, q3)


def _to_cgd(row):
    return jnp.concatenate([row[:, 0:64], row[:, 128:192], row[:, 64:128], row[:, 192:256]], axis=1)


def _pick_lane(mat, lane, target):
    return jnp.sum(jnp.where(lane == target, mat, 0.0), axis=-1, keepdims=True)


def _sel_win_sample_kernel(pt_ref, idx_ref, cache_ref, win_ref, q_ref, ks_ref, kw_ref, gl_ref, oc_ref, o_ref,
                           buf_ref, sem, *, past):
    b = pl.program_id(0)
    slot_rows = N_SELECT * SEL_BLOCK

    def blk_copy(slot, row):
        return pltpu.make_async_copy(cache_ref.at[row], buf_ref.at[pl.ds(slot * SEL_BLOCK, SEL_BLOCK)], sem)

    for g in range(KV_HEADS):
        for k in range(N_POOL_SEL):
            blk = idx_ref[b, g * N_SELECT + k]
            page = pt_ref[b, blk // BLK_PER_PAGE]
            blk_copy(g * N_SELECT + k, page * BLK_PER_PAGE + blk % BLK_PER_PAGE).start()
        buf_ref[pl.ds((g * N_SELECT + N_POOL_SEL) * SEL_BLOCK, SEL_BLOCK), :] = jnp.zeros((SEL_BLOCK, 256), F32)

    q_row = q_ref[0]
    ks_new = _to_cgd(ks_ref[0])
    kw_new = _to_cgd(kw_ref[0])
    gl_all = jax.nn.sigmoid(gl_ref[0])
    lane128 = lax.broadcasted_iota(jnp.int32, (8, 128), 1)
    row8 = lax.broadcasted_iota(jnp.int32, (8, 1), 0)
    win = win_ref[0].astype(BF16)
    w_len = win.shape[0]

    def two_piece(q_aug, slope, kv_b, dist_i, valid, new_row):
        s = lax.dot_general(q_aug.astype(BF16), kv_b, _NT, preferred_element_type=F32)
        s = jnp.where(valid, s - slope * dist_i.astype(F32), NEG)
        s_n = jnp.sum(q_aug * new_row, axis=-1, keepdims=True)
        m =jnp.maximum(jnp.max(s, axis=-1, keepdims=True), s_n)
        p = jnp.where(valid, jnp.exp(s - m), 0.0)
        p_n = jnp.exp(s_n - m)
        l = jnp.sum(p, axis=-1, keepdims=True) + p_n
        o = jnp.dot(p.astype(BF16), kv_b, preferred_element_type=F32) + p_n * new_row
        return o / l

    outs = []
    for g in range(KV_HEADS):
        qh = _head_rows(q_row, g)
        pieces = [jnp.zeros((8, 64), F32)] * 4
        pieces[g] = qh
        q_aug = jnp.concatenate(pieces, axis=1)
        slope = _slope_col(g)
        j_ix = lax.broadcasted_iota(jnp.int32, (1, w_len), 1)
        dist_w = w_len - j_ix
        o_w = two_piece(q_aug, slope, win, dist_w, dist_w < WINDOW, kw_new)
        outs.append((q_aug, slope, o_w))

    for g in range(KV_HEADS):
        for k in range(N_POOL_SEL):
            blk_copy(g * N_SELECT + k, 0).wait()

    lane_s = lax.broadcasted_iota(jnp.int32, (1, slot_rows), 1)
    for g in range(KV_HEADS):
        q_aug, slope, o_w = outs[g]
        pos_k = jnp.full((1, slot_rows), past + 1, jnp.int32)
        for k in range(N_POOL_SEL):
            blk = idx_ref[b, g * N_SELECT + k]
            pos_k = jnp.where(lane_s // SEL_BLOCK == k, blk * SEL_BLOCK + lane_s % SEL_BLOCK, pos_k)
        dist_s = past - pos_k
        kv_b = buf_ref[pl.ds(g * slot_rows, slot_rows), :].astype(BF16)
        o_s = two_piece(q_aug, slope, kv_b, dist_s, dist_s >= 0, ks_new)
        o_c = oc_ref[0, g]
        o_c = pltpu.roll(jnp.concatenate([o_c, jnp.zeros((8, 128), F32)], axis=1), 64 + g * 64, 1)
        gates = jnp.broadcast_to(gl_all[:, g * 128:(g + 1) * 128], (8, 128))
        g_c = _pick_lane(gates, lane128, 3 * row8)
        g_s = _pick_lane(gates, lane128, 3 * row8 + 1)
        g_w = _pick_lane(gates, lane128, 3 * row8 + 2)
        o_ref[0, g] = g_c * o_c + g_s * o_s + g_w * o_w


def _sel_win_sample(page_table, idx, cache_sel3, cache_win3, q3, ks3, kw3, gl3, o_c, past):
    bsz = page_table.shape[0]
    w_len = cache_win3.shape[1]
    grid_spec = pltpu.PrefetchScalarGridSpec(
        num_scalar_prefetch=2,
        grid=(bsz,),
        in_specs=[pl.BlockSpec(memory_space=pl.ANY),
                  pl.BlockSpec((1, w_len, 256), lambda i, pt, ix: (i, 0, 0)),
                  pl.BlockSpec((1, 1, Q_A), lambda i, pt, ix: (i, 0, 0)),
                  pl.BlockSpec((1, 1, 256), lambda i, pt, ix: (i, 0, 0)),
                  pl.BlockSpec((1, 1, 256), lambda i, pt, ix: (i, 0, 0)),
                  pl.BlockSpec((1, 1, 256), lambda i, pt, ix: (i, 0, 0)),
                  pl.BlockSpec((1, KV_HEADS, 8, 128), lambda i, pt, ix: (i, 0, 0, 0))],
        out_specs=pl.BlockSpec((1, KV_HEADS, 8, 256), lambda i, pt, ix: (i, 0, 0, 0)),
        scratch_shapes=[pltpu.VMEM((KV_HEADS * N_SELECT * SEL_BLOCK, 256), F32),
                        pltpu.SemaphoreType.DMA(())],
    )
    return pl.pallas_call(
        functools.partial(_sel_win_sample_kernel, past=past),
        grid_spec=grid_spec,
        out_shape=jax.ShapeDtypeStruct((bsz, KV_HEADS, 8, 256), F32),
        compiler_params=_cparams(("arbitrary",)),
        name="sel_win_sample",
    )(page_table, idx, cache_sel3, cache_win3, q3, ks3, kw3, gl3, o_c)


def _hgrn_sample_kernel(q_ref, f_ref, v_ref, g_ref, lb_ref, ng_ref, s_ref, o_ref, s_out_ref):
    z = lb_ref[...]
    e = jnp.exp(z - jnp.max(z, axis=0, keepdims=True))
    lb = e[0] / jnp.sum(e, axis=0)
    f = lb + (1.0 - lb) * jax.nn.sigmoid(f_ref[0])
    decay = jnp.exp(jnp.log(f))
    k = 1.0 - f
    q = q_ref[0]
    v = v_ref[0]
    gate = g_ref[0]
    ng = ng_ref[...]
    for hh in range(B_HEADS):
        vs = slice(hh * B_DV, (hh + 1) * B_DV)
        s_new = decay[:, hh:hh + 1] * s_ref[0, hh] + k[:, hh:hh + 1] * v[:, vs]
        s_out_ref[0, hh] = s_new
        o = jnp.sum(q[:, hh:hh + 1] * s_new, axis=0, keepdims=True)
        o = o * lax.rsqrt(jnp.mean(o * o, axis=-1, keepdims=True) + LN_EPS) * ng
        gt = gate[:, vs]
        o_ref[0, :, vs] = o * (gt * jax.nn.sigmoid(gt))


def _hgrn_sample(q_t, f_t, v3, g3, lb_t, norm_g, state):
    bsz = q_t.shape[0]
    return pl.pallas_call(
        _hgrn_sample_kernel,
        grid=(bsz,),
        in_specs=[pl.BlockSpec((1, B_DK, B_HEADS), lambda i: (i, 0, 0)),
                  pl.BlockSpec((1, B_DK, B_HEADS), lambda i: (i, 0, 0)),
                  pl.BlockSpec((1, 1, I_B), lambda i: (i, 0, 0)),
                  pl.BlockSpec((1, 1, I_B), lambda i: (i, 0, 0)),
                  pl.BlockSpec((2, B_DK, B_HEADS), lambda i: (0, 0, 0)),
                  pl.BlockSpec((1, B_DV), lambda i: (0, 0)),
                  pl.BlockSpec((1, B_HEADS, B_DK, B_DV), lambda i: (i, 0, 0, 0))],
        out_specs=[pl.BlockSpec((1, 1, I_B), lambda i: (i, 0, 0)),
                   pl.BlockSpec((1, B_HEADS, B_DK, B_DV), lambda i: (i, 0, 0, 0))],
        out_shape=[jax.ShapeDtypeStruct((bsz, 1, I_B), F32),
                   jax.ShapeDtypeStruct((bsz, B_HEADS, B_DK, B_DV), F32)],
        compiler_params=_cparams(("arbitrary",)),
        name="hgrn_sample",
    )(q_t, f_t, v3, g3, lb_t, norm_g, state)


def _sample_layer(x, cache_cmp, cache_sel, cache_win, state, page_table, w_pad, cmpw, w_s, lb_logits, norm_g,
                  w_a, w_b, w_o, ln1_g, ln1_b, w_up, w_down, ln2_g, ln2_b):
    bsz, t, _ = x.shape
    assert t == 1, "the sample group decodes one token per request"
    n_pages = page_table.shape[1]
    past = n_pages * PAGE_SIZE
    n_pool = cache_cmp.shape[0]
    x2d = x.reshape(bsz, D_MODEL)
    h = _project(x2d, w_pad)
    h3 = h.reshape(bsz, 1, D_PAD)
    q3 = h3[:, :, C_QA:C_QA + Q_A]
    wl, wb, posf = cmpw
    o_c, idx = _cmp_sample(page_table, cache_cmp.reshape(n_pool, CH_PER_PAGE, CMP_STRIDE * 256), w_s, posf, wb,
                           q3, past)
    idx2 = jnp.pad(idx[:, :KV_HEADS, :N_POOL_SEL], ((0, 0), (0, 0), (0, 1))).reshape(bsz, KV_HEADS * N_SELECT)
    o_pad = _sel_win_sample(page_table, idx2,
                            cache_sel.reshape(n_pool * BLK_PER_PAGE, SEL_BLOCK, 256),
                            cache_win.reshape(bsz, cache_win.shape[1], 256),
                            q3, h3[:, :, C_KVS:C_KVS + 256], h3[:, :, C_KVW:C_KVW + 256],
                            h3[:, :, C_GL:C_GL + 256], o_c, past)
    o_a = jnp.concatenate([o_pad[:, g, :GROUP, 128 + g * 64:192 + g * 64].reshape(bsz, GROUP * HEAD_DIM)
                           for g in range(KV_HEADS)], axis=1)
    to_kh = lambda a: jnp.swapaxes(a.reshape(-1, B_HEADS, B_DK), 1, 2)
    o_b, s_new = _hgrn_sample(to_kh(h[:, C_QB:C_QB + Q_B]), to_kh(h[:, C_FB:C_FB + Q_B]),
                              h3[:, :, C_IB:C_IB + I_B], h3[:, :, C_GB:C_GB + I_B],
                              to_kh(lb_logits), norm_g, state)
    x1 = _merge(x2d, o_a, o_b.reshape(bsz, I_B), h, w_a, w_b, w_o, ln1_g, ln1_b)
    y = _mlp(x1, w_up, w_down, ln2_g, ln2_b)
    return y.reshape(bsz, 1, D_MODEL), h, s_new


def _cmp_sample_weights(cmp_w):
    w = cmp_w.reshape(2, 2, CMP_STRIDE, HEAD_DIM, HEAD_DIM)
    ws = jnp.zeros((CMP_STRIDE, 2, 2, HEAD_DIM, 2, 2, 2, HEAD_DIM), F32)
    for g in range(2):
        for c in range(2):
            ws = ws.at[:, c, g, :, :, g, c, :].set(jnp.transpose(w[c], (1, 2, 0, 3)))
    return ws.reshape(CMP_STRIDE * 256, 512).astype(BF16)


def kernel(x_prompt, x_sample, cache_cmp_kv, cache_sel_kv, cache_win_kv, state_hgrn, page_table,
           w_in, cmp_w, cmp_pos, hgrn_lb_logits, hgrn_norm_g, w_br_a, w_br_b, w_out,
           ln1_g, ln1_b, w_up, w_down, ln2_g, ln2_b):
    assert w_in.shape[0] == 1, "one layer"
    b, t, _ = x_prompt.shape
    bsz = x_sample.shape[0]
    assert t % HB == 0 and t // SEL_BLOCK <= 128
    perm = jnp.asarray(np.maximum(_PERM, 0))
    w_pad = jnp.where(jnp.asarray(_PERM >= 0)[None, :], jnp.take(w_in[0], perm, axis=1), 0.0).astype(BF16)
    cmpw = _cmp_weights(cmp_w[0], cmp_pos[0])
    w_s = _cmp_sample_weights(cmp_w[0])
    dense = (w_br_a[0].astype(BF16), w_br_b[0].astype(BF16), w_out[0].astype(BF16), ln1_g, ln1_b,
             w_up[0].astype(BF16), w_down[0].astype(BF16), ln2_g, ln2_b)

    y_p, h_p, s_p = _prompt_layer(x_prompt, w_pad, cmpw, hgrn_lb_logits, hgrn_norm_g, *dense)
    y_s, h_s, s_s = _sample_layer(x_sample, cache_cmp_kv[0], cache_sel_kv[0], cache_win_kv[0], state_hgrn[0],
                                  page_table, w_pad, cmpw, w_s, hgrn_lb_logits, hgrn_norm_g, *dense)

    win_p = min(WINDOW, t)
    kvw_p = _kv_out(h_p, C_KVW, b, t)
    kvw_s = _kv_out(h_s, C_KVW, bsz, 1)
    new_win_s = jnp.concatenate([cache_win_kv[0], kvw_s], axis=1)[:, -min(WINDOW, cache_win_kv.shape[2] + 1):]
    return (y_p, y_s,
            _kv_out(h_p, C_KVC, b, t)[None], _kv_out(h_p, C_KVS, b, t)[None], kvw_p[:, -win_p:][None], s_p[None],
            _kv_out(h_s, C_KVC, bsz, 1)[None], _kv_out(h_s, C_KVS, bsz, 1)[None], new_win_s[None], s_s[None])
```

```python
import functools

import numpy as np
import jax
import jax.numpy as jnp
from jax import lax
from jax.experimental import pallas as pl
from jax.experimental.pallas import tpu as pltpu

F32 = jnp.float32
BF16 = jnp.bfloat16

D_MODEL = 1024
HEAD_DIM = 64
A_HEADS = 8
KV_HEADS = 2
GROUP = A_HEADS // KV_HEADS
CMP_STRIDE = 16
CMP_LEN = 32
SEL_BLOCK = 64
N_SELECT = 16
WINDOW = 512
PAGE_SIZE = 128
B_HEADS = 8
B_DK = 128
B_DV = 64
HGRN_CHUNK = 32
D_FF = 4 * D_MODEL
DEEPNORM_ALPHA = 2.0 ** 0.25
LN_EPS = 1e-5
NEG = -1e30
FORCE = 1e6
ATTN_SCALE = HEAD_DIM ** -0.5
Q_A = A_HEADS * HEAD_DIM
KV_A = 2 * KV_HEADS * HEAD_DIM
GATE_A = 3 * A_HEADS
Q_B = B_HEADS * B_DK
I_B = B_HEADS * B_DV

C_MG = 0
C_QB = 2048
C_FB = 3072
C_QA = 4096
C_IB = 4608
C_GB = 5120
C_KVC = 5632
C_KVS = 5888
C_KVW = 6144
C_GL = 6400
D_PAD = 6656
PROJ_TN = 1664

VMEM_LIMIT = 56 * 1024 * 1024
UNSEL = float(2.0 ** 100)


def _cparams(sem):
    return pltpu.CompilerParams(dimension_semantics=sem, vmem_limit_bytes=VMEM_LIMIT)


def _proj_perm():
    perm = np.full((D_PAD,), -1, np.int64)
    o_qa, o_kvc, o_kvs, o_kvw = 0, Q_A, Q_A + KV_A, Q_A + 2 * KV_A
    o_gl = Q_A + 3 * KV_A
    o_qb = o_gl + GATE_A
    o_fb = o_qb + Q_B
    o_ib = o_fb + Q_B
    o_gb = o_ib + I_B
    o_mg = o_gb + I_B
    perm[C_MG:C_MG + 2 * D_MODEL] = o_mg + np.arange(2 * D_MODEL)
    perm[C_QB:C_QB + Q_B] = o_qb + np.arange(Q_B)
    perm[C_FB:C_FB + Q_B] = o_fb + np.arange(Q_B)
    perm[C_QA:C_QA + Q_A] = o_qa + np.arange(Q_A)
    perm[C_IB:C_IB + I_B] = o_ib + np.arange(I_B)
    perm[C_GB:C_GB + I_B] = o_gb + np.arange(I_B)
    for new, old in ((C_KVC, o_kvc), (C_KVS, o_kvs), (C_KVW, o_kvw)):
        for g in range(KV_HEADS):
            for c in range(2):
                dst = new + g * 128 + c * 64
                src = old + c * 128 + g * 64
                perm[dst:dst + 64] = src + np.arange(64)
    for g in range(KV_HEADS):
        perm[C_GL + g * 128:C_GL + g * 128 + 12] = o_gl + g * 12 + np.arange(12)
    return perm


_PERM = _proj_perm()


def _proj_kernel(x_ref, w_ref, o_ref):
    o_ref[...] = jnp.dot(x_ref[...].astype(BF16), w_ref[...], preferred_element_type=F32)


def _project(x2d, w_pad):
    n = x2d.shape[0]
    tm = min(512, n)
    return pl.pallas_call(
        _proj_kernel,
        grid=(D_PAD // PROJ_TN, n // tm),
        in_specs=[pl.BlockSpec((tm, D_MODEL), lambda j, i: (i, 0)),
                  pl.BlockSpec((D_MODEL, PROJ_TN), lambda j, i: (0, j))],
        out_specs=pl.BlockSpec((tm, PROJ_TN), lambda j, i: (i, j)),
        out_shape=jax.ShapeDtypeStruct((n, D_PAD), F32),
        compiler_params=_cparams(("arbitrary", "arbitrary")),
        name="proj",
    )(x2d, w_pad)


def _layer_norm(v, g, b):
    mu = jnp.mean(v, axis=-1, keepdims=True)
    d = v - mu
    var = jnp.mean(d * d, axis=-1, keepdims=True)
    return d * lax.rsqrt(var + LN_EPS) * g + b


def _merge_kernel(x_ref, oa_ref, ob_ref, mga_ref, mgb_ref, wa_ref, wb_ref, wo_ref, g_ref, b_ref, o_ref):
    br_a = jnp.dot(oa_ref[...].astype(BF16), wa_ref[...], preferred_element_type=F32)
    br_b = jnp.dot(ob_ref[...].astype(BF16), wb_ref[...], preferred_element_type=F32)
    merged = jax.nn.sigmoid(mga_ref[...]) * br_a + jax.nn.sigmoid(mgb_ref[...]) * br_b
    mix = jnp.dot(merged.astype(BF16), wo_ref[...], preferred_element_type=F32)
    o_ref[...] = _layer_norm(DEEPNORM_ALPHA * x_ref[...] + mix, g_ref[...], b_ref[...])


def _merge(x2d, o_a, o_b, h, w_a, w_b, w_o, ln_g, ln_b):
    n = x2d.shape[0]
    tm = min(256, n)
    const = lambda i: (0, 0)
    return pl.pallas_call(
        _merge_kernel,
        grid=(n // tm,),
        in_specs=[pl.BlockSpec((tm, D_MODEL), lambda i: (i, 0)),
                  pl.BlockSpec((tm, Q_A), lambda i: (i, 0)),
                  pl.BlockSpec((tm, I_B), lambda i: (i, 0)),
                  pl.BlockSpec((tm, D_MODEL), lambda i: (i, C_MG // D_MODEL)),
                  pl.BlockSpec((tm, D_MODEL), lambda i: (i, C_MG // D_MODEL + 1)),
                  pl.BlockSpec((Q_A, D_MODEL), const),
                  pl.BlockSpec((I_B, D_MODEL), const),
                  pl.BlockSpec((D_MODEL, D_MODEL), const),
                  pl.BlockSpec((1, D_MODEL), const),
                  pl.BlockSpec((1, D_MODEL), const)],
        out_specs=pl.BlockSpec((tm, D_MODEL), lambda i: (i, 0)),
        out_shape=jax.ShapeDtypeStruct((n, D_MODEL), F32),
        compiler_params=_cparams(("arbitrary",)),
        name="merge_ln1",
    )(x2d, o_a, o_b, h, h, w_a, w_b, w_o, ln_g, ln_b)


def _mlp_kernel(x_ref, wu_ref, wd_ref, g_ref, b_ref, o_ref, xb_ref, acc_ref):
    j = pl.program_id(1)

    @pl.when(j == 0)
    def _():
        xb_ref[...] = x_ref[...].astype(BF16)
        acc_ref[...] = jnp.zeros_like(acc_ref)

    u = jnp.dot(xb_ref[...], wu_ref[...], preferred_element_type=F32)
    u = jnp.maximum(u, 0.0)
    acc_ref[...] += jnp.dot((u * u).astype(BF16), wd_ref[...], preferred_element_type=F32)

    @pl.when(j == pl.num_programs(1) - 1)
    def _():
        o_ref[...] = _layer_norm(DEEPNORM_ALPHA * x_ref[...] + acc_ref[...], g_ref[...], b_ref[...])


def _mlp(x1, w_up, w_down, ln_g, ln_b):
    n = x1.shape[0]
    tm = min(1024, n)
    tf = 1024
    return pl.pallas_call(
        _mlp_kernel,
        grid=(n // tm, D_FF // tf),
        in_specs=[pl.BlockSpec((tm, D_MODEL), lambda i, j: (i, 0)),
                  pl.BlockSpec((D_MODEL, tf), lambda i, j: (0, j)),
                  pl.BlockSpec((tf, D_MODEL), lambda i, j: (j, 0)),
                  pl.BlockSpec((1, D_MODEL), lambda i, j: (0, 0)),
                  pl.BlockSpec((1, D_MODEL), lambda i, j: (0, 0))],
        out_specs=pl.BlockSpec((tm, D_MODEL), lambda i, j: (i, 0)),
        out_shape=jax.ShapeDtypeStruct((n, D_MODEL), F32),
        scratch_shapes=[pltpu.VMEM((tm, D_MODEL), BF16), pltpu.VMEM((tm, D_MODEL), F32)],
        compiler_params=_cparams(("arbitrary", "arbitrary")),
        name="mlp_ln2",
    )(x1, w_up, w_down, ln_g, ln_b)


def _dot01(a01, x):
    hi = x.astype(BF16)
    r1 = x - hi.astype(F32)
    mid = r1.astype(BF16)
    lo = (r1 - mid.astype(F32)).astype(BF16)
    out = jnp.dot(a01, hi, preferred_element_type=F32)
    out += jnp.dot(a01, mid, preferred_element_type=F32)
    out += jnp.dot(a01, lo, preferred_element_type=F32)
    return out


def _x_dot01(x, b01):
    hi = x.astype(BF16)
    r1 = x - hi.astype(F32)
    mid = r1.astype(BF16)
    lo = (r1 - mid.astype(F32)).astype(BF16)
    out = jnp.dot(hi, b01, preferred_element_type=F32)
    out += jnp.dot(mid, b01, preferred_element_type=F32)
    out += jnp.dot(lo, b01, preferred_element_type=F32)
    return out


_NT = (((1,), (1,)), ((), ()))
_TN = (((0,), (0,)), ((), ()))


def _top_k_mask(score, k):
    rows, lanes = score.shape
    lane = lax.broadcasted_iota(jnp.int32, (rows, lanes), 1)
    sel = jnp.zeros((rows, lanes), jnp.bool_)
    work = score
    for _ in range(k):
        m = jnp.max(work, axis=-1, keepdims=True)
        first = jnp.min(jnp.where(work == m, lane, lanes), axis=-1, keepdims=True)
        pick = lane == first
        sel = jnp.logical_or(sel, pick)
        work = jnp.where(pick, -jnp.inf, work)
    return sel


def _softmax_tile(s, valid, carry, kv_b):
    m, l, acc = carry
    m_new = jnp.maximum(m, jnp.max(s, axis=-1, keepdims=True))
    alpha = jnp.exp(m - m_new)
    p = jnp.exp(s - m_new)
    if valid is not None:
        p = jnp.where(valid, p, 0.0)
    l = alpha * l + jnp.sum(p, axis=-1, keepdims=True)
    acc = alpha * acc + jnp.dot(p.astype(BF16), kv_b, preferred_element_type=F32)
    return m_new, l, acc


def _cmp_prompt_kernel(kv_ref, w_ref, posf_ref, wb_ref, o_ref, *, n_ch):
    acc = jnp.zeros((n_ch, 256), F32)
    for l in range(CMP_STRIDE):
        xl = kv_ref[pl.ds(l, n_ch, stride=CMP_STRIDE), :].astype(BF16)
        acc += jnp.dot(xl, w_ref[l], preferred_element_type=F32)
    bias = jnp.dot(posf_ref[...].astype(BF16), wb_ref[...], preferred_element_type=F32)[0:1]
    nxt = pltpu.roll(acc[:, 128:], n_ch - 1, 0)
    o_ref[0] = (acc[:, :128] + nxt + bias).astype(BF16)


def _cmp_weights(cmp_w, cmp_pos):
    w = cmp_w.reshape(2, 2, CMP_STRIDE, HEAD_DIM, HEAD_DIM)
    wl = jnp.zeros((CMP_STRIDE, 2, HEAD_DIM, 2, 2, HEAD_DIM), F32)
    wb = jnp.zeros((2, CMP_LEN * HEAD_DIM, 2, HEAD_DIM), F32)
    for c in range(2):
        wl = wl.at[:, c, :, :, c, :].set(jnp.transpose(w[c], (1, 2, 0, 3)))
        wb = wb.at[c, :, c, :].set(cmp_w[c].reshape(CMP_LEN * HEAD_DIM, HEAD_DIM))
    wl = wl.reshape(CMP_STRIDE, 128, 256).astype(BF16)
    wb = wb.reshape(2 * CMP_LEN * HEAD_DIM, 128).astype(BF16)
    posf = jnp.broadcast_to(cmp_pos.reshape(1, 2 * CMP_LEN * HEAD_DIM), (8, 2 * CMP_LEN * HEAD_DIM))
    return wl, wb, posf


def _cmp_prompt(h, b, t, wl, wb, posf):
    n_ch = t // CMP_STRIDE
    return pl.pallas_call(
        functools.partial(_cmp_prompt_kernel, n_ch=n_ch),
        grid=(b, KV_HEADS),
        in_specs=[pl.BlockSpec((t, 128), lambda i, g: (i, C_KVC // 128 + g)),
                  pl.BlockSpec((CMP_STRIDE, 128, 256), lambda i, g: (0, 0, 0)),
                  pl.BlockSpec((8, 4096), lambda i, g: (0, 0)),
                  pl.BlockSpec((4096, 128), lambda i, g: (0, 0))],
        out_specs=pl.BlockSpec((1, n_ch, 128), lambda i, g: (i, 0, g)),
        out_shape=jax.ShapeDtypeStruct((b, n_ch, 256), BF16),
        compiler_params=_cparams(("arbitrary", "arbitrary")),
        name="cmp_prompt",
    )(h, wl, posf, wb)


QB = 128
KT = 256
WT = 128


def _nsa_prompt_kernel(q_ref, gl_ref, kc_ref, ks_ref, kw_ref, o_ref, *, t, n_ch):
    g = pl.program_id(1)
    c = pl.program_id(2)
    t0 = c * QB
    rows = GROUP * QB
    q = q_ref[...]
    qs = jnp.concatenate([q[:, r * 64:(r + 1) * 64] for r in range(GROUP)], axis=0) * ATTN_SCALE
    qpad = jnp.concatenate([qs, jnp.zeros((rows, 64), F32)], axis=1).astype(BF16)
    pos_q = t0 + lax.broadcasted_iota(jnp.int32, (QB, 1), 0)
    slopes = [jnp.where(g == 0, 2.0 ** -(r + 1), 2.0 ** -(r + 1 + GROUP)).astype(F32) for r in range(GROUP)]

    kc = kc_ref[0]
    s = lax.dot_general(qpad, kc, _NT, preferred_element_type=F32)
    n_ix = lax.broadcasted_iota(jnp.int32, (1, n_ch), 1)
    dist_i = pos_q - (n_ix * CMP_STRIDE + (CMP_LEN - 1))
    valid = jnp.logical_and(dist_i >= 0, n_ix < n_ch - 1)
    dist = dist_i.astype(F32)
    p_rows = []
    imp = jnp.zeros((QB, n_ch), F32)
    for r in range(GROUP):
        sr = jnp.where(valid, s[r * QB:(r + 1) * QB] - slopes[r] * dist, NEG)
        m = jnp.max(sr, axis=-1, keepdims=True)
        p = jnp.where(valid, jnp.exp(sr - m), 0.0)
        p = p / jnp.maximum(jnp.sum(p, axis=-1, keepdims=True), 1e-30)
        p_rows.append(p)
        imp = imp + p
    pc = jnp.concatenate(p_rows, axis=0)
    o_c = jnp.dot(pc.astype(BF16), kc, preferred_element_type=F32)

    ratio = SEL_BLOCK // CMP_STRIDE
    pool = (lax.broadcasted_iota(jnp.int32, (n_ch, 128), 0) // ratio
            == lax.broadcasted_iota(jnp.int32, (n_ch, 128), 1)).astype(BF16)
    imp_blk = _x_dot01(imp, pool)
    blk = lax.broadcasted_iota(jnp.int32, (1, 128), 1)
    cur = pos_q // SEL_BLOCK
    forced = jnp.logical_or(jnp.logical_or(blk == 0, blk == cur), blk == cur - 1)
    allowed = blk * SEL_BLOCK <= pos_q
    score = jnp.where(allowed, jnp.where(forced, FORCE, imp_blk), -FORCE)
    sel = jnp.logical_and(_top_k_mask(score, N_SELECT), allowed)
    unsel = jnp.where(sel, 0.0, -UNSEL).astype(BF16)
    q_aug = jnp.concatenate([qpad, jnp.concatenate([unsel] * GROUP, axis=0)], axis=1)

    def attend(s, dist_i, valid, carry, kv_b):
        dist = dist_i.astype(F32)
        parts = [jnp.where(valid, s[r * QB:(r + 1) * QB] - slopes[r] * dist, NEG) for r in range(GROUP)]
        return _softmax_tile(jnp.concatenate(parts, axis=0), None, carry, kv_b)

    init = (jnp.full((rows, 1), NEG, F32), jnp.zeros((rows, 1), F32), jnp.zeros((rows, 128), F32))

    def sel_body(j, carry):
        k0 = pl.multiple_of(j * KT, KT)
        kv_b = ks_ref[pl.ds(k0, KT), :].astype(BF16)
        key = k0 + lax.broadcasted_iota(jnp.int32, (KT, 128), 0)
        onehot = (key // SEL_BLOCK == lax.broadcasted_iota(jnp.int32, (KT, 128), 1)).astype(BF16)
        s = lax.dot_general(q_aug, jnp.concatenate([kv_b, onehot], axis=1), _NT, preferred_element_type=F32)
        dist_i = pos_q - (k0 + lax.broadcasted_iota(jnp.int32, (1, KT), 1))
        return attend(s, dist_i, dist_i >= 0, carry, kv_b)

    n_sel_tiles = (t0 + QB + KT - 1) // KT
    _, l_s, acc_s = lax.fori_loop(0, n_sel_tiles, sel_body, init)
    o_s = acc_s / l_s

    n_wt = WINDOW // WT + 1

    def win_body(i, carry):
        k0 = pl.multiple_of(t0 - i * WT, WT)
        kv_b = kw_ref[pl.ds(k0, WT), :].astype(BF16)
        s = lax.dot_general(qpad, kv_b, _NT, preferred_element_type=F32)
        dist_i = pos_q - (k0 + lax.broadcasted_iota(jnp.int32, (1, WT), 1))
        valid = jnp.logical_and(dist_i >= 0, dist_i < WINDOW)
        return attend(s, dist_i, valid, carry, kv_b)

    _, l_w, acc_w = lax.fori_loop(0, jnp.minimum(n_wt, c + 1), win_body, init)
    o_w = acc_w / l_w

    gates = jax.nn.sigmoid(gl_ref[...])
    for r in range(GROUP):
        sl = slice(r * QB, (r + 1) * QB)
        o = (gates[:, 3 * r:3 * r + 1] * o_c[sl] + gates[:, 3 * r + 1:3 * r + 2] * o_s[sl]
             + gates[:, 3 * r + 2:3 * r + 3] * o_w[sl])
        o_ref[:, r * 64:(r + 1) * 64] = o[:, 64:128]


def _nsa_prompt(h, kcmp, b, t):
    n_ch = t // CMP_STRIDE
    nq = t // QB
    return pl.pallas_call(
        functools.partial(_nsa_prompt_kernel, t=t, n_ch=n_ch),
        grid=(b, KV_HEADS, nq),
        in_specs=[pl.BlockSpec((QB, 256), lambda i, g, c: (i * nq + c, C_QA // 256 + g)),
                  pl.BlockSpec((QB, 128), lambda i, g, c: (i * nq + c, C_GL // 128 + g)),
                  pl.BlockSpec((1, n_ch, 128), lambda i, g, c: (i, 0, g)),
                  pl.BlockSpec((t, 128), lambda i, g, c: (i, C_KVS // 128 + g)),
                  pl.BlockSpec((t, 128), lambda i, g, c: (i, C_KVW // 128 + g))],
        out_specs=pl.BlockSpec((QB, 256), lambda i, g, c: (i * nq + c, g)),
        out_shape=jax.ShapeDtypeStruct((b * t, Q_A), F32),
        compiler_params=_cparams(("arbitrary", "arbitrary", "arbitrary")),
        name="nsa_prompt",
    )(h, h, kcmp, h, h)


HB = 256


def _lower_bound(lb_ref):
    z = lb_ref[...]
    e = jnp.exp(z - jnp.max(z, axis=0, keepdims=True))
    return e[0:1] / jnp.sum(e, axis=0, keepdims=True)


def _hgrn_prompt_kernel(q_ref, f_ref, i_ref, g_ref, lb_ref, ng_ref, o_ref, s_out_ref, s_ref):
    tb = pl.program_id(1)

    @pl.when(tb == 0)
    def _():
        s_ref[...] = jnp.zeros_like(s_ref)

    n_c = HB // HGRN_CHUNK
    lb = _lower_bound(lb_ref)
    f = lb + (1.0 - lb) * jax.nn.sigmoid(f_ref[...])
    log_f = jnp.log(f)
    row = lax.broadcasted_iota(jnp.int32, (HB, HB), 0)
    col = lax.broadcasted_iota(jnp.int32, (HB, HB), 1)
    tril = jnp.logical_and(col <= row, row // HGRN_CHUNK == col // HGRN_CHUNK)
    lc = _dot01(tril.astype(BF16), log_f)
    lc3 = lc.reshape(n_c, HGRN_CHUNK, Q_B)
    lend = jnp.broadcast_to(lc3[:, HGRN_CHUNK - 1:HGRN_CHUNK, :], lc3.shape).reshape(HB, Q_B)
    q_t = (q_ref[...] * jnp.exp(lc)).astype(BF16)
    k = 1.0 - f
    k_t = (k * jnp.exp(-lc)).astype(BF16)
    k_e = (k * jnp.exp(lend - lc)).astype(BF16)
    dec = jnp.exp(lend)
    v_all = i_ref[...]
    gate = g_ref[...]
    ng = ng_ref[...]
    for hh in range(B_HEADS):
        ks = slice(hh * B_DK, (hh + 1) * B_DK)
        vs = slice(hh * B_DV, (hh + 1) * B_DV)
        v = v_all[:, vs].astype(BF16)
        a = lax.dot_general(q_t[:, ks], k_t[:, ks], _NT, preferred_element_type=F32)
        a = jnp.where(tril, a, 0.0).astype(BF16)
        o = jnp.dot(a, v, preferred_element_type=F32)
        dec_t = jnp.transpose(dec[:, ks])
        st = s_ref[hh]
        inter = []
        for cc in range(n_c):
            rs = slice(cc * HGRN_CHUNK, (cc + 1) * HGRN_CHUNK)
            inter.append(jnp.dot(q_t[rs, ks], st.astype(BF16), preferred_element_type=F32))
            u = lax.dot_general(k_e[rs, ks], v[rs], _TN, preferred_element_type=F32)
            last = (cc + 1) * HGRN_CHUNK - 1
            st = dec_t[:, last:last + 1] * st + u
        s_ref[hh] = st
        o = o + jnp.concatenate(inter, axis=0)
        o = o * lax.rsqrt(jnp.mean(o * o, axis=-1, keepdims=True) + LN_EPS) * ng
        gt = gate[:, vs]
        o_ref[:, vs] = o * (gt * jax.nn.sigmoid(gt))

    @pl.when(tb == pl.num_programs(1) - 1)
    def _():
        s_out_ref[0] = s_ref[...]


def _hgrn_prompt(h, b, t, lb_logits, norm_g):
    nt = t // HB
    return pl.pallas_call(
        _hgrn_prompt_kernel,
        grid=(b, nt),
        in_specs=[pl.BlockSpec((HB, Q_B), lambda i, j: (i * nt + j, C_QB // Q_B)),
                  pl.BlockSpec((HB, Q_B), lambda i, j: (i * nt + j, C_FB // Q_B)),
                  pl.BlockSpec((HB, I_B), lambda i, j: (i * nt + j, C_IB // I_B)),
                  pl.BlockSpec((HB, I_B), lambda i, j: (i * nt + j, C_GB // I_B)),
                  pl.BlockSpec((2, Q_B), lambda i, j: (0, 0)),
                  pl.BlockSpec((1, B_DV), lambda i, j: (0, 0))],
        out_specs=[pl.BlockSpec((HB, I_B), lambda i, j: (i * nt + j, 0)),
                   pl.BlockSpec((1, B_HEADS, B_DK, B_DV), lambda i, j: (i, 0, 0, 0))],
        out_shape=[jax.ShapeDtypeStruct((b * t, I_B), F32),
                   jax.ShapeDtypeStruct((b, B_HEADS, B_DK, B_DV), F32)],
        scratch_shapes=[pltpu.VMEM((B_HEADS, B_DK, B_DV), F32)],
        compiler_params=_cparams(("arbitrary", "arbitrary")),
        name="hgrn_prompt",
    )(h, h, h, h, lb_logits, norm_g)


def _prompt_layer(x, w_pad, cmpw, lb_logits, norm_g, w_a, w_b, w_o, ln1_g, ln1_b, w_up, w_down, ln2_g, ln2_b):
    b, t, _ = x.shape
    x2d = x.reshape(b * t, D_MODEL)
    h = _project(x2d, w_pad)
    kcmp = _cmp_prompt(h, b, t, *cmpw)
    o_a = _nsa_prompt(h, kcmp, b, t)
    o_b, s_end = _hgrn_prompt(h, b, t, lb_logits, norm_g)
    x1 = _merge(x2d, o_a, o_b, h, w_a, w_b, w_o, ln1_g, ln1_b)
    y = _mlp(x1, w_up, w_down, ln2_g, ln2_b)
    return y.reshape(b, t, D_MODEL), h, s_end


def _kv_out(h, col, b, t):
    kv = h[:, col:col + 256].reshape(b, t, KV_HEADS, 2, HEAD_DIM)
    return jnp.swapaxes(kv, 2, 3)


CMP_ROWS = 256
CH_PER_PAGE = PAGE_SIZE // CMP_STRIDE
BLK_PER_PAGE = PAGE_SIZE // SEL_BLOCK
N_POOL_SEL = N_SELECT - 1


def _head_rows(q_row, g):
    rows = [q_row[:, g * 256 + r * 64:g * 256 + (r + 1) * 64] for r in range(GROUP)]
    return jnp.concatenate(rows + [jnp.zeros((8 - GROUP, 64), F32)], axis=0) * ATTN_SCALE


def _slope_col(g):
    row = lax.broadcasted_iota(jnp.int32, (8, 1), 0)
    col = jnp.zeros((8, 1), F32)
    for r in range(GROUP):
        col = jnp.where(row == r, 2.0 ** -(g * GROUP + r + 1), col)
    return col


def _cmp_sample_kernel(pt_ref, cache_ref, w_ref, posf_ref, wb_ref, q_ref, oc_ref, idx_ref,
                       buf_ref, xk_ref, xv_ref, sem, *, n_pages, past):
    b = pl.program_id(0)
    n_ch = n_pages * CH_PER_PAGE
    n_blk = past // SEL_BLOCK
    lanes = idx_ref.shape[-1]

    def page_copy(pg, page):
        return pltpu.make_async_copy(cache_ref.at[page], buf_ref.at[pg], sem.at[pg])

    def issue(pg, carry):
        page_copy(pg, pt_ref[b, pg]).start()
        return carry

    def land(pg, carry):
        page_copy(pg, 0).wait()
        r0 = pl.multiple_of(pg * PAGE_SIZE, PAGE_SIZE)
        xk_ref[pl.ds(r0, PAGE_SIZE), :] = jnp.transpose(buf_ref[pg, 0].reshape(128, PAGE_SIZE))
        xv_ref[pl.ds(r0, PAGE_SIZE), :] = jnp.transpose(buf_ref[pg, 1].reshape(128, PAGE_SIZE))
        return carry

    lax.fori_loop(0, n_pages, issue, 0)
    lax.fori_loop(0, n_pages, land, 0)

    bias = jnp.dot(posf_ref[...].astype(BF16), wb_ref[...], preferred_element_type=F32)[0:1]
    blocks = []
    for c, x_ref in enumerate((xk_ref, xv_ref)):
        acc = jnp.zeros((n_ch, 256), F32)
        for l in range(CMP_STRIDE):
            xl = x_ref[pl.ds(l, n_ch, stride=CMP_STRIDE), :].astype(BF16)
            acc += jnp.dot(xl, w_ref[c, l], preferred_element_type=F32)
        bias_c = jnp.concatenate([bias[:, c * 64:(c + 1) * 64]] * KV_HEADS, axis=1)
        blocks.append((acc[:, :128] + pltpu.roll(acc[:, 128:], n_ch - 1, 0) + bias_c).astype(BF16))
    kc_all, vc_all = blocks

    q_row = q_ref[0]
    n_ix = lax.broadcasted_iota(jnp.int32, (1, n_ch), 1)
    dist_i = past - (n_ix * CMP_STRIDE + (CMP_LEN - 1))
    valid = jnp.logical_and(dist_i >= 0, n_ix < n_ch - 1)
    dist = dist_i.astype(F32)
    ratio = SEL_BLOCK // CMP_STRIDE
    pool = (lax.broadcasted_iota(jnp.int32, (n_ch, lanes), 0) // ratio
            == lax.broadcasted_iota(jnp.int32, (n_ch, lanes), 1)).astype(BF16)
    row8 = lax.broadcasted_iota(jnp.int32, (8, lanes), 0)
    score = jnp.full((8, lanes), -FORCE, F32)
    lane_g = lax.broadcasted_iota(jnp.int32, (8, 128), 1) // HEAD_DIM
    o_c = jnp.zeros((8, 128), F32)
    for g in range(KV_HEADS):
        pieces = [jnp.zeros((8, 64), F32)] * KV_HEADS
        pieces[g] = _head_rows(q_row, g)
        qpad = jnp.concatenate(pieces, axis=1).astype(BF16)
        s = lax.dot_general(qpad, kc_all, _NT, preferred_element_type=F32)
        s = jnp.where(valid, s - _slope_col(g) * dist, NEG)
        m = jnp.max(s, axis=-1, keepdims=True)
        p = jnp.where(valid, jnp.exp(s - m), 0.0)
        p = p / jnp.maximum(jnp.sum(p, axis=-1, keepdims=True), 1e-30)
        o_c = jnp.where(lane_g == g, jnp.dot(p.astype(BF16), vc_all, preferred_element_type=F32), o_c)
        imp = p[0:1] + p[1:2] + p[2:3] + p[3:4]
        imp_blk = _x_dot01(jnp.broadcast_to(imp, (8, n_ch)), pool)
        score = jnp.where(row8 == g, imp_blk, score)
    lane = lax.broadcasted_iota(jnp.int32, (8, lanes), 1)
    forced = jnp.logical_or(lane == 0, lane == n_blk - 1)
    score = jnp.where(lane < n_blk, jnp.where(forced, FORCE, score), -jnp.inf)
    out = jnp.zeros((8, lanes), jnp.int32)
    for r in range(N_POOL_SEL):
        m = jnp.max(score, axis=-1, keepdims=True)
        first = jnp.min(jnp.where(score == m, lane, lanes), axis=-1, keepdims=True)
        out = jnp.where(lane == r, first, out)
        score = jnp.where(lane == first, -jnp.inf, score)
    idx_ref[0] = out
    oc_ref[0] = o_c


def _cmp_sample(page_table, cache5, w_s, posf, wb, q3, past):
    bsz, n_pages = page_table.shape
    n_ch = n_pages * CH_PER_PAGE
    lanes = -(-(past // SEL_BLOCK) // 128) * 128
    grid_spec = pltpu.PrefetchScalarGridSpec(
        num_scalar_prefetch=1,
        grid=(bsz,),
        in_specs=[pl.BlockSpec(memory_space=pl.ANY),
                  pl.BlockSpec((2, CMP_STRIDE, 128, 256), lambda i, pt: (0, 0, 0, 0)),
                  pl.BlockSpec((8, 4096), lambda i, pt: (0, 0)),
                  pl.BlockSpec((4096, 128), lambda i, pt: (0, 0)),
                  pl.BlockSpec((1, 1, Q_A), lambda i, pt: (i, 0, 0))],
        out_specs=[pl.BlockSpec((1, 8, 128), lambda i, pt: (i, 0, 0)),
                   pl.BlockSpec((1, 8, lanes), lambda i, pt: (i, 0, 0))],
        scratch_shapes=[pltpu.VMEM((n_pages, 2, KV_HEADS, HEAD_DIM, PAGE_SIZE), F32),
                        pltpu.VMEM((n_pages * PAGE_SIZE, 128), F32),
                        pltpu.VMEM((n_pages * PAGE_SIZE, 128), F32),
                        pltpu.SemaphoreType.DMA((n_pages,))],
    )
    return pl.pallas_call(
        functools.partial(_cmp_sample_kernel, n_pages=n_pages, past=past),
        grid_spec=grid_spec,
        out_shape=[jax.ShapeDtypeStruct((bsz, 8, 128), F32),
                   jax.ShapeDtypeStruct((bsz, 8, lanes), jnp.int32)],
        compiler_params=_cparams(("arbitrary",)),
        name="cmp_sample",
    )(page_table, cache5, w_s, posf, wb, q3)


def _pick_lane(mat, lane, target):
    return jnp.sum(jnp.where(lane == target, mat, 0.0), axis=-1, keepdims=True)


def _sel_win_sample_kernel(pt_ref, idx_ref, cache_ref, win_ref, q_ref, ks_ref, kw_ref, gl_ref, oc_ref, o_ref,
                           kbuf_ref, vbuf_ref, sem, *, past):
    b = pl.program_id(0)
    slot_lanes = N_SELECT * PAGE_SIZE

    def page_copies(g, k, page):
        dst = pl.ds(k * PAGE_SIZE, PAGE_SIZE)
        return (pltpu.make_async_copy(cache_ref.at[page, 0, g], kbuf_ref.at[g, :, dst], sem),
                pltpu.make_async_copy(cache_ref.at[page, 1, g], vbuf_ref.at[g, :, dst], sem))

    for g in range(KV_HEADS):
        for k in range(N_POOL_SEL):
            page = pt_ref[b, idx_ref[b, g * N_SELECT + k] // BLK_PER_PAGE]
            for cp in page_copies(g, k, page):
                cp.start()
        pad = pl.ds(N_POOL_SEL * PAGE_SIZE, PAGE_SIZE)
        kbuf_ref[g, :, pad] = jnp.zeros((HEAD_DIM, PAGE_SIZE), F32)
        vbuf_ref[g, :, pad] = jnp.zeros((HEAD_DIM, PAGE_SIZE), F32)

    q_row = q_ref[0]
    ks_new = ks_ref[0]
    kw_new = kw_ref[0]
    gl_all = jax.nn.sigmoid(gl_ref[0])
    lane128 = lax.broadcasted_iota(jnp.int32, (8, 128), 1)
    row8 = lax.broadcasted_iota(jnp.int32, (8, 1), 0)
    w_len = win_ref.shape[-1]

    def two_piece(qh, slope, k_t, v_t, dist_i, valid, k_new, v_new):
        s = jnp.dot(qh.astype(BF16), k_t.astype(BF16), preferred_element_type=F32)
        s = jnp.where(valid, s - slope * dist_i.astype(F32), NEG)
        s_n = jnp.sum(qh * k_new, axis=-1, keepdims=True)
        m = jnp.maximum(jnp.max(s, axis=-1, keepdims=True), s_n)
        p = jnp.where(valid, jnp.exp(s - m), 0.0)
        p_n = jnp.exp(s_n - m)
        l = jnp.sum(p, axis=-1, keepdims=True) + p_n
        o = lax.dot_general(p.astype(BF16), v_t.astype(BF16), _NT, preferred_element_type=F32) + p_n * v_new
        return o / l

    outs = []
    for g in range(KV_HEADS):
        qh = _head_rows(q_row, g)
        slope = _slope_col(g)
        j_ix = lax.broadcasted_iota(jnp.int32, (1, w_len), 1)
        dist_w = w_len - j_ix
        o_w = two_piece(qh, slope, win_ref[0, 0, g], win_ref[0, 1, g], dist_w, dist_w < WINDOW,
                        kw_new[:, g * 128:g * 128 + 64], kw_new[:, g * 128 + 64:(g + 1) * 128])
        outs.append((qh, slope, o_w))

    for g in range(KV_HEADS):
        for k in range(N_POOL_SEL):
            for cp in page_copies(g, k, 0):
                cp.wait()

    lane_s = lax.broadcasted_iota(jnp.int32, (1, slot_lanes), 1)
    o_all = []
    for g in range(KV_HEADS):
        qh, slope, o_w = outs[g]
        pos_k = jnp.full((1, slot_lanes), past + 1, jnp.int32)
        for k in range(N_POOL_SEL):
            blk = idx_ref[b, g * N_SELECT + k]
            r = lane_s % PAGE_SIZE
            in_blk = jnp.logical_and(lane_s // PAGE_SIZE == k, r // SEL_BLOCK == blk % BLK_PER_PAGE)
            pos_k = jnp.where(in_blk, (blk // BLK_PER_PAGE) * PAGE_SIZE + r, pos_k)
        dist_s = past - pos_k
        o_s = two_piece(qh, slope, kbuf_ref[g], vbuf_ref[g], dist_s, dist_s >= 0,
                        ks_new[:, g * 128:g * 128 + 64], ks_new[:, g * 128 + 64:(g + 1) * 128])
        o_c = oc_ref[0][:, g * 64:(g + 1) * 64]
        gates = jnp.broadcast_to(gl_all[:, g * 128:(g + 1) * 128], (8, 128))
        g_c = _pick_lane(gates, lane128, 3 * row8)
        g_s = _pick_lane(gates, lane128, 3 * row8 + 1)
        g_w = _pick_lane(gates, lane128, 3 * row8 + 2)
        o_all.append(g_c * o_c + g_s * o_s + g_w * o_w)
    o_ref[0] = jnp.concatenate(o_all, axis=1)


def _sel_win_sample(page_table, idx, cache_sel5, cache_win5, q3, ks3, kw3, gl3, o_c, past):
    bsz = page_table.shape[0]
    w_len = cache_win5.shape[-1]
    grid_spec = pltpu.PrefetchScalarGridSpec(
        num_scalar_prefetch=2,
        grid=(bsz,),
        in_specs=[pl.BlockSpec(memory_space=pl.ANY),
                  pl.BlockSpec((1, 2, KV_HEADS, HEAD_DIM, w_len), lambda i, pt, ix: (i, 0, 0, 0, 0)),
                  pl.BlockSpec((1, 1, Q_A), lambda i, pt, ix: (i, 0, 0)),
                  pl.BlockSpec((1, 1, 256), lambda i, pt, ix: (i, 0, 0)),
                  pl.BlockSpec((1, 1, 256), lambda i, pt, ix: (i, 0, 0)),
                  pl.BlockSpec((1, 1, 256), lambda i, pt, ix: (i, 0, 0)),
                  pl.BlockSpec((1, 8, 128), lambda i, pt, ix: (i, 0, 0))],
        out_specs=pl.BlockSpec((1, 8, 128), lambda i, pt, ix: (i, 0, 0)),
        scratch_shapes=[pltpu.VMEM((KV_HEADS, HEAD_DIM, N_SELECT * PAGE_SIZE), F32),
                        pltpu.VMEM((KV_HEADS, HEAD_DIM, N_SELECT * PAGE_SIZE), F32),
                        pltpu.SemaphoreType.DMA(())],
    )
    return pl.pallas_call(
        functools.partial(_sel_win_sample_kernel, past=past),
        grid_spec=grid_spec,
        out_shape=jax.ShapeDtypeStruct((bsz, 8, 128), F32),
        compiler_params=_cparams(("arbitrary",)),
        name="sel_win_sample",
    )(page_table, idx, cache_sel5, cache_win5, q3, ks3, kw3, gl3, o_c)


def _hgrn_sample_kernel(q_ref, f_ref, v_ref, g_ref, lb_ref, ng_ref, s_ref, o_ref, s_out_ref):
    z = lb_ref[...]
    e = jnp.exp(z - jnp.max(z, axis=0, keepdims=True))
    lb = e[0] / jnp.sum(e, axis=0)
    f = lb + (1.0 - lb) * jax.nn.sigmoid(f_ref[0])
    decay = jnp.exp(jnp.log(f))
    k = 1.0 - f
    q = q_ref[0]
    v = v_ref[0]
    gate = g_ref[0]
    ng = ng_ref[...]
    for hh in range(B_HEADS):
        hs = slice(hh, hh + 1)
        s_new = decay[hs] * s_ref[0, hh] + v[:, hs] * k[hs]
        s_out_ref[0, hh] = s_new
        o = jnp.sum(q[hs] * s_new, axis=1, keepdims=True)
        o = o * lax.rsqrt(jnp.mean(o * o, axis=0, keepdims=True) + LN_EPS) * ng
        gt = gate[:, hs]
        o_ref[0, :, hs] = o * (gt * jax.nn.sigmoid(gt))


def _hgrn_sample(q_hk, f_hk, v_vh, g_vh, lb_hk, ng_col, state_t):
    bsz = q_hk.shape[0]
    return pl.pallas_call(
        _hgrn_sample_kernel,
        grid=(bsz,),
        in_specs=[pl.BlockSpec((1, B_HEADS, B_DK), lambda i: (i, 0, 0)),
                  pl.BlockSpec((1, B_HEADS, B_DK), lambda i: (i, 0, 0)),
                  pl.BlockSpec((1, B_DV, B_HEADS), lambda i: (i, 0, 0)),
                  pl.BlockSpec((1, B_DV, B_HEADS), lambda i: (i, 0, 0)),
                  pl.BlockSpec((2, B_HEADS, B_DK), lambda i: (0, 0, 0)),
                  pl.BlockSpec((B_DV, 1), lambda i: (0, 0)),
                  pl.BlockSpec((1, B_HEADS, B_DV, B_DK), lambda i: (i, 0, 0, 0))],
        out_specs=[pl.BlockSpec((1, B_DV, B_HEADS), lambda i: (i, 0, 0)),
                   pl.BlockSpec((1, B_HEADS, B_DV, B_DK), lambda i: (i, 0, 0, 0))],
        out_shape=[jax.ShapeDtypeStruct((bsz, B_DV, B_HEADS), F32),
                   jax.ShapeDtypeStruct((bsz, B_HEADS, B_DV, B_DK), F32)],
        compiler_params=_cparams(("arbitrary",)),
        name="hgrn_sample",
    )(q_hk, f_hk, v_vh, g_vh, lb_hk, ng_col, state_t)


def _rows_last(cache):
    return jnp.moveaxis(cache, -4, -1)


def _sample_layer(x, cache_cmp, cache_sel, cache_win, state, page_table, w_pad, cmpw, w_s, lb_logits, norm_g,
                  w_a, w_b, w_o, ln1_g, ln1_b, w_up, w_down, ln2_g, ln2_b):
    bsz, t, _ = x.shape
    assert t == 1, "the sample group decodes one token per request"
    n_pages = page_table.shape[1]
    past = n_pages * PAGE_SIZE
    x2d = x.reshape(bsz, D_MODEL)
    h = _project(x2d, w_pad)
    h3 = h.reshape(bsz, 1, D_PAD)
    q3 = h3[:, :, C_QA:C_QA + Q_A]
    wl, wb, posf = cmpw
    o_c, idx = _cmp_sample(page_table, _rows_last(cache_cmp), w_s, posf, wb, q3, past)
    idx2 = jnp.pad(idx[:, :KV_HEADS, :N_POOL_SEL], ((0, 0), (0, 0), (0, 1))).reshape(bsz, KV_HEADS * N_SELECT)
    o_rd = _sel_win_sample(page_table, idx2, _rows_last(cache_sel), _rows_last(cache_win),
                           q3, h3[:, :, C_KVS:C_KVS + 256], h3[:, :, C_KVW:C_KVW + 256],
                           h3[:, :, C_GL:C_GL + 256], o_c, past)
    o_a = jnp.swapaxes(o_rd[:, :GROUP].reshape(bsz, GROUP, KV_HEADS, HEAD_DIM), 1, 2).reshape(bsz, Q_A)
    to_hk = lambda a: a.reshape(-1, B_HEADS, B_DK)
    to_vh = lambda a: jnp.swapaxes(a.reshape(-1, B_HEADS, B_DV), 1, 2)
    o_vh, s_t = _hgrn_sample(to_hk(h[:, C_QB:C_QB + Q_B]), to_hk(h[:, C_FB:C_FB + Q_B]),
                             to_vh(h[:, C_IB:C_IB + I_B]), to_vh(h[:, C_GB:C_GB + I_B]),
                             to_hk(lb_logits), norm_g.reshape(B_DV, 1), jnp.swapaxes(state, 2, 3))
    o_b = jnp.swapaxes(o_vh, 1, 2).reshape(bsz, I_B)
    x1 = _merge(x2d, o_a, o_b, h, w_a, w_b, w_o, ln1_g, ln1_b)
    y = _mlp(x1, w_up, w_down, ln2_g, ln2_b)
    return y.reshape(bsz, 1, D_MODEL), h, jnp.swapaxes(s_t, 2, 3)


def _cmp_sample_weights(cmp_w):
    w = cmp_w.reshape(2, 2, CMP_STRIDE, HEAD_DIM, HEAD_DIM)
    ws = jnp.zeros((2, CMP_STRIDE, 2, HEAD_DIM, 2, 2, HEAD_DIM), F32)
    for g in range(2):
        ws = ws.at[:, :, g, :, :, g, :].set(jnp.transpose(w, (0, 2, 3, 1, 4)))
    return ws.reshape(2, CMP_STRIDE, 128, 256).astype(BF16)


def kernel(x_prompt, x_sample, cache_cmp_kv, cache_sel_kv, cache_win_kv, state_hgrn, page_table,
           w_in, cmp_w, cmp_pos, hgrn_lb_logits, hgrn_norm_g, w_br_a, w_br_b, w_out,
           ln1_g, ln1_b, w_up, w_down, ln2_g, ln2_b):
    assert w_in.shape[0] == 1, "one layer"
    b, t, _ = x_prompt.shape
    bsz = x_sample.shape[0]
    assert t % HB == 0 and t // SEL_BLOCK <= 128
    perm = jnp.asarray(np.maximum(_PERM, 0))
    w_pad = jnp.where(jnp.asarray(_PERM >= 0)[None, :], jnp.take(w_in[0], perm, axis=1), 0.0).astype(BF16)
    cmpw = _cmp_weights(cmp_w[0], cmp_pos[0])
    w_s = _cmp_sample_weights(cmp_w[0])
    dense = (w_br_a[0].astype(BF16), w_br_b[0].astype(BF16), w_out[0].astype(BF16), ln1_g, ln1_b,
             w_up[0].astype(BF16), w_down[0].astype(BF16), ln2_g, ln2_b)

    y_p, h_p, s_p = _prompt_layer(x_prompt, w_pad, cmpw, hgrn_lb_logits, hgrn_norm_g, *dense)
    y_s, h_s, s_s = _sample_layer(x_sample, cache_cmp_kv[0], cache_sel_kv[0], cache_win_kv[0], state_hgrn[0],
                                  page_table, w_pad, cmpw, w_s, hgrn_lb_logits, hgrn_norm_g, *dense)

    win_p = min(WINDOW, t)
    kvw_p = _kv_out(h_p, C_KVW, b, t)
    kvw_s = _kv_out(h_s, C_KVW, bsz, 1)
    new_win_s = jnp.concatenate([cache_win_kv[0], kvw_s], axis=1)[:, -min(WINDOW, cache_win_kv.shape[2] + 1):]
    return (y_p, y_s,
            _kv_out(h_p, C_KVC, b, t)[None], _kv_out(h_p, C_KVS, b, t)[None], kvw_p[:, -win_p:][None], s_p[None],
            _kv_out(h_s, C_KVC, bsz, 1)[None], _kv_out(h_s, C_KVS, bsz, 1)[None], new_win_s[None], s_s[None])
```

```python
import functools

import numpy as np
import jax
import jax.numpy as jnp
from jax import lax
from jax.experimental import pallas as pl
from jax.experimental.pallas import tpu as pltpu

F32 = jnp.float32
BF16 = jnp.bfloat16

D_MODEL = 1024
HEAD_DIM = 64
A_HEADS = 8
KV_HEADS = 2
GROUP = A_HEADS // KV_HEADS
CMP_STRIDE = 16
CMP_LEN = 32
SEL_BLOCK = 64
N_SELECT = 16
WINDOW = 512
PAGE_SIZE = 128
B_HEADS = 8
B_DK = 128
B_DV = 64
HGRN_CHUNK = 32
D_FF = 4 * D_MODEL
DEEPNORM_ALPHA = 2.0 ** 0.25
LN_EPS = 1e-5
NEG = -1e30
FORCE = 1e6
ATTN_SCALE = HEAD_DIM ** -0.5
Q_A = A_HEADS * HEAD_DIM
KV_A = 2 * KV_HEADS * HEAD_DIM
GATE_A = 3 * A_HEADS
Q_B = B_HEADS * B_DK
I_B = B_HEADS * B_DV

C_MG = 0
C_QB = 2048
C_FB = 3072
C_QA = 4096
C_IB = 4608
C_GB = 5120
C_KVC = 5632
C_KVS = 5888
C_KVW = 6144
C_GL = 6400
D_PAD = 6656
PROJ_TN = 1664

VMEM_LIMIT = 56 * 1024 * 1024
UNSEL = float(2.0 ** 100)


def _cparams(sem):
    return pltpu.CompilerParams(dimension_semantics=sem, vmem_limit_bytes=VMEM_LIMIT)


def _proj_perm():
    perm = np.full((D_PAD,), -1, np.int64)
    o_qa, o_kvc, o_kvs, o_kvw = 0, Q_A, Q_A + KV_A, Q_A + 2 * KV_A
    o_gl = Q_A + 3 * KV_A
    o_qb = o_gl + GATE_A
    o_fb = o_qb + Q_B
    o_ib = o_fb + Q_B
    o_gb = o_ib + I_B
    o_mg = o_gb + I_B
    perm[C_MG:C_MG + 2 * D_MODEL] = o_mg + np.arange(2 * D_MODEL)
    perm[C_QB:C_QB + Q_B] = o_qb + np.arange(Q_B)
    perm[C_FB:C_FB + Q_B] = o_fb + np.arange(Q_B)
    perm[C_QA:C_QA + Q_A] = o_qa + np.arange(Q_A)
    perm[C_IB:C_IB + I_B] = o_ib + np.arange(I_B)
    perm[C_GB:C_GB + I_B] = o_gb + np.arange(I_B)
    for new, old in ((C_KVC, o_kvc), (C_KVS, o_kvs), (C_KVW, o_kvw)):
        for g in range(KV_HEADS):
            for c in range(2):
                dst = new + g * 128 + c * 64
                src = old + c * 128 + g * 64
                perm[dst:dst + 64] = src + np.arange(64)
    for g in range(KV_HEADS):
        perm[C_GL + g * 128:C_GL + g * 128 + 12] = o_gl + g * 12 + np.arange(12)
    return perm


_PERM = _proj_perm()


def _proj_kernel(x_ref, w_ref, o_ref):
    o_ref[...] = jnp.dot(x_ref[...].astype(BF16), w_ref[...], preferred_element_type=F32)


def _project(x2d, w_pad):
    n = x2d.shape[0]
    tm = min(512, n)
    return pl.pallas_call(
        _proj_kernel,
        grid=(D_PAD // PROJ_TN, n // tm),
        in_specs=[pl.BlockSpec((tm, D_MODEL), lambda j, i: (i, 0)),
                  pl.BlockSpec((D_MODEL, PROJ_TN), lambda j, i: (0, j))],
        out_specs=pl.BlockSpec((tm, PROJ_TN), lambda j, i: (i, j)),
        out_shape=jax.ShapeDtypeStruct((n, D_PAD), F32),
        compiler_params=_cparams(("arbitrary", "arbitrary")),
        name="proj",
    )(x2d, w_pad)


def _layer_norm(v, g, b):
    mu = jnp.mean(v, axis=-1, keepdims=True)
    d = v - mu
    var = jnp.mean(d * d, axis=-1, keepdims=True)
    return d * lax.rsqrt(var + LN_EPS) * g + b


def _merge_kernel(x_ref, oa_ref, ob_ref, mga_ref, mgb_ref, wa_ref, wb_ref, wo_ref, g_ref, b_ref, o_ref):
    br_a = jnp.dot(oa_ref[...].astype(BF16), wa_ref[...], preferred_element_type=F32)
    br_b = jnp.dot(ob_ref[...].astype(BF16), wb_ref[...], preferred_element_type=F32)
    merged = jax.nn.sigmoid(mga_ref[...]) * br_a + jax.nn.sigmoid(mgb_ref[...]) * br_b
    mix = jnp.dot(merged.astype(BF16), wo_ref[...], preferred_element_type=F32)
    o_ref[...] = _layer_norm(DEEPNORM_ALPHA * x_ref[...] + mix, g_ref[...], b_ref[...])


def _merge(x2d, o_a, o_b, h, w_a, w_b, w_o, ln_g, ln_b):
    n = x2d.shape[0]
    tm = min(256, n)
    const = lambda i: (0, 0)
    return pl.pallas_call(
        _merge_kernel,
        grid=(n // tm,),
        in_specs=[pl.BlockSpec((tm, D_MODEL), lambda i: (i, 0)),
                  pl.BlockSpec((tm, Q_A), lambda i: (i, 0)),
                  pl.BlockSpec((tm, I_B), lambda i: (i, 0)),
                  pl.BlockSpec((tm, D_MODEL), lambda i: (i, C_MG // D_MODEL)),
                  pl.BlockSpec((tm, D_MODEL), lambda i: (i, C_MG // D_MODEL + 1)),
                  pl.BlockSpec((Q_A, D_MODEL), const),
                  pl.BlockSpec((I_B, D_MODEL), const),
                  pl.BlockSpec((D_MODEL, D_MODEL), const),
                  pl.BlockSpec((1, D_MODEL), const),
                  pl.BlockSpec((1, D_MODEL), const)],
        out_specs=pl.BlockSpec((tm, D_MODEL), lambda i: (i, 0)),
        out_shape=jax.ShapeDtypeStruct((n, D_MODEL), F32),
        compiler_params=_cparams(("arbitrary",)),
        name="merge_ln1",
    )(x2d, o_a, o_b, h, h, w_a, w_b, w_o, ln_g, ln_b)


def _mlp_kernel(x_ref, wu_ref, wd_ref, g_ref, b_ref, o_ref, xb_ref, acc_ref):
    j = pl.program_id(1)

    @pl.when(j == 0)
    def _():
        xb_ref[...] = x_ref[...].astype(BF16)
        acc_ref[...] = jnp.zeros_like(acc_ref)

    u = jnp.dot(xb_ref[...], wu_ref[...], preferred_element_type=F32)
    u = jnp.maximum(u, 0.0)
    acc_ref[...] += jnp.dot((u * u).astype(BF16), wd_ref[...], preferred_element_type=F32)

    @pl.when(j == pl.num_programs(1) - 1)
    def _():
        o_ref[...] = _layer_norm(DEEPNORM_ALPHA * x_ref[...] + acc_ref[...], g_ref[...], b_ref[...])


def _mlp(x1, w_up, w_down, ln_g, ln_b):
    n = x1.shape[0]
    tm = min(1024, n)
    tf = 1024
    return pl.pallas_call(
        _mlp_kernel,
        grid=(n // tm, D_FF // tf),
        in_specs=[pl.BlockSpec((tm, D_MODEL), lambda i, j: (i, 0)),
                  pl.BlockSpec((D_MODEL, tf), lambda i, j: (0, j)),
                  pl.BlockSpec((tf, D_MODEL), lambda i, j: (j, 0)),
                  pl.BlockSpec((1, D_MODEL), lambda i, j: (0, 0)),
                  pl.BlockSpec((1, D_MODEL), lambda i, j: (0, 0))],
        out_specs=pl.BlockSpec((tm, D_MODEL), lambda i, j: (i, 0)),
        out_shape=jax.ShapeDtypeStruct((n, D_MODEL), F32),
        scratch_shapes=[pltpu.VMEM((tm, D_MODEL), BF16), pltpu.VMEM((tm, D_MODEL), F32)],
        compiler_params=_cparams(("arbitrary", "arbitrary")),
        name="mlp_ln2",
    )(x1, w_up, w_down, ln_g, ln_b)


def _dot01(a01, x):
    hi = x.astype(BF16)
    r1 = x - hi.astype(F32)
    mid = r1.astype(BF16)
    lo = (r1 - mid.astype(F32)).astype(BF16)
    out = jnp.dot(a01, hi, preferred_element_type=F32)
    out += jnp.dot(a01, mid, preferred_element_type=F32)
    out += jnp.dot(a01, lo, preferred_element_type=F32)
    return out


def _x_dot01(x, b01):
    hi = x.astype(BF16)
    r1 = x - hi.astype(F32)
    mid = r1.astype(BF16)
    lo = (r1 - mid.astype(F32)).astype(BF16)
    out = jnp.dot(hi, b01, preferred_element_type=F32)
    out += jnp.dot(mid, b01, preferred_element_type=F32)
    out += jnp.dot(lo, b01, preferred_element_type=F32)
    return out


_NT = (((1,), (1,)), ((), ()))
_TN = (((0,), (0,)), ((), ()))


def _top_k_mask(score, k):
    rows, lanes = score.shape
    lane = lax.broadcasted_iota(jnp.int32, (rows, lanes), 1)
    sel = jnp.zeros((rows, lanes), jnp.bool_)
    work = score
    for _ in range(k):
        m = jnp.max(work, axis=-1, keepdims=True)
        first = jnp.min(jnp.where(work == m, lane, lanes), axis=-1, keepdims=True)
        pick = lane == first
        sel = jnp.logical_or(sel, pick)
        work = jnp.where(pick, -jnp.inf, work)
    return sel


def _softmax_tile(s, valid, carry, kv_b):
    m, l, acc = carry
    m_new = jnp.maximum(m, jnp.max(s, axis=-1, keepdims=True))
    alpha = jnp.exp(m - m_new)
    p = jnp.exp(s - m_new)
    if valid is not None:
        p = jnp.where(valid, p, 0.0)
    l = alpha * l + jnp.sum(p, axis=-1, keepdims=True)
    acc = alpha * acc + jnp.dot(p.astype(BF16), kv_b, preferred_element_type=F32)
    return m_new, l, acc


def _cmp_prompt_kernel(kv_ref, w_ref, posf_ref, wb_ref, o_ref, *, n_ch):
    acc = jnp.zeros((n_ch, 256), F32)
    for l in range(CMP_STRIDE):
        xl = kv_ref[pl.ds(l, n_ch, stride=CMP_STRIDE), :].astype(BF16)
        acc += jnp.dot(xl, w_ref[l], preferred_element_type=F32)
    bias = jnp.dot(posf_ref[...].astype(BF16), wb_ref[...], preferred_element_type=F32)[0:1]
    nxt = pltpu.roll(acc[:, 128:], n_ch - 1, 0)
    o_ref[0] = (acc[:, :128] + nxt + bias).astype(BF16)


def _cmp_weights(cmp_w, cmp_pos):
    w = cmp_w.reshape(2, 2, CMP_STRIDE, HEAD_DIM, HEAD_DIM)
    wl = jnp.zeros((CMP_STRIDE, 2, HEAD_DIM, 2, 2, HEAD_DIM), F32)
    wb = jnp.zeros((2, CMP_LEN * HEAD_DIM, 2, HEAD_DIM), F32)
    for c in range(2):
        wl = wl.at[:, c, :, :, c, :].set(jnp.transpose(w[c], (1, 2, 0, 3)))
        wb = wb.at[c, :, c, :].set(cmp_w[c].reshape(CMP_LEN * HEAD_DIM, HEAD_DIM))
    wl = wl.reshape(CMP_STRIDE, 128, 256).astype(BF16)
    wb = wb.reshape(2 * CMP_LEN * HEAD_DIM, 128).astype(BF16)
    posf = jnp.broadcast_to(cmp_pos.reshape(1, 2 * CMP_LEN * HEAD_DIM), (8, 2 * CMP_LEN * HEAD_DIM))
    return wl, wb, posf


def _cmp_prompt(h, b, t, wl, wb, posf):
    n_ch = t // CMP_STRIDE
    return pl.pallas_call(
        functools.partial(_cmp_prompt_kernel, n_ch=n_ch),
        grid=(b, KV_HEADS),
        in_specs=[pl.BlockSpec((t, 128), lambda i, g: (i, C_KVC // 128 + g)),
                  pl.BlockSpec((CMP_STRIDE, 128, 256), lambda i, g: (0, 0, 0)),
                  pl.BlockSpec((8, 4096), lambda i, g: (0, 0)),
                  pl.BlockSpec((4096, 128), lambda i, g: (0, 0))],
        out_specs=pl.BlockSpec((1, n_ch, 128), lambda i, g: (i, 0, g)),
        out_shape=jax.ShapeDtypeStruct((b, n_ch, 256), BF16),
        compiler_params=_cparams(("arbitrary", "arbitrary")),
        name="cmp_prompt",
    )(h, wl, posf, wb)


QB = 128
KT = 256
WT = 128


ROWS = GROUP * QB
TL_COUNT = 64


def _head_major(q):
    return jnp.concatenate([q[:, r * 64:(r + 1) * 64] for r in range(GROUP)], axis=0) * ATTN_SCALE


def _head_slope(g, r):
    return jnp.where(g == 0, 2.0 ** -(r + 1), 2.0 ** -(r + 1 + GROUP)).astype(F32)


def _slope_row(g):
    r_ix = lax.broadcasted_iota(jnp.int32, (1, ROWS), 1) // QB
    row = jnp.zeros((1, ROWS), F32)
    for r in range(GROUP):
        row = jnp.where(r_ix == r, _head_slope(g, r), row)
    return row


def _nsa_select_kernel(q_ref, kc_ref, oc_ref, un_ref, tl_ref, *, n_ch):
    g = pl.program_id(1)
    c = pl.program_id(2)
    t0 = c * QB
    qpad = jnp.concatenate([_head_major(q_ref[...]), jnp.zeros((ROWS, 64), F32)], axis=1).astype(BF16)
    kc = kc_ref[0]
    s = lax.dot_general(kc, qpad, _NT, preferred_element_type=F32)
    pos_row = t0 + lax.broadcasted_iota(jnp.int32, (1, ROWS), 1) % QB
    n_col = lax.broadcasted_iota(jnp.int32, (n_ch, 1), 0)
    dist_i = pos_row - (n_col * CMP_STRIDE + (CMP_LEN - 1))
    valid = jnp.logical_and(dist_i >= 0, n_col < n_ch - 1)
    s = jnp.where(valid, s - _slope_row(g) * dist_i.astype(F32), NEG)
    m = jnp.max(s, axis=0, keepdims=True)
    p = jnp.where(valid, jnp.exp(s - m), 0.0)
    p = p * (1.0 / jnp.maximum(jnp.sum(p, axis=0, keepdims=True), 1e-30))
    o_c = lax.dot_general(p.astype(BF16), kc, _TN, preferred_element_type=F32)
    for r in range(GROUP):
        oc_ref[:, r * 64:(r + 1) * 64] = o_c[r * QB:(r + 1) * QB, 64:128]

    imp = p[:, 0:QB]
    for r in range(1, GROUP):
        imp = imp + p[:, r * QB:(r + 1) * QB]
    ratio = SEL_BLOCK // CMP_STRIDE
    pool_t = (lax.broadcasted_iota(jnp.int32, (128, n_ch), 1) // ratio
              == lax.broadcasted_iota(jnp.int32, (128, n_ch), 0)).astype(BF16)
    imp_blk = _dot01(pool_t, imp)
    blk = lax.broadcasted_iota(jnp.int32, (128, QB), 0)
    pos_q = t0 + lax.broadcasted_iota(jnp.int32, (1, QB), 1)
    cur = pos_q // SEL_BLOCK
    forced = jnp.logical_or(jnp.logical_or(blk == 0, blk == cur), blk == cur - 1)
    allowed = blk * SEL_BLOCK <= pos_q
    work = jnp.where(jnp.logical_and(allowed, jnp.logical_not(forced)), imp_blk, -jnp.inf)
    sel = forced
    for _ in range(N_SELECT - 3):
        mx = jnp.max(work, axis=0, keepdims=True)
        first = jnp.min(jnp.where(work == mx, blk, 128), axis=0, keepdims=True)
        pick = blk == first
        sel = jnp.logical_or(sel, pick)
        work = jnp.where(pick, -jnp.inf, work)
    sel = jnp.logical_and(sel, allowed)
    un_ref[...] = jnp.transpose(jnp.where(sel, 0.0, -UNSEL)).astype(BF16)

    tile_of = (lax.broadcasted_iota(jnp.int32, (128, 128), 1) // (KT // SEL_BLOCK)
               == lax.broadcasted_iota(jnp.int32, (128, 128), 0)).astype(BF16)
    sel_b = jnp.where(sel, 1.0, 0.0).astype(BF16)
    cnt = jnp.sum(jnp.dot(tile_of, sel_b, preferred_element_type=F32), axis=1, keepdims=True)
    tile_col = lax.broadcasted_iota(jnp.int32, (128, 1), 0)
    flag = jnp.logical_and(cnt > 0.0, tile_col < t0 // KT)
    flag_m = jnp.where(jnp.broadcast_to(flag, (128, 128)), 1.0, 0.0).astype(BF16)
    row_i = lax.broadcasted_iota(jnp.int32, (128, 128), 0)
    lane_i = lax.broadcasted_iota(jnp.int32, (128, 128), 1)
    before = jnp.dot((lane_i < row_i).astype(BF16), flag_m, preferred_element_type=F32)
    slot = jnp.where(jnp.logical_and(flag, before == lane_i.astype(F32)), 1.0, 0.0).astype(BF16)
    j_rows = lax.broadcasted_iota(jnp.int32, (8, 128), 1).astype(BF16)
    listed = jnp.dot(j_rows, slot, preferred_element_type=F32)
    total = jnp.dot(jnp.ones((8, 128), BF16), flag_m, preferred_element_type=F32)
    lane8 = lax.broadcasted_iota(jnp.int32, (8, 128), 1)
    tl_ref[0] = jnp.where(lane8 == TL_COUNT, total, listed).astype(jnp.int32)


def _nsa_select(h, kcmp, b, t):
    n_ch = t // CMP_STRIDE
    nq = t // QB
    steps = b * KV_HEADS * nq
    return pl.pallas_call(
        functools.partial(_nsa_select_kernel, n_ch=n_ch),
        grid=(b, KV_HEADS, nq),
        in_specs=[pl.BlockSpec((QB, 256), lambda i, g, c: (i * nq + c, C_QA // 256 + g)),
                  pl.BlockSpec((1, n_ch, 128), lambda i, g, c: (i, 0, g))],
        out_specs=[pl.BlockSpec((QB, 256), lambda i, g, c: (i * nq + c, g)),
                   pl.BlockSpec((QB, 128), lambda i, g, c: (i * nq + c, g)),
                   pl.BlockSpec((1, 8, 128), lambda i, g, c: ((i * KV_HEADS + g) * nq + c, 0, 0))],
        out_shape=[jax.ShapeDtypeStruct((b * t, Q_A), F32),
                   jax.ShapeDtypeStruct((b * t, KV_HEADS * 128), BF16),
                   jax.ShapeDtypeStruct((steps, 8, 128), jnp.int32)],
        compiler_params=_cparams(("arbitrary", "arbitrary", "arbitrary")),
        name="nsa_select",
    )(h, kcmp)


def _nsa_attend_kernel(tl_ref, q_ref, un_ref, gl_ref, oc_ref, ks_ref, kw_ref, o_ref,
                       ksb_ref, ksa_ref, kwb_ref, kwa_ref, *, nq):
    i = pl.program_id(0)
    g = pl.program_id(1)
    c = pl.program_id(2)
    step = (i * KV_HEADS + g) * nq + c
    t0 = c * QB
    half = ROWS // 2
    qs = _head_major(q_ref[...])
    slope_col = jnp.concatenate([jnp.broadcast_to(_head_slope(g, r), (QB, 1)) for r in range(GROUP)], axis=0)
    lane64 = lax.broadcasted_iota(jnp.int32, (ROWS, 64), 1)
    ali_q = jnp.where(lane64 == 0, -64.0 * slope_col, jnp.where(lane64 == 1, -slope_col, 0.0))
    q_win = jnp.concatenate([qs, ali_q], axis=1).astype(BF16)
    un = un_ref[...]
    q_sel = jnp.concatenate([q_win, jnp.concatenate([un] * GROUP, axis=0)], axis=1)
    slope_row = _slope_row(g)
    n_wt = WINDOW // WT

    @pl.when(c == 0)
    def _():
        def fill(j, carry):
            r0 = pl.multiple_of(j * KT, KT)
            row = lax.broadcasted_iota(jnp.int32, (KT, 128), 0)
            lane = lax.broadcasted_iota(jnp.int32, (KT, 128), 1)
            for src, dst_kv, dst_aug, tile in ((ks_ref, ksb_ref, ksa_ref, KT), (kw_ref, kwb_ref, kwa_ref, WT)):
                kv_b = src[pl.ds(r0, KT), :].astype(BF16)
                d = (QB - 1) - row % tile
                ali = jnp.where(lane == 64, d >> 6, jnp.where(lane == 65, d & 63, 0)).astype(F32).astype(BF16)
                dst_kv[pl.ds(r0, KT), :] = kv_b
                dst_aug[pl.ds(r0, KT), 0:128] = jnp.where(lane < 64, kv_b, ali)
            ksa_ref[pl.ds(r0, KT), 128:256] = jnp.where((r0 + row) // SEL_BLOCK == lane, 1.0, 0.0).astype(BF16)
            return carry

        lax.fori_loop(0, ks_ref.shape[0] // KT, fill, 0)

    def update(carry, s, kv_b, offset):
        out = []
        for hh, ((m, l, acc), sh) in enumerate(zip(carry, s)):
            shift = slope_row[:, hh * half:(hh + 1) * half] * offset
            m_new = jnp.maximum(m, jnp.max(sh, axis=0, keepdims=True) - shift)
            alpha = jnp.exp(m - m_new)
            p = jnp.exp(sh - (m_new + shift))
            l = alpha * l + jnp.sum(p, axis=0, keepdims=True)
            acc = alpha * acc + lax.dot_general(kv_b, p.astype(BF16), _TN, preferred_element_type=F32)
            out.append((m_new, l, acc))
        return tuple(out)

    def init():
        return tuple((jnp.full((1, half), NEG, F32), jnp.zeros((1, half), F32), jnp.zeros((128, half), F32))
                     for _ in range(2))

    def tile(carry, aug_ref, kvb_ref, q_side, k0, n_keys, keep):
        k_aug = aug_ref[pl.ds(k0, n_keys), :]
        s = []
        for hh in range(2):
            sh = lax.dot_general(k_aug, q_side[hh * half:(hh + 1) * half], _NT, preferred_element_type=F32)
            if keep is not None:
                sh = jnp.where(jnp.concatenate([keep] * (half // QB), axis=1), sh, NEG)
            s.append(sh)
        return update(carry, s, kvb_ref[pl.ds(k0, n_keys), :], (t0 - k0).astype(F32))

    k_tail = pl.multiple_of((t0 // KT) * KT, KT)
    key_x = lax.broadcasted_iota(jnp.int32, (KT, QB), 0)
    q_x = lax.broadcasted_iota(jnp.int32, (KT, QB), 1)
    carry = tile(init(), ksa_ref, ksb_ref, q_sel, k_tail, KT, k_tail + key_x <= t0 + q_x)

    def sel_body(n, carry):
        return tile(carry, ksa_ref, ksb_ref, q_sel, pl.multiple_of(tl_ref[step, n] * KT, KT), KT, None)

    carry = lax.fori_loop(0, tl_ref[step, TL_COUNT], sel_body, carry)
    o_sel = [acc * (1.0 / l) for (_, l, acc) in carry]

    key_w = lax.broadcasted_iota(jnp.int32, (WT, QB), 0)
    q_w = lax.broadcasted_iota(jnp.int32, (WT, QB), 1)
    carry = tile(init(), kwa_ref, kwb_ref, q_win, pl.multiple_of(t0, WT), WT, key_w <= q_w)

    def win_body(n, carry):
        return tile(carry, kwa_ref, kwb_ref, q_win, pl.multiple_of(t0 - n * WT, WT), WT, None)

    carry = lax.fori_loop(1, jnp.minimum(n_wt, c + 1), win_body, carry)
    k_old = pl.multiple_of(jnp.maximum(t0 - n_wt * WT, 0), WT)
    carry = tile(carry, kwa_ref, kwb_ref, q_win, k_old, WT, jnp.logical_and(q_w < key_w, c >= n_wt))
    o_win = [acc * (1.0 / l) for (_, l, acc) in carry]

    gates = jax.nn.sigmoid(gl_ref[...])
    o_c = oc_ref[...]
    for r in range(GROUP):
        hh, cs = divmod(r * QB, half)
        o_s = jnp.transpose(o_sel[hh][:, cs:cs + QB])[:, 64:128]
        o_w = jnp.transpose(o_win[hh][:, cs:cs + QB])[:, 64:128]
        o_ref[:, r * 64:(r + 1) * 64] = (gates[:, 3 * r:3 * r + 1] * o_c[:, r * 64:(r + 1) * 64]
                                         + gates[:, 3 * r + 1:3 * r + 2] * o_s
                                         + gates[:, 3 * r + 2:3 * r + 3] * o_w)


def _nsa_attend(tiles, h, unsel, o_c, b, t):
    nq = t // QB
    grid_spec = pltpu.PrefetchScalarGridSpec(
        num_scalar_prefetch=1,
        grid=(b, KV_HEADS, nq),
        in_specs=[pl.BlockSpec((QB, 256), lambda i, g, c, tl: (i * nq + c, C_QA // 256 + g)),
                  pl.BlockSpec((QB, 128), lambda i, g, c, tl: (i * nq + c, g)),
                  pl.BlockSpec((QB, 128), lambda i, g, c, tl: (i * nq + c, C_GL // 128 + g)),
                  pl.BlockSpec((QB, 256), lambda i, g, c, tl: (i * nq + c, g)),
                  pl.BlockSpec((t, 128), lambda i, g, c, tl: (i, C_KVS // 128 + g)),
                  pl.BlockSpec((t, 128), lambda i, g, c, tl: (i, C_KVW // 128 + g))],
        out_specs=pl.BlockSpec((QB, 256), lambda i, g, c, tl: (i * nq + c, g)),
        scratch_shapes=[pltpu.VMEM((t, 128), BF16), pltpu.VMEM((t, 256), BF16),
                        pltpu.VMEM((t, 128), BF16), pltpu.VMEM((t, 128), BF16)],
    )
    return pl.pallas_call(
        functools.partial(_nsa_attend_kernel, nq=nq),
        grid_spec=grid_spec,
        out_shape=jax.ShapeDtypeStruct((b * t, Q_A), F32),
        compiler_params=_cparams(("arbitrary", "arbitrary", "arbitrary")),
        name="nsa_attend",
    )(tiles, h, unsel, h, o_c, h, h)


def _nsa_prompt(h, kcmp, b, t):
    o_c, unsel, tl = _nsa_select(h, kcmp, b, t)
    return _nsa_attend(tl[:, 0, :], h, unsel, o_c, b, t)


HB = 256


def _lower_bound(lb_ref):
    z = lb_ref[...]
    e = jnp.exp(z - jnp.max(z, axis=0, keepdims=True))
    return e[0:1] / jnp.sum(e, axis=0, keepdims=True)


def _hgrn_prompt_kernel(q_ref, f_ref, i_ref, g_ref, lb_ref, ng_ref, o_ref, s_out_ref, s_ref):
    tb = pl.program_id(1)

    @pl.when(tb == 0)
    def _():
        s_ref[...] = jnp.zeros_like(s_ref)

    n_c = HB // HGRN_CHUNK
    lb = _lower_bound(lb_ref)
    f = lb + (1.0 - lb) * jax.nn.sigmoid(f_ref[...])
    log_f = jnp.log(f)
    row = lax.broadcasted_iota(jnp.int32, (HB, HB), 0)
    col = lax.broadcasted_iota(jnp.int32, (HB, HB), 1)
    tril = jnp.logical_and(col <= row, row // HGRN_CHUNK == col // HGRN_CHUNK)
    lc = _dot01(tril.astype(BF16), log_f)
    lc3 = lc.reshape(n_c, HGRN_CHUNK, Q_B)
    lend = jnp.broadcast_to(lc3[:, HGRN_CHUNK - 1:HGRN_CHUNK, :], lc3.shape).reshape(HB, Q_B)
    q_t = (q_ref[...] * jnp.exp(lc)).astype(BF16)
    k = 1.0 - f
    k_t = (k * jnp.exp(-lc)).astype(BF16)
    k_e = (k * jnp.exp(lend - lc)).astype(BF16)
    dec = jnp.exp(lend)
    v_all = i_ref[...]
    gate = g_ref[...]
    ng = ng_ref[...]
    for hh in range(B_HEADS):
        ks = slice(hh * B_DK, (hh + 1) * B_DK)
        vs = slice(hh * B_DV, (hh + 1) * B_DV)
        v = v_all[:, vs].astype(BF16)
        a = lax.dot_general(q_t[:, ks], k_t[:, ks], _NT, preferred_element_type=F32)
        a = jnp.where(tril, a, 0.0).astype(BF16)
        o = jnp.dot(a, v, preferred_element_type=F32)
        dec_t = jnp.transpose(dec[:, ks])
        st = s_ref[hh]
        inter = []
        for cc in range(n_c):
            rs = slice(cc * HGRN_CHUNK, (cc + 1) * HGRN_CHUNK)
            inter.append(jnp.dot(q_t[rs, ks], st.astype(BF16), preferred_element_type=F32))
            u = lax.dot_general(k_e[rs, ks], v[rs], _TN, preferred_element_type=F32)
            last = (cc + 1) * HGRN_CHUNK - 1
            st = dec_t[:, last:last + 1] * st + u
        s_ref[hh] = st
        o = o + jnp.concatenate(inter, axis=0)
        o = o * lax.rsqrt(jnp.mean(o * o, axis=-1, keepdims=True) + LN_EPS) * ng
        gt = gate[:, vs]
        o_ref[:, vs] = o * (gt * jax.nn.sigmoid(gt))

    @pl.when(tb == pl.num_programs(1) - 1)
    def _():
        s_out_ref[0] = s_ref[...]


def _hgrn_prompt(h, b, t, lb_logits, norm_g):
    nt = t // HB
    return pl.pallas_call(
        _hgrn_prompt_kernel,
        grid=(b, nt),
        in_specs=[pl.BlockSpec((HB, Q_B), lambda i, j: (i * nt + j, C_QB // Q_B)),
                  pl.BlockSpec((HB, Q_B), lambda i, j: (i * nt + j, C_FB // Q_B)),
                  pl.BlockSpec((HB, I_B), lambda i, j: (i * nt + j, C_IB // I_B)),
                  pl.BlockSpec((HB, I_B), lambda i, j: (i * nt + j, C_GB // I_B)),
                  pl.BlockSpec((2, Q_B), lambda i, j: (0, 0)),
                  pl.BlockSpec((1, B_DV), lambda i, j: (0, 0))],
        out_specs=[pl.BlockSpec((HB, I_B), lambda i, j: (i * nt + j, 0)),
                   pl.BlockSpec((1, B_HEADS, B_DK, B_DV), lambda i, j: (i, 0, 0, 0))],
        out_shape=[jax.ShapeDtypeStruct((b * t, I_B), F32),
                   jax.ShapeDtypeStruct((b, B_HEADS, B_DK, B_DV), F32)],
        scratch_shapes=[pltpu.VMEM((B_HEADS, B_DK, B_DV), F32)],
        compiler_params=_cparams(("arbitrary", "arbitrary")),
        name="hgrn_prompt",
    )(h, h, h, h, lb_logits, norm_g)


def _prompt_layer(x, w_pad, cmpw, lb_logits, norm_g, w_a, w_b, w_o, ln1_g, ln1_b, w_up, w_down, ln2_g, ln2_b):
    b, t, _ = x.shape
    x2d = x.reshape(b * t, D_MODEL)
    h = _project(x2d, w_pad)
    kcmp = _cmp_prompt(h, b, t, *cmpw)
    o_a = _nsa_prompt(h, kcmp, b, t)
    o_b, s_end = _hgrn_prompt(h, b, t, lb_logits, norm_g)
    x1 = _merge(x2d, o_a, o_b, h, w_a, w_b, w_o, ln1_g, ln1_b)
    y = _mlp(x1, w_up, w_down, ln2_g, ln2_b)
    return y.reshape(b, t, D_MODEL), h, s_end


def _kv_out(h, col, b, t):
    kv = h[:, col:col + 256].reshape(b, t, KV_HEADS, 2, HEAD_DIM)
    return jnp.swapaxes(kv, 2, 3)


CMP_ROWS = 256
CH_PER_PAGE = PAGE_SIZE // CMP_STRIDE
BLK_PER_PAGE = PAGE_SIZE // SEL_BLOCK
N_POOL_SEL = N_SELECT - 1


def _head_rows(q_row, g):
    rows = [q_row[:, g * 256 + r * 64:g * 256 + (r + 1) * 64] for r in range(GROUP)]
    return jnp.concatenate(rows + [jnp.zeros((8 - GROUP, 64), F32)], axis=0) * ATTN_SCALE


def _slope_col(g):
    row = lax.broadcasted_iota(jnp.int32, (8, 1), 0)
    col = jnp.zeros((8, 1), F32)
    for r in range(GROUP):
        col = jnp.where(row == r, 2.0 ** -(g * GROUP + r + 1), col)
    return col


def _cmp_sample_kernel(pt_ref, cache_ref, w_ref, posf_ref, wb_ref, q_ref, oc_ref, idx_ref,
                       buf_ref, xk_ref, xv_ref, sem, *, n_pages, past):
    b = pl.program_id(0)
    n_ch = n_pages * CH_PER_PAGE
    n_blk = past // SEL_BLOCK
    lanes = idx_ref.shape[-1]

    def page_copy(pg, page):
        return pltpu.make_async_copy(cache_ref.at[page], buf_ref.at[pg], sem.at[pg])

    def issue(pg, carry):
        page_copy(pg, pt_ref[b, pg]).start()
        return carry

    def land(pg, carry):
        page_copy(pg, 0).wait()
        r0 = pl.multiple_of(pg * PAGE_SIZE, PAGE_SIZE)
        xk_ref[pl.ds(r0, PAGE_SIZE), :] = jnp.transpose(buf_ref[pg, 0].reshape(128, PAGE_SIZE))
        xv_ref[pl.ds(r0, PAGE_SIZE), :] = jnp.transpose(buf_ref[pg, 1].reshape(128, PAGE_SIZE))
        return carry

    lax.fori_loop(0, n_pages, issue, 0)
    lax.fori_loop(0, n_pages, land, 0)

    bias = jnp.dot(posf_ref[...].astype(BF16), wb_ref[...], preferred_element_type=F32)[0:1]
    blocks = []
    for c, x_ref in enumerate((xk_ref, xv_ref)):
        acc = jnp.zeros((n_ch, 256), F32)
        for l in range(CMP_STRIDE):
            xl = x_ref[pl.ds(l, n_ch, stride=CMP_STRIDE), :].astype(BF16)
            acc += jnp.dot(xl, w_ref[c, l], preferred_element_type=F32)
        bias_c = jnp.concatenate([bias[:, c * 64:(c + 1) * 64]] * KV_HEADS, axis=1)
        blocks.append((acc[:, :128] + pltpu.roll(acc[:, 128:], n_ch - 1, 0) + bias_c).astype(BF16))
    kc_all, vc_all = blocks

    q_row = q_ref[0]
    n_ix = lax.broadcasted_iota(jnp.int32, (1, n_ch), 1)
    dist_i = past - (n_ix * CMP_STRIDE + (CMP_LEN - 1))
    valid = jnp.logical_and(dist_i >= 0, n_ix < n_ch - 1)
    dist = dist_i.astype(F32)
    ratio = SEL_BLOCK // CMP_STRIDE
    pool = (lax.broadcasted_iota(jnp.int32, (n_ch, lanes), 0) // ratio
            == lax.broadcasted_iota(jnp.int32, (n_ch, lanes), 1)).astype(BF16)
    row8 = lax.broadcasted_iota(jnp.int32, (8, lanes), 0)
    score = jnp.full((8, lanes), -FORCE, F32)
    lane_g = lax.broadcasted_iota(jnp.int32, (8, 128), 1) // HEAD_DIM
    o_c = jnp.zeros((8, 128), F32)
    for g in range(KV_HEADS):
        pieces = [jnp.zeros((8, 64), F32)] * KV_HEADS
        pieces[g] = _head_rows(q_row, g)
        qpad = jnp.concatenate(pieces, axis=1).astype(BF16)
        s = lax.dot_general(qpad, kc_all, _NT, preferred_element_type=F32)
        s = jnp.where(valid, s - _slope_col(g) * dist, NEG)
        m = jnp.max(s, axis=-1, keepdims=True)
        p = jnp.where(valid, jnp.exp(s - m), 0.0)
        p = p / jnp.maximum(jnp.sum(p, axis=-1, keepdims=True), 1e-30)
        o_c = jnp.where(lane_g == g, jnp.dot(p.astype(BF16), vc_all, preferred_element_type=F32), o_c)
        imp = p[0:1] + p[1:2] + p[2:3] + p[3:4]
        imp_blk = _x_dot01(jnp.broadcast_to(imp, (8, n_ch)), pool)
        score = jnp.where(row8 == g, imp_blk, score)
    lane = lax.broadcasted_iota(jnp.int32, (8, lanes), 1)
    forced = jnp.logical_or(lane == 0, lane == n_blk - 1)
    score = jnp.where(lane < n_blk, jnp.where(forced, FORCE, score), -jnp.inf)
    out = jnp.zeros((8, lanes), jnp.int32)
    for r in range(N_POOL_SEL):
        m = jnp.max(score, axis=-1, keepdims=True)
        first = jnp.min(jnp.where(score == m, lane, lanes), axis=-1, keepdims=True)
        out = jnp.where(lane == r, first, out)
        score = jnp.where(lane == first, -jnp.inf, score)
    idx_ref[0] = out
    oc_ref[0] = o_c


def _cmp_sample(page_table, cache5, w_s, posf, wb, q3, past):
    bsz, n_pages = page_table.shape
    n_ch = n_pages * CH_PER_PAGE
    lanes = -(-(past // SEL_BLOCK) // 128) * 128
    grid_spec = pltpu.PrefetchScalarGridSpec(
        num_scalar_prefetch=1,
        grid=(bsz,),
        in_specs=[pl.BlockSpec(memory_space=pl.ANY),
                  pl.BlockSpec((2, CMP_STRIDE, 128, 256), lambda i, pt: (0, 0, 0, 0)),
                  pl.BlockSpec((8, 4096), lambda i, pt: (0, 0)),
                  pl.BlockSpec((4096, 128), lambda i, pt: (0, 0)),
                  pl.BlockSpec((1, 1, Q_A), lambda i, pt: (i, 0, 0))],
        out_specs=[pl.BlockSpec((1, 8, 128), lambda i, pt: (i, 0, 0)),
                   pl.BlockSpec((1, 8, lanes), lambda i, pt: (i, 0, 0))],
        scratch_shapes=[pltpu.VMEM((n_pages, 2, KV_HEADS, HEAD_DIM, PAGE_SIZE), F32),
                        pltpu.VMEM((n_pages * PAGE_SIZE, 128), F32),
                        pltpu.VMEM((n_pages * PAGE_SIZE, 128), F32),
                        pltpu.SemaphoreType.DMA((n_pages,))],
    )
    return pl.pallas_call(
        functools.partial(_cmp_sample_kernel, n_pages=n_pages, past=past),
        grid_spec=grid_spec,
        out_shape=[jax.ShapeDtypeStruct((bsz, 8, 128), F32),
                   jax.ShapeDtypeStruct((bsz, 8, lanes), jnp.int32)],
        compiler_params=_cparams(("arbitrary",)),
        name="cmp_sample",
    )(page_table, cache5, w_s, posf, wb, q3)


def _pick_lane(mat, lane, target):
    return jnp.sum(jnp.where(lane == target, mat, 0.0), axis=-1, keepdims=True)


def _sel_win_sample_kernel(pt_ref, idx_ref, cache_ref, win_ref, q_ref, ks_ref, kw_ref, gl_ref, oc_ref, o_ref,
                           kbuf_ref, vbuf_ref, sem, *, past):
    b = pl.program_id(0)
    slot_lanes = N_SELECT * PAGE_SIZE

    def page_copies(g, k, page):
        dst = pl.ds(k * PAGE_SIZE, PAGE_SIZE)
        return (pltpu.make_async_copy(cache_ref.at[page, 0, g], kbuf_ref.at[g, :, dst], sem),
                pltpu.make_async_copy(cache_ref.at[page, 1, g], vbuf_ref.at[g, :, dst], sem))

    for g in range(KV_HEADS):
        for k in range(N_POOL_SEL):
            page = pt_ref[b, idx_ref[b, g * N_SELECT + k] // BLK_PER_PAGE]
            for cp in page_copies(g, k, page):
                cp.start()
        pad = pl.ds(N_POOL_SEL * PAGE_SIZE, PAGE_SIZE)
        kbuf_ref[g, :, pad] = jnp.zeros((HEAD_DIM, PAGE_SIZE), F32)
        vbuf_ref[g, :, pad] = jnp.zeros((HEAD_DIM, PAGE_SIZE), F32)

    q_row = q_ref[0]
    ks_new = ks_ref[0]
    kw_new = kw_ref[0]
    gl_all = jax.nn.sigmoid(gl_ref[0])
    lane128 = lax.broadcasted_iota(jnp.int32, (8, 128), 1)
    row8 = lax.broadcasted_iota(jnp.int32, (8, 1), 0)
    w_len = win_ref.shape[-1]

    def two_piece(qh, slope, k_t, v_t, dist_i, valid, k_new, v_new):
        s = jnp.dot(qh.astype(BF16), k_t.astype(BF16), preferred_element_type=F32)
        s = jnp.where(valid, s - slope * dist_i.astype(F32), NEG)
        s_n = jnp.sum(qh * k_new, axis=-1, keepdims=True)
        m = jnp.maximum(jnp.max(s, axis=-1, keepdims=True), s_n)
        p = jnp.where(valid, jnp.exp(s - m), 0.0)
        p_n = jnp.exp(s_n - m)
        l = jnp.sum(p, axis=-1, keepdims=True) + p_n
        o = lax.dot_general(p.astype(BF16), v_t.astype(BF16), _NT, preferred_element_type=F32) + p_n * v_new
        return o / l

    outs = []
    for g in range(KV_HEADS):
        qh = _head_rows(q_row, g)
        slope = _slope_col(g)
        j_ix = lax.broadcasted_iota(jnp.int32, (1, w_len), 1)
        dist_w = w_len - j_ix
        o_w = two_piece(qh, slope, win_ref[0, 0, g], win_ref[0, 1, g], dist_w, dist_w < WINDOW,
                        kw_new[:, g * 128:g * 128 + 64], kw_new[:, g * 128 + 64:(g + 1) * 128])
        outs.append((qh, slope, o_w))

    for g in range(KV_HEADS):
        for k in range(N_POOL_SEL):
            for cp in page_copies(g, k, 0):
                cp.wait()

    lane_s = lax.broadcasted_iota(jnp.int32, (1, slot_lanes), 1)
    o_all = []
    for g in range(KV_HEADS):
        qh, slope, o_w = outs[g]
        pos_k = jnp.full((1, slot_lanes), past + 1, jnp.int32)
        for k in range(N_POOL_SEL):
            blk = idx_ref[b, g * N_SELECT + k]
            r = lane_s % PAGE_SIZE
            in_blk = jnp.logical_and(lane_s // PAGE_SIZE == k, r // SEL_BLOCK == blk % BLK_PER_PAGE)
            pos_k = jnp.where(in_blk, (blk // BLK_PER_PAGE) * PAGE_SIZE + r, pos_k)
        dist_s = past - pos_k
        o_s = two_piece(qh, slope, kbuf_ref[g], vbuf_ref[g], dist_s, dist_s >= 0,
                        ks_new[:, g * 128:g * 128 + 64], ks_new[:, g * 128 + 64:(g + 1) * 128])
        o_c = oc_ref[0][:, g * 64:(g + 1) * 64]
        gates = jnp.broadcast_to(gl_all[:, g * 128:(g + 1) * 128], (8, 128))
        g_c = _pick_lane(gates, lane128, 3 * row8)
        g_s = _pick_lane(gates, lane128, 3 * row8 + 1)
        g_w = _pick_lane(gates, lane128, 3 * row8 + 2)
        o_all.append(g_c * o_c + g_s * o_s + g_w * o_w)
    o_ref[0] = jnp.concatenate(o_all, axis=1)


def _sel_win_sample(page_table, idx, cache_sel5, cache_win5, q3, ks3, kw3, gl3, o_c, past):
    bsz = page_table.shape[0]
    w_len = cache_win5.shape[-1]
    grid_spec = pltpu.PrefetchScalarGridSpec(
        num_scalar_prefetch=2,
        grid=(bsz,),
        in_specs=[pl.BlockSpec(memory_space=pl.ANY),
                  pl.BlockSpec((1, 2, KV_HEADS, HEAD_DIM, w_len), lambda i, pt, ix: (i, 0, 0, 0, 0)),
                  pl.BlockSpec((1, 1, Q_A), lambda i, pt, ix: (i, 0, 0)),
                  pl.BlockSpec((1, 1, 256), lambda i, pt, ix: (i, 0, 0)),
                  pl.BlockSpec((1, 1, 256), lambda i, pt, ix: (i, 0, 0)),
                  pl.BlockSpec((1, 1, 256), lambda i, pt, ix: (i, 0, 0)),
                  pl.BlockSpec((1, 8, 128), lambda i, pt, ix: (i, 0, 0))],
        out_specs=pl.BlockSpec((1, 8, 128), lambda i, pt, ix: (i, 0, 0)),
        scratch_shapes=[pltpu.VMEM((KV_HEADS, HEAD_DIM, N_SELECT * PAGE_SIZE), F32),
                        pltpu.VMEM((KV_HEADS, HEAD_DIM, N_SELECT * PAGE_SIZE), F32),
                        pltpu.SemaphoreType.DMA(())],
    )
    return pl.pallas_call(
        functools.partial(_sel_win_sample_kernel, past=past),
        grid_spec=grid_spec,
        out_shape=jax.ShapeDtypeStruct((bsz, 8, 128), F32),
        compiler_params=_cparams(("arbitrary",)),
        name="sel_win_sample",
    )(page_table, idx, cache_sel5, cache_win5, q3, ks3, kw3, gl3, o_c)


def _hgrn_sample_kernel(q_ref, f_ref, v_ref, g_ref, lb_ref, ng_ref, s_ref, o_ref, s_out_ref):
    z = lb_ref[...]
    e = jnp.exp(z - jnp.max(z, axis=0, keepdims=True))
    lb = e[0] / jnp.sum(e, axis=0)
    f = lb + (1.0 - lb) * jax.nn.sigmoid(f_ref[0])
    decay = jnp.exp(jnp.log(f))
    k = 1.0 - f
    q = q_ref[0]
    v = v_ref[0]
    gate = g_ref[0]
    ng = ng_ref[...]
    for hh in range(B_HEADS):
        hs = slice(hh, hh + 1)
        s_new = decay[hs] * s_ref[0, hh] + v[:, hs] * k[hs]
        s_out_ref[0, hh] = s_new
        o = jnp.sum(q[hs] * s_new, axis=1, keepdims=True)
        o = o * lax.rsqrt(jnp.mean(o * o, axis=0, keepdims=True) + LN_EPS) * ng
        gt = gate[:, hs]
        o_ref[0, :, hs] = o * (gt * jax.nn.sigmoid(gt))


def _hgrn_sample(q_hk, f_hk, v_vh, g_vh, lb_hk, ng_col, state_t):
    bsz = q_hk.shape[0]
    return pl.pallas_call(
        _hgrn_sample_kernel,
        grid=(bsz,),
        in_specs=[pl.BlockSpec((1, B_HEADS, B_DK), lambda i: (i, 0, 0)),
                  pl.BlockSpec((1, B_HEADS, B_DK), lambda i: (i, 0, 0)),
                  pl.BlockSpec((1, B_DV, B_HEADS), lambda i: (i, 0, 0)),
                  pl.BlockSpec((1, B_DV, B_HEADS), lambda i: (i, 0, 0)),
                  pl.BlockSpec((2, B_HEADS, B_DK), lambda i: (0, 0, 0)),
                  pl.BlockSpec((B_DV, 1), lambda i: (0, 0)),
                  pl.BlockSpec((1, B_HEADS, B_DV, B_DK), lambda i: (i, 0, 0, 0))],
        out_specs=[pl.BlockSpec((1, B_DV, B_HEADS), lambda i: (i, 0, 0)),
                   pl.BlockSpec((1, B_HEADS, B_DV, B_DK), lambda i: (i, 0, 0, 0))],
        out_shape=[jax.ShapeDtypeStruct((bsz, B_DV, B_HEADS), F32),
                   jax.ShapeDtypeStruct((bsz, B_HEADS, B_DV, B_DK), F32)],
        compiler_params=_cparams(("arbitrary",)),
        name="hgrn_sample",
    )(q_hk, f_hk, v_vh, g_vh, lb_hk, ng_col, state_t)


def _rows_last(cache):
    return jnp.moveaxis(cache, -4, -1)


def _sample_layer(x, cache_cmp, cache_sel, cache_win, state, page_table, w_pad, cmpw, w_s, lb_logits, norm_g,
                  w_a, w_b, w_o, ln1_g, ln1_b, w_up, w_down, ln2_g, ln2_b):
    bsz, t, _ = x.shape
    assert t == 1, "the sample group decodes one token per request"
    n_pages = page_table.shape[1]
    past = n_pages * PAGE_SIZE
    x2d = x.reshape(bsz, D_MODEL)
    h = _project(x2d, w_pad)
    h3 = h.reshape(bsz, 1, D_PAD)
    q3 = h3[:, :, C_QA:C_QA + Q_A]
    wl, wb, posf = cmpw
    o_c, idx = _cmp_sample(page_table, _rows_last(cache_cmp), w_s, posf, wb, q3, past)
    idx2 = jnp.pad(idx[:, :KV_HEADS, :N_POOL_SEL], ((0, 0), (0, 0), (0, 1))).reshape(bsz, KV_HEADS * N_SELECT)
    o_rd = _sel_win_sample(page_table, idx2, _rows_last(cache_sel), _rows_last(cache_win),
                           q3, h3[:, :, C_KVS:C_KVS + 256], h3[:, :, C_KVW:C_KVW + 256],
                           h3[:, :, C_GL:C_GL + 256], o_c, past)
    o_a = jnp.swapaxes(o_rd[:, :GROUP].reshape(bsz, GROUP, KV_HEADS, HEAD_DIM), 1, 2).reshape(bsz, Q_A)
    to_hk = lambda a: a.reshape(-1, B_HEADS, B_DK)
    to_vh = lambda a: jnp.swapaxes(a.reshape(-1, B_HEADS, B_DV), 1, 2)
    o_vh, s_t = _hgrn_sample(to_hk(h[:, C_QB:C_QB + Q_B]), to_hk(h[:, C_FB:C_FB + Q_B]),
                             to_vh(h[:, C_IB:C_IB + I_B]), to_vh(h[:, C_GB:C_GB + I_B]),
                             to_hk(lb_logits), norm_g.reshape(B_DV, 1), jnp.swapaxes(state, 2, 3))
    o_b = jnp.swapaxes(o_vh, 1, 2).reshape(bsz, I_B)
    x1 = _merge(x2d, o_a, o_b, h, w_a, w_b, w_o, ln1_g, ln1_b)
    y = _mlp(x1, w_up, w_down, ln2_g, ln2_b)
    return y.reshape(bsz, 1, D_MODEL), h, jnp.swapaxes(s_t, 2, 3)


def _cmp_sample_weights(cmp_w):
    w = cmp_w.reshape(2, 2, CMP_STRIDE, HEAD_DIM, HEAD_DIM)
    ws = jnp.zeros((2, CMP_STRIDE, 2, HEAD_DIM, 2, 2, HEAD_DIM), F32)
    for g in range(2):
        ws = ws.at[:, :, g, :, :, g, :].set(jnp.transpose(w, (0, 2, 3, 1, 4)))
    return ws.reshape(2, CMP_STRIDE, 128, 256).astype(BF16)


def kernel(x_prompt, x_sample, cache_cmp_kv, cache_sel_kv, cache_win_kv, state_hgrn, page_table,
           w_in, cmp_w, cmp_pos, hgrn_lb_logits, hgrn_norm_g, w_br_a, w_br_b, w_out,
           ln1_g, ln1_b, w_up, w_down, ln2_g, ln2_b):
    assert w_in.shape[0] == 1, "one layer"
    b, t, _ = x_prompt.shape
    bsz = x_sample.shape[0]
    assert t % HB == 0 and t // SEL_BLOCK <= 128
    perm = jnp.asarray(np.maximum(_PERM, 0))
    w_pad = jnp.where(jnp.asarray(_PERM >= 0)[None, :], jnp.take(w_in[0], perm, axis=1), 0.0).astype(BF16)
    cmpw = _cmp_weights(cmp_w[0], cmp_pos[0])
    w_s = _cmp_sample_weights(cmp_w[0])
    dense = (w_br_a[0].astype(BF16), w_br_b[0].astype(BF16), w_out[0].astype(BF16), ln1_g, ln1_b,
             w_up[0].astype(BF16), w_down[0].astype(BF16), ln2_g, ln2_b)

    y_p, h_p, s_p = _prompt_layer(x_prompt, w_pad, cmpw, hgrn_lb_logits, hgrn_norm_g, *dense)
    y_s, h_s, s_s = _sample_layer(x_sample, cache_cmp_kv[0], cache_sel_kv[0], cache_win_kv[0], state_hgrn[0],
                                  page_table, w_pad, cmpw, w_s, hgrn_lb_logits, hgrn_norm_g, *dense)

    win_p = min(WINDOW, t)
    kvw_p = _kv_out(h_p, C_KVW, b, t)
    kvw_s = _kv_out(h_s, C_KVW, bsz, 1)
    new_win_s = jnp.concatenate([cache_win_kv[0], kvw_s], axis=1)[:, -min(WINDOW, cache_win_kv.shape[2] + 1):]
    return (y_p, y_s,
            _kv_out(h_p, C_KVC, b, t)[None], _kv_out(h_p, C_KVS, b, t)[None], kvw_p[:, -win_p:][None], s_p[None],
            _kv_out(h_s, C_KVC, bsz, 1)[None], _kv_out(h_s, C_KVS, bsz, 1)[None], new_win_s[None], s_s[None])
```

```python
import functools

import numpy as np
import jax
import jax.numpy as jnp
from jax import lax
from jax.experimental import pallas as pl
from jax.experimental.pallas import tpu as pltpu

F32 = jnp.float32
BF16 = jnp.bfloat16

D_MODEL = 1024
HEAD_DIM = 64
A_HEADS = 8
KV_HEADS = 2
GROUP = A_HEADS // KV_HEADS
CMP_STRIDE = 16
CMP_LEN = 32
SEL_BLOCK = 64
N_SELECT = 16
WINDOW = 512
PAGE_SIZE = 128
B_HEADS = 8
B_DK = 128
B_DV = 64
HGRN_CHUNK = 32
D_FF = 4 * D_MODEL
DEEPNORM_ALPHA = 2.0 ** 0.25
LN_EPS = 1e-5
NEG = -1e30
FORCE = 1e6
ATTN_SCALE = HEAD_DIM ** -0.5
Q_A = A_HEADS * HEAD_DIM
KV_A = 2 * KV_HEADS * HEAD_DIM
GATE_A = 3 * A_HEADS
Q_B = B_HEADS * B_DK
I_B = B_HEADS * B_DV

C_MG = 0
C_QB = 2048
C_FB = 3072
C_QA = 4096
C_IB = 4608
C_GB = 5120
C_KVC = 5632
C_KVS = 5888
C_KVW = 6144
C_GL = 6400
D_PAD = 6656
PROJ_TN = 1664

VMEM_LIMIT = 56 * 1024 * 1024
UNSEL = float(2.0 ** 100)


def _cparams(sem):
    return pltpu.CompilerParams(dimension_semantics=sem, vmem_limit_bytes=VMEM_LIMIT)


def _proj_perm():
    perm = np.full((D_PAD,), -1, np.int64)
    o_qa, o_kvc, o_kvs, o_kvw = 0, Q_A, Q_A + KV_A, Q_A + 2 * KV_A
    o_gl = Q_A + 3 * KV_A
    o_qb = o_gl + GATE_A
    o_fb = o_qb + Q_B
    o_ib = o_fb + Q_B
    o_gb = o_ib + I_B
    o_mg = o_gb + I_B
    perm[C_MG:C_MG + 2 * D_MODEL] = o_mg + np.arange(2 * D_MODEL)
    perm[C_QB:C_QB + Q_B] = o_qb + np.arange(Q_B)
    perm[C_FB:C_FB + Q_B] = o_fb + np.arange(Q_B)
    perm[C_QA:C_QA + Q_A] = o_qa + np.arange(Q_A)
    perm[C_IB:C_IB + I_B] = o_ib + np.arange(I_B)
    perm[C_GB:C_GB + I_B] = o_gb + np.arange(I_B)
    for new, old in ((C_KVC, o_kvc), (C_KVS, o_kvs), (C_KVW, o_kvw)):
        for g in range(KV_HEADS):
            for c in range(2):
                dst = new + g * 128 + c * 64
                src = old + c * 128 + g * 64
                perm[dst:dst + 64] = src + np.arange(64)
    for g in range(KV_HEADS):
        perm[C_GL + g * 128:C_GL + g * 128 + 12] = o_gl + g * 12 + np.arange(12)
    return perm


_PERM = _proj_perm()


def _proj_kernel(x_ref, w_ref, o_ref):
    o_ref[...] = jnp.dot(x_ref[...].astype(BF16), w_ref[...], preferred_element_type=F32)


def _project(x2d, w_pad):
    n = x2d.shape[0]
    tm = min(512, n)
    return pl.pallas_call(
        _proj_kernel,
        grid=(D_PAD // PROJ_TN, n // tm),
        in_specs=[pl.BlockSpec((tm, D_MODEL), lambda j, i: (i, 0)),
                  pl.BlockSpec((D_MODEL, PROJ_TN), lambda j, i: (0, j))],
        out_specs=pl.BlockSpec((tm, PROJ_TN), lambda j, i: (i, j)),
        out_shape=jax.ShapeDtypeStruct((n, D_PAD), F32),
        compiler_params=_cparams(("arbitrary", "arbitrary")),
        name="proj",
    )(x2d, w_pad)


def _layer_norm(v, g, b):
    mu = jnp.mean(v, axis=-1, keepdims=True)
    d = v - mu
    var = jnp.mean(d * d, axis=-1, keepdims=True)
    return d * lax.rsqrt(var + LN_EPS) * g + b


def _merge_kernel(x_ref, oa_ref, ob_ref, mga_ref, mgb_ref, wa_ref, wb_ref, wo_ref, g_ref, b_ref, o_ref):
    br_a = jnp.dot(oa_ref[...].astype(BF16), wa_ref[...], preferred_element_type=F32)
    br_b = jnp.dot(ob_ref[...].astype(BF16), wb_ref[...], preferred_element_type=F32)
    merged = jax.nn.sigmoid(mga_ref[...]) * br_a + jax.nn.sigmoid(mgb_ref[...]) * br_b
    mix = jnp.dot(merged.astype(BF16), wo_ref[...], preferred_element_type=F32)
    o_ref[...] = _layer_norm(DEEPNORM_ALPHA * x_ref[...] + mix, g_ref[...], b_ref[...])


def _merge(x2d, o_a, o_b, h, w_a, w_b, w_o, ln_g, ln_b):
    n = x2d.shape[0]
    tm = min(256, n)
    const = lambda i: (0, 0)
    return pl.pallas_call(
        _merge_kernel,
        grid=(n // tm,),
        in_specs=[pl.BlockSpec((tm, D_MODEL), lambda i: (i, 0)),
                  pl.BlockSpec((tm, Q_A), lambda i: (i, 0)),
                  pl.BlockSpec((tm, I_B), lambda i: (i, 0)),
                  pl.BlockSpec((tm, D_MODEL), lambda i: (i, C_MG // D_MODEL)),
                  pl.BlockSpec((tm, D_MODEL), lambda i: (i, C_MG // D_MODEL + 1)),
                  pl.BlockSpec((Q_A, D_MODEL), const),
                  pl.BlockSpec((I_B, D_MODEL), const),
                  pl.BlockSpec((D_MODEL, D_MODEL), const),
                  pl.BlockSpec((1, D_MODEL), const),
                  pl.BlockSpec((1, D_MODEL), const)],
        out_specs=pl.BlockSpec((tm, D_MODEL), lambda i: (i, 0)),
        out_shape=jax.ShapeDtypeStruct((n, D_MODEL), F32),
        compiler_params=_cparams(("arbitrary",)),
        name="merge_ln1",
    )(x2d, o_a, o_b, h, h, w_a, w_b, w_o, ln_g, ln_b)


def _mlp_kernel(x_ref, wu_ref, wd_ref, g_ref, b_ref, o_ref, xb_ref, acc_ref):
    j = pl.program_id(1)

    @pl.when(j == 0)
    def _():
        xb_ref[...] = x_ref[...].astype(BF16)
        acc_ref[...] = jnp.zeros_like(acc_ref)

    u = jnp.dot(xb_ref[...], wu_ref[...], preferred_element_type=F32)
    u = jnp.maximum(u, 0.0)
    acc_ref[...] += jnp.dot((u * u).astype(BF16), wd_ref[...], preferred_element_type=F32)

    @pl.when(j == pl.num_programs(1) - 1)
    def _():
        o_ref[...] = _layer_norm(DEEPNORM_ALPHA * x_ref[...] + acc_ref[...], g_ref[...], b_ref[...])


def _mlp(x1, w_up, w_down, ln_g, ln_b):
    n = x1.shape[0]
    tm = min(1024, n)
    tf = 1024
    return pl.pallas_call(
        _mlp_kernel,
        grid=(n // tm, D_FF // tf),
        in_specs=[pl.BlockSpec((tm, D_MODEL), lambda i, j: (i, 0)),
                  pl.BlockSpec((D_MODEL, tf), lambda i, j: (0, j)),
                  pl.BlockSpec((tf, D_MODEL), lambda i, j: (j, 0)),
                  pl.BlockSpec((1, D_MODEL), lambda i, j: (0, 0)),
                  pl.BlockSpec((1, D_MODEL), lambda i, j: (0, 0))],
        out_specs=pl.BlockSpec((tm, D_MODEL), lambda i, j: (i, 0)),
        out_shape=jax.ShapeDtypeStruct((n, D_MODEL), F32),
        scratch_shapes=[pltpu.VMEM((tm, D_MODEL), BF16), pltpu.VMEM((tm, D_MODEL), F32)],
        compiler_params=_cparams(("arbitrary", "arbitrary")),
        name="mlp_ln2",
    )(x1, w_up, w_down, ln_g, ln_b)


def _dot01(a01, x):
    hi = x.astype(BF16)
    r1 = x - hi.astype(F32)
    mid = r1.astype(BF16)
    lo = (r1 - mid.astype(F32)).astype(BF16)
    out = jnp.dot(a01, hi, preferred_element_type=F32)
    out += jnp.dot(a01, mid, preferred_element_type=F32)
    out += jnp.dot(a01, lo, preferred_element_type=F32)
    return out


def _x_dot01(x, b01):
    hi = x.astype(BF16)
    r1 = x - hi.astype(F32)
    mid = r1.astype(BF16)
    lo = (r1 - mid.astype(F32)).astype(BF16)
    out = jnp.dot(hi, b01, preferred_element_type=F32)
    out += jnp.dot(mid, b01, preferred_element_type=F32)
    out += jnp.dot(lo, b01, preferred_element_type=F32)
    return out


_NT = (((1,), (1,)), ((), ()))
_TN = (((0,), (0,)), ((), ()))


def _cmp_prompt_kernel(kv_ref, w_ref, posf_ref, wb_ref, o_ref, *, n_ch):
    acc = jnp.zeros((n_ch, 256), F32)
    for l in range(CMP_STRIDE):
        xl = kv_ref[pl.ds(l, n_ch, stride=CMP_STRIDE), :].astype(BF16)
        acc += jnp.dot(xl, w_ref[l], preferred_element_type=F32)
    bias = jnp.dot(posf_ref[...].astype(BF16), wb_ref[...], preferred_element_type=F32)[0:1]
    nxt = pltpu.roll(acc[:, 128:], n_ch - 1, 0)
    o_ref[0] = (acc[:, :128] + nxt + bias).astype(BF16)


def _cmp_weights(cmp_w, cmp_pos):
    w = cmp_w.reshape(2, 2, CMP_STRIDE, HEAD_DIM, HEAD_DIM)
    wl = jnp.zeros((CMP_STRIDE, 2, HEAD_DIM, 2, 2, HEAD_DIM), F32)
    wb = jnp.zeros((2, CMP_LEN * HEAD_DIM, 2, HEAD_DIM), F32)
    for c in range(2):
        wl = wl.at[:, c, :, :, c, :].set(jnp.transpose(w[c], (1, 2, 0, 3)))
        wb = wb.at[c, :, c, :].set(cmp_w[c].reshape(CMP_LEN * HEAD_DIM, HEAD_DIM))
    wl = wl.reshape(CMP_STRIDE, 128, 256).astype(BF16)
    wb = wb.reshape(2 * CMP_LEN * HEAD_DIM, 128).astype(BF16)
    posf = jnp.broadcast_to(cmp_pos.reshape(1, 2 * CMP_LEN * HEAD_DIM), (8, 2 * CMP_LEN * HEAD_DIM))
    return wl, wb, posf


def _cmp_prompt(h, b, t, wl, wb, posf):
    n_ch = t // CMP_STRIDE
    return pl.pallas_call(
        functools.partial(_cmp_prompt_kernel, n_ch=n_ch),
        grid=(b, KV_HEADS),
        in_specs=[pl.BlockSpec((t, 128), lambda i, g: (i, C_KVC // 128 + g)),
                  pl.BlockSpec((CMP_STRIDE, 128, 256), lambda i, g: (0, 0, 0)),
                  pl.BlockSpec((8, 4096), lambda i, g: (0, 0)),
                  pl.BlockSpec((4096, 128), lambda i, g: (0, 0))],
        out_specs=pl.BlockSpec((1, n_ch, 128), lambda i, g: (i, 0, g)),
        out_shape=jax.ShapeDtypeStruct((b, n_ch, 256), BF16),
        compiler_params=_cparams(("arbitrary", "arbitrary")),
        name="cmp_prompt",
    )(h, wl, posf, wb)


QB = 128
KT = 256
WT = 128


ROWS = GROUP * QB
TL_COUNT = 64


def _head_major(q):
    return jnp.concatenate([q[:, r * 64:(r + 1) * 64] for r in range(GROUP)], axis=0) * ATTN_SCALE


def _head_slope(g, r):
    return jnp.where(g == 0, 2.0 ** -(r + 1), 2.0 ** -(r + 1 + GROUP)).astype(F32)


def _slope_row(g):
    r_ix = lax.broadcasted_iota(jnp.int32, (1, ROWS), 1) // QB
    row = jnp.zeros((1, ROWS), F32)
    for r in range(GROUP):
        row = jnp.where(r_ix == r, _head_slope(g, r), row)
    return row


def _nsa_select_kernel(q_ref, kc_ref, oc_ref, un_ref, tl_ref, *, n_ch):
    g = pl.program_id(1)
    c = pl.program_id(2)
    t0 = c * QB
    qpad = jnp.concatenate([_head_major(q_ref[...]), jnp.zeros((ROWS, 64), F32)], axis=1).astype(BF16)
    kc = kc_ref[0]
    s = lax.dot_general(kc, qpad, _NT, preferred_element_type=F32)
    pos_row = t0 + lax.broadcasted_iota(jnp.int32, (1, ROWS), 1) % QB
    n_col = lax.broadcasted_iota(jnp.int32, (n_ch, 1), 0)
    dist_i = pos_row - (n_col * CMP_STRIDE + (CMP_LEN - 1))
    valid = jnp.logical_and(dist_i >= 0, n_col < n_ch - 1)
    s = jnp.where(valid, s - _slope_row(g) * dist_i.astype(F32), NEG)
    m = jnp.max(s, axis=0, keepdims=True)
    p = jnp.where(valid, jnp.exp(s - m), 0.0)
    p = p * (1.0 / jnp.maximum(jnp.sum(p, axis=0, keepdims=True), 1e-30))
    o_c = lax.dot_general(p.astype(BF16), kc, _TN, preferred_element_type=F32)
    for r in range(GROUP):
        oc_ref[:, r * 64:(r + 1) * 64] = o_c[r * QB:(r + 1) * QB, 64:128]

    imp = p[:, 0:QB]
    for r in range(1, GROUP):
        imp = imp + p[:, r * QB:(r + 1) * QB]
    ratio = SEL_BLOCK // CMP_STRIDE
    pool_t = (lax.broadcasted_iota(jnp.int32, (128, n_ch), 1) // ratio
              == lax.broadcasted_iota(jnp.int32, (128, n_ch), 0)).astype(BF16)
    imp_blk = _dot01(pool_t, imp)
    blk = lax.broadcasted_iota(jnp.int32, (128, QB), 0)
    pos_q = t0 + lax.broadcasted_iota(jnp.int32, (1, QB), 1)
    cur = pos_q // SEL_BLOCK
    forced = jnp.logical_or(jnp.logical_or(blk == 0, blk == cur), blk == cur - 1)
    allowed = blk * SEL_BLOCK <= pos_q
    work = jnp.where(jnp.logical_and(allowed, jnp.logical_not(forced)), imp_blk, -jnp.inf)
    sel = forced
    for _ in range(N_SELECT - 3):
        mx = jnp.max(work, axis=0, keepdims=True)
        first = jnp.min(jnp.where(work == mx, blk, 128), axis=0, keepdims=True)
        pick = blk == first
        sel = jnp.logical_or(sel, pick)
        work = jnp.where(pick, -jnp.inf, work)
    sel = jnp.logical_and(sel, allowed)
    un_ref[...] = jnp.transpose(jnp.where(sel, 0.0, -UNSEL)).astype(BF16)

    tile_of = (lax.broadcasted_iota(jnp.int32, (128, 128), 1) // (KT // SEL_BLOCK)
               == lax.broadcasted_iota(jnp.int32, (128, 128), 0)).astype(BF16)
    sel_b = jnp.where(sel, 1.0, 0.0).astype(BF16)
    cnt = jnp.sum(jnp.dot(tile_of, sel_b, preferred_element_type=F32), axis=1, keepdims=True)
    tile_col = lax.broadcasted_iota(jnp.int32, (128, 1), 0)
    flag = jnp.logical_and(cnt > 0.0, tile_col < t0 // KT)
    flag_m = jnp.where(jnp.broadcast_to(flag, (128, 128)), 1.0, 0.0).astype(BF16)
    row_i = lax.broadcasted_iota(jnp.int32, (128, 128), 0)
    lane_i = lax.broadcasted_iota(jnp.int32, (128, 128), 1)
    before = jnp.dot((lane_i < row_i).astype(BF16), flag_m, preferred_element_type=F32)
    slot = jnp.where(jnp.logical_and(flag, before == lane_i.astype(F32)), 1.0, 0.0).astype(BF16)
    j_rows = lax.broadcasted_iota(jnp.int32, (8, 128), 1).astype(BF16)
    listed = jnp.dot(j_rows, slot, preferred_element_type=F32)
    total = jnp.dot(jnp.ones((8, 128), BF16), flag_m, preferred_element_type=F32)
    lane8 = lax.broadcasted_iota(jnp.int32, (8, 128), 1)
    tl_ref[0] = jnp.where(lane8 == TL_COUNT, total, listed).astype(jnp.int32)


def _nsa_select(h, kcmp, b, t):
    n_ch = t // CMP_STRIDE
    nq = t // QB
    steps = b * KV_HEADS * nq
    return pl.pallas_call(
        functools.partial(_nsa_select_kernel, n_ch=n_ch),
        grid=(b, KV_HEADS, nq),
        in_specs=[pl.BlockSpec((QB, 256), lambda i, g, c: (i * nq + c, C_QA // 256 + g)),
                  pl.BlockSpec((1, n_ch, 128), lambda i, g, c: (i, 0, g))],
        out_specs=[pl.BlockSpec((QB, 256), lambda i, g, c: (i * nq + c, g)),
                   pl.BlockSpec((QB, 128), lambda i, g, c: (i * nq + c, g)),
                   pl.BlockSpec((1, 8, 128), lambda i, g, c: ((i * KV_HEADS + g) * nq + c, 0, 0))],
        out_shape=[jax.ShapeDtypeStruct((b * t, Q_A), F32),
                   jax.ShapeDtypeStruct((b * t, KV_HEADS * 128), BF16),
                   jax.ShapeDtypeStruct((steps, 8, 128), jnp.int32)],
        compiler_params=_cparams(("arbitrary", "arbitrary", "arbitrary")),
        name="nsa_select",
    )(h, kcmp)


def _nsa_attend_kernel(tl_ref, q_ref, un_ref, gl_ref, oc_ref, ks_ref, kw_ref, o_ref,
                       ksb_ref, ksa_ref, kwb_ref, kwa_ref, *, nq):
    i = pl.program_id(0)
    g = pl.program_id(1)
    c = pl.program_id(2)
    step = (i * KV_HEADS + g) * nq + c
    t0 = c * QB
    half = ROWS // 2
    qs = _head_major(q_ref[...])
    slope_col = jnp.concatenate([jnp.broadcast_to(_head_slope(g, r), (QB, 1)) for r in range(GROUP)], axis=0)
    lane64 = lax.broadcasted_iota(jnp.int32, (ROWS, 64), 1)
    ali_q = jnp.where(lane64 == 0, -64.0 * slope_col, jnp.where(lane64 == 1, -slope_col, 0.0))
    q_win = jnp.concatenate([qs, ali_q], axis=1).astype(BF16)
    un = un_ref[...]
    q_sel = jnp.concatenate([q_win, jnp.concatenate([un] * GROUP, axis=0)], axis=1)
    slope_row = _slope_row(g)

    @pl.when(c == 0)
    def _():
        def fill(j, carry):
            r0 = pl.multiple_of(j * KT, KT)
            row = lax.broadcasted_iota(jnp.int32, (KT, 128), 0)
            lane = lax.broadcasted_iota(jnp.int32, (KT, 128), 1)
            for src, dst_kv, dst_aug, tile in ((ks_ref, ksb_ref, ksa_ref, KT), (kw_ref, kwb_ref, kwa_ref, WT)):
                kv_b = src[pl.ds(r0, KT), :].astype(BF16)
                d = (QB - 1) - row % tile
                ali = jnp.where(lane == 64, d >> 6, jnp.where(lane == 65, d & 63, 0)).astype(F32).astype(BF16)
                dst_kv[pl.ds(r0, KT), :] = kv_b
                dst_aug[pl.ds(r0, KT), 0:128] = jnp.where(lane < 64, kv_b, ali)
            ksa_ref[pl.ds(r0, KT), 128:256] = jnp.where((r0 + row) // SEL_BLOCK == lane, 1.0, 0.0).astype(BF16)
            return carry

        lax.fori_loop(0, ks_ref.shape[0] // KT, fill, 0)

    def update(carry, s, kv_b, offset):
        out = []
        for hh, ((m, l, acc), sh) in enumerate(zip(carry, s)):
            shift = slope_row[:, hh * half:(hh + 1) * half] * offset
            m_new = jnp.maximum(m, jnp.max(sh, axis=0, keepdims=True) - shift)
            alpha = jnp.exp(m - m_new)
            p = jnp.exp(sh - (m_new + shift))
            l = alpha * l + jnp.sum(p, axis=0, keepdims=True)
            acc = alpha * acc + lax.dot_general(kv_b, p.astype(BF16), _TN, preferred_element_type=F32)
            out.append((m_new, l, acc))
        return tuple(out)

    def init():
        return tuple((jnp.full((1, half), NEG, F32), jnp.zeros((1, half), F32), jnp.zeros((128, half), F32))
                     for _ in range(2))

    def tile(carry, aug_ref, kvb_ref, q_side, k0, n_keys, keep):
        k_aug = aug_ref[pl.ds(k0, n_keys), :]
        s = []
        for hh in range(2):
            sh = lax.dot_general(k_aug, q_side[hh * half:(hh + 1) * half], _NT, preferred_element_type=F32)
            if keep is not None:
                sh = jnp.where(jnp.concatenate([keep] * (half // QB), axis=1), sh, NEG)
            s.append(sh)
        return update(carry, s, kvb_ref[pl.ds(k0, n_keys), :], (t0 - k0).astype(F32))

    k_tail = pl.multiple_of((t0 // KT) * KT, KT)
    key_x = lax.broadcasted_iota(jnp.int32, (KT, QB), 0)
    q_x = lax.broadcasted_iota(jnp.int32, (KT, QB), 1)
    carry = tile(init(), ksa_ref, ksb_ref, q_sel, k_tail, KT, k_tail + key_x <= t0 + q_x)

    def sel_listed(n, carry):
        return tile(carry, ksa_ref, ksb_ref, q_sel, pl.multiple_of(tl_ref[step, n] * KT, KT), KT, None)

    n_listed = tl_ref[step, TL_COUNT]
    carry = lax.fori_loop(0, n_listed // 2, lambda n, cr: sel_listed(2 * n + 1, sel_listed(2 * n, cr)), carry)
    carry = lax.fori_loop(0, n_listed % 2, lambda n, cr: sel_listed(n_listed - 1, cr), carry)
    o_sel = [acc * (1.0 / l) for (_, l, acc) in carry]

    w_keys = WINDOW + QB
    k0w = pl.multiple_of(jnp.maximum(t0 - WINDOW, 0), WT)
    k_aug = kwa_ref[pl.ds(k0w, w_keys), :]
    kv_w = kwb_ref[pl.ds(k0w, w_keys), :]
    dq = lax.broadcasted_iota(jnp.int32, (WT, QB), 1) - lax.broadcasted_iota(jnp.int32, (WT, QB), 0)
    offs = [t0 - (k0w + gi * WT) for gi in range(w_keys // WT)]
    keeps = [jnp.concatenate([jnp.logical_and(dq + off >= 0, dq + off < WINDOW)] * (half // QB), axis=1)
             for off in offs]
    o_win = []
    for hh in range(2):
        sl = slope_row[:, hh * half:(hh + 1) * half]
        sh = lax.dot_general(k_aug, q_win[hh * half:(hh + 1) * half], _NT, preferred_element_type=F32)
        parts = [jnp.where(keep, sh[gi * WT:(gi + 1) * WT], NEG) for gi, keep in enumerate(keeps)]
        shifts = [sl * off.astype(F32) for off in offs]
        m_w = functools.reduce(jnp.maximum, [jnp.max(pt, axis=0, keepdims=True) - sf
                                             for pt, sf in zip(parts, shifts)])
        ps = [jnp.exp(pt - (m_w + sf)) for pt, sf in zip(parts, shifts)]
        l_w = functools.reduce(jnp.add, [jnp.sum(p, axis=0, keepdims=True) for p in ps])
        acc = lax.dot_general(kv_w, jnp.concatenate(ps, axis=0).astype(BF16), _TN, preferred_element_type=F32)
        o_win.append(acc * (1.0 / l_w))

    gates = jax.nn.sigmoid(gl_ref[...])
    o_c = oc_ref[...]
    for r in range(GROUP):
        hh, cs = divmod(r * QB, half)
        o_s = jnp.transpose(o_sel[hh][:, cs:cs + QB])[:, 64:128]
        o_w = jnp.transpose(o_win[hh][:, cs:cs + QB])[:, 64:128]
        o_ref[:, r * 64:(r + 1) * 64] = (gates[:, 3 * r:3 * r + 1] * o_c[:, r * 64:(r + 1) * 64]
                                         + gates[:, 3 * r + 1:3 * r + 2] * o_s
                                         + gates[:, 3 * r + 2:3 * r + 3] * o_w)


def _nsa_attend(tiles, h, unsel, o_c, b, t):
    nq = t // QB
    grid_spec = pltpu.PrefetchScalarGridSpec(
        num_scalar_prefetch=1,
        grid=(b, KV_HEADS, nq),
        in_specs=[pl.BlockSpec((QB, 256), lambda i, g, c, tl: (i * nq + c, C_QA // 256 + g)),
                  pl.BlockSpec((QB, 128), lambda i, g, c, tl: (i * nq + c, g)),
                  pl.BlockSpec((QB, 128), lambda i, g, c, tl: (i * nq + c, C_GL // 128 + g)),
                  pl.BlockSpec((QB, 256), lambda i, g, c, tl: (i * nq + c, g)),
                  pl.BlockSpec((t, 128), lambda i, g, c, tl: (i, C_KVS // 128 + g)),
                  pl.BlockSpec((t, 128), lambda i, g, c, tl: (i, C_KVW // 128 + g))],
        out_specs=pl.BlockSpec((QB, 256), lambda i, g, c, tl: (i * nq + c, g)),
        scratch_shapes=[pltpu.VMEM((t, 128), BF16), pltpu.VMEM((t, 256), BF16),
                        pltpu.VMEM((t, 128), BF16), pltpu.VMEM((t, 128), BF16)],
    )
    return pl.pallas_call(
        functools.partial(_nsa_attend_kernel, nq=nq),
        grid_spec=grid_spec,
        out_shape=jax.ShapeDtypeStruct((b * t, Q_A), F32),
        compiler_params=_cparams(("arbitrary", "arbitrary", "arbitrary")),
        name="nsa_attend",
    )(tiles, h, unsel, h, o_c, h, h)


def _nsa_prompt(h, kcmp, b, t):
    o_c, unsel, tl = _nsa_select(h, kcmp, b, t)
    return _nsa_attend(tl[:, 0, :], h, unsel, o_c, b, t)


HB = 256


def _lower_bound(lb_ref):
    z = lb_ref[...]
    e = jnp.exp(z - jnp.max(z, axis=0, keepdims=True))
    return e[0:1] / jnp.sum(e, axis=0, keepdims=True)


def _hgrn_prompt_kernel(q_ref, f_ref, i_ref, g_ref, lb_ref, ng_ref, o_ref, s_out_ref, s_ref):
    tb = pl.program_id(1)

    @pl.when(tb == 0)
    def _():
        s_ref[...] = jnp.zeros_like(s_ref)

    n_c = HB // HGRN_CHUNK
    lb = _lower_bound(lb_ref)
    f = lb + (1.0 - lb) * jax.nn.sigmoid(f_ref[...])
    log_f = jnp.log(f)
    row = lax.broadcasted_iota(jnp.int32, (HB, HB), 0)
    col = lax.broadcasted_iota(jnp.int32, (HB, HB), 1)
    tril = jnp.logical_and(col <= row, row // HGRN_CHUNK == col // HGRN_CHUNK)
    lc = _dot01(tril.astype(BF16), log_f)
    lc3 = lc.reshape(n_c, HGRN_CHUNK, Q_B)
    lend = jnp.broadcast_to(lc3[:, HGRN_CHUNK - 1:HGRN_CHUNK, :], lc3.shape).reshape(HB, Q_B)
    q_t = (q_ref[...] * jnp.exp(lc)).astype(BF16)
    k = 1.0 - f
    k_t = (k * jnp.exp(-lc)).astype(BF16)
    k_e = (k * jnp.exp(lend - lc)).astype(BF16)
    dec = jnp.exp(lend)
    v_all = i_ref[...]
    gate = g_ref[...]
    ng = ng_ref[...]
    for hh in range(B_HEADS):
        ks = slice(hh * B_DK, (hh + 1) * B_DK)
        vs = slice(hh * B_DV, (hh + 1) * B_DV)
        v = v_all[:, vs].astype(BF16)
        a = lax.dot_general(q_t[:, ks], k_t[:, ks], _NT, preferred_element_type=F32)
        a = jnp.where(tril, a, 0.0).astype(BF16)
        o = jnp.dot(a, v, preferred_element_type=F32)
        dec_t = jnp.transpose(dec[:, ks])
        st = s_ref[hh]
        inter = []
        for cc in range(n_c):
            rs = slice(cc * HGRN_CHUNK, (cc + 1) * HGRN_CHUNK)
            inter.append(jnp.dot(q_t[rs, ks], st.astype(BF16), preferred_element_type=F32))
            u = lax.dot_general(k_e[rs, ks], v[rs], _TN, preferred_element_type=F32)
            last = (cc + 1) * HGRN_CHUNK - 1
            st = dec_t[:, last:last + 1] * st + u
        s_ref[hh] = st
        o = o + jnp.concatenate(inter, axis=0)
        o = o * lax.rsqrt(jnp.mean(o * o, axis=-1, keepdims=True) + LN_EPS) * ng
        gt = gate[:, vs]
        o_ref[:, vs] = o * (gt * jax.nn.sigmoid(gt))

    @pl.when(tb == pl.num_programs(1) - 1)
    def _():
        s_out_ref[0] = s_ref[...]


def _hgrn_prompt(h, b, t, lb_logits, norm_g):
    nt = t // HB
    return pl.pallas_call(
        _hgrn_prompt_kernel,
        grid=(b, nt),
        in_specs=[pl.BlockSpec((HB, Q_B), lambda i, j: (i * nt + j, C_QB // Q_B)),
                  pl.BlockSpec((HB, Q_B), lambda i, j: (i * nt + j, C_FB // Q_B)),
                  pl.BlockSpec((HB, I_B), lambda i, j: (i * nt + j, C_IB // I_B)),
                  pl.BlockSpec((HB, I_B), lambda i, j: (i * nt + j, C_GB // I_B)),
                  pl.BlockSpec((2, Q_B), lambda i, j: (0, 0)),
                  pl.BlockSpec((1, B_DV), lambda i, j: (0, 0))],
        out_specs=[pl.BlockSpec((HB, I_B), lambda i, j: (i * nt + j, 0)),
                   pl.BlockSpec((1, B_HEADS, B_DK, B_DV), lambda i, j: (i, 0, 0, 0))],
        out_shape=[jax.ShapeDtypeStruct((b * t, I_B), F32),
                   jax.ShapeDtypeStruct((b, B_HEADS, B_DK, B_DV), F32)],
        scratch_shapes=[pltpu.VMEM((B_HEADS, B_DK, B_DV), F32)],
        compiler_params=_cparams(("arbitrary", "arbitrary")),
        name="hgrn_prompt",
    )(h, h, h, h, lb_logits, norm_g)


def _prompt_layer(x, w_pad, cmpw, lb_logits, norm_g, w_a, w_b, w_o, ln1_g, ln1_b, w_up, w_down, ln2_g, ln2_b):
    b, t, _ = x.shape
    x2d = x.reshape(b * t, D_MODEL)
    h = _project(x2d, w_pad)
    kcmp = _cmp_prompt(h, b, t, *cmpw)
    o_a = _nsa_prompt(h, kcmp, b, t)
    o_b, s_end = _hgrn_prompt(h, b, t, lb_logits, norm_g)
    x1 = _merge(x2d, o_a, o_b, h, w_a, w_b, w_o, ln1_g, ln1_b)
    y = _mlp(x1, w_up, w_down, ln2_g, ln2_b)
    return y.reshape(b, t, D_MODEL), h, s_end


def _kv_out(h, col, b, t):
    kv = h[:, col:col + 256].reshape(b, t, KV_HEADS, 2, HEAD_DIM)
    return jnp.swapaxes(kv, 2, 3)


LAND_PAGES = 4
CH_PER_PAGE = PAGE_SIZE // CMP_STRIDE
BLK_PER_PAGE = PAGE_SIZE // SEL_BLOCK
N_POOL_SEL = N_SELECT - 1


def _head_rows(q_row, g):
    rows = [q_row[:, g * 256 + r * 64:g * 256 + (r + 1) * 64] for r in range(GROUP)]
    return jnp.concatenate(rows + [jnp.zeros((8 - GROUP, 64), F32)], axis=0) * ATTN_SCALE


def _slope_col(g):
    row = lax.broadcasted_iota(jnp.int32, (8, 1), 0)
    col = jnp.zeros((8, 1), F32)
    for r in range(GROUP):
        col = jnp.where(row == r, 2.0 ** -(g * GROUP + r + 1), col)
    return col


def _cmp_sample_kernel(pt_ref, cache_ref, w_ref, posf_ref, wb_ref, q_ref, oc_ref, idx_ref,
                       buf_ref, xk_ref, xv_ref, sem, *, n_pages, past):
    b = pl.program_id(0)
    n_ch = n_pages * CH_PER_PAGE
    n_blk = past // SEL_BLOCK
    lanes = idx_ref.shape[-1]

    def page_copy(pg, page):
        return pltpu.make_async_copy(cache_ref.at[page], buf_ref.at[pg], sem.at[pg])

    def issue(pg, carry):
        page_copy(pg, pt_ref[b, pg]).start()
        return carry

    def land(i, carry):
        pages = [i * LAND_PAGES + u for u in range(LAND_PAGES)]
        for pg in pages:
            page_copy(pg, 0).wait()
        for pg in pages:
            r0 = pl.multiple_of(pg * PAGE_SIZE, PAGE_SIZE)
            xk_ref[pl.ds(r0, PAGE_SIZE), :] = jnp.transpose(buf_ref[pg, 0].reshape(128, PAGE_SIZE))
            xv_ref[pl.ds(r0, PAGE_SIZE), :] = jnp.transpose(buf_ref[pg, 1].reshape(128, PAGE_SIZE))
        return carry

    lax.fori_loop(0, n_pages, issue, 0)
    lax.fori_loop(0, n_pages // LAND_PAGES, land, 0)

    bias = jnp.dot(posf_ref[...].astype(BF16), wb_ref[...], preferred_element_type=F32)[0:1]
    blocks = []
    for c, x_ref in enumerate((xk_ref, xv_ref)):
        acc = jnp.zeros((n_ch, 256), F32)
        for l2 in range(CMP_STRIDE // 2):
            xl = jnp.concatenate([x_ref[pl.ds(2 * l2 + u, n_ch, stride=CMP_STRIDE), :] for u in range(2)],
                                 axis=1).astype(BF16)
            acc += jnp.dot(xl, w_ref[c, l2], preferred_element_type=F32)
        bias_c = jnp.concatenate([bias[:, c * 64:(c + 1) * 64]] * KV_HEADS, axis=1)
        blocks.append((acc[:, :128] + pltpu.roll(acc[:, 128:], n_ch - 1, 0) + bias_c).astype(BF16))
    kc_all, vc_all = blocks

    q_row = q_ref[0]
    n_ix = lax.broadcasted_iota(jnp.int32, (1, n_ch), 1)
    dist_i = past - (n_ix * CMP_STRIDE + (CMP_LEN - 1))
    valid = jnp.logical_and(dist_i >= 0, n_ix < n_ch - 1)
    dist = dist_i.astype(F32)
    ratio = SEL_BLOCK // CMP_STRIDE
    pool = (lax.broadcasted_iota(jnp.int32, (n_ch, lanes), 0) // ratio
            == lax.broadcasted_iota(jnp.int32, (n_ch, lanes), 1)).astype(BF16)
    row8 = lax.broadcasted_iota(jnp.int32, (8, lanes), 0)
    score = jnp.full((8, lanes), -FORCE, F32)
    lane_g = lax.broadcasted_iota(jnp.int32, (8, 128), 1) // HEAD_DIM
    o_c = jnp.zeros((8, 128), F32)
    for g in range(KV_HEADS):
        pieces = [jnp.zeros((8, 64), F32)] * KV_HEADS
        pieces[g] = _head_rows(q_row, g)
        qpad = jnp.concatenate(pieces, axis=1).astype(BF16)
        s = lax.dot_general(qpad, kc_all, _NT, preferred_element_type=F32)
        s = jnp.where(valid, s - _slope_col(g) * dist, NEG)
        m = jnp.max(s, axis=-1, keepdims=True)
        p = jnp.where(valid, jnp.exp(s - m), 0.0)
        p = p / jnp.maximum(jnp.sum(p, axis=-1, keepdims=True), 1e-30)
        o_c = jnp.where(lane_g == g, jnp.dot(p.astype(BF16), vc_all, preferred_element_type=F32), o_c)
        imp = p[0:1] + p[1:2] + p[2:3] + p[3:4]
        imp_blk = _x_dot01(jnp.broadcast_to(imp, (8, n_ch)), pool)
        score = jnp.where(row8 == g, imp_blk, score)
    lane = lax.broadcasted_iota(jnp.int32, (8, lanes), 1)
    forced = jnp.logical_or(lane == 0, lane == n_blk - 1)
    score = jnp.where(lane < n_blk, jnp.where(forced, FORCE, score), -jnp.inf)
    out = jnp.zeros((8, lanes), jnp.int32)
    for r in range(N_POOL_SEL):
        m = jnp.max(score, axis=-1, keepdims=True)
        first = jnp.min(jnp.where(score == m, lane, lanes), axis=-1, keepdims=True)
        out = jnp.where(lane == r, first, out)
        score = jnp.where(lane == first, -jnp.inf, score)
    idx_ref[0] = out
    oc_ref[0] = o_c


def _cmp_sample(page_table, cache5, w_s, posf, wb, q3, past):
    bsz, n_pages = page_table.shape
    n_ch = n_pages * CH_PER_PAGE
    lanes = -(-(past // SEL_BLOCK) // 128) * 128
    grid_spec = pltpu.PrefetchScalarGridSpec(
        num_scalar_prefetch=1,
        grid=(bsz,),
        in_specs=[pl.BlockSpec(memory_space=pl.ANY),
                  pl.BlockSpec((2, CMP_STRIDE // 2, 256, 256), lambda i, pt: (0, 0, 0, 0)),
                  pl.BlockSpec((8, 4096), lambda i, pt: (0, 0)),
                  pl.BlockSpec((4096, 128), lambda i, pt: (0, 0)),
                  pl.BlockSpec((1, 1, Q_A), lambda i, pt: (i, 0, 0))],
        out_specs=[pl.BlockSpec((1, 8, 128), lambda i, pt: (i, 0, 0)),
                   pl.BlockSpec((1, 8, lanes), lambda i, pt: (i, 0, 0))],
        scratch_shapes=[pltpu.VMEM((n_pages, 2, KV_HEADS, HEAD_DIM, PAGE_SIZE), F32),
                        pltpu.VMEM((n_pages * PAGE_SIZE, 128), F32),
                        pltpu.VMEM((n_pages * PAGE_SIZE, 128), F32),
                        pltpu.SemaphoreType.DMA((n_pages,))],
    )
    return pl.pallas_call(
        functools.partial(_cmp_sample_kernel, n_pages=n_pages, past=past),
        grid_spec=grid_spec,
        out_shape=[jax.ShapeDtypeStruct((bsz, 8, 128), F32),
                   jax.ShapeDtypeStruct((bsz, 8, lanes), jnp.int32)],
        compiler_params=_cparams(("arbitrary",)),
        name="cmp_sample",
    )(page_table, cache5, w_s, posf, wb, q3)


def _pick_lane(mat, lane, target):
    return jnp.sum(jnp.where(lane == target, mat, 0.0), axis=-1, keepdims=True)


def _sel_win_sample_kernel(pt_ref, idx_ref, cache_ref, win_ref, q_ref, ks_ref, kw_ref, gl_ref, oc_ref, o_ref,
                           kbuf_ref, vbuf_ref, sem, *, past):
    b = pl.program_id(0)
    slot_lanes = N_SELECT * PAGE_SIZE

    def page_copies(g, k, page):
        dst = pl.ds(k * PAGE_SIZE, PAGE_SIZE)
        return (pltpu.make_async_copy(cache_ref.at[page, 0, g], kbuf_ref.at[g, :, dst], sem),
                pltpu.make_async_copy(cache_ref.at[page, 1, g], vbuf_ref.at[g, :, dst], sem))

    for g in range(KV_HEADS):
        for k in range(N_POOL_SEL):
            page = pt_ref[b, idx_ref[b, g * N_SELECT + k] // BLK_PER_PAGE]
            for cp in page_copies(g, k, page):
                cp.start()
        pad = pl.ds(N_POOL_SEL * PAGE_SIZE, PAGE_SIZE)
        kbuf_ref[g, :, pad] = jnp.zeros((HEAD_DIM, PAGE_SIZE), F32)
        vbuf_ref[g, :, pad] = jnp.zeros((HEAD_DIM, PAGE_SIZE), F32)

    q_row = q_ref[0]
    ks_new = ks_ref[0]
    kw_new = kw_ref[0]
    gl_all = jax.nn.sigmoid(gl_ref[0])
    lane128 = lax.broadcasted_iota(jnp.int32, (8, 128), 1)
    row8 = lax.broadcasted_iota(jnp.int32, (8, 1), 0)
    w_len = win_ref.shape[-1]

    def two_piece(qh, slope, k_t, v_t, dist_i, valid, k_new, v_new):
        s = jnp.dot(qh.astype(BF16), k_t.astype(BF16), preferred_element_type=F32)
        s = jnp.where(valid, s - slope * dist_i.astype(F32), NEG)
        s_n = jnp.sum(qh * k_new, axis=-1, keepdims=True)
        m = jnp.maximum(jnp.max(s, axis=-1, keepdims=True), s_n)
        p = jnp.where(valid, jnp.exp(s - m), 0.0)
        p_n = jnp.exp(s_n - m)
        l = jnp.sum(p, axis=-1, keepdims=True) + p_n
        o = lax.dot_general(p.astype(BF16), v_t.astype(BF16), _NT, preferred_element_type=F32) + p_n * v_new
        return o / l

    outs = []
    for g in range(KV_HEADS):
        qh = _head_rows(q_row, g)
        slope = _slope_col(g)
        j_ix = lax.broadcasted_iota(jnp.int32, (1, w_len), 1)
        dist_w = w_len - j_ix
        o_w = two_piece(qh, slope, win_ref[0, 0, g], win_ref[0, 1, g], dist_w, dist_w < WINDOW,
                        kw_new[:, g * 128:g * 128 + 64], kw_new[:, g * 128 + 64:(g + 1) * 128])
        outs.append((qh, slope, o_w))

    for g in range(KV_HEADS):
        for k in range(N_POOL_SEL):
            for cp in page_copies(g, k, 0):
                cp.wait()

    lane_s = lax.broadcasted_iota(jnp.int32, (1, slot_lanes), 1)
    o_all = []
    for g in range(KV_HEADS):
        qh, slope, o_w = outs[g]
        pos_k = jnp.full((1, slot_lanes), past + 1, jnp.int32)
        for k in range(N_POOL_SEL):
            blk = idx_ref[b, g * N_SELECT + k]
            r = lane_s % PAGE_SIZE
            in_blk = jnp.logical_and(lane_s // PAGE_SIZE == k, r // SEL_BLOCK == blk % BLK_PER_PAGE)
            pos_k = jnp.where(in_blk, (blk // BLK_PER_PAGE) * PAGE_SIZE + r, pos_k)
        dist_s = past - pos_k
        o_s = two_piece(qh, slope, kbuf_ref[g], vbuf_ref[g], dist_s, dist_s >= 0,
                        ks_new[:, g * 128:g * 128 + 64], ks_new[:, g * 128 + 64:(g + 1) * 128])
        o_c = oc_ref[0][:, g * 64:(g + 1) * 64]
        gates = jnp.broadcast_to(gl_all[:, g * 128:(g + 1) * 128], (8, 128))
        g_c = _pick_lane(gates, lane128, 3 * row8)
        g_s = _pick_lane(gates, lane128, 3 * row8 + 1)
        g_w = _pick_lane(gates, lane128, 3 * row8 + 2)
        o_all.append(g_c * o_c + g_s * o_s + g_w * o_w)
    o_ref[0] = jnp.concatenate(o_all, axis=1)


def _sel_win_sample(page_table, idx, cache_sel5, cache_win5, q3, ks3, kw3, gl3, o_c, past):
    bsz = page_table.shape[0]
    w_len = cache_win5.shape[-1]
    grid_spec = pltpu.PrefetchScalarGridSpec(
        num_scalar_prefetch=2,
        grid=(bsz,),
        in_specs=[pl.BlockSpec(memory_space=pl.ANY),
                  pl.BlockSpec((1, 2, KV_HEADS, HEAD_DIM, w_len), lambda i, pt, ix: (i, 0, 0, 0, 0)),
                  pl.BlockSpec((1, 1, Q_A), lambda i, pt, ix: (i, 0, 0)),
                  pl.BlockSpec((1, 1, 256), lambda i, pt, ix: (i, 0, 0)),
                  pl.BlockSpec((1, 1, 256), lambda i, pt, ix: (i, 0, 0)),
                  pl.BlockSpec((1, 1, 256), lambda i, pt, ix: (i, 0, 0)),
                  pl.BlockSpec((1, 8, 128), lambda i, pt, ix: (i, 0, 0))],
        out_specs=pl.BlockSpec((1, 8, 128), lambda i, pt, ix: (i, 0, 0)),
        scratch_shapes=[pltpu.VMEM((KV_HEADS, HEAD_DIM, N_SELECT * PAGE_SIZE), F32),
                        pltpu.VMEM((KV_HEADS, HEAD_DIM, N_SELECT * PAGE_SIZE), F32),
                        pltpu.SemaphoreType.DMA(())],
    )
    return pl.pallas_call(
        functools.partial(_sel_win_sample_kernel, past=past),
        grid_spec=grid_spec,
        out_shape=jax.ShapeDtypeStruct((bsz, 8, 128), F32),
        compiler_params=_cparams(("arbitrary",)),
        name="sel_win_sample",
    )(page_table, idx, cache_sel5, cache_win5, q3, ks3, kw3, gl3, o_c)


def _hgrn_sample_kernel(q_ref, f_ref, v_ref, g_ref, lb_ref, ng_ref, s_ref, o_ref, s_out_ref):
    z = lb_ref[...]
    e = jnp.exp(z - jnp.max(z, axis=0, keepdims=True))
    lb = e[0] / jnp.sum(e, axis=0)
    f = lb + (1.0 - lb) * jax.nn.sigmoid(f_ref[0])
    decay = jnp.exp(jnp.log(f))
    k = 1.0 - f
    q = q_ref[0]
    v = v_ref[0]
    gate = g_ref[0]
    ng = ng_ref[...]
    for hh in range(B_HEADS):
        hs = slice(hh, hh + 1)
        s_new = decay[hs] * s_ref[0, hh] + v[:, hs] * k[hs]
        s_out_ref[0, hh] = s_new
        o = jnp.sum(q[hs] * s_new, axis=1, keepdims=True)
        o = o * lax.rsqrt(jnp.mean(o * o, axis=0, keepdims=True) + LN_EPS) * ng
        gt = gate[:, hs]
        o_ref[0, :, hs] = o * (gt * jax.nn.sigmoid(gt))


def _hgrn_sample(q_hk, f_hk, v_vh, g_vh, lb_hk, ng_col, state_t):
    bsz = q_hk.shape[0]
    return pl.pallas_call(
        _hgrn_sample_kernel,
        grid=(bsz,),
        in_specs=[pl.BlockSpec((1, B_HEADS, B_DK), lambda i: (i, 0, 0)),
                  pl.BlockSpec((1, B_HEADS, B_DK), lambda i: (i, 0, 0)),
                  pl.BlockSpec((1, B_DV, B_HEADS), lambda i: (i, 0, 0)),
                  pl.BlockSpec((1, B_DV, B_HEADS), lambda i: (i, 0, 0)),
                  pl.BlockSpec((2, B_HEADS, B_DK), lambda i: (0, 0, 0)),
                  pl.BlockSpec((B_DV, 1), lambda i: (0, 0)),
                  pl.BlockSpec((1, B_HEADS, B_DV, B_DK), lambda i: (i, 0, 0, 0))],
        out_specs=[pl.BlockSpec((1, B_DV, B_HEADS), lambda i: (i, 0, 0)),
                   pl.BlockSpec((1, B_HEADS, B_DV, B_DK), lambda i: (i, 0, 0, 0))],
        out_shape=[jax.ShapeDtypeStruct((bsz, B_DV, B_HEADS), F32),
                   jax.ShapeDtypeStruct((bsz, B_HEADS, B_DV, B_DK), F32)],
        compiler_params=_cparams(("arbitrary",)),
        name="hgrn_sample",
    )(q_hk, f_hk, v_vh, g_vh, lb_hk, ng_col, state_t)


def _rows_last(cache):
    return jnp.moveaxis(cache, -4, -1)


def _sample_layer(x, cache_cmp, cache_sel, cache_win, state, page_table, w_pad, cmpw, w_s, lb_logits, norm_g,
                  w_a, w_b, w_o, ln1_g, ln1_b, w_up, w_down, ln2_g, ln2_b):
    bsz, t, _ = x.shape
    assert t == 1, "the sample group decodes one token per request"
    n_pages = page_table.shape[1]
    past = n_pages * PAGE_SIZE
    x2d = x.reshape(bsz, D_MODEL)
    h = _project(x2d, w_pad)
    h3 = h.reshape(bsz, 1, D_PAD)
    q3 = h3[:, :, C_QA:C_QA + Q_A]
    wl, wb, posf = cmpw
    o_c, idx = _cmp_sample(page_table, _rows_last(cache_cmp), w_s, posf, wb, q3, past)
    idx2 = jnp.pad(idx[:, :KV_HEADS, :N_POOL_SEL], ((0, 0), (0, 0), (0, 1))).reshape(bsz, KV_HEADS * N_SELECT)
    o_rd = _sel_win_sample(page_table, idx2, _rows_last(cache_sel), _rows_last(cache_win),
                           q3, h3[:, :, C_KVS:C_KVS + 256], h3[:, :, C_KVW:C_KVW + 256],
                           h3[:, :, C_GL:C_GL + 256], o_c, past)
    o_a = jnp.swapaxes(o_rd[:, :GROUP].reshape(bsz, GROUP, KV_HEADS, HEAD_DIM), 1, 2).reshape(bsz, Q_A)
    to_hk = lambda a: a.reshape(-1, B_HEADS, B_DK)
    to_vh = lambda a: jnp.swapaxes(a.reshape(-1, B_HEADS, B_DV), 1, 2)
    o_vh, s_t = _hgrn_sample(to_hk(h[:, C_QB:C_QB + Q_B]), to_hk(h[:, C_FB:C_FB + Q_B]),
                             to_vh(h[:, C_IB:C_IB + I_B]), to_vh(h[:, C_GB:C_GB + I_B]),
                             to_hk(lb_logits), norm_g.reshape(B_DV, 1), jnp.swapaxes(state, 2, 3))
    o_b = jnp.swapaxes(o_vh, 1, 2).reshape(bsz, I_B)
    x1 = _merge(x2d, o_a, o_b, h, w_a, w_b, w_o, ln1_g, ln1_b)
    y = _mlp(x1, w_up, w_down, ln2_g, ln2_b)
    return y.reshape(bsz, 1, D_MODEL), h, jnp.swapaxes(s_t, 2, 3)


def _cmp_sample_weights(cmp_w):
    w = cmp_w.reshape(2, 2, CMP_STRIDE, HEAD_DIM, HEAD_DIM)
    ws = jnp.zeros((2, CMP_STRIDE, 2, HEAD_DIM, 2, 2, HEAD_DIM), F32)
    for g in range(2):
        ws = ws.at[:, :, g, :, :, g, :].set(jnp.transpose(w, (0, 2, 3, 1, 4)))
    return ws.reshape(2, CMP_STRIDE // 2, 256, 256).astype(BF16)


def kernel(x_prompt, x_sample, cache_cmp_kv, cache_sel_kv, cache_win_kv, state_hgrn, page_table,
           w_in, cmp_w, cmp_pos, hgrn_lb_logits, hgrn_norm_g, w_br_a, w_br_b, w_out,
           ln1_g, ln1_b, w_up, w_down, ln2_g, ln2_b):
    assert w_in.shape[0] == 1, "one layer"
    b, t, _ = x_prompt.shape
    bsz = x_sample.shape[0]
    assert t % HB == 0 and t // SEL_BLOCK <= 128
    perm = jnp.asarray(np.maximum(_PERM, 0))
    w_pad = jnp.where(jnp.asarray(_PERM >= 0)[None, :], jnp.take(w_in[0], perm, axis=1), 0.0).astype(BF16)
    cmpw = _cmp_weights(cmp_w[0], cmp_pos[0])
    w_s = _cmp_sample_weights(cmp_w[0])
    dense = (w_br_a[0].astype(BF16), w_br_b[0].astype(BF16), w_out[0].astype(BF16), ln1_g, ln1_b,
             w_up[0].astype(BF16), w_down[0].astype(BF16), ln2_g, ln2_b)

    y_p, h_p, s_p = _prompt_layer(x_prompt, w_pad, cmpw, hgrn_lb_logits, hgrn_norm_g, *dense)
    y_s, h_s, s_s = _sample_layer(x_sample, cache_cmp_kv[0], cache_sel_kv[0], cache_win_kv[0], state_hgrn[0],
                                  page_table, w_pad, cmpw, w_s, hgrn_lb_logits, hgrn_norm_g, *dense)

    win_p = min(WINDOW, t)
    kvw_p = _kv_out(h_p, C_KVW, b, t)
    kvw_s = _kv_out(h_s, C_KVW, bsz, 1)
    new_win_s = jnp.concatenate([cache_win_kv[0], kvw_s], axis=1)[:, -min(WINDOW, cache_win_kv.shape[2] + 1):]
    return (y_p, y_s,
            _kv_out(h_p, C_KVC, b, t)[None], _kv_out(h_p, C_KVS, b, t)[None], kvw_p[:, -win_p:][None], s_p[None],
            _kv_out(h_s, C_KVC, bsz, 1)[None], _kv_out(h_s, C_KVS, bsz, 1)[None], new_win_s[None], s_s[None])
```

```python
import functools

import numpy as np
import jax
import jax.numpy as jnp
from jax import lax
from jax.experimental import pallas as pl
from jax.experimental.pallas import tpu as pltpu

F32 = jnp.float32
BF16 = jnp.bfloat16

D_MODEL = 1024
HEAD_DIM = 64
A_HEADS = 8
KV_HEADS = 2
GROUP = A_HEADS // KV_HEADS
CMP_STRIDE = 16
CMP_LEN = 32
SEL_BLOCK = 64
N_SELECT = 16
WINDOW = 512
PAGE_SIZE = 128
B_HEADS = 8
B_DK = 128
B_DV = 64
HGRN_CHUNK = 32
D_FF = 4 * D_MODEL
DEEPNORM_ALPHA = 2.0 ** 0.25
LN_EPS = 1e-5
NEG = -1e30
FORCE = 1e6
ATTN_SCALE = HEAD_DIM ** -0.5
Q_A = A_HEADS * HEAD_DIM
KV_A = 2 * KV_HEADS * HEAD_DIM
GATE_A = 3 * A_HEADS
Q_B = B_HEADS * B_DK
I_B = B_HEADS * B_DV

C_MG = 0
C_QB = 2048
C_FB = 3072
C_QA = 4096
C_IB = 4608
C_GB = 5120
C_KVC = 5632
C_KVS = 5888
C_KVW = 6144
C_GL = 6400
D_PAD = 6656
PROJ_TN = 1664

VMEM_LIMIT = 56 * 1024 * 1024
UNSEL = float(2.0 ** 100)


def _cparams(sem):
    return pltpu.CompilerParams(dimension_semantics=sem, vmem_limit_bytes=VMEM_LIMIT)


def _proj_perm():
    perm = np.full((D_PAD,), -1, np.int64)
    o_qa, o_kvc, o_kvs, o_kvw = 0, Q_A, Q_A + KV_A, Q_A + 2 * KV_A
    o_gl = Q_A + 3 * KV_A
    o_qb = o_gl + GATE_A
    o_fb = o_qb + Q_B
    o_ib = o_fb + Q_B
    o_gb = o_ib + I_B
    o_mg = o_gb + I_B
    perm[C_MG:C_MG + 2 * D_MODEL] = o_mg + np.arange(2 * D_MODEL)
    perm[C_QB:C_QB + Q_B] = o_qb + np.arange(Q_B)
    perm[C_FB:C_FB + Q_B] = o_fb + np.arange(Q_B)
    perm[C_QA:C_QA + Q_A] = o_qa + np.arange(Q_A)
    perm[C_IB:C_IB + I_B] = o_ib + np.arange(I_B)
    perm[C_GB:C_GB + I_B] = o_gb + np.arange(I_B)
    for new, old in ((C_KVC, o_kvc), (C_KVS, o_kvs), (C_KVW, o_kvw)):
        for g in range(KV_HEADS):
            for c in range(2):
                dst = new + g * 128 + c * 64
                src = old + c * 128 + g * 64
                perm[dst:dst + 64] = src + np.arange(64)
    for g in range(KV_HEADS):
        perm[C_GL + g * 128:C_GL + g * 128 + 12] = o_gl + g * 12 + np.arange(12)
    return perm


def _perm_runs(perm):
    runs = []
    for col in perm:
        if runs and ((col < 0 and runs[-1][0] < 0) or (col >= 0 and runs[-1][0] >= 0
                                                      and runs[-1][0] + runs[-1][1] == col)):
            runs[-1][1] += 1
        else:
            runs.append([int(col), 1])
    return [tuple(r) for r in runs]


_PERM_RUNS = _perm_runs(_proj_perm())


def _reorder_w_in(w):
    pieces = [jnp.zeros((w.shape[0], n), w.dtype) if s < 0 else w[:, s:s + n] for s, n in _PERM_RUNS]
    return jnp.concatenate(pieces, axis=1).astype(BF16)


def _proj_kernel(x_ref, w_ref, o_ref):
    o_ref[...] = jnp.dot(x_ref[...].astype(BF16), w_ref[...], preferred_element_type=F32)


def _project(x2d, w_pad):
    n = x2d.shape[0]
    tm = min(512, n)
    return pl.pallas_call(
        _proj_kernel,
        grid=(D_PAD // PROJ_TN, n // tm),
        in_specs=[pl.BlockSpec((tm, D_MODEL), lambda j, i: (i, 0)),
                  pl.BlockSpec((D_MODEL, PROJ_TN), lambda j, i: (0, j))],
        out_specs=pl.BlockSpec((tm, PROJ_TN), lambda j, i: (i, j)),
        out_shape=jax.ShapeDtypeStruct((n, D_PAD), F32),
        compiler_params=_cparams(("arbitrary", "arbitrary")),
        name="proj",
    )(x2d, w_pad)


def _layer_norm(v, g, b):
    mu = jnp.mean(v, axis=-1, keepdims=True)
    d = v - mu
    var = jnp.mean(d * d, axis=-1, keepdims=True)
    return d * lax.rsqrt(var + LN_EPS) * g + b


def _merge_kernel(x_ref, oa_ref, ob_ref, mga_ref, mgb_ref, wa_ref, wb_ref, wo_ref, g_ref, b_ref, o_ref):
    br_a = jnp.dot(oa_ref[...].astype(BF16), wa_ref[...], preferred_element_type=F32)
    br_b = jnp.dot(ob_ref[...].astype(BF16), wb_ref[...], preferred_element_type=F32)
    merged = jax.nn.sigmoid(mga_ref[...]) * br_a + jax.nn.sigmoid(mgb_ref[...]) * br_b
    mix = jnp.dot(merged.astype(BF16), wo_ref[...], preferred_element_type=F32)
    o_ref[...] = _layer_norm(DEEPNORM_ALPHA * x_ref[...] + mix, g_ref[...], b_ref[...])


def _merge(x2d, o_a, o_b, h, w_a, w_b, w_o, ln_g, ln_b):
    n = x2d.shape[0]
    tm = min(256, n)
    const = lambda i: (0, 0)
    return pl.pallas_call(
        _merge_kernel,
        grid=(n // tm,),
        in_specs=[pl.BlockSpec((tm, D_MODEL), lambda i: (i, 0)),
                  pl.BlockSpec((tm, Q_A), lambda i: (i, 0)),
                  pl.BlockSpec((tm, I_B), lambda i: (i, 0)),
                  pl.BlockSpec((tm, D_MODEL), lambda i: (i, C_MG // D_MODEL)),
                  pl.BlockSpec((tm, D_MODEL), lambda i: (i, C_MG // D_MODEL + 1)),
                  pl.BlockSpec((Q_A, D_MODEL), const),
                  pl.BlockSpec((I_B, D_MODEL), const),
                  pl.BlockSpec((D_MODEL, D_MODEL), const),
                  pl.BlockSpec((1, D_MODEL), const),
                  pl.BlockSpec((1, D_MODEL), const)],
        out_specs=pl.BlockSpec((tm, D_MODEL), lambda i: (i, 0)),
        out_shape=jax.ShapeDtypeStruct((n, D_MODEL), F32),
        compiler_params=_cparams(("arbitrary",)),
        name="merge_ln1",
    )(x2d, o_a, o_b, h, h, w_a, w_b, w_o, ln_g, ln_b)


def _mlp_kernel(x_ref, wu_ref, wd_ref, g_ref, b_ref, o_ref, xb_ref, acc_ref):
    j = pl.program_id(1)

    @pl.when(j == 0)
    def _():
        xb_ref[...] = x_ref[...].astype(BF16)
        acc_ref[...] = jnp.zeros_like(acc_ref)

    u = jnp.dot(xb_ref[...], wu_ref[...], preferred_element_type=F32)
    u = jnp.maximum(u, 0.0)
    acc_ref[...] += jnp.dot((u * u).astype(BF16), wd_ref[...], preferred_element_type=F32)

    @pl.when(j == pl.num_programs(1) - 1)
    def _():
        o_ref[...] = _layer_norm(DEEPNORM_ALPHA * x_ref[...] + acc_ref[...], g_ref[...], b_ref[...])


def _mlp(x1, w_up, w_down, ln_g, ln_b):
    n = x1.shape[0]
    tm = min(1024, n)
    tf = 1024
    return pl.pallas_call(
        _mlp_kernel,
        grid=(n // tm, D_FF // tf),
        in_specs=[pl.BlockSpec((tm, D_MODEL), lambda i, j: (i, 0)),
                  pl.BlockSpec((D_MODEL, tf), lambda i, j: (0, j)),
                  pl.BlockSpec((tf, D_MODEL), lambda i, j: (j, 0)),
                  pl.BlockSpec((1, D_MODEL), lambda i, j: (0, 0)),
                  pl.BlockSpec((1, D_MODEL), lambda i, j: (0, 0))],
        out_specs=pl.BlockSpec((tm, D_MODEL), lambda i, j: (i, 0)),
        out_shape=jax.ShapeDtypeStruct((n, D_MODEL), F32),
        scratch_shapes=[pltpu.VMEM((tm, D_MODEL), BF16), pltpu.VMEM((tm, D_MODEL), F32)],
        compiler_params=_cparams(("arbitrary", "arbitrary")),
        name="mlp_ln2",
    )(x1, w_up, w_down, ln_g, ln_b)


def _dot01(a01, x):
    hi = x.astype(BF16)
    r1 = x - hi.astype(F32)
    mid = r1.astype(BF16)
    lo = (r1 - mid.astype(F32)).astype(BF16)
    out = jnp.dot(a01, hi, preferred_element_type=F32)
    out += jnp.dot(a01, mid, preferred_element_type=F32)
    out += jnp.dot(a01, lo, preferred_element_type=F32)
    return out


def _x_dot01(x, b01):
    hi = x.astype(BF16)
    r1 = x - hi.astype(F32)
    mid = r1.astype(BF16)
    lo = (r1 - mid.astype(F32)).astype(BF16)
    out = jnp.dot(hi, b01, preferred_element_type=F32)
    out += jnp.dot(mid, b01, preferred_element_type=F32)
    out += jnp.dot(lo, b01, preferred_element_type=F32)
    return out


_NT = (((1,), (1,)), ((), ()))
_TN = (((0,), (0,)), ((), ()))


def _cmp_prompt_kernel(kv_ref, w_ref, posf_ref, wb_ref, o_ref, *, n_ch):
    acc = jnp.zeros((n_ch, 256), F32)
    for l in range(CMP_STRIDE):
        xl = kv_ref[pl.ds(l, n_ch, stride=CMP_STRIDE), :].astype(BF16)
        acc += jnp.dot(xl, w_ref[l], preferred_element_type=F32)
    bias = jnp.dot(posf_ref[...].astype(BF16), wb_ref[...], preferred_element_type=F32)[0:1]
    nxt = pltpu.roll(acc[:, 128:], n_ch - 1, 0)
    o_ref[0] = (acc[:, :128] + nxt + bias).astype(BF16)


def _at_slot(a, slot, axis):
    pads = [(0, 0)] * (a.ndim + 1)
    pads[axis] = (slot, 1 - slot)
    return jnp.pad(jnp.expand_dims(a, axis), pads)


def _cmp_weights(cmp_w, cmp_pos):
    w = cmp_w.reshape(2, 2, CMP_STRIDE, HEAD_DIM, HEAD_DIM)
    wl = jnp.stack([_at_slot(jnp.transpose(w[c], (1, 2, 0, 3)), c, 3) for c in range(2)], axis=1)
    wb = jnp.stack([_at_slot(cmp_w[c].reshape(CMP_LEN * HEAD_DIM, HEAD_DIM), c, 1) for c in range(2)])
    wl = wl.reshape(CMP_STRIDE, 128, 256).astype(BF16)
    wb = wb.reshape(2 * CMP_LEN * HEAD_DIM, 128).astype(BF16)
    posf = jnp.broadcast_to(cmp_pos.reshape(1, 2 * CMP_LEN * HEAD_DIM), (8, 2 * CMP_LEN * HEAD_DIM))
    return wl, wb, posf


def _cmp_prompt(h, b, t, wl, wb, posf):
    n_ch = t // CMP_STRIDE
    return pl.pallas_call(
        functools.partial(_cmp_prompt_kernel, n_ch=n_ch),
        grid=(b, KV_HEADS),
        in_specs=[pl.BlockSpec((t, 128), lambda i, g: (i, C_KVC // 128 + g)),
                  pl.BlockSpec((CMP_STRIDE, 128, 256), lambda i, g: (0, 0, 0)),
                  pl.BlockSpec((8, 4096), lambda i, g: (0, 0)),
                  pl.BlockSpec((4096, 128), lambda i, g: (0, 0))],
        out_specs=pl.BlockSpec((1, n_ch, 128), lambda i, g: (i, 0, g)),
        out_shape=jax.ShapeDtypeStruct((b, n_ch, 256), BF16),
        compiler_params=_cparams(("arbitrary", "arbitrary")),
        name="cmp_prompt",
    )(h, wl, posf, wb)


QB = 256
PIECE = 256
KT = 256
WT = 128


ROWS = GROUP * QB
TL_COUNT = 64


def _head_major(q):
    return jnp.concatenate([q[:, r * 64:(r + 1) * 64] for r in range(GROUP)], axis=0) * ATTN_SCALE


def _head_slope(g, r):
    return jnp.where(g == 0, 2.0 ** -(r + 1), 2.0 ** -(r + 1 + GROUP)).astype(F32)


def _slope_row(g):
    r_ix = lax.broadcasted_iota(jnp.int32, (1, ROWS), 1) // QB
    row = jnp.zeros((1, ROWS), F32)
    for r in range(GROUP):
        row = jnp.where(r_ix == r, _head_slope(g, r), row)
    return row


def _nsa_select_kernel(q_ref, kc_ref, oc_ref, un_ref, tl_ref, *, n_ch):
    g = pl.program_id(1)
    c = pl.program_id(2)
    t0 = c * QB
    qpad = jnp.concatenate([_head_major(q_ref[...]), jnp.zeros((ROWS, 64), F32)], axis=1).astype(BF16)
    kc = kc_ref[0]
    s = lax.dot_general(kc, qpad, _NT, preferred_element_type=F32)
    pos_row = t0 + lax.broadcasted_iota(jnp.int32, (1, ROWS), 1) % QB
    n_col = lax.broadcasted_iota(jnp.int32, (n_ch, 1), 0)
    dist_i = pos_row - (n_col * CMP_STRIDE + (CMP_LEN - 1))
    valid = jnp.logical_and(dist_i >= 0, n_col < n_ch - 1)
    s = jnp.where(valid, s - _slope_row(g) * dist_i.astype(F32), NEG)
    m = jnp.max(s, axis=0, keepdims=True)
    p = jnp.where(valid, jnp.exp(s - m), 0.0)
    p = p * (1.0 / jnp.maximum(jnp.sum(p, axis=0, keepdims=True), 1e-30))
    o_c = lax.dot_general(p.astype(BF16), kc, _TN, preferred_element_type=F32)
    for r in range(GROUP):
        oc_ref[:, r * 64:(r + 1) * 64] = o_c[r * QB:(r + 1) * QB, 64:128]

    imp = p[:, 0:QB]
    for r in range(1, GROUP):
        imp = imp + p[:, r * QB:(r + 1) * QB]
    ratio = SEL_BLOCK // CMP_STRIDE
    pool_t = (lax.broadcasted_iota(jnp.int32, (128, n_ch), 1) // ratio
              == lax.broadcasted_iota(jnp.int32, (128, n_ch), 0)).astype(BF16)
    imp_blk = _dot01(pool_t, imp)
    blk = lax.broadcasted_iota(jnp.int32, (128, QB), 0)
    pos_q = t0 + lax.broadcasted_iota(jnp.int32, (1, QB), 1)
    cur = pos_q // SEL_BLOCK
    forced = jnp.logical_or(jnp.logical_or(blk == 0, blk == cur), blk == cur - 1)
    allowed = blk * SEL_BLOCK <= pos_q
    work = jnp.where(jnp.logical_and(allowed, jnp.logical_not(forced)), imp_blk, -jnp.inf)
    sel = forced
    for _ in range(N_SELECT - 3):
        mx = jnp.max(work, axis=0, keepdims=True)
        first = jnp.min(jnp.where(work == mx, blk, 128), axis=0, keepdims=True)
        pick = blk == first
        sel = jnp.logical_or(sel, pick)
        work = jnp.where(pick, -jnp.inf, work)
    sel = jnp.logical_and(sel, allowed)
    un_ref[...] = jnp.transpose(jnp.where(sel, 0.0, -UNSEL)).astype(BF16)

    tile_of = (lax.broadcasted_iota(jnp.int32, (128, 128), 1) // (KT // SEL_BLOCK)
               == lax.broadcasted_iota(jnp.int32, (128, 128), 0)).astype(BF16)
    sel_b = jnp.where(sel, 1.0, 0.0).astype(BF16)
    cnt = jnp.sum(jnp.dot(tile_of, sel_b, preferred_element_type=F32), axis=1, keepdims=True)
    tile_col = lax.broadcasted_iota(jnp.int32, (128, 1), 0)
    flag = jnp.logical_and(cnt > 0.0, tile_col < t0 // KT)
    flag_m = jnp.where(jnp.broadcast_to(flag, (128, 128)), 1.0, 0.0).astype(BF16)
    row_i = lax.broadcasted_iota(jnp.int32, (128, 128), 0)
    lane_i = lax.broadcasted_iota(jnp.int32, (128, 128), 1)
    before = jnp.dot((lane_i < row_i).astype(BF16), flag_m, preferred_element_type=F32)
    slot = jnp.where(jnp.logical_and(flag, before == lane_i.astype(F32)), 1.0, 0.0).astype(BF16)
    j_rows = lax.broadcasted_iota(jnp.int32, (8, 128), 1).astype(BF16)
    listed = jnp.dot(j_rows, slot, preferred_element_type=F32)
    total = jnp.dot(jnp.ones((8, 128), BF16), flag_m, preferred_element_type=F32)
    lane8 = lax.broadcasted_iota(jnp.int32, (8, 128), 1)
    tl_ref[0] = jnp.where(lane8 == TL_COUNT, total, listed).astype(jnp.int32)


def _nsa_select(h, kcmp, b, t):
    n_ch = t // CMP_STRIDE
    nq = t // QB
    steps = b * KV_HEADS * nq
    return pl.pallas_call(
        functools.partial(_nsa_select_kernel, n_ch=n_ch),
        grid=(b, KV_HEADS, nq),
        in_specs=[pl.BlockSpec((QB, 256), lambda i, g, c: (i * nq + c, C_QA // 256 + g)),
                  pl.BlockSpec((1, n_ch, 128), lambda i, g, c: (i, 0, g))],
        out_specs=[pl.BlockSpec((QB, 256), lambda i, g, c: (i * nq + c, g)),
                   pl.BlockSpec((QB, 128), lambda i, g, c: (i * nq + c, g)),
                   pl.BlockSpec((1, 8, 128), lambda i, g, c: ((i * KV_HEADS + g) * nq + c, 0, 0))],
        out_shape=[jax.ShapeDtypeStruct((b * t, Q_A), F32),
                   jax.ShapeDtypeStruct((b * t, KV_HEADS * 128), BF16),
                   jax.ShapeDtypeStruct((steps, 8, 128), jnp.int32)],
        compiler_params=_cparams(("arbitrary", "arbitrary", "arbitrary")),
        name="nsa_select",
    )(h, kcmp)


def _nsa_attend_kernel(tl_ref, q_ref, un_ref, gl_ref, oc_ref, ks_ref, kw_ref, o_ref,
                       ksb_ref, ksa_ref, kwb_ref, kwa_ref, *, nq):
    i = pl.program_id(0)
    g = pl.program_id(1)
    c = pl.program_id(2)
    step = (i * KV_HEADS + g) * nq + c
    t0 = c * QB
    half = PIECE
    n_piece = ROWS // PIECE
    qs = _head_major(q_ref[...])
    slope_col = jnp.concatenate([jnp.broadcast_to(_head_slope(g, r), (QB, 1)) for r in range(GROUP)], axis=0)
    lane64 = lax.broadcasted_iota(jnp.int32, (ROWS, 64), 1)
    ali_q = jnp.where(lane64 == 0, -64.0 * slope_col, jnp.where(lane64 == 1, -slope_col, 0.0))
    q_win = jnp.concatenate([qs, ali_q], axis=1).astype(BF16)
    un = un_ref[...]
    q_sel = jnp.concatenate([q_win, jnp.concatenate([un] * GROUP, axis=0)], axis=1)
    slope_row = _slope_row(g)

    @pl.when(c == 0)
    def _():
        def fill(j, carry):
            r0 = pl.multiple_of(j * KT, KT)
            row = lax.broadcasted_iota(jnp.int32, (KT, 128), 0)
            lane = lax.broadcasted_iota(jnp.int32, (KT, 128), 1)
            for src, dst_kv, dst_aug, tile in ((ks_ref, ksb_ref, ksa_ref, KT), (kw_ref, kwb_ref, kwa_ref, WT)):
                kv_b = src[pl.ds(r0, KT), :].astype(BF16)
                d = (QB - 1) - row % tile
                ali = jnp.where(lane == 64, d >> 6, jnp.where(lane == 65, d & 63, 0)).astype(F32).astype(BF16)
                dst_kv[pl.ds(r0, KT), :] = kv_b
                dst_aug[pl.ds(r0, KT), 0:128] = jnp.where(lane < 64, kv_b, ali)
            ksa_ref[pl.ds(r0, KT), 128:256] = jnp.where((r0 + row) // SEL_BLOCK == lane, 1.0, 0.0).astype(BF16)
            return carry

        lax.fori_loop(0, ks_ref.shape[0] // KT, fill, 0)

    def update(carry, s, kv_b, offset):
        out = []
        for hh, ((m, l, acc), sh) in enumerate(zip(carry, s)):
            shift = slope_row[:, hh * half:(hh + 1) * half] * offset
            m_new = jnp.maximum(m, jnp.max(sh, axis=0, keepdims=True) - shift)
            alpha = jnp.exp(m - m_new)
            p = jnp.exp(sh - (m_new + shift))
            l = alpha * l + jnp.sum(p, axis=0, keepdims=True)
            acc = alpha * acc + lax.dot_general(kv_b, p.astype(BF16), _TN, preferred_element_type=F32)
            out.append((m_new, l, acc))
        return tuple(out)

    def init():
        return tuple((jnp.full((1, half), NEG, F32), jnp.zeros((1, half), F32), jnp.zeros((128, half), F32))
                     for _ in range(n_piece))

    def tile(carry, aug_ref, kvb_ref, q_side, k0, n_keys, keep):
        k_aug = aug_ref[pl.ds(k0, n_keys), :]
        s = []
        for hh in range(n_piece):
            sh = lax.dot_general(k_aug, q_side[hh * half:(hh + 1) * half], _NT, preferred_element_type=F32)
            if keep is not None:
                sh = jnp.where(jnp.concatenate([keep] * (half // QB), axis=1), sh, NEG)
            s.append(sh)
        return update(carry, s, kvb_ref[pl.ds(k0, n_keys), :], (t0 - k0).astype(F32))

    k_tail = pl.multiple_of((t0 // KT) * KT, KT)
    key_x = lax.broadcasted_iota(jnp.int32, (KT, QB), 0)
    q_x = lax.broadcasted_iota(jnp.int32, (KT, QB), 1)
    carry = tile(init(), ksa_ref, ksb_ref, q_sel, k_tail, KT, k_tail + key_x <= t0 + q_x)

    def sel_listed(n, carry):
        return tile(carry, ksa_ref, ksb_ref, q_sel, pl.multiple_of(tl_ref[step, n] * KT, KT), KT, None)

    n_listed = tl_ref[step, TL_COUNT]
    carry = lax.fori_loop(0, n_listed // 2, lambda n, cr: sel_listed(2 * n + 1, sel_listed(2 * n, cr)), carry)
    carry = lax.fori_loop(0, n_listed % 2, lambda n, cr: sel_listed(n_listed - 1, cr), carry)
    o_sel = [acc * (1.0 / l) for (_, l, acc) in carry]

    w_keys = WINDOW + QB
    k0w = pl.multiple_of(jnp.maximum(t0 - WINDOW, 0), WT)
    k_aug = kwa_ref[pl.ds(k0w, w_keys), :]
    kv_w = kwb_ref[pl.ds(k0w, w_keys), :]
    dq = lax.broadcasted_iota(jnp.int32, (WT, QB), 1) - lax.broadcasted_iota(jnp.int32, (WT, QB), 0)
    offs = [t0 - (k0w + gi * WT) for gi in range(w_keys // WT)]
    keeps = [jnp.concatenate([jnp.logical_and(dq + off >= 0, dq + off < WINDOW)] * (half // QB), axis=1)
             for off in offs]
    o_win = []
    for hh in range(n_piece):
        sl = slope_row[:, hh * half:(hh + 1) * half]
        sh = lax.dot_general(k_aug, q_win[hh * half:(hh + 1) * half], _NT, preferred_element_type=F32)
        parts = [jnp.where(keep, sh[gi * WT:(gi + 1) * WT], NEG) for gi, keep in enumerate(keeps)]
        shifts = [sl * off.astype(F32) for off in offs]
        m_w = functools.reduce(jnp.maximum, [jnp.max(pt, axis=0, keepdims=True) - sf
                                             for pt, sf in zip(parts, shifts)])
        ps = [jnp.exp(pt - (m_w + sf)) for pt, sf in zip(parts, shifts)]
        l_w = functools.reduce(jnp.add, [jnp.sum(p, axis=0, keepdims=True) for p in ps])
        acc = lax.dot_general(kv_w, jnp.concatenate(ps, axis=0).astype(BF16), _TN, preferred_element_type=F32)
        o_win.append(acc * (1.0 / l_w))

    gates = jax.nn.sigmoid(gl_ref[...])
    o_c = oc_ref[...]
    for r in range(GROUP):
        hh, cs = divmod(r * QB, half)
        o_s = jnp.transpose(o_sel[hh][:, cs:cs + QB])[:, 64:128]
        o_w = jnp.transpose(o_win[hh][:, cs:cs + QB])[:, 64:128]
        o_ref[:, r * 64:(r + 1) * 64] = (gates[:, 3 * r:3 * r + 1] * o_c[:, r * 64:(r + 1) * 64]
                                         + gates[:, 3 * r + 1:3 * r + 2] * o_s
                                         + gates[:, 3 * r + 2:3 * r + 3] * o_w)


def _nsa_attend(tiles, h, unsel, o_c, b, t):
    nq = t // QB
    grid_spec = pltpu.PrefetchScalarGridSpec(
        num_scalar_prefetch=1,
        grid=(b, KV_HEADS, nq),
        in_specs=[pl.BlockSpec((QB, 256), lambda i, g, c, tl: (i * nq + c, C_QA // 256 + g)),
                  pl.BlockSpec((QB, 128), lambda i, g, c, tl: (i * nq + c, g)),
                  pl.BlockSpec((QB, 128), lambda i, g, c, tl: (i * nq + c, C_GL // 128 + g)),
                  pl.BlockSpec((QB, 256), lambda i, g, c, tl: (i * nq + c, g)),
                  pl.BlockSpec((t, 128), lambda i, g, c, tl: (i, C_KVS // 128 + g)),
                  pl.BlockSpec((t, 128), lambda i, g, c, tl: (i, C_KVW // 128 + g))],
        out_specs=pl.BlockSpec((QB, 256), lambda i, g, c, tl: (i * nq + c, g)),
        scratch_shapes=[pltpu.VMEM((t, 128), BF16), pltpu.VMEM((t, 256), BF16),
                        pltpu.VMEM((t, 128), BF16), pltpu.VMEM((t, 128), BF16)],
    )
    return pl.pallas_call(
        functools.partial(_nsa_attend_kernel, nq=nq),
        grid_spec=grid_spec,
        out_shape=jax.ShapeDtypeStruct((b * t, Q_A), F32),
        compiler_params=_cparams(("arbitrary", "arbitrary", "arbitrary")),
        name="nsa_attend",
    )(tiles, h, unsel, h, o_c, h, h)


def _nsa_prompt(h, kcmp, b, t):
    o_c, unsel, tl = _nsa_select(h, kcmp, b, t)
    return _nsa_attend(tl[:, 0, :], h, unsel, o_c, b, t)


HB = 256


def _lower_bound(lb_ref):
    z = lb_ref[...]
    e = jnp.exp(z - jnp.max(z, axis=0, keepdims=True))
    return e[0:1] / jnp.sum(e, axis=0, keepdims=True)


def _hgrn_prompt_kernel(q_ref, f_ref, i_ref, g_ref, lb_ref, ng_ref, o_ref, s_out_ref, s_ref):
    tb = pl.program_id(1)

    @pl.when(tb == 0)
    def _():
        s_ref[...] = jnp.zeros_like(s_ref)

    n_c = HB // HGRN_CHUNK
    lb = _lower_bound(lb_ref)
    f = lb + (1.0 - lb) * jax.nn.sigmoid(f_ref[...])
    log_f = jnp.log(f)
    row = lax.broadcasted_iota(jnp.int32, (HB, HB), 0)
    col = lax.broadcasted_iota(jnp.int32, (HB, HB), 1)
    tril = jnp.logical_and(col <= row, row // HGRN_CHUNK == col // HGRN_CHUNK)
    lc = _dot01(tril.astype(BF16), log_f)
    lc3 = lc.reshape(n_c, HGRN_CHUNK, Q_B)
    lend = jnp.broadcast_to(lc3[:, HGRN_CHUNK - 1:HGRN_CHUNK, :], lc3.shape).reshape(HB, Q_B)
    q_t = (q_ref[...] * jnp.exp(lc)).astype(BF16)
    k = 1.0 - f
    k_t = (k * jnp.exp(-lc)).astype(BF16)
    k_e = (k * jnp.exp(lend - lc)).astype(BF16)
    dec = jnp.exp(lend)
    v_all = i_ref[...]
    gate = g_ref[...]
    ng = ng_ref[...]
    for hh in range(B_HEADS):
        ks = slice(hh * B_DK, (hh + 1) * B_DK)
        vs = slice(hh * B_DV, (hh + 1) * B_DV)
        v = v_all[:, vs].astype(BF16)
        a = lax.dot_general(q_t[:, ks], k_t[:, ks], _NT, preferred_element_type=F32)
        a = jnp.where(tril, a, 0.0).astype(BF16)
        o = jnp.dot(a, v, preferred_element_type=F32)
        dec_t = jnp.transpose(dec[:, ks])
        st = s_ref[hh]
        inter = []
        for cc in range(n_c):
            rs = slice(cc * HGRN_CHUNK, (cc + 1) * HGRN_CHUNK)
            inter.append(jnp.dot(q_t[rs, ks], st.astype(BF16), preferred_element_type=F32))
            u = lax.dot_general(k_e[rs, ks], v[rs], _TN, preferred_element_type=F32)
            last = (cc + 1) * HGRN_CHUNK - 1
            st = dec_t[:, last:last + 1] * st + u
        s_ref[hh] = st
        o = o + jnp.concatenate(inter, axis=0)
        o = o * lax.rsqrt(jnp.mean(o * o, axis=-1, keepdims=True) + LN_EPS) * ng
        gt = gate[:, vs]
        o_ref[:, vs] = o * (gt * jax.nn.sigmoid(gt))

    @pl.when(tb == pl.num_programs(1) - 1)
    def _():
        s_out_ref[0] = s_ref[...]


def _hgrn_prompt(h, b, t, lb_logits, norm_g):
    nt = t // HB
    return pl.pallas_call(
        _hgrn_prompt_kernel,
        grid=(b, nt),
        in_specs=[pl.BlockSpec((HB, Q_B), lambda i, j: (i * nt + j, C_QB // Q_B)),
                  pl.BlockSpec((HB, Q_B), lambda i, j: (i * nt + j, C_FB // Q_B)),
                  pl.BlockSpec((HB, I_B), lambda i, j: (i * nt + j, C_IB // I_B)),
                  pl.BlockSpec((HB, I_B), lambda i, j: (i * nt + j, C_GB // I_B)),
                  pl.BlockSpec((2, Q_B), lambda i, j: (0, 0)),
                  pl.BlockSpec((1, B_DV), lambda i, j: (0, 0))],
        out_specs=[pl.BlockSpec((HB, I_B), lambda i, j: (i * nt + j, 0)),
                   pl.BlockSpec((1, B_HEADS, B_DK, B_DV), lambda i, j: (i, 0, 0, 0))],
        out_shape=[jax.ShapeDtypeStruct((b * t, I_B), F32),
                   jax.ShapeDtypeStruct((b, B_HEADS, B_DK, B_DV), F32)],
        scratch_shapes=[pltpu.VMEM((B_HEADS, B_DK, B_DV), F32)],
        compiler_params=_cparams(("arbitrary", "arbitrary")),
        name="hgrn_prompt",
    )(h, h, h, h, lb_logits, norm_g)


def _prompt_layer(x, w_pad, cmpw, lb_logits, norm_g, w_a, w_b, w_o, ln1_g, ln1_b, w_up, w_down, ln2_g, ln2_b):
    b, t, _ = x.shape
    x2d = x.reshape(b * t, D_MODEL)
    h = _project(x2d, w_pad)
    kcmp = _cmp_prompt(h, b, t, *cmpw)
    o_a = _nsa_prompt(h, kcmp, b, t)
    o_b, s_end = _hgrn_prompt(h, b, t, lb_logits, norm_g)
    x1 = _merge(x2d, o_a, o_b, h, w_a, w_b, w_o, ln1_g, ln1_b)
    y = _mlp(x1, w_up, w_down, ln2_g, ln2_b)
    return y.reshape(b, t, D_MODEL), h, s_end


def _kv_out(h, col, b, t):
    kv = h[:, col:col + 256].reshape(b, t, KV_HEADS, 2, HEAD_DIM)
    return jnp.swapaxes(kv, 2, 3)


LAND_PAGES = 4
CH_PER_PAGE = PAGE_SIZE // CMP_STRIDE
BLK_PER_PAGE = PAGE_SIZE // SEL_BLOCK
N_POOL_SEL = N_SELECT - 1


def _head_rows(q_row, g):
    rows = [q_row[:, g * 256 + r * 64:g * 256 + (r + 1) * 64] for r in range(GROUP)]
    return jnp.concatenate(rows + [jnp.zeros((8 - GROUP, 64), F32)], axis=0) * ATTN_SCALE


def _slope_col(g):
    row = lax.broadcasted_iota(jnp.int32, (8, 1), 0)
    col = jnp.zeros((8, 1), F32)
    for r in range(GROUP):
        col = jnp.where(row == r, 2.0 ** -(g * GROUP + r + 1), col)
    return col


def _cmp_sample_kernel(pt_ref, cache_ref, w_ref, posf_ref, wb_ref, q_ref, oc_ref, idx_ref,
                       buf_ref, xk_ref, xv_ref, sem, *, n_pages, past):
    b = pl.program_id(0)
    n_ch = n_pages * CH_PER_PAGE
    n_blk = past // SEL_BLOCK
    lanes = idx_ref.shape[-1]

    def page_copy(pg, page):
        return pltpu.make_async_copy(cache_ref.at[page], buf_ref.at[pg], sem.at[pg])

    def issue(pg, carry):
        page_copy(pg, pt_ref[b, pg]).start()
        return carry

    def land(i, carry):
        pages = [i * LAND_PAGES + u for u in range(LAND_PAGES)]
        for pg in pages:
            page_copy(pg, 0).wait()
        for pg in pages:
            r0 = pl.multiple_of(pg * PAGE_SIZE, PAGE_SIZE)
            xk_ref[pl.ds(r0, PAGE_SIZE), :] = jnp.transpose(buf_ref[pg, 0].reshape(128, PAGE_SIZE))
            xv_ref[pl.ds(r0, PAGE_SIZE), :] = jnp.transpose(buf_ref[pg, 1].reshape(128, PAGE_SIZE))
        return carry

    lax.fori_loop(0, n_pages, issue, 0)
    lax.fori_loop(0, n_pages // LAND_PAGES, land, 0)

    bias = jnp.dot(posf_ref[...].astype(BF16), wb_ref[...], preferred_element_type=F32)[0:1]
    blocks = []
    for c, x_ref in enumerate((xk_ref, xv_ref)):
        acc = jnp.zeros((n_ch, 256), F32)
        for l2 in range(CMP_STRIDE // 2):
            xl = jnp.concatenate([x_ref[pl.ds(2 * l2 + u, n_ch, stride=CMP_STRIDE), :] for u in range(2)],
                                 axis=1).astype(BF16)
            acc += jnp.dot(xl, w_ref[c, l2], preferred_element_type=F32)
        bias_c = jnp.concatenate([bias[:, c * 64:(c + 1) * 64]] * KV_HEADS, axis=1)
        blocks.append((acc[:, :128] + pltpu.roll(acc[:, 128:], n_ch - 1, 0) + bias_c).astype(BF16))
    kc_all, vc_all = blocks

    q_row = q_ref[0]
    n_ix = lax.broadcasted_iota(jnp.int32, (1, n_ch), 1)
    dist_i = past - (n_ix * CMP_STRIDE + (CMP_LEN - 1))
    valid = jnp.logical_and(dist_i >= 0, n_ix < n_ch - 1)
    dist = dist_i.astype(F32)
    ratio = SEL_BLOCK // CMP_STRIDE
    pool = (lax.broadcasted_iota(jnp.int32, (n_ch, lanes), 0) // ratio
            == lax.broadcasted_iota(jnp.int32, (n_ch, lanes), 1)).astype(BF16)
    row8 = lax.broadcasted_iota(jnp.int32, (8, lanes), 0)
    score = jnp.full((8, lanes), -FORCE, F32)
    lane_g = lax.broadcasted_iota(jnp.int32, (8, 128), 1) // HEAD_DIM
    o_c = jnp.zeros((8, 128), F32)
    for g in range(KV_HEADS):
        pieces = [jnp.zeros((8, 64), F32)] * KV_HEADS
        pieces[g] = _head_rows(q_row, g)
        qpad = jnp.concatenate(pieces, axis=1).astype(BF16)
        s = lax.dot_general(qpad, kc_all, _NT, preferred_element_type=F32)
        s = jnp.where(valid, s - _slope_col(g) * dist, NEG)
        m = jnp.max(s, axis=-1, keepdims=True)
        p = jnp.where(valid, jnp.exp(s - m), 0.0)
        p = p / jnp.maximum(jnp.sum(p, axis=-1, keepdims=True), 1e-30)
        o_c = jnp.where(lane_g == g, jnp.dot(p.astype(BF16), vc_all, preferred_element_type=F32), o_c)
        imp = p[0:1] + p[1:2] + p[2:3] + p[3:4]
        imp_blk = _x_dot01(jnp.broadcast_to(imp, (8, n_ch)), pool)
        score = jnp.where(row8 == g, imp_blk, score)
    lane = lax.broadcasted_iota(jnp.int32, (8, lanes), 1)
    forced = jnp.logical_or(lane == 0, lane == n_blk - 1)
    score = jnp.where(lane < n_blk, jnp.where(forced, FORCE, score), -jnp.inf)
    out = jnp.zeros((8, lanes), jnp.int32)
    for r in range(N_POOL_SEL):
        m = jnp.max(score, axis=-1, keepdims=True)
        first = jnp.min(jnp.where(score == m, lane, lanes), axis=-1, keepdims=True)
        out = jnp.where(lane == r, first, out)
        score = jnp.where(lane == first, -jnp.inf, score)
    idx_ref[0] = out
    oc_ref[0] = o_c


def _cmp_sample(page_table, cache5, w_s, posf, wb, q3, past):
    bsz, n_pages = page_table.shape
    n_ch = n_pages * CH_PER_PAGE
    lanes = -(-(past // SEL_BLOCK) // 128) * 128
    grid_spec = pltpu.PrefetchScalarGridSpec(
        num_scalar_prefetch=1,
        grid=(bsz,),
        in_specs=[pl.BlockSpec(memory_space=pl.ANY),
                  pl.BlockSpec((2, CMP_STRIDE // 2, 256, 256), lambda i, pt: (0, 0, 0, 0)),
                  pl.BlockSpec((8, 4096), lambda i, pt: (0, 0)),
                  pl.BlockSpec((4096, 128), lambda i, pt: (0, 0)),
                  pl.BlockSpec((1, 1, Q_A), lambda i, pt: (i, 0, 0))],
        out_specs=[pl.BlockSpec((1, 8, 128), lambda i, pt: (i, 0, 0)),
                   pl.BlockSpec((1, 8, lanes), lambda i, pt: (i, 0, 0))],
        scratch_shapes=[pltpu.VMEM((n_pages, 2, KV_HEADS, HEAD_DIM, PAGE_SIZE), F32),
                        pltpu.VMEM((n_pages * PAGE_SIZE, 128), F32),
                        pltpu.VMEM((n_pages * PAGE_SIZE, 128), F32),
                        pltpu.SemaphoreType.DMA((n_pages,))],
    )
    return pl.pallas_call(
        functools.partial(_cmp_sample_kernel, n_pages=n_pages, past=past),
        grid_spec=grid_spec,
        out_shape=[jax.ShapeDtypeStruct((bsz, 8, 128), F32),
                   jax.ShapeDtypeStruct((bsz, 8, lanes), jnp.int32)],
        compiler_params=_cparams(("arbitrary",)),
        name="cmp_sample",
    )(page_table, cache5, w_s, posf, wb, q3)


def _pick_lane(mat, lane, target):
    return jnp.sum(jnp.where(lane == target, mat, 0.0), axis=-1, keepdims=True)


def _sel_win_sample_kernel(pt_ref, idx_ref, cache_ref, win_ref, q_ref, ks_ref, kw_ref, gl_ref, oc_ref, o_ref,
                           kbuf_ref, vbuf_ref, sem, *, past):
    b = pl.program_id(0)
    slot_lanes = N_SELECT * PAGE_SIZE

    def page_copies(g, k, page):
        dst = pl.ds(k * PAGE_SIZE, PAGE_SIZE)
        return (pltpu.make_async_copy(cache_ref.at[page, 0, g], kbuf_ref.at[g, :, dst], sem),
                pltpu.make_async_copy(cache_ref.at[page, 1, g], vbuf_ref.at[g, :, dst], sem))

    for g in range(KV_HEADS):
        for k in range(N_POOL_SEL):
            page = pt_ref[b, idx_ref[b, g * N_SELECT + k] // BLK_PER_PAGE]
            for cp in page_copies(g, k, page):
                cp.start()
        pad = pl.ds(N_POOL_SEL * PAGE_SIZE, PAGE_SIZE)
        kbuf_ref[g, :, pad] = jnp.zeros((HEAD_DIM, PAGE_SIZE), F32)
        vbuf_ref[g, :, pad] = jnp.zeros((HEAD_DIM, PAGE_SIZE), F32)

    q_row = q_ref[0]
    ks_new = ks_ref[0]
    kw_new = kw_ref[0]
    gl_all = jax.nn.sigmoid(gl_ref[0])
    lane128 = lax.broadcasted_iota(jnp.int32, (8, 128), 1)
    row8 = lax.broadcasted_iota(jnp.int32, (8, 1), 0)
    w_len = win_ref.shape[-1]

    def two_piece(qh, slope, k_t, v_t, dist_i, valid, k_new, v_new):
        s = jnp.dot(qh.astype(BF16), k_t.astype(BF16), preferred_element_type=F32)
        s = jnp.where(valid, s - slope * dist_i.astype(F32), NEG)
        s_n = jnp.sum(qh * k_new, axis=-1, keepdims=True)
        m = jnp.maximum(jnp.max(s, axis=-1, keepdims=True), s_n)
        p = jnp.where(valid, jnp.exp(s - m), 0.0)
        p_n = jnp.exp(s_n - m)
        l = jnp.sum(p, axis=-1, keepdims=True) + p_n
        o = lax.dot_general(p.astype(BF16), v_t.astype(BF16), _NT, preferred_element_type=F32) + p_n * v_new
        return o / l

    outs = []
    for g in range(KV_HEADS):
        qh = _head_rows(q_row, g)
        slope = _slope_col(g)
        j_ix = lax.broadcasted_iota(jnp.int32, (1, w_len), 1)
        dist_w = w_len - j_ix
        o_w = two_piece(qh, slope, win_ref[0, 0, g], win_ref[0, 1, g], dist_w, dist_w < WINDOW,
                        kw_new[:, g * 128:g * 128 + 64], kw_new[:, g * 128 + 64:(g + 1) * 128])
        outs.append((qh, slope, o_w))

    for g in range(KV_HEADS):
        for k in range(N_POOL_SEL):
            for cp in page_copies(g, k, 0):
                cp.wait()

    lane_s = lax.broadcasted_iota(jnp.int32, (1, slot_lanes), 1)
    o_all = []
    for g in range(KV_HEADS):
        qh, slope, o_w = outs[g]
        pos_k = jnp.full((1, slot_lanes), past + 1, jnp.int32)
        for k in range(N_POOL_SEL):
            blk = idx_ref[b, g * N_SELECT + k]
            r = lane_s % PAGE_SIZE
            in_blk = jnp.logical_and(lane_s // PAGE_SIZE == k, r // SEL_BLOCK == blk % BLK_PER_PAGE)
            pos_k = jnp.where(in_blk, (blk // BLK_PER_PAGE) * PAGE_SIZE + r, pos_k)
        dist_s = past - pos_k
        o_s = two_piece(qh, slope, kbuf_ref[g], vbuf_ref[g], dist_s, dist_s >= 0,
                        ks_new[:, g * 128:g * 128 + 64], ks_new[:, g * 128 + 64:(g + 1) * 128])
        o_c = oc_ref[0][:, g * 64:(g + 1) * 64]
        gates = jnp.broadcast_to(gl_all[:, g * 128:(g + 1) * 128], (8, 128))
        g_c = _pick_lane(gates, lane128, 3 * row8)
        g_s = _pick_lane(gates, lane128, 3 * row8 + 1)
        g_w = _pick_lane(gates, lane128, 3 * row8 + 2)
        o_all.append(g_c * o_c + g_s * o_s + g_w * o_w)
    o_ref[0] = jnp.concatenate(o_all, axis=1)


def _sel_win_sample(page_table, idx, cache_sel5, cache_win5, q3, ks3, kw3, gl3, o_c, past):
    bsz = page_table.shape[0]
    w_len = cache_win5.shape[-1]
    grid_spec = pltpu.PrefetchScalarGridSpec(
        num_scalar_prefetch=2,
        grid=(bsz,),
        in_specs=[pl.BlockSpec(memory_space=pl.ANY),
                  pl.BlockSpec((1, 2, KV_HEADS, HEAD_DIM, w_len), lambda i, pt, ix: (i, 0, 0, 0, 0)),
                  pl.BlockSpec((1, 1, Q_A), lambda i, pt, ix: (i, 0, 0)),
                  pl.BlockSpec((1, 1, 256), lambda i, pt, ix: (i, 0, 0)),
                  pl.BlockSpec((1, 1, 256), lambda i, pt, ix: (i, 0, 0)),
                  pl.BlockSpec((1, 1, 256), lambda i, pt, ix: (i, 0, 0)),
                  pl.BlockSpec((1, 8, 128), lambda i, pt, ix: (i, 0, 0))],
        out_specs=pl.BlockSpec((1, 8, 128), lambda i, pt, ix: (i, 0, 0)),
        scratch_shapes=[pltpu.VMEM((KV_HEADS, HEAD_DIM, N_SELECT * PAGE_SIZE), F32),
                        pltpu.VMEM((KV_HEADS, HEAD_DIM, N_SELECT * PAGE_SIZE), F32),
                        pltpu.SemaphoreType.DMA(())],
    )
    return pl.pallas_call(
        functools.partial(_sel_win_sample_kernel, past=past),
        grid_spec=grid_spec,
        out_shape=jax.ShapeDtypeStruct((bsz, 8, 128), F32),
        compiler_params=_cparams(("arbitrary",)),
        name="sel_win_sample",
    )(page_table, idx, cache_sel5, cache_win5, q3, ks3, kw3, gl3, o_c)


def _hgrn_sample_kernel(q_ref, f_ref, v_ref, g_ref, lb_ref, ng_ref, s_ref, o_ref, s_out_ref):
    z = lb_ref[...]
    e = jnp.exp(z - jnp.max(z, axis=0, keepdims=True))
    lb = e[0] / jnp.sum(e, axis=0)
    f = lb + (1.0 - lb) * jax.nn.sigmoid(f_ref[0])
    decay = jnp.exp(jnp.log(f))
    k = 1.0 - f
    q = q_ref[0]
    v = v_ref[0]
    gate = g_ref[0]
    ng = ng_ref[...]
    for hh in range(B_HEADS):
        hs = slice(hh, hh + 1)
        s_new = decay[hs] * s_ref[0, hh] + v[:, hs] * k[hs]
        s_out_ref[0, hh] = s_new
        o = jnp.sum(q[hs] * s_new, axis=1, keepdims=True)
        o = o * lax.rsqrt(jnp.mean(o * o, axis=0, keepdims=True) + LN_EPS) * ng
        gt = gate[:, hs]
        o_ref[0, :, hs] = o * (gt * jax.nn.sigmoid(gt))


def _hgrn_sample(q_hk, f_hk, v_vh, g_vh, lb_hk, ng_col, state_t):
    bsz = q_hk.shape[0]
    return pl.pallas_call(
        _hgrn_sample_kernel,
        grid=(bsz,),
        in_specs=[pl.BlockSpec((1, B_HEADS, B_DK), lambda i: (i, 0, 0)),
                  pl.BlockSpec((1, B_HEADS, B_DK), lambda i: (i, 0, 0)),
                  pl.BlockSpec((1, B_DV, B_HEADS), lambda i: (i, 0, 0)),
                  pl.BlockSpec((1, B_DV, B_HEADS), lambda i: (i, 0, 0)),
                  pl.BlockSpec((2, B_HEADS, B_DK), lambda i: (0, 0, 0)),
                  pl.BlockSpec((B_DV, 1), lambda i: (0, 0)),
                  pl.BlockSpec((1, B_HEADS, B_DV, B_DK), lambda i: (i, 0, 0, 0))],
        out_specs=[pl.BlockSpec((1, B_DV, B_HEADS), lambda i: (i, 0, 0)),
                   pl.BlockSpec((1, B_HEADS, B_DV, B_DK), lambda i: (i, 0, 0, 0))],
        out_shape=[jax.ShapeDtypeStruct((bsz, B_DV, B_HEADS), F32),
                   jax.ShapeDtypeStruct((bsz, B_HEADS, B_DV, B_DK), F32)],
        compiler_params=_cparams(("arbitrary",)),
        name="hgrn_sample",
    )(q_hk, f_hk, v_vh, g_vh, lb_hk, ng_col, state_t)


def _rows_last(cache):
    return jnp.moveaxis(cache, -4, -1)


def _sample_layer(x, cache_cmp, cache_sel, cache_win, state, page_table, w_pad, cmpw, w_s, lb_logits, norm_g,
                  w_a, w_b, w_o, ln1_g, ln1_b, w_up, w_down, ln2_g, ln2_b):
    bsz, t, _ = x.shape
    assert t == 1, "the sample group decodes one token per request"
    n_pages = page_table.shape[1]
    past = n_pages * PAGE_SIZE
    x2d = x.reshape(bsz, D_MODEL)
    h = _project(x2d, w_pad)
    h3 = h.reshape(bsz, 1, D_PAD)
    q3 = h3[:, :, C_QA:C_QA + Q_A]
    wl, wb, posf = cmpw
    o_c, idx = _cmp_sample(page_table, _rows_last(cache_cmp), w_s, posf, wb, q3, past)
    idx2 = jnp.pad(idx[:, :KV_HEADS, :N_POOL_SEL], ((0, 0), (0, 0), (0, 1))).reshape(bsz, KV_HEADS * N_SELECT)
    o_rd = _sel_win_sample(page_table, idx2, _rows_last(cache_sel), _rows_last(cache_win),
                           q3, h3[:, :, C_KVS:C_KVS + 256], h3[:, :, C_KVW:C_KVW + 256],
                           h3[:, :, C_GL:C_GL + 256], o_c, past)
    o_a = jnp.swapaxes(o_rd[:, :GROUP].reshape(bsz, GROUP, KV_HEADS, HEAD_DIM), 1, 2).reshape(bsz, Q_A)
    to_hk = lambda a: a.reshape(-1, B_HEADS, B_DK)
    to_vh = lambda a: jnp.swapaxes(a.reshape(-1, B_HEADS, B_DV), 1, 2)
    o_vh, s_t = _hgrn_sample(to_hk(h[:, C_QB:C_QB + Q_B]), to_hk(h[:, C_FB:C_FB + Q_B]),
                             to_vh(h[:, C_IB:C_IB + I_B]), to_vh(h[:, C_GB:C_GB + I_B]),
                             to_hk(lb_logits), norm_g.reshape(B_DV, 1), jnp.swapaxes(state, 2, 3))
    o_b = jnp.swapaxes(o_vh, 1, 2).reshape(bsz, I_B)
    x1 = _merge(x2d, o_a, o_b, h, w_a, w_b, w_o, ln1_g, ln1_b)
    y = _mlp(x1, w_up, w_down, ln2_g, ln2_b)
    return y.reshape(bsz, 1, D_MODEL), h, jnp.swapaxes(s_t, 2, 3)


def _cmp_sample_weights(cmp_w):
    w = cmp_w.reshape(2, 2, CMP_STRIDE, HEAD_DIM, HEAD_DIM)
    base = jnp.transpose(w, (0, 2, 3, 1, 4))
    ws = jnp.stack([_at_slot(base, g, 4) for g in range(2)], axis=2)
    return ws.reshape(2, CMP_STRIDE // 2, 256, 256).astype(BF16)


def kernel(x_prompt, x_sample, cache_cmp_kv, cache_sel_kv, cache_win_kv, state_hgrn, page_table,
           w_in, cmp_w, cmp_pos, hgrn_lb_logits, hgrn_norm_g, w_br_a, w_br_b, w_out,
           ln1_g, ln1_b, w_up, w_down, ln2_g, ln2_b):
    assert w_in.shape[0] == 1, "one layer"
    b, t, _ = x_prompt.shape
    bsz = x_sample.shape[0]
    assert t % HB == 0 and t % QB == 0 and WINDOW + QB <= t <= 128 * SEL_BLOCK
    w_pad = _reorder_w_in(w_in[0])
    cmpw = _cmp_weights(cmp_w[0], cmp_pos[0])
    w_s = _cmp_sample_weights(cmp_w[0])
    dense = (w_br_a[0].astype(BF16), w_br_b[0].astype(BF16), w_out[0].astype(BF16), ln1_g, ln1_b,
             w_up[0].astype(BF16), w_down[0].astype(BF16), ln2_g, ln2_b)

    y_p, h_p, s_p = _prompt_layer(x_prompt, w_pad, cmpw, hgrn_lb_logits, hgrn_norm_g, *dense)
    y_s, h_s, s_s = _sample_layer(x_sample, cache_cmp_kv[0], cache_sel_kv[0], cache_win_kv[0], state_hgrn[0],
                                  page_table, w_pad, cmpw, w_s, hgrn_lb_logits, hgrn_norm_g, *dense)

    win_p = min(WINDOW, t)
    kvw_p = _kv_out(h_p, C_KVW, b, t)
    kvw_s = _kv_out(h_s, C_KVW, bsz, 1)
    new_win_s = jnp.concatenate([cache_win_kv[0], kvw_s], axis=1)[:, -min(WINDOW, cache_win_kv.shape[2] + 1):]
    return (y_p, y_s,
            _kv_out(h_p, C_KVC, b, t)[None], _kv_out(h_p, C_KVS, b, t)[None], kvw_p[:, -win_p:][None], s_p[None],
            _kv_out(h_s, C_KVC, bsz, 1)[None], _kv_out(h_s, C_KVS, bsz, 1)[None], new_win_s[None], s_s[None])
```

```python
import functools

import numpy as np
import jax
import jax.numpy as jnp
from jax import lax
from jax.experimental import pallas as pl
from jax.experimental.pallas import tpu as pltpu

F32 = jnp.float32
BF16 = jnp.bfloat16

D_MODEL = 1024
HEAD_DIM = 64
A_HEADS = 8
KV_HEADS = 2
GROUP = A_HEADS // KV_HEADS
CMP_STRIDE = 16
CMP_LEN = 32
SEL_BLOCK = 64
N_SELECT = 16
WINDOW = 512
PAGE_SIZE = 128
B_HEADS = 8
B_DK = 128
B_DV = 64
HGRN_CHUNK = 32
D_FF = 4 * D_MODEL
DEEPNORM_ALPHA = 2.0 ** 0.25
LN_EPS = 1e-5
NEG = -1e30
FORCE = 1e6
ATTN_SCALE = HEAD_DIM ** -0.5
Q_A = A_HEADS * HEAD_DIM
KV_A = 2 * KV_HEADS * HEAD_DIM
GATE_A = 3 * A_HEADS
Q_B = B_HEADS * B_DK
I_B = B_HEADS * B_DV

C_MG = 0
C_QB = 2048
C_FB = 3072
C_QA = 4096
C_IB = 4608
C_GB = 5120
C_KVC = 5632
C_KVS = 5888
C_KVW = 6144
C_GL = 6400
D_PAD = 6656
PROJ_TN = 1664

VMEM_LIMIT = 56 * 1024 * 1024
UNSEL = float(2.0 ** 100)


def _cparams(sem):
    return pltpu.CompilerParams(dimension_semantics=sem, vmem_limit_bytes=VMEM_LIMIT)


def _proj_perm():
    perm = np.full((D_PAD,), -1, np.int64)
    o_qa, o_kvc, o_kvs, o_kvw = 0, Q_A, Q_A + KV_A, Q_A + 2 * KV_A
    o_gl = Q_A + 3 * KV_A
    o_qb = o_gl + GATE_A
    o_fb = o_qb + Q_B
    o_ib = o_fb + Q_B
    o_gb = o_ib + I_B
    o_mg = o_gb + I_B
    perm[C_MG:C_MG + 2 * D_MODEL] = o_mg + np.arange(2 * D_MODEL)
    perm[C_QB:C_QB + Q_B] = o_qb + np.arange(Q_B)
    perm[C_FB:C_FB + Q_B] = o_fb + np.arange(Q_B)
    perm[C_QA:C_QA + Q_A] = o_qa + np.arange(Q_A)
    perm[C_IB:C_IB + I_B] = o_ib + np.arange(I_B)
    perm[C_GB:C_GB + I_B] = o_gb + np.arange(I_B)
    for new, old in ((C_KVC, o_kvc), (C_KVS, o_kvs), (C_KVW, o_kvw)):
        for g in range(KV_HEADS):
            for c in range(2):
                dst = new + g * 128 + c * 64
                src = old + c * 128 + g * 64
                perm[dst:dst + 64] = src + np.arange(64)
    for g in range(KV_HEADS):
        perm[C_GL + g * 128:C_GL + g * 128 + 12] = o_gl + g * 12 + np.arange(12)
    return perm


def _perm_runs(perm):
    runs = []
    for col in perm:
        if runs and ((col < 0 and runs[-1][0] < 0) or (col >= 0 and runs[-1][0] >= 0
                                                      and runs[-1][0] + runs[-1][1] == col)):
            runs[-1][1] += 1
        else:
            runs.append([int(col), 1])
    return [tuple(r) for r in runs]


_PERM_RUNS = _perm_runs(_proj_perm())


def _reorder_w_in(w):
    pieces = [jnp.zeros((w.shape[0], n), w.dtype) if s < 0 else w[:, s:s + n] for s, n in _PERM_RUNS]
    return jnp.concatenate(pieces, axis=1).astype(BF16)


def _proj_kernel(x_ref, w_ref, o_ref):
    xb = x_ref[...].astype(BF16)
    for j in range(D_PAD // PROJ_TN):
        cols = slice(j * PROJ_TN, (j + 1) * PROJ_TN)
        o_ref[:, cols] = jnp.dot(xb, w_ref[:, cols], preferred_element_type=F32)


def _project(x2d, w_pad):
    n = x2d.shape[0]
    tm = min(512, n)
    return pl.pallas_call(
        _proj_kernel,
        grid=(n // tm,),
        in_specs=[pl.BlockSpec((tm, D_MODEL), lambda i: (i, 0)),
                  pl.BlockSpec((D_MODEL, D_PAD), lambda i: (0, 0), pipeline_mode=pl.Buffered(1))],
        out_specs=pl.BlockSpec((tm, D_PAD), lambda i: (i, 0)),
        out_shape=jax.ShapeDtypeStruct((n, D_PAD), F32),
        compiler_params=_cparams(("arbitrary",)),
        name="proj",
    )(x2d, w_pad)


def _layer_norm(v, g, b):
    mu = jnp.mean(v, axis=-1, keepdims=True)
    d = v - mu
    var = jnp.mean(d * d, axis=-1, keepdims=True)
    return d * lax.rsqrt(var + LN_EPS) * g + b


def _merge_kernel(x_ref, oa_ref, ob_ref, mga_ref, mgb_ref, wa_ref, wb_ref, wo_ref, g_ref, b_ref, o_ref):
    br_a = jnp.dot(oa_ref[...].astype(BF16), wa_ref[...], preferred_element_type=F32)
    br_b = jnp.dot(ob_ref[...].astype(BF16), wb_ref[...], preferred_element_type=F32)
    merged = jax.nn.sigmoid(mga_ref[...]) * br_a + jax.nn.sigmoid(mgb_ref[...]) * br_b
    mix = jnp.dot(merged.astype(BF16), wo_ref[...], preferred_element_type=F32)
    o_ref[...] = _layer_norm(DEEPNORM_ALPHA * x_ref[...] + mix, g_ref[...], b_ref[...])


def _merge(x2d, o_a, o_b, h, w_a, w_b, w_o, ln_g, ln_b):
    n = x2d.shape[0]
    tm = min(256, n)
    const = lambda i: (0, 0)
    return pl.pallas_call(
        _merge_kernel,
        grid=(n // tm,),
        in_specs=[pl.BlockSpec((tm, D_MODEL), lambda i: (i, 0)),
                  pl.BlockSpec((tm, Q_A), lambda i: (i, 0)),
                  pl.BlockSpec((tm, I_B), lambda i: (i, 0)),
                  pl.BlockSpec((tm, D_MODEL), lambda i: (i, C_MG // D_MODEL)),
                  pl.BlockSpec((tm, D_MODEL), lambda i: (i, C_MG // D_MODEL + 1)),
                  pl.BlockSpec((Q_A, D_MODEL), const),
                  pl.BlockSpec((I_B, D_MODEL), const),
                  pl.BlockSpec((D_MODEL, D_MODEL), const),
                  pl.BlockSpec((1, D_MODEL), const),
                  pl.BlockSpec((1, D_MODEL), const)],
        out_specs=pl.BlockSpec((tm, D_MODEL), lambda i: (i, 0)),
        out_shape=jax.ShapeDtypeStruct((n, D_MODEL), F32),
        compiler_params=_cparams(("arbitrary",)),
        name="merge_ln1",
    )(x2d, o_a, o_b, h, h, w_a, w_b, w_o, ln_g, ln_b)


def _mlp_kernel(x_ref, wu_ref, wd_ref, g_ref, b_ref, o_ref, xb_ref, acc_ref):
    j = pl.program_id(1)

    @pl.when(j == 0)
    def _():
        xb_ref[...] = x_ref[...].astype(BF16)
        acc_ref[...] = jnp.zeros_like(acc_ref)

    u = jnp.dot(xb_ref[...], wu_ref[...], preferred_element_type=F32)
    u = jnp.maximum(u, 0.0)
    acc_ref[...] += jnp.dot((u * u).astype(BF16), wd_ref[...], preferred_element_type=F32)

    @pl.when(j == pl.num_programs(1) - 1)
    def _():
        o_ref[...] = _layer_norm(DEEPNORM_ALPHA * x_ref[...] + acc_ref[...], g_ref[...], b_ref[...])


def _mlp(x1, w_up, w_down, ln_g, ln_b):
    n = x1.shape[0]
    tm = min(1024, n)
    tf = 1024
    return pl.pallas_call(
        _mlp_kernel,
        grid=(n // tm, D_FF // tf),
        in_specs=[pl.BlockSpec((tm, D_MODEL), lambda i, j: (i, 0)),
                  pl.BlockSpec((D_MODEL, tf), lambda i, j: (0, j)),
                  pl.BlockSpec((tf, D_MODEL), lambda i, j: (j, 0)),
                  pl.BlockSpec((1, D_MODEL), lambda i, j: (0, 0)),
                  pl.BlockSpec((1, D_MODEL), lambda i, j: (0, 0))],
        out_specs=pl.BlockSpec((tm, D_MODEL), lambda i, j: (i, 0)),
        out_shape=jax.ShapeDtypeStruct((n, D_MODEL), F32),
        scratch_shapes=[pltpu.VMEM((tm, D_MODEL), BF16), pltpu.VMEM((tm, D_MODEL), F32)],
        compiler_params=_cparams(("arbitrary", "arbitrary")),
        name="mlp_ln2",
    )(x1, w_up, w_down, ln_g, ln_b)


def _dot01(a01, x):
    hi = x.astype(BF16)
    r1 = x - hi.astype(F32)
    mid = r1.astype(BF16)
    lo = (r1 - mid.astype(F32)).astype(BF16)
    out = jnp.dot(a01, hi, preferred_element_type=F32)
    out += jnp.dot(a01, mid, preferred_element_type=F32)
    out += jnp.dot(a01, lo, preferred_element_type=F32)
    return out


def _x_dot01(x, b01):
    hi = x.astype(BF16)
    r1 = x - hi.astype(F32)
    mid = r1.astype(BF16)
    lo = (r1 - mid.astype(F32)).astype(BF16)
    out = jnp.dot(hi, b01, preferred_element_type=F32)
    out += jnp.dot(mid, b01, preferred_element_type=F32)
    out += jnp.dot(lo, b01, preferred_element_type=F32)
    return out


_NT = (((1,), (1,)), ((), ()))
_TN = (((0,), (0,)), ((), ()))


def _cmp_prompt_kernel(kv_ref, w_ref, posf_ref, wb_ref, o_ref, *, n_ch):
    acc = jnp.zeros((n_ch, 256), F32)
    for l in range(CMP_STRIDE):
        xl = kv_ref[pl.ds(l, n_ch, stride=CMP_STRIDE), :].astype(BF16)
        acc += jnp.dot(xl, w_ref[l], preferred_element_type=F32)
    bias = jnp.dot(posf_ref[...].astype(BF16), wb_ref[...], preferred_element_type=F32)[0:1]
    nxt = pltpu.roll(acc[:, 128:], n_ch - 1, 0)
    o_ref[0] = (acc[:, :128] + nxt + bias).astype(BF16)


def _at_slot(a, slot, axis):
    pads = [(0, 0)] * (a.ndim + 1)
    pads[axis] = (slot, 1 - slot)
    return jnp.pad(jnp.expand_dims(a, axis), pads)


def _cmp_weights(cmp_w, cmp_pos):
    w = cmp_w.reshape(2, 2, CMP_STRIDE, HEAD_DIM, HEAD_DIM)
    wl = jnp.stack([_at_slot(jnp.transpose(w[c], (1, 2, 0, 3)), c, 3) for c in range(2)], axis=1)
    wb = jnp.stack([_at_slot(cmp_w[c].reshape(CMP_LEN * HEAD_DIM, HEAD_DIM), c, 1) for c in range(2)])
    wl = wl.reshape(CMP_STRIDE, 128, 256).astype(BF16)
    wb = wb.reshape(2 * CMP_LEN * HEAD_DIM, 128).astype(BF16)
    posf = jnp.broadcast_to(cmp_pos.reshape(1, 2 * CMP_LEN * HEAD_DIM), (8, 2 * CMP_LEN * HEAD_DIM))
    return wl, wb, posf


def _cmp_prompt(h, b, t, wl, wb, posf):
    n_ch = t // CMP_STRIDE
    return pl.pallas_call(
        functools.partial(_cmp_prompt_kernel, n_ch=n_ch),
        grid=(b, KV_HEADS),
        in_specs=[pl.BlockSpec((t, 128), lambda i, g: (i, C_KVC // 128 + g)),
                  pl.BlockSpec((CMP_STRIDE, 128, 256), lambda i, g: (0, 0, 0)),
                  pl.BlockSpec((8, 4096), lambda i, g: (0, 0)),
                  pl.BlockSpec((4096, 128), lambda i, g: (0, 0))],
        out_specs=pl.BlockSpec((1, n_ch, 128), lambda i, g: (i, 0, g)),
        out_shape=jax.ShapeDtypeStruct((b, n_ch, 256), BF16),
        compiler_params=_cparams(("arbitrary", "arbitrary")),
        name="cmp_prompt",
    )(h, wl, posf, wb)


QB = 256
PIECE = 256
KT = 256
WT = 128


ROWS = GROUP * QB
TL_COUNT = 64


def _head_major(q):
    return jnp.concatenate([q[:, r * 64:(r + 1) * 64] for r in range(GROUP)], axis=0) * ATTN_SCALE


def _head_slope(g, r):
    return jnp.where(g == 0, 2.0 ** -(r + 1), 2.0 ** -(r + 1 + GROUP)).astype(F32)


def _slope_row(g):
    r_ix = lax.broadcasted_iota(jnp.int32, (1, ROWS), 1) // QB
    row = jnp.zeros((1, ROWS), F32)
    for r in range(GROUP):
        row = jnp.where(r_ix == r, _head_slope(g, r), row)
    return row


def _nsa_select_kernel(q_ref, kc_ref, oc_ref, un_ref, tl_ref, *, n_ch):
    g = pl.program_id(1)
    c = pl.program_id(2)
    t0 = c * QB
    qpad = jnp.concatenate([_head_major(q_ref[...]), jnp.zeros((ROWS, 64), F32)], axis=1).astype(BF16)
    kc = kc_ref[0]
    s = lax.dot_general(kc, qpad, _NT, preferred_element_type=F32)
    pos_row = t0 + lax.broadcasted_iota(jnp.int32, (1, ROWS), 1) % QB
    n_col = lax.broadcasted_iota(jnp.int32, (n_ch, 1), 0)
    dist_i = pos_row - (n_col * CMP_STRIDE + (CMP_LEN - 1))
    valid = jnp.logical_and(dist_i >= 0, n_col < n_ch - 1)
    s = jnp.where(valid, s - _slope_row(g) * dist_i.astype(F32), NEG)
    m = jnp.max(s, axis=0, keepdims=True)
    p = jnp.where(valid, jnp.exp(s - m), 0.0)
    p = p * (1.0 / jnp.maximum(jnp.sum(p, axis=0, keepdims=True), 1e-30))
    o_c = lax.dot_general(p.astype(BF16), kc, _TN, preferred_element_type=F32)
    for r in range(GROUP):
        oc_ref[:, r * 64:(r + 1) * 64] = o_c[r * QB:(r + 1) * QB, 64:128]

    imp = p[:, 0:QB]
    for r in range(1, GROUP):
        imp = imp + p[:, r * QB:(r + 1) * QB]
    ratio = SEL_BLOCK // CMP_STRIDE
    pool_t = (lax.broadcasted_iota(jnp.int32, (128, n_ch), 1) // ratio
              == lax.broadcasted_iota(jnp.int32, (128, n_ch), 0)).astype(BF16)
    imp_blk = _dot01(pool_t, imp)
    blk = lax.broadcasted_iota(jnp.int32, (128, QB), 0)
    pos_q = t0 + lax.broadcasted_iota(jnp.int32, (1, QB), 1)
    cur = pos_q // SEL_BLOCK
    forced = jnp.logical_or(jnp.logical_or(blk == 0, blk == cur), blk == cur - 1)
    allowed = blk * SEL_BLOCK <= pos_q
    work = jnp.where(jnp.logical_and(allowed, jnp.logical_not(forced)), imp_blk, -jnp.inf)
    sel = forced
    for _ in range(N_SELECT - 3):
        mx = jnp.max(work, axis=0, keepdims=True)
        first = jnp.min(jnp.where(work == mx, blk, 128), axis=0, keepdims=True)
        pick = blk == first
        sel = jnp.logical_or(sel, pick)
        work = jnp.where(pick, -jnp.inf, work)
    sel = jnp.logical_and(sel, allowed)
    un_ref[...] = jnp.transpose(jnp.where(sel, 0.0, -UNSEL)).astype(BF16)

    tile_of = (lax.broadcasted_iota(jnp.int32, (128, 128), 1) // (KT // SEL_BLOCK)
               == lax.broadcasted_iota(jnp.int32, (128, 128), 0)).astype(BF16)
    sel_b = jnp.where(sel, 1.0, 0.0).astype(BF16)
    cnt = jnp.sum(jnp.dot(tile_of, sel_b, preferred_element_type=F32), axis=1, keepdims=True)
    tile_col = lax.broadcasted_iota(jnp.int32, (128, 1), 0)
    flag = jnp.logical_and(cnt > 0.0, tile_col < t0 // KT)
    flag_m = jnp.where(jnp.broadcast_to(flag, (128, 128)), 1.0, 0.0).astype(BF16)
    row_i = lax.broadcasted_iota(jnp.int32, (128, 128), 0)
    lane_i = lax.broadcasted_iota(jnp.int32, (128, 128), 1)
    before = jnp.dot((lane_i < row_i).astype(BF16), flag_m, preferred_element_type=F32)
    slot = jnp.where(jnp.logical_and(flag, before == lane_i.astype(F32)), 1.0, 0.0).astype(BF16)
    j_rows = lax.broadcasted_iota(jnp.int32, (8, 128), 1).astype(BF16)
    listed = jnp.dot(j_rows, slot, preferred_element_type=F32)
    total = jnp.dot(jnp.ones((8, 128), BF16), flag_m, preferred_element_type=F32)
    lane8 = lax.broadcasted_iota(jnp.int32, (8, 128), 1)
    tl_ref[0] = jnp.where(lane8 == TL_COUNT, total, listed).astype(jnp.int32)


def _nsa_select(h, kcmp, b, t):
    n_ch = t // CMP_STRIDE
    nq = t // QB
    steps = b * KV_HEADS * nq
    return pl.pallas_call(
        functools.partial(_nsa_select_kernel, n_ch=n_ch),
        grid=(b, KV_HEADS, nq),
        in_specs=[pl.BlockSpec((QB, 256), lambda i, g, c: (i * nq + c, C_QA // 256 + g)),
                  pl.BlockSpec((1, n_ch, 128), lambda i, g, c: (i, 0, g))],
        out_specs=[pl.BlockSpec((QB, 256), lambda i, g, c: (i * nq + c, g)),
                   pl.BlockSpec((QB, 128), lambda i, g, c: (i * nq + c, g)),
                   pl.BlockSpec((1, 8, 128), lambda i, g, c: ((i * KV_HEADS + g) * nq + c, 0, 0))],
        out_shape=[jax.ShapeDtypeStruct((b * t, Q_A), F32),
                   jax.ShapeDtypeStruct((b * t, KV_HEADS * 128), BF16),
                   jax.ShapeDtypeStruct((steps, 8, 128), jnp.int32)],
        compiler_params=_cparams(("arbitrary", "arbitrary", "arbitrary")),
        name="nsa_select",
    )(h, kcmp)


def _nsa_attend_kernel(tl_ref, q_ref, un_ref, gl_ref, oc_ref, ks_ref, kw_ref, o_ref,
                       ksb_ref, ksa_ref, kwb_ref, kwa_ref, *, nq):
    i = pl.program_id(0)
    g = pl.program_id(1)
    c = pl.program_id(2)
    step = (i * KV_HEADS + g) * nq + c
    t0 = c * QB
    half = PIECE
    n_piece = ROWS // PIECE
    qs = _head_major(q_ref[...])
    slope_col = jnp.concatenate([jnp.broadcast_to(_head_slope(g, r), (QB, 1)) for r in range(GROUP)], axis=0)
    lane64 = lax.broadcasted_iota(jnp.int32, (ROWS, 64), 1)
    ali_q = jnp.where(lane64 == 0, -64.0 * slope_col, jnp.where(lane64 == 1, -slope_col, 0.0))
    q_win = jnp.concatenate([qs, ali_q], axis=1).astype(BF16)
    un = un_ref[...]
    q_sel = jnp.concatenate([q_win, jnp.concatenate([un] * GROUP, axis=0)], axis=1)
    slope_row = _slope_row(g)

    @pl.when(c == 0)
    def _():
        def fill(j, carry):
            r0 = pl.multiple_of(j * KT, KT)
            row = lax.broadcasted_iota(jnp.int32, (KT, 128), 0)
            lane = lax.broadcasted_iota(jnp.int32, (KT, 128), 1)
            for src, dst_kv, dst_aug, tile in ((ks_ref, ksb_ref, ksa_ref, KT), (kw_ref, kwb_ref, kwa_ref, WT)):
                kv_b = src[pl.ds(r0, KT), :].astype(BF16)
                d = (QB - 1) - row % tile
                ali = jnp.where(lane == 64, d >> 6, jnp.where(lane == 65, d & 63, 0)).astype(F32).astype(BF16)
                dst_kv[pl.ds(r0, KT), :] = kv_b
                dst_aug[pl.ds(r0, KT), 0:128] = jnp.where(lane < 64, kv_b, ali)
            ksa_ref[pl.ds(r0, KT), 128:256] = jnp.where((r0 + row) // SEL_BLOCK == lane, 1.0, 0.0).astype(BF16)
            return carry

        lax.fori_loop(0, ks_ref.shape[0] // KT, fill, 0)

    def update(carry, s, kv_b, offset):
        out = []
        for hh, ((m, l, acc), sh) in enumerate(zip(carry, s)):
            shift = slope_row[:, hh * half:(hh + 1) * half] * offset
            m_new = jnp.maximum(m, jnp.max(sh, axis=0, keepdims=True) - shift)
            alpha = jnp.exp(m - m_new)
            p = jnp.exp(sh - (m_new + shift))
            l = alpha * l + jnp.sum(p, axis=0, keepdims=True)
            acc = alpha * acc + lax.dot_general(kv_b, p.astype(BF16), _TN, preferred_element_type=F32)
            out.append((m_new, l, acc))
        return tuple(out)

    def init():
        return tuple((jnp.full((1, half), NEG, F32), jnp.zeros((1, half), F32), jnp.zeros((128, half), F32))
                     for _ in range(n_piece))

    def tile(carry, aug_ref, kvb_ref, q_side, k0, n_keys, keep):
        k_aug = aug_ref[pl.ds(k0, n_keys), :]
        s = []
        for hh in range(n_piece):
            sh = lax.dot_general(k_aug, q_side[hh * half:(hh + 1) * half], _NT, preferred_element_type=F32)
            if keep is not None:
                sh = jnp.where(jnp.concatenate([keep] * (half // QB), axis=1), sh, NEG)
            s.append(sh)
        return update(carry, s, kvb_ref[pl.ds(k0, n_keys), :], (t0 - k0).astype(F32))

    k_tail = pl.multiple_of((t0 // KT) * KT, KT)
    key_x = lax.broadcasted_iota(jnp.int32, (KT, QB), 0)
    q_x = lax.broadcasted_iota(jnp.int32, (KT, QB), 1)
    carry = tile(init(), ksa_ref, ksb_ref, q_sel, k_tail, KT, k_tail + key_x <= t0 + q_x)

    def sel_listed(n, carry):
        return tile(carry, ksa_ref, ksb_ref, q_sel, pl.multiple_of(tl_ref[step, n] * KT, KT), KT, None)

    n_listed = tl_ref[step, TL_COUNT]
    carry = lax.fori_loop(0, n_listed // 2, lambda n, cr: sel_listed(2 * n + 1, sel_listed(2 * n, cr)), carry)
    carry = lax.fori_loop(0, n_listed % 2, lambda n, cr: sel_listed(n_listed - 1, cr), carry)
    o_sel = [acc * (1.0 / l) for (_, l, acc) in carry]

    w_keys = WINDOW + QB
    k0w = pl.multiple_of(jnp.maximum(t0 - WINDOW, 0), WT)
    k_aug = kwa_ref[pl.ds(k0w, w_keys), :]
    kv_w = kwb_ref[pl.ds(k0w, w_keys), :]
    dq = lax.broadcasted_iota(jnp.int32, (WT, QB), 1) - lax.broadcasted_iota(jnp.int32, (WT, QB), 0)
    offs = [t0 - (k0w + gi * WT) for gi in range(w_keys // WT)]
    keeps = [jnp.concatenate([jnp.logical_and(dq + off >= 0, dq + off < WINDOW)] * (half // QB), axis=1)
             for off in offs]
    o_win = []
    for hh in range(n_piece):
        sl = slope_row[:, hh * half:(hh + 1) * half]
        sh = lax.dot_general(k_aug, q_win[hh * half:(hh + 1) * half], _NT, preferred_element_type=F32)
        parts = [jnp.where(keep, sh[gi * WT:(gi + 1) * WT], NEG) for gi, keep in enumerate(keeps)]
        shifts = [sl * off.astype(F32) for off in offs]
        m_w = functools.reduce(jnp.maximum, [jnp.max(pt, axis=0, keepdims=True) - sf
                                             for pt, sf in zip(parts, shifts)])
        ps = [jnp.exp(pt - (m_w + sf)) for pt, sf in zip(parts, shifts)]
        l_w = functools.reduce(jnp.add, [jnp.sum(p, axis=0, keepdims=True) for p in ps])
        acc = lax.dot_general(kv_w, jnp.concatenate(ps, axis=0).astype(BF16), _TN, preferred_element_type=F32)
        o_win.append(acc * (1.0 / l_w))

    gates = jax.nn.sigmoid(gl_ref[...])
    o_c = oc_ref[...]
    for r in range(GROUP):
        hh, cs = divmod(r * QB, half)
        o_s = jnp.transpose(o_sel[hh][:, cs:cs + QB])[:, 64:128]
        o_w = jnp.transpose(o_win[hh][:, cs:cs + QB])[:, 64:128]
        o_ref[:, r * 64:(r + 1) * 64] = (gates[:, 3 * r:3 * r + 1] * o_c[:, r * 64:(r + 1) * 64]
                                         + gates[:, 3 * r + 1:3 * r + 2] * o_s
                                         + gates[:, 3 * r + 2:3 * r + 3] * o_w)


def _nsa_attend(tiles, h, unsel, o_c, b, t):
    nq = t // QB
    grid_spec = pltpu.PrefetchScalarGridSpec(
        num_scalar_prefetch=1,
        grid=(b, KV_HEADS, nq),
        in_specs=[pl.BlockSpec((QB, 256), lambda i, g, c, tl: (i * nq + c, C_QA // 256 + g)),
                  pl.BlockSpec((QB, 128), lambda i, g, c, tl: (i * nq + c, g)),
                  pl.BlockSpec((QB, 128), lambda i, g, c, tl: (i * nq + c, C_GL // 128 + g)),
                  pl.BlockSpec((QB, 256), lambda i, g, c, tl: (i * nq + c, g)),
                  pl.BlockSpec((t, 128), lambda i, g, c, tl: (i, C_KVS // 128 + g)),
                  pl.BlockSpec((t, 128), lambda i, g, c, tl: (i, C_KVW // 128 + g))],
        out_specs=pl.BlockSpec((QB, 256), lambda i, g, c, tl: (i * nq + c, g)),
        scratch_shapes=[pltpu.VMEM((t, 128), BF16), pltpu.VMEM((t, 256), BF16),
                        pltpu.VMEM((t, 128), BF16), pltpu.VMEM((t, 128), BF16)],
    )
    return pl.pallas_call(
        functools.partial(_nsa_attend_kernel, nq=nq),
        grid_spec=grid_spec,
        out_shape=jax.ShapeDtypeStruct((b * t, Q_A), F32),
        compiler_params=_cparams(("arbitrary", "arbitrary", "arbitrary")),
        name="nsa_attend",
    )(tiles, h, unsel, h, o_c, h, h)


def _nsa_prompt(h, kcmp, b, t):
    o_c, unsel, tl = _nsa_select(h, kcmp, b, t)
    return _nsa_attend(tl[:, 0, :], h, unsel, o_c, b, t)


HB = 256


def _lower_bound(lb_ref):
    z = lb_ref[...]
    e = jnp.exp(z - jnp.max(z, axis=0, keepdims=True))
    return e[0:1] / jnp.sum(e, axis=0, keepdims=True)


def _hgrn_prompt_kernel(q_ref, f_ref, i_ref, g_ref, lb_ref, ng_ref, o_ref, s_out_ref, s_ref):
    tb = pl.program_id(1)

    @pl.when(tb == 0)
    def _():
        s_ref[...] = jnp.zeros_like(s_ref)

    n_c = HB // HGRN_CHUNK
    lb = _lower_bound(lb_ref)
    f = lb + (1.0 - lb) * jax.nn.sigmoid(f_ref[...])
    log_f = jnp.log(f)
    row = lax.broadcasted_iota(jnp.int32, (HB, HB), 0)
    col = lax.broadcasted_iota(jnp.int32, (HB, HB), 1)
    tril = jnp.logical_and(col <= row, row // HGRN_CHUNK == col // HGRN_CHUNK)
    lc = _dot01(tril.astype(BF16), log_f)
    lc3 = lc.reshape(n_c, HGRN_CHUNK, Q_B)
    lend = jnp.broadcast_to(lc3[:, HGRN_CHUNK - 1:HGRN_CHUNK, :], lc3.shape).reshape(HB, Q_B)
    q_t = (q_ref[...] * jnp.exp(lc)).astype(BF16)
    k = 1.0 - f
    k_t = (k * jnp.exp(-lc)).astype(BF16)
    k_e = (k * jnp.exp(lend - lc)).astype(BF16)
    dec = jnp.exp(lend)
    v_all = i_ref[...]
    gate = g_ref[...]
    ng = ng_ref[...]
    for hh in range(B_HEADS):
        ks = slice(hh * B_DK, (hh + 1) * B_DK)
        vs = slice(hh * B_DV, (hh + 1) * B_DV)
        v = v_all[:, vs].astype(BF16)
        a = lax.dot_general(q_t[:, ks], k_t[:, ks], _NT, preferred_element_type=F32)
        a = jnp.where(tril, a, 0.0).astype(BF16)
        o = jnp.dot(a, v, preferred_element_type=F32)
        st = s_ref[hh]
        inter = []
        for cc in range(n_c):
            rs = slice(cc * HGRN_CHUNK, (cc + 1) * HGRN_CHUNK)
            inter.append(lax.dot_general(q_t[rs, ks], st.astype(BF16), _NT, preferred_element_type=F32))
            u = lax.dot_general(v[rs], k_e[rs, ks], _TN, preferred_element_type=F32)
            last = (cc + 1) * HGRN_CHUNK - 1
            st = dec[last:last + 1, ks] * st + u
        s_ref[hh] = st
        o = o + jnp.concatenate(inter, axis=0)
        o = o * lax.rsqrt(jnp.mean(o * o, axis=-1, keepdims=True) + LN_EPS) * ng
        gt = gate[:, vs]
        o_ref[:, vs] = o * (gt * jax.nn.sigmoid(gt))

    @pl.when(tb == pl.num_programs(1) - 1)
    def _():
        s_out_ref[0] = s_ref[...]


def _hgrn_prompt(h, b, t, lb_logits, norm_g):
    nt = t // HB
    return pl.pallas_call(
        _hgrn_prompt_kernel,
        grid=(b, nt),
        in_specs=[pl.BlockSpec((HB, Q_B), lambda i, j: (i * nt + j, C_QB // Q_B)),
                  pl.BlockSpec((HB, Q_B), lambda i, j: (i * nt + j, C_FB // Q_B)),
                  pl.BlockSpec((HB, I_B), lambda i, j: (i * nt + j, C_IB // I_B)),
                  pl.BlockSpec((HB, I_B), lambda i, j: (i * nt + j, C_GB // I_B)),
                  pl.BlockSpec((2, Q_B), lambda i, j: (0, 0)),
                  pl.BlockSpec((1, B_DV), lambda i, j: (0, 0))],
        out_specs=[pl.BlockSpec((HB, I_B), lambda i, j: (i * nt + j, 0)),
                   pl.BlockSpec((1, B_HEADS, B_DV, B_DK), lambda i, j: (i, 0, 0, 0))],
        out_shape=[jax.ShapeDtypeStruct((b * t, I_B), F32),
                   jax.ShapeDtypeStruct((b, B_HEADS, B_DV, B_DK), F32)],
        scratch_shapes=[pltpu.VMEM((B_HEADS, B_DV, B_DK), F32)],
        compiler_params=_cparams(("arbitrary", "arbitrary")),
        name="hgrn_prompt",
    )(h, h, h, h, lb_logits, norm_g)


def _prompt_layer(x, w_pad, cmpw, lb_logits, norm_g, w_a, w_b, w_o, ln1_g, ln1_b, w_up, w_down, ln2_g, ln2_b):
    b, t, _ = x.shape
    x2d = x.reshape(b * t, D_MODEL)
    h = _project(x2d, w_pad)
    kcmp = _cmp_prompt(h, b, t, *cmpw)
    o_a = _nsa_prompt(h, kcmp, b, t)
    o_b, s_end = _hgrn_prompt(h, b, t, lb_logits, norm_g)
    x1 = _merge(x2d, o_a, o_b, h, w_a, w_b, w_o, ln1_g, ln1_b)
    y = _mlp(x1, w_up, w_down, ln2_g, ln2_b)
    return y.reshape(b, t, D_MODEL), h, jnp.swapaxes(s_end, 2, 3)


def _kv_out(h, col, b, t):
    kv = h[:, col:col + 256].reshape(b, t, KV_HEADS, 2, HEAD_DIM)
    return jnp.swapaxes(kv, 2, 3)


LAND_PAGES = 4
CH_PER_PAGE = PAGE_SIZE // CMP_STRIDE
BLK_PER_PAGE = PAGE_SIZE // SEL_BLOCK
N_POOL_SEL = N_SELECT - 1


def _head_rows(q_row, g):
    rows = [q_row[:, g * 256 + r * 64:g * 256 + (r + 1) * 64] for r in range(GROUP)]
    return jnp.concatenate(rows + [jnp.zeros((8 - GROUP, 64), F32)], axis=0) * ATTN_SCALE


def _slope_col(g):
    row = lax.broadcasted_iota(jnp.int32, (8, 1), 0)
    col = jnp.zeros((8, 1), F32)
    for r in range(GROUP):
        col = jnp.where(row == r, 2.0 ** -(g * GROUP + r + 1), col)
    return col


def _cmp_sample_kernel(pt_ref, cache_ref, w_ref, posf_ref, wb_ref, q_ref, oc_ref, idx_ref,
                       buf_ref, xk_ref, xv_ref, sem, *, n_pages, past):
    b = pl.program_id(0)
    n_ch = n_pages * CH_PER_PAGE
    n_blk = past // SEL_BLOCK
    lanes = -(-n_blk // 128) * 128

    def page_copy(pg, page):
        return pltpu.make_async_copy(cache_ref.at[page], buf_ref.at[pg], sem.at[pg])

    def issue(pg, carry):
        page_copy(pg, pt_ref[b, pg]).start()
        return carry

    def land(i, carry):
        pages = [i * LAND_PAGES + u for u in range(LAND_PAGES)]
        for pg in pages:
            page_copy(pg, 0).wait()
        for pg in pages:
            r0 = pl.multiple_of(pg * PAGE_SIZE, PAGE_SIZE)
            xk_ref[pl.ds(r0, PAGE_SIZE), :] = jnp.transpose(buf_ref[pg, 0].reshape(128, PAGE_SIZE))
            xv_ref[pl.ds(r0, PAGE_SIZE), :] = jnp.transpose(buf_ref[pg, 1].reshape(128, PAGE_SIZE))
        return carry

    lax.fori_loop(0, n_pages, issue, 0)
    lax.fori_loop(0, n_pages // LAND_PAGES, land, 0)

    bias = jnp.dot(posf_ref[...].astype(BF16), wb_ref[...], preferred_element_type=F32)[0:1]
    blocks = []
    for c, x_ref in enumerate((xk_ref, xv_ref)):
        acc = jnp.zeros((n_ch, 256), F32)
        for l2 in range(CMP_STRIDE // 2):
            xl = jnp.concatenate([x_ref[pl.ds(2 * l2 + u, n_ch, stride=CMP_STRIDE), :] for u in range(2)],
                                 axis=1).astype(BF16)
            acc += jnp.dot(xl, w_ref[c, l2], preferred_element_type=F32)
        bias_c = jnp.concatenate([bias[:, c * 64:(c + 1) * 64]] * KV_HEADS, axis=1)
        blocks.append((acc[:, :128] + pltpu.roll(acc[:, 128:], n_ch - 1, 0) + bias_c).astype(BF16))
    kc_all, vc_all = blocks

    q_row = q_ref[0]
    n_ix = lax.broadcasted_iota(jnp.int32, (1, n_ch), 1)
    dist_i = past - (n_ix * CMP_STRIDE + (CMP_LEN - 1))
    valid = jnp.logical_and(dist_i >= 0, n_ix < n_ch - 1)
    dist = dist_i.astype(F32)
    ratio = SEL_BLOCK // CMP_STRIDE
    pool = (lax.broadcasted_iota(jnp.int32, (n_ch, lanes), 0) // ratio
            == lax.broadcasted_iota(jnp.int32, (n_ch, lanes), 1)).astype(BF16)
    row8 = lax.broadcasted_iota(jnp.int32, (8, lanes), 0)
    score = jnp.full((8, lanes), -FORCE, F32)
    lane_g = lax.broadcasted_iota(jnp.int32, (8, 128), 1) // HEAD_DIM
    o_c = jnp.zeros((8, 128), F32)
    for g in range(KV_HEADS):
        pieces = [jnp.zeros((8, 64), F32)] * KV_HEADS
        pieces[g] = _head_rows(q_row, g)
        qpad = jnp.concatenate(pieces, axis=1).astype(BF16)
        s = lax.dot_general(qpad, kc_all, _NT, preferred_element_type=F32)
        s = jnp.where(valid, s - _slope_col(g) * dist, NEG)
        m = jnp.max(s, axis=-1, keepdims=True)
        p = jnp.where(valid, jnp.exp(s - m), 0.0)
        p = p / jnp.maximum(jnp.sum(p, axis=-1, keepdims=True), 1e-30)
        o_c = jnp.where(lane_g == g, jnp.dot(p.astype(BF16), vc_all, preferred_element_type=F32), o_c)
        imp = p[0:1] + p[1:2] + p[2:3] + p[3:4]
        imp_blk = _x_dot01(jnp.broadcast_to(imp, (8, n_ch)), pool)
        score = jnp.where(row8 == g, imp_blk, score)
    lane = lax.broadcasted_iota(jnp.int32, (8, lanes), 1)
    forced = jnp.logical_or(lane == 0, lane == n_blk - 1)
    score = jnp.where(lane < n_blk, jnp.where(forced, FORCE, score), -jnp.inf)
    sc_t = jnp.transpose(jnp.concatenate([score, jnp.full((120, lanes), -jnp.inf, F32)], axis=0))
    blk = lax.broadcasted_iota(jnp.int32, (lanes, 128), 0)
    picks = []
    for r in range(N_POOL_SEL):
        m = jnp.max(sc_t, axis=0, keepdims=True)
        first = jnp.min(jnp.where(sc_t == m, blk, lanes), axis=0, keepdims=True)
        picks.append(first)
        sc_t = jnp.where(blk == first, -jnp.inf, sc_t)
    idx_ref[0] = jnp.concatenate(picks + [jnp.zeros((N_SELECT - N_POOL_SEL, 128), jnp.int32)], axis=0)
    oc_ref[0] = o_c


def _cmp_sample(page_table, cache5, w_s, posf, wb, q3, past):
    bsz, n_pages = page_table.shape
    grid_spec = pltpu.PrefetchScalarGridSpec(
        num_scalar_prefetch=1,
        grid=(bsz,),
        in_specs=[pl.BlockSpec(memory_space=pl.ANY),
                  pl.BlockSpec((2, CMP_STRIDE // 2, 256, 256), lambda i, pt: (0, 0, 0, 0)),
                  pl.BlockSpec((8, 4096), lambda i, pt: (0, 0)),
                  pl.BlockSpec((4096, 128), lambda i, pt: (0, 0)),
                  pl.BlockSpec((1, 1, Q_A), lambda i, pt: (i, 0, 0))],
        out_specs=[pl.BlockSpec((1, 8, 128), lambda i, pt: (i, 0, 0)),
                   pl.BlockSpec((1, N_SELECT, 128), lambda i, pt: (i, 0, 0))],
        scratch_shapes=[pltpu.VMEM((n_pages, 2, KV_HEADS, HEAD_DIM, PAGE_SIZE), F32),
                        pltpu.VMEM((n_pages * PAGE_SIZE, 128), F32),
                        pltpu.VMEM((n_pages * PAGE_SIZE, 128), F32),
                        pltpu.SemaphoreType.DMA((n_pages,))],
    )
    return pl.pallas_call(
        functools.partial(_cmp_sample_kernel, n_pages=n_pages, past=past),
        grid_spec=grid_spec,
        out_shape=[jax.ShapeDtypeStruct((bsz, 8, 128), F32),
                   jax.ShapeDtypeStruct((bsz, N_SELECT, 128), jnp.int32)],
        compiler_params=_cparams(("arbitrary",)),
        name="cmp_sample",
    )(page_table, cache5, w_s, posf, wb, q3)


def _pick_lane(mat, lane, target):
    return jnp.sum(jnp.where(lane == target, mat, 0.0), axis=-1, keepdims=True)


def _sel_win_sample_kernel(pt_ref, idx_ref, cache_ref, win_ref, q_ref, ks_ref, kw_ref, gl_ref, oc_ref, o_ref,
                           kbuf_ref, vbuf_ref, sem, *, past):
    b = pl.program_id(0)
    slot_lanes = N_SELECT * PAGE_SIZE

    def page_copies(g, k, page):
        dst = pl.ds(k * PAGE_SIZE, PAGE_SIZE)
        return (pltpu.make_async_copy(cache_ref.at[page, 0, g], kbuf_ref.at[g, :, dst], sem),
                pltpu.make_async_copy(cache_ref.at[page, 1, g], vbuf_ref.at[g, :, dst], sem))

    for g in range(KV_HEADS):
        for k in range(N_POOL_SEL):
            page = pt_ref[b, idx_ref[b, g * N_SELECT + k] // BLK_PER_PAGE]
            for cp in page_copies(g, k, page):
                cp.start()
        pad = pl.ds(N_POOL_SEL * PAGE_SIZE, PAGE_SIZE)
        kbuf_ref[g, :, pad] = jnp.zeros((HEAD_DIM, PAGE_SIZE), F32)
        vbuf_ref[g, :, pad] = jnp.zeros((HEAD_DIM, PAGE_SIZE), F32)

    q_row = q_ref[0]
    ks_new = ks_ref[0]
    kw_new = kw_ref[0]
    gl_all = jax.nn.sigmoid(gl_ref[0])
    lane128 = lax.broadcasted_iota(jnp.int32, (8, 128), 1)
    row8 = lax.broadcasted_iota(jnp.int32, (8, 1), 0)
    w_len = win_ref.shape[-1]

    def two_piece(qh, slope, k_t, v_t, dist_i, valid, k_new, v_new):
        s = jnp.dot(qh.astype(BF16), k_t.astype(BF16), preferred_element_type=F32)
        s = jnp.where(valid, s - slope * dist_i.astype(F32), NEG)
        s_n = jnp.sum(qh * k_new, axis=-1, keepdims=True)
        m = jnp.maximum(jnp.max(s, axis=-1, keepdims=True), s_n)
        p = jnp.where(valid, jnp.exp(s - m), 0.0)
        p_n = jnp.exp(s_n - m)
        l = jnp.sum(p, axis=-1, keepdims=True) + p_n
        o = lax.dot_general(p.astype(BF16), v_t.astype(BF16), _NT, preferred_element_type=F32) + p_n * v_new
        return o / l

    outs = []
    for g in range(KV_HEADS):
        qh = _head_rows(q_row, g)
        slope = _slope_col(g)
        j_ix = lax.broadcasted_iota(jnp.int32, (1, w_len), 1)
        dist_w = w_len - j_ix
        o_w = two_piece(qh, slope, win_ref[0, 0, g], win_ref[0, 1, g], dist_w, dist_w < WINDOW,
                        kw_new[:, g * 128:g * 128 + 64], kw_new[:, g * 128 + 64:(g + 1) * 128])
        outs.append((qh, slope, o_w))

    for g in range(KV_HEADS):
        for k in range(N_POOL_SEL):
            for cp in page_copies(g, k, 0):
                cp.wait()

    lane_s = lax.broadcasted_iota(jnp.int32, (1, slot_lanes), 1)
    o_all = []
    for g in range(KV_HEADS):
        qh, slope, o_w = outs[g]
        pos_k = jnp.full((1, slot_lanes), past + 1, jnp.int32)
        for k in range(N_POOL_SEL):
            blk = idx_ref[b, g * N_SELECT + k]
            r = lane_s % PAGE_SIZE
            in_blk = jnp.logical_and(lane_s // PAGE_SIZE == k, r // SEL_BLOCK == blk % BLK_PER_PAGE)
            pos_k = jnp.where(in_blk, (blk // BLK_PER_PAGE) * PAGE_SIZE + r, pos_k)
        dist_s = past - pos_k
        o_s = two_piece(qh, slope, kbuf_ref[g], vbuf_ref[g], dist_s, dist_s >= 0,
                        ks_new[:, g * 128:g * 128 + 64], ks_new[:, g * 128 + 64:(g + 1) * 128])
        o_c = oc_ref[0][:, g * 64:(g + 1) * 64]
        gates = jnp.broadcast_to(gl_all[:, g * 128:(g + 1) * 128], (8, 128))
        g_c = _pick_lane(gates, lane128, 3 * row8)
        g_s = _pick_lane(gates, lane128, 3 * row8 + 1)
        g_w = _pick_lane(gates, lane128, 3 * row8 + 2)
        o_all.append(g_c * o_c + g_s * o_s + g_w * o_w)
    o_ref[0] = jnp.concatenate(o_all, axis=1)


def _sel_win_sample(page_table, idx, cache_sel5, cache_win5, q3, ks3, kw3, gl3, o_c, past):
    bsz = page_table.shape[0]
    w_len = cache_win5.shape[-1]
    grid_spec = pltpu.PrefetchScalarGridSpec(
        num_scalar_prefetch=2,
        grid=(bsz,),
        in_specs=[pl.BlockSpec(memory_space=pl.ANY),
                  pl.BlockSpec((1, 2, KV_HEADS, HEAD_DIM, w_len), lambda i, pt, ix: (i, 0, 0, 0, 0)),
                  pl.BlockSpec((1, 1, Q_A), lambda i, pt, ix: (i, 0, 0)),
                  pl.BlockSpec((1, 1, 256), lambda i, pt, ix: (i, 0, 0)),
                  pl.BlockSpec((1, 1, 256), lambda i, pt, ix: (i, 0, 0)),
                  pl.BlockSpec((1, 1, 256), lambda i, pt, ix: (i, 0, 0)),
                  pl.BlockSpec((1, 8, 128), lambda i, pt, ix: (i, 0, 0))],
        out_specs=pl.BlockSpec((1, 8, 128), lambda i, pt, ix: (i, 0, 0)),
        scratch_shapes=[pltpu.VMEM((KV_HEADS, HEAD_DIM, N_SELECT * PAGE_SIZE), F32),
                        pltpu.VMEM((KV_HEADS, HEAD_DIM, N_SELECT * PAGE_SIZE), F32),
                        pltpu.SemaphoreType.DMA(())],
    )
    return pl.pallas_call(
        functools.partial(_sel_win_sample_kernel, past=past),
        grid_spec=grid_spec,
        out_shape=jax.ShapeDtypeStruct((bsz, 8, 128), F32),
        compiler_params=_cparams(("arbitrary",)),
        name="sel_win_sample",
    )(page_table, idx, cache_sel5, cache_win5, q3, ks3, kw3, gl3, o_c)


def _hgrn_sample_kernel(q_ref, f_ref, v_ref, g_ref, lb_ref, ng_ref, s_ref, o_ref, s_out_ref):
    z = lb_ref[...]
    e = jnp.exp(z - jnp.max(z, axis=0, keepdims=True))
    lb = e[0] / jnp.sum(e, axis=0)
    f = lb + (1.0 - lb) * jax.nn.sigmoid(f_ref[0])
    decay = jnp.exp(jnp.log(f))
    k = 1.0 - f
    q = q_ref[0]
    v = v_ref[0]
    gate = g_ref[0]
    ng = ng_ref[...]
    for hh in range(B_HEADS):
        hs = slice(hh, hh + 1)
        s_new = decay[hs] * s_ref[0, hh] + v[:, hs] * k[hs]
        s_out_ref[0, hh] = s_new
        o = jnp.sum(q[hs] * s_new, axis=1, keepdims=True)
        o = o * lax.rsqrt(jnp.mean(o * o, axis=0, keepdims=True) + LN_EPS) * ng
        gt = gate[:, hs]
        o_ref[0, :, hs] = o * (gt * jax.nn.sigmoid(gt))


def _hgrn_sample(q_hk, f_hk, v_vh, g_vh, lb_hk, ng_col, state_t):
    bsz = q_hk.shape[0]
    return pl.pallas_call(
        _hgrn_sample_kernel,
        grid=(bsz,),
        in_specs=[pl.BlockSpec((1, B_HEADS, B_DK), lambda i: (i, 0, 0)),
                  pl.BlockSpec((1, B_HEADS, B_DK), lambda i: (i, 0, 0)),
                  pl.BlockSpec((1, B_DV, B_HEADS), lambda i: (i, 0, 0)),
                  pl.BlockSpec((1, B_DV, B_HEADS), lambda i: (i, 0, 0)),
                  pl.BlockSpec((2, B_HEADS, B_DK), lambda i: (0, 0, 0)),
                  pl.BlockSpec((B_DV, 1), lambda i: (0, 0)),
                  pl.BlockSpec((1, B_HEADS, B_DV, B_DK), lambda i: (i, 0, 0, 0))],
        out_specs=[pl.BlockSpec((1, B_DV, B_HEADS), lambda i: (i, 0, 0)),
                   pl.BlockSpec((1, B_HEADS, B_DV, B_DK), lambda i: (i, 0, 0, 0))],
        out_shape=[jax.ShapeDtypeStruct((bsz, B_DV, B_HEADS), F32),
                   jax.ShapeDtypeStruct((bsz, B_HEADS, B_DV, B_DK), F32)],
        compiler_params=_cparams(("arbitrary",)),
        name="hgrn_sample",
    )(q_hk, f_hk, v_vh, g_vh, lb_hk, ng_col, state_t)


def _rows_last(cache):
    return jnp.moveaxis(cache, -4, -1)


def _sample_layer(x, cache_cmp, cache_sel, cache_win, state, page_table, w_pad, cmpw, w_s, lb_logits, norm_g,
                  w_a, w_b, w_o, ln1_g, ln1_b, w_up, w_down, ln2_g, ln2_b):
    bsz, t, _ = x.shape
    assert t == 1, "the sample group decodes one token per request"
    n_pages = page_table.shape[1]
    past = n_pages * PAGE_SIZE
    x2d = x.reshape(bsz, D_MODEL)
    h = _project(x2d, w_pad)
    h3 = h.reshape(bsz, 1, D_PAD)
    q3 = h3[:, :, C_QA:C_QA + Q_A]
    wl, wb, posf = cmpw
    o_c, idx = _cmp_sample(page_table, _rows_last(cache_cmp), w_s, posf, wb, q3, past)
    idx2 = jnp.swapaxes(idx[:, :, :KV_HEADS], 1, 2).reshape(bsz, KV_HEADS * N_SELECT)
    o_rd = _sel_win_sample(page_table, idx2, _rows_last(cache_sel), _rows_last(cache_win),
                           q3, h3[:, :, C_KVS:C_KVS + 256], h3[:, :, C_KVW:C_KVW + 256],
                           h3[:, :, C_GL:C_GL + 256], o_c, past)
    o_a = jnp.swapaxes(o_rd[:, :GROUP].reshape(bsz, GROUP, KV_HEADS, HEAD_DIM), 1, 2).reshape(bsz, Q_A)
    to_hk = lambda a: a.reshape(-1, B_HEADS, B_DK)
    to_vh = lambda a: jnp.swapaxes(a.reshape(-1, B_HEADS, B_DV), 1, 2)
    o_vh, s_t = _hgrn_sample(to_hk(h[:, C_QB:C_QB + Q_B]), to_hk(h[:, C_FB:C_FB + Q_B]),
                             to_vh(h[:, C_IB:C_IB + I_B]), to_vh(h[:, C_GB:C_GB + I_B]),
                             to_hk(lb_logits), norm_g.reshape(B_DV, 1), jnp.swapaxes(state, 2, 3))
    o_b = jnp.swapaxes(o_vh, 1, 2).reshape(bsz, I_B)
    x1 = _merge(x2d, o_a, o_b, h, w_a, w_b, w_o, ln1_g, ln1_b)
    y = _mlp(x1, w_up, w_down, ln2_g, ln2_b)
    return y.reshape(bsz, 1, D_MODEL), h, jnp.swapaxes(s_t, 2, 3)


def _cmp_sample_weights(cmp_w):
    w = cmp_w.reshape(2, 2, CMP_STRIDE, HEAD_DIM, HEAD_DIM)
    base = jnp.transpose(w, (0, 2, 3, 1, 4))
    ws = jnp.stack([_at_slot(base, g, 4) for g in range(2)], axis=2)
    return ws.reshape(2, CMP_STRIDE // 2, 256, 256).astype(BF16)


def kernel(x_prompt, x_sample, cache_cmp_kv, cache_sel_kv, cache_win_kv, state_hgrn, page_table,
           w_in, cmp_w, cmp_pos, hgrn_lb_logits, hgrn_norm_g, w_br_a, w_br_b, w_out,
           ln1_g, ln1_b, w_up, w_down, ln2_g, ln2_b):
    assert w_in.shape[0] == 1, "one layer"
    b, t, _ = x_prompt.shape
    bsz = x_sample.shape[0]
    assert t % HB == 0 and t % QB == 0 and WINDOW + QB <= t <= 128 * SEL_BLOCK
    w_pad = _reorder_w_in(w_in[0])
    cmpw = _cmp_weights(cmp_w[0], cmp_pos[0])
    w_s = _cmp_sample_weights(cmp_w[0])
    dense = (w_br_a[0].astype(BF16), w_br_b[0].astype(BF16), w_out[0].astype(BF16), ln1_g, ln1_b,
             w_up[0].astype(BF16), w_down[0].astype(BF16), ln2_g, ln2_b)

    y_p, h_p, s_p = _prompt_layer(x_prompt, w_pad, cmpw, hgrn_lb_logits, hgrn_norm_g, *dense)
    y_s, h_s, s_s = _sample_layer(x_sample, cache_cmp_kv[0], cache_sel_kv[0], cache_win_kv[0], state_hgrn[0],
                                  page_table, w_pad, cmpw, w_s, hgrn_lb_logits, hgrn_norm_g, *dense)

    win_p = min(WINDOW, t)
    kvw_p = _kv_out(h_p, C_KVW, b, t)
    kvw_s = _kv_out(h_s, C_KVW, bsz, 1)
    new_win_s = jnp.concatenate([cache_win_kv[0], kvw_s], axis=1)[:, -min(WINDOW, cache_win_kv.shape[2] + 1):]
    return (y_p, y_s,
            _kv_out(h_p, C_KVC, b, t)[None], _kv_out(h_p, C_KVS, b, t)[None], kvw_p[:, -win_p:][None], s_p[None],
            _kv_out(h_s, C_KVC, bsz, 1)[None], _kv_out(h_s, C_KVS, bsz, 1)[None], new_win_s[None], s_s[None])
```

```python
import functools

import numpy as np
import jax
import jax.numpy as jnp
from jax import lax
from jax.experimental import pallas as pl
from jax.experimental.pallas import tpu as pltpu

F32 = jnp.float32
BF16 = jnp.bfloat16

D_MODEL = 1024
HEAD_DIM = 64
A_HEADS = 8
KV_HEADS = 2
GROUP = A_HEADS // KV_HEADS
CMP_STRIDE = 16
CMP_LEN = 32
SEL_BLOCK = 64
N_SELECT = 16
WINDOW = 512
PAGE_SIZE = 128
B_HEADS = 8
B_DK = 128
B_DV = 64
HGRN_CHUNK = 32
D_FF = 4 * D_MODEL
DEEPNORM_ALPHA = 2.0 ** 0.25
LN_EPS = 1e-5
NEG = -1e30
FORCE = 1e6
ATTN_SCALE = HEAD_DIM ** -0.5
LOG2E = 1.4426950408889634
Q_A = A_HEADS * HEAD_DIM
KV_A = 2 * KV_HEADS * HEAD_DIM
GATE_A = 3 * A_HEADS
Q_B = B_HEADS * B_DK
I_B = B_HEADS * B_DV

C_MG = 0
C_QB = 2048
C_FB = 3072
C_QA = 4096
C_IB = 4608
C_GB = 5120
C_KVC = 5632
C_KVS = 5888
C_KVW = 6144
C_GL = 6400
D_PAD = 6656
PROJ_TN = 1664

VMEM_LIMIT = 56 * 1024 * 1024
UNSEL = float(2.0 ** 100)


def _cparams(sem):
    return pltpu.CompilerParams(dimension_semantics=sem, vmem_limit_bytes=VMEM_LIMIT)


def _proj_perm():
    perm = np.full((D_PAD,), -1, np.int64)
    o_qa, o_kvc, o_kvs, o_kvw = 0, Q_A, Q_A + KV_A, Q_A + 2 * KV_A
    o_gl = Q_A + 3 * KV_A
    o_qb = o_gl + GATE_A
    o_fb = o_qb + Q_B
    o_ib = o_fb + Q_B
    o_gb = o_ib + I_B
    o_mg = o_gb + I_B
    perm[C_MG:C_MG + 2 * D_MODEL] = o_mg + np.arange(2 * D_MODEL)
    perm[C_QB:C_QB + Q_B] = o_qb + np.arange(Q_B)
    perm[C_FB:C_FB + Q_B] = o_fb + np.arange(Q_B)
    perm[C_QA:C_QA + Q_A] = o_qa + np.arange(Q_A)
    perm[C_IB:C_IB + I_B] = o_ib + np.arange(I_B)
    perm[C_GB:C_GB + I_B] = o_gb + np.arange(I_B)
    for new, old in ((C_KVC, o_kvc), (C_KVS, o_kvs), (C_KVW, o_kvw)):
        for g in range(KV_HEADS):
            for c in range(2):
                dst = new + g * 128 + c * 64
                src = old + c * 128 + g * 64
                perm[dst:dst + 64] = src + np.arange(64)
    for g in range(KV_HEADS):
        perm[C_GL + g * 128:C_GL + g * 128 + 12] = o_gl + g * 12 + np.arange(12)
    return perm


def _perm_runs(perm):
    runs = []
    for col in perm:
        if runs and ((col < 0 and runs[-1][0] < 0) or (col >= 0 and runs[-1][0] >= 0
                                                      and runs[-1][0] + runs[-1][1] == col)):
            runs[-1][1] += 1
        else:
            runs.append([int(col), 1])
    return [tuple(r) for r in runs]


_PERM_RUNS = _perm_runs(_proj_perm())


def _reorder_w_in(w):
    pieces = [jnp.zeros((w.shape[0], n), w.dtype) if s < 0 else w[:, s:s + n] for s, n in _PERM_RUNS]
    return jnp.concatenate(pieces, axis=1).astype(BF16)


def _proj_kernel(x_ref, w_ref, o_ref):
    xb = x_ref[...].astype(BF16)
    for j in range(D_PAD // PROJ_TN):
        cols = slice(j * PROJ_TN, (j + 1) * PROJ_TN)
        o_ref[:, cols] = jnp.dot(xb, w_ref[:, cols], preferred_element_type=F32)


def _proj_kvt_kernel(x_ref, w_ref, wt_ref, o_ref, kc_ref, ks_ref, kw_ref):
    xb = x_ref[...].astype(BF16)
    for j in range(D_PAD // PROJ_TN):
        cols = slice(j * PROJ_TN, (j + 1) * PROJ_TN)
        o_ref[:, cols] = jnp.dot(xb, w_ref[:, cols], preferred_element_type=F32)
    for j, kt_ref in enumerate((kc_ref, ks_ref, kw_ref)):
        kt_ref[0] = lax.dot_general(wt_ref[j * KV_A:(j + 1) * KV_A, :], xb, _NT, preferred_element_type=F32)


def _project_prompt(x2d, w_pad, w_kvt, b, t):
    tm = 512
    nt = t // tm
    kvt = jax.ShapeDtypeStruct((b, KV_A, t), F32)
    kvt_spec = pl.BlockSpec((1, KV_A, tm), lambda i: (i // nt, 0, i % nt))
    return pl.pallas_call(
        _proj_kvt_kernel,
        grid=(b * nt,),
        in_specs=[pl.BlockSpec((tm, D_MODEL), lambda i: (i, 0)),
                  pl.BlockSpec((D_MODEL, D_PAD), lambda i: (0, 0), pipeline_mode=pl.Buffered(1)),
                  pl.BlockSpec((3 * KV_A, D_MODEL), lambda i: (0, 0), pipeline_mode=pl.Buffered(1))],
        out_specs=[pl.BlockSpec((tm, D_PAD), lambda i: (i, 0)), kvt_spec, kvt_spec, kvt_spec],
        out_shape=[jax.ShapeDtypeStruct((b * t, D_PAD), F32), kvt, kvt, kvt],
        compiler_params=_cparams(("arbitrary",)),
        name="proj_prompt",
    )(x2d, w_pad, w_kvt)


def _project(x2d, w_pad):
    n = x2d.shape[0]
    tm = min(512, n)
    return pl.pallas_call(
        _proj_kernel,
        grid=(n // tm,),
        in_specs=[pl.BlockSpec((tm, D_MODEL), lambda i: (i, 0)),
                  pl.BlockSpec((D_MODEL, D_PAD), lambda i: (0, 0), pipeline_mode=pl.Buffered(1))],
        out_specs=pl.BlockSpec((tm, D_PAD), lambda i: (i, 0)),
        out_shape=jax.ShapeDtypeStruct((n, D_PAD), F32),
        compiler_params=_cparams(("arbitrary",)),
        name="proj",
    )(x2d, w_pad)


def _layer_norm(v, g, b):
    mu = jnp.mean(v, axis=-1, keepdims=True)
    d = v - mu
    var = jnp.mean(d * d, axis=-1, keepdims=True)
    return d * lax.rsqrt(var + LN_EPS) * g + b


def _merge_kernel(x_ref, oa_ref, ob_ref, mga_ref, mgb_ref, wa_ref, wb_ref, wo_ref, g_ref, b_ref, o_ref):
    br_a = jnp.dot(oa_ref[...].astype(BF16), wa_ref[...], preferred_element_type=F32)
    br_b = jnp.dot(ob_ref[...].astype(BF16), wb_ref[...], preferred_element_type=F32)
    merged = jax.nn.sigmoid(mga_ref[...]) * br_a + jax.nn.sigmoid(mgb_ref[...]) * br_b
    mix = jnp.dot(merged.astype(BF16), wo_ref[...], preferred_element_type=F32)
    o_ref[...] = _layer_norm(DEEPNORM_ALPHA * x_ref[...] + mix, g_ref[...], b_ref[...])


def _merge(x2d, o_a, o_b, h, w_a, w_b, w_o, ln_g, ln_b):
    n = x2d.shape[0]
    tm = min(256, n)
    const = lambda i: (0, 0)
    return pl.pallas_call(
        _merge_kernel,
        grid=(n // tm,),
        in_specs=[pl.BlockSpec((tm, D_MODEL), lambda i: (i, 0)),
                  pl.BlockSpec((tm, Q_A), lambda i: (i, 0)),
                  pl.BlockSpec((tm, I_B), lambda i: (i, 0)),
                  pl.BlockSpec((tm, D_MODEL), lambda i: (i, C_MG // D_MODEL)),
                  pl.BlockSpec((tm, D_MODEL), lambda i: (i, C_MG // D_MODEL + 1)),
                  pl.BlockSpec((Q_A, D_MODEL), const),
                  pl.BlockSpec((I_B, D_MODEL), const),
                  pl.BlockSpec((D_MODEL, D_MODEL), const),
                  pl.BlockSpec((1, D_MODEL), const),
                  pl.BlockSpec((1, D_MODEL), const)],
        out_specs=pl.BlockSpec((tm, D_MODEL), lambda i: (i, 0)),
        out_shape=jax.ShapeDtypeStruct((n, D_MODEL), F32),
        compiler_params=_cparams(("arbitrary",)),
        name="merge_ln1",
    )(x2d, o_a, o_b, h, h, w_a, w_b, w_o, ln_g, ln_b)


def _mlp_kernel(x_ref, wu_ref, wd_ref, g_ref, b_ref, o_ref, xb_ref, acc_ref):
    j = pl.program_id(1)

    @pl.when(j == 0)
    def _():
        xb_ref[...] = x_ref[...].astype(BF16)
        acc_ref[...] = jnp.zeros_like(acc_ref)

    u = jnp.dot(xb_ref[...], wu_ref[...], preferred_element_type=F32)
    u = jnp.maximum(u, 0.0)
    acc_ref[...] += jnp.dot((u * u).astype(BF16), wd_ref[...], preferred_element_type=F32)

    @pl.when(j == pl.num_programs(1) - 1)
    def _():
        o_ref[...] = _layer_norm(DEEPNORM_ALPHA * x_ref[...] + acc_ref[...], g_ref[...], b_ref[...])


def _mlp(x1, w_up, w_down, ln_g, ln_b):
    n = x1.shape[0]
    tm = min(1024, n)
    tf = 1024
    return pl.pallas_call(
        _mlp_kernel,
        grid=(n // tm, D_FF // tf),
        in_specs=[pl.BlockSpec((tm, D_MODEL), lambda i, j: (i, 0)),
                  pl.BlockSpec((D_MODEL, tf), lambda i, j: (0, j)),
                  pl.BlockSpec((tf, D_MODEL), lambda i, j: (j, 0)),
                  pl.BlockSpec((1, D_MODEL), lambda i, j: (0, 0)),
                  pl.BlockSpec((1, D_MODEL), lambda i, j: (0, 0))],
        out_specs=pl.BlockSpec((tm, D_MODEL), lambda i, j: (i, 0)),
        out_shape=jax.ShapeDtypeStruct((n, D_MODEL), F32),
        scratch_shapes=[pltpu.VMEM((tm, D_MODEL), BF16), pltpu.VMEM((tm, D_MODEL), F32)],
        compiler_params=_cparams(("arbitrary", "arbitrary")),
        name="mlp_ln2",
    )(x1, w_up, w_down, ln_g, ln_b)


def _dot01(a01, x):
    hi = x.astype(BF16)
    r1 = x - hi.astype(F32)
    mid = r1.astype(BF16)
    lo = (r1 - mid.astype(F32)).astype(BF16)
    out = jnp.dot(a01, hi, preferred_element_type=F32)
    out += jnp.dot(a01, mid, preferred_element_type=F32)
    out += jnp.dot(a01, lo, preferred_element_type=F32)
    return out


def _x_dot01(x, b01):
    hi = x.astype(BF16)
    r1 = x - hi.astype(F32)
    mid = r1.astype(BF16)
    lo = (r1 - mid.astype(F32)).astype(BF16)
    out = jnp.dot(hi, b01, preferred_element_type=F32)
    out += jnp.dot(mid, b01, preferred_element_type=F32)
    out += jnp.dot(lo, b01, preferred_element_type=F32)
    return out


_NT = (((1,), (1,)), ((), ()))
_TN = (((0,), (0,)), ((), ()))


def _cmp_prompt_kernel(kv_ref, w_ref, posf_ref, wb_ref, o_ref, *, n_ch):
    acc = jnp.zeros((n_ch, 256), F32)
    for l in range(CMP_STRIDE):
        xl = kv_ref[pl.ds(l, n_ch, stride=CMP_STRIDE), :].astype(BF16)
        acc += jnp.dot(xl, w_ref[l], preferred_element_type=F32)
    bias = jnp.dot(posf_ref[...].astype(BF16), wb_ref[...], preferred_element_type=F32)[0:1]
    nxt = pltpu.roll(acc[:, 128:], n_ch - 1, 0)
    o_ref[0] = (acc[:, :128] + nxt + bias).astype(BF16)


def _at_slot(a, slot, axis):
    pads = [(0, 0)] * (a.ndim + 1)
    pads[axis] = (slot, 1 - slot)
    return jnp.pad(jnp.expand_dims(a, axis), pads)


def _cmp_weights(cmp_w, cmp_pos):
    w = cmp_w.reshape(2, 2, CMP_STRIDE, HEAD_DIM, HEAD_DIM)
    wl = jnp.stack([_at_slot(jnp.transpose(w[c], (1, 2, 0, 3)), c, 3) for c in range(2)], axis=1)
    wb = jnp.stack([_at_slot(cmp_w[c].reshape(CMP_LEN * HEAD_DIM, HEAD_DIM), c, 1) for c in range(2)])
    wl = wl.reshape(CMP_STRIDE, 128, 256).astype(BF16)
    wb = wb.reshape(2 * CMP_LEN * HEAD_DIM, 128).astype(BF16)
    posf = jnp.broadcast_to(cmp_pos.reshape(1, 2 * CMP_LEN * HEAD_DIM), (8, 2 * CMP_LEN * HEAD_DIM))
    return wl, wb, posf


def _cmp_prompt(h, b, t, wl, wb, posf):
    n_ch = t // CMP_STRIDE
    return pl.pallas_call(
        functools.partial(_cmp_prompt_kernel, n_ch=n_ch),
        grid=(b, KV_HEADS),
        in_specs=[pl.BlockSpec((t, 128), lambda i, g: (i, C_KVC // 128 + g)),
                  pl.BlockSpec((CMP_STRIDE, 128, 256), lambda i, g: (0, 0, 0)),
                  pl.BlockSpec((8, 4096), lambda i, g: (0, 0)),
                  pl.BlockSpec((4096, 128), lambda i, g: (0, 0))],
        out_specs=pl.BlockSpec((1, n_ch, 128), lambda i, g: (i, 0, g)),
        out_shape=jax.ShapeDtypeStruct((b, n_ch, 256), BF16),
        compiler_params=_cparams(("arbitrary", "arbitrary")),
        name="cmp_prompt",
    )(h, wl, posf, wb)


QB = 256
PIECE = 256
SEL_UNROLL = 2
KT = 256
WT = 128


ROWS = GROUP * QB
TL_COUNT = 64


def _head_major(q):
    return jnp.concatenate([q[:, r * 64:(r + 1) * 64] for r in range(GROUP)], axis=0) * ATTN_SCALE


def _head_slope(g, r):
    return jnp.where(g == 0, 2.0 ** -(r + 1), 2.0 ** -(r + 1 + GROUP)).astype(F32)


def _slope_row(g):
    r_ix = lax.broadcasted_iota(jnp.int32, (1, ROWS), 1) // QB
    row = jnp.zeros((1, ROWS), F32)
    for r in range(GROUP):
        row = jnp.where(r_ix == r, _head_slope(g, r), row)
    return row


def _nsa_select_kernel(q_ref, kc_ref, oc_ref, un_ref, tl_ref, *, n_ch):
    g = pl.program_id(1)
    c = pl.program_id(2)
    t0 = c * QB
    qpad = jnp.concatenate([_head_major(q_ref[...]), jnp.zeros((ROWS, 64), F32)], axis=1).astype(BF16)
    kc = kc_ref[0]
    s = lax.dot_general(kc, qpad, _NT, preferred_element_type=F32)
    pos_row = t0 + lax.broadcasted_iota(jnp.int32, (1, ROWS), 1) % QB
    n_col = lax.broadcasted_iota(jnp.int32, (n_ch, 1), 0)
    dist_i = pos_row - (n_col * CMP_STRIDE + (CMP_LEN - 1))
    valid = jnp.logical_and(dist_i >= 0, n_col < n_ch - 1)
    s = jnp.where(valid, s - _slope_row(g) * dist_i.astype(F32), NEG)
    m = jnp.max(s, axis=0, keepdims=True)
    p = jnp.exp(s - m)
    inv = jnp.where(m > 0.5 * NEG, 1.0 / jnp.maximum(jnp.sum(p, axis=0, keepdims=True), 1e-30), 0.0)
    p = p * inv
    o_c = lax.dot_general(p.astype(BF16), kc, _TN, preferred_element_type=F32)
    for r in range(GROUP):
        oc_ref[:, r * 64:(r + 1) * 64] = o_c[r * QB:(r + 1) * QB, 64:128]

    imp = p[:, 0:QB]
    for r in range(1, GROUP):
        imp = imp + p[:, r * QB:(r + 1) * QB]
    ratio = SEL_BLOCK // CMP_STRIDE
    pool_t = (lax.broadcasted_iota(jnp.int32, (128, n_ch), 1) // ratio
              == lax.broadcasted_iota(jnp.int32, (128, n_ch), 0)).astype(BF16)
    imp_blk = _dot01(pool_t, imp)
    blk = lax.broadcasted_iota(jnp.int32, (128, QB), 0)
    pos_q = t0 + lax.broadcasted_iota(jnp.int32, (1, QB), 1)
    cur = pos_q // SEL_BLOCK
    forced = jnp.logical_or(jnp.logical_or(blk == 0, blk == cur), blk == cur - 1)
    allowed = blk * SEL_BLOCK <= pos_q
    work = jnp.where(jnp.logical_and(allowed, jnp.logical_not(forced)), imp_blk, -jnp.inf)
    sel = forced
    for _ in range(N_SELECT - 3):
        mx = jnp.max(work, axis=0, keepdims=True)
        first = jnp.min(jnp.where(work == mx, blk, 128), axis=0, keepdims=True)
        pick = blk == first
        sel = jnp.logical_or(sel, pick)
        work = jnp.where(pick, -jnp.inf, work)
    sel = jnp.logical_and(sel, allowed)
    un_ref[...] = jnp.transpose(jnp.where(sel, 0.0, -UNSEL)).astype(BF16)

    tile_of = (lax.broadcasted_iota(jnp.int32, (128, 128), 1) // (KT // SEL_BLOCK)
               == lax.broadcasted_iota(jnp.int32, (128, 128), 0)).astype(BF16)
    sel_b = jnp.where(sel, 1.0, 0.0).astype(BF16)
    cnt = jnp.sum(jnp.dot(tile_of, sel_b, preferred_element_type=F32), axis=1, keepdims=True)
    tile_col = lax.broadcasted_iota(jnp.int32, (128, 1), 0)
    flag = jnp.logical_and(cnt > 0.0, tile_col < t0 // KT)
    flag_m = jnp.where(jnp.broadcast_to(flag, (128, 128)), 1.0, 0.0).astype(BF16)
    row_i = lax.broadcasted_iota(jnp.int32, (128, 128), 0)
    lane_i = lax.broadcasted_iota(jnp.int32, (128, 128), 1)
    before = jnp.dot((lane_i < row_i).astype(BF16), flag_m, preferred_element_type=F32)
    slot = jnp.where(jnp.logical_and(flag, before == lane_i.astype(F32)), 1.0, 0.0).astype(BF16)
    j_rows = lax.broadcasted_iota(jnp.int32, (8, 128), 1).astype(BF16)
    listed = jnp.dot(j_rows, slot, preferred_element_type=F32)
    total = jnp.dot(jnp.ones((8, 128), BF16), flag_m, preferred_element_type=F32)
    lane8 = lax.broadcasted_iota(jnp.int32, (8, 128), 1)
    tl_ref[0] = jnp.where(lane8 == TL_COUNT, total, listed).astype(jnp.int32)


def _nsa_select(h, kcmp, b, t):
    n_ch = t // CMP_STRIDE
    nq = t // QB
    steps = b * KV_HEADS * nq
    return pl.pallas_call(
        functools.partial(_nsa_select_kernel, n_ch=n_ch),
        grid=(b, KV_HEADS, nq),
        in_specs=[pl.BlockSpec((QB, 256), lambda i, g, c: (i * nq + c, C_QA // 256 + g)),
                  pl.BlockSpec((1, n_ch, 128), lambda i, g, c: (i, 0, g))],
        out_specs=[pl.BlockSpec((QB, 256), lambda i, g, c: (i * nq + c, g)),
                   pl.BlockSpec((QB, 128), lambda i, g, c: (i * nq + c, g)),
                   pl.BlockSpec((1, 8, 128), lambda i, g, c: ((i * KV_HEADS + g) * nq + c, 0, 0))],
        out_shape=[jax.ShapeDtypeStruct((b * t, Q_A), F32),
                   jax.ShapeDtypeStruct((b * t, KV_HEADS * 128), BF16),
                   jax.ShapeDtypeStruct((steps, 8, 128), jnp.int32)],
        compiler_params=_cparams(("arbitrary", "arbitrary", "arbitrary")),
        name="nsa_select",
    )(h, kcmp)


def _nsa_attend_kernel(tl_ref, q_ref, un_ref, gl_ref, oc_ref, ks_ref, kw_ref, o_ref,
                       ksb_ref, ksa_ref, kwb_ref, kwa_ref, *, nq):
    i = pl.program_id(0)
    g = pl.program_id(1)
    c = pl.program_id(2)
    step = (i * KV_HEADS + g) * nq + c
    t0 = c * QB
    half = PIECE
    n_piece = ROWS // PIECE
    qs = _head_major(q_ref[...]) * LOG2E
    s2 = jnp.concatenate([jnp.broadcast_to(_head_slope(g, r) * LOG2E, (QB, 1)) for r in range(GROUP)], axis=0)
    s2_hi = s2.astype(BF16).astype(F32)
    s2_lo = s2 - s2_hi
    lane64 = lax.broadcasted_iota(jnp.int32, (ROWS, 64), 1)
    ali_q = jnp.where(lane64 == 0, -64.0 * s2_hi, jnp.where(lane64 == 1, -s2_hi,
                      jnp.where(lane64 == 2, -64.0 * s2_lo, jnp.where(lane64 == 3, -s2_lo, 0.0))))
    q_win = jnp.concatenate([qs, ali_q], axis=1).astype(BF16)
    un = un_ref[...]
    q_sel = jnp.concatenate([q_win, jnp.concatenate([un] * GROUP, axis=0)], axis=1)
    slope_row = _slope_row(g) * LOG2E

    @pl.when(c == 0)
    def _():
        def fill(j, carry):
            r0 = pl.multiple_of(j * KT, KT)
            row = lax.broadcasted_iota(jnp.int32, (KT, 128), 0)
            lane = lax.broadcasted_iota(jnp.int32, (KT, 128), 1)
            for src, dst_kv, dst_aug, tile in ((ks_ref, ksb_ref, ksa_ref, KT), (kw_ref, kwb_ref, kwa_ref, WT)):
                kv_b = src[pl.ds(r0, KT), :].astype(BF16)
                d = (QB - 1) - row % tile
                ali = jnp.where(lane // 4 == 16, jnp.where(lane % 2 == 0, d >> 6, d & 63), 0).astype(F32).astype(BF16)
                dst_kv[pl.ds(r0, KT), :] = kv_b
                dst_aug[pl.ds(r0, KT), 0:128] = jnp.where(lane < 64, kv_b, ali)
            ksa_ref[pl.ds(r0, KT), 128:256] = jnp.where((r0 + row) // SEL_BLOCK == lane, 1.0, 0.0).astype(BF16)
            return carry

        lax.fori_loop(0, ks_ref.shape[0] // KT, fill, 0)

    def update(carry, s, kv_b, offset):
        out = []
        for hh, ((m, l, acc), sh) in enumerate(zip(carry, s)):
            shift = slope_row[:, hh * half:(hh + 1) * half] * offset
            m_new = jnp.maximum(m, jnp.max(sh, axis=0, keepdims=True) - shift)
            alpha = jnp.exp2(m - m_new)
            p = jnp.exp2(sh - (m_new + shift))
            l = alpha * l + jnp.sum(p, axis=0, keepdims=True)
            acc = alpha * acc + lax.dot_general(kv_b, p.astype(BF16), _TN, preferred_element_type=F32)
            out.append((m_new, l, acc))
        return tuple(out)

    def init():
        return tuple((jnp.full((1, half), NEG, F32), jnp.zeros((1, half), F32), jnp.zeros((128, half), F32))
                     for _ in range(n_piece))

    def tile(carry, aug_ref, kvb_ref, q_side, k0, n_keys, keep):
        k_aug = aug_ref[pl.ds(k0, n_keys), :]
        s = []
        for hh in range(n_piece):
            sh = lax.dot_general(k_aug, q_side[hh * half:(hh + 1) * half], _NT, preferred_element_type=F32)
            if keep is not None:
                sh = jnp.where(jnp.concatenate([keep] * (half // QB), axis=1), sh, NEG)
            s.append(sh)
        return update(carry, s, kvb_ref[pl.ds(k0, n_keys), :], (t0 - k0).astype(F32))

    k_tail = pl.multiple_of((t0 // KT) * KT, KT)
    key_x = lax.broadcasted_iota(jnp.int32, (KT, QB), 0)
    q_x = lax.broadcasted_iota(jnp.int32, (KT, QB), 1)
    carry = tile(init(), ksa_ref, ksb_ref, q_sel, k_tail, KT, k_tail + key_x <= t0 + q_x)

    def sel_listed(n, carry):
        return tile(carry, ksa_ref, ksb_ref, q_sel, pl.multiple_of(tl_ref[step, n] * KT, KT), KT, None)

    n_listed = tl_ref[step, TL_COUNT]

    def sel_group(n, cr):
        for u in range(SEL_UNROLL):
            cr = sel_listed(SEL_UNROLL * n + u, cr)
        return cr

    carry = lax.fori_loop(0, n_listed // SEL_UNROLL, sel_group, carry)
    carry = lax.fori_loop(n_listed - n_listed % SEL_UNROLL, n_listed, sel_listed, carry)
    o_sel = [acc * (1.0 / l) for (_, l, acc) in carry]

    w_keys = WINDOW + QB
    k0w = pl.multiple_of(jnp.maximum(t0 - WINDOW, 0), WT)
    k_aug = kwa_ref[pl.ds(k0w, w_keys), :]
    kv_w = kwb_ref[pl.ds(k0w, w_keys), :]
    dq = lax.broadcasted_iota(jnp.int32, (WT, QB), 1) - lax.broadcasted_iota(jnp.int32, (WT, QB), 0)
    offs = [t0 - (k0w + gi * WT) for gi in range(w_keys // WT)]
    keeps = [jnp.concatenate([jnp.logical_and(dq + off >= 0, dq + off < WINDOW)] * (half // QB), axis=1)
             for off in offs]
    o_win = []
    for hh in range(n_piece):
        sl = slope_row[:, hh * half:(hh + 1) * half]
        sh = lax.dot_general(k_aug, q_win[hh * half:(hh + 1) * half], _NT, preferred_element_type=F32)
        parts = [jnp.where(keep, sh[gi * WT:(gi + 1) * WT], NEG) for gi, keep in enumerate(keeps)]
        shifts = [sl * off.astype(F32) for off in offs]
        m_w = functools.reduce(jnp.maximum, [jnp.max(pt, axis=0, keepdims=True) - sf
                                             for pt, sf in zip(parts, shifts)])
        ps = [jnp.exp2(pt - (m_w + sf)) for pt, sf in zip(parts, shifts)]
        l_w = functools.reduce(jnp.add, [jnp.sum(p, axis=0, keepdims=True) for p in ps])
        acc = lax.dot_general(kv_w, jnp.concatenate(ps, axis=0).astype(BF16), _TN, preferred_element_type=F32)
        o_win.append(acc * (1.0 / l_w))

    gates = jax.nn.sigmoid(gl_ref[...])
    o_c = oc_ref[...]
    for r in range(GROUP):
        hh, cs = divmod(r * QB, half)
        o_s = jnp.transpose(o_sel[hh][:, cs:cs + QB])[:, 64:128]
        o_w = jnp.transpose(o_win[hh][:, cs:cs + QB])[:, 64:128]
        o_ref[:, r * 64:(r + 1) * 64] = (gates[:, 3 * r:3 * r + 1] * o_c[:, r * 64:(r + 1) * 64]
                                         + gates[:, 3 * r + 1:3 * r + 2] * o_s
                                         + gates[:, 3 * r + 2:3 * r + 3] * o_w)


def _nsa_attend(tiles, h, unsel, o_c, b, t):
    nq = t // QB
    grid_spec = pltpu.PrefetchScalarGridSpec(
        num_scalar_prefetch=1,
        grid=(b, KV_HEADS, nq),
        in_specs=[pl.BlockSpec((QB, 256), lambda i, g, c, tl: (i * nq + c, C_QA // 256 + g)),
                  pl.BlockSpec((QB, 128), lambda i, g, c, tl: (i * nq + c, g)),
                  pl.BlockSpec((QB, 128), lambda i, g, c, tl: (i * nq + c, C_GL // 128 + g)),
                  pl.BlockSpec((QB, 256), lambda i, g, c, tl: (i * nq + c, g)),
                  pl.BlockSpec((t, 128), lambda i, g, c, tl: (i, C_KVS // 128 + g)),
                  pl.BlockSpec((t, 128), lambda i, g, c, tl: (i, C_KVW // 128 + g))],
        out_specs=pl.BlockSpec((QB, 256), lambda i, g, c, tl: (i * nq + c, g)),
        scratch_shapes=[pltpu.VMEM((t, 128), BF16), pltpu.VMEM((t, 256), BF16),
                        pltpu.VMEM((t, 128), BF16), pltpu.VMEM((t, 128), BF16)],
    )
    return pl.pallas_call(
        functools.partial(_nsa_attend_kernel, nq=nq),
        grid_spec=grid_spec,
        out_shape=jax.ShapeDtypeStruct((b * t, Q_A), F32),
        compiler_params=_cparams(("arbitrary", "arbitrary", "arbitrary")),
        name="nsa_attend",
    )(tiles, h, unsel, h, o_c, h, h)


def _nsa_prompt(h, kcmp, b, t):
    o_c, unsel, tl = _nsa_select(h, kcmp, b, t)
    return _nsa_attend(tl[:, 0, :], h, unsel, o_c, b, t)


HB = 256


def _lower_bound(lb_ref):
    z = lb_ref[...]
    e = jnp.exp(z - jnp.max(z, axis=0, keepdims=True))
    return e[0:1] / jnp.sum(e, axis=0, keepdims=True)


def _hgrn_prompt_kernel(q_ref, f_ref, i_ref, g_ref, lb_ref, ng_ref, o_ref, s_out_ref, s_ref):
    tb = pl.program_id(1)

    @pl.when(tb == 0)
    def _():
        s_ref[...] = jnp.zeros_like(s_ref)

    n_c = HB // HGRN_CHUNK
    lb = _lower_bound(lb_ref)
    f = lb + (1.0 - lb) * jax.nn.sigmoid(f_ref[...])
    log_f = jnp.log(f)
    row = lax.broadcasted_iota(jnp.int32, (HB, HB), 0)
    col = lax.broadcasted_iota(jnp.int32, (HB, HB), 1)
    tril = jnp.logical_and(col <= row, row // HGRN_CHUNK == col // HGRN_CHUNK)
    lc = _dot01(tril.astype(BF16), log_f)
    lc3 = lc.reshape(n_c, HGRN_CHUNK, Q_B)
    lend = jnp.broadcast_to(lc3[:, HGRN_CHUNK - 1:HGRN_CHUNK, :], lc3.shape).reshape(HB, Q_B)
    q_t = (q_ref[...] * jnp.exp(lc)).astype(BF16)
    k = 1.0 - f
    k_t = (k * jnp.exp(-lc)).astype(BF16)
    k_e = (k * jnp.exp(lend - lc)).astype(BF16)
    dec = jnp.exp(lend)
    v_all = i_ref[...]
    gate = g_ref[...]
    ng = ng_ref[...]
    for hh in range(B_HEADS):
        ks = slice(hh * B_DK, (hh + 1) * B_DK)
        vs = slice(hh * B_DV, (hh + 1) * B_DV)
        v = v_all[:, vs].astype(BF16)
        a = lax.dot_general(q_t[:, ks], k_t[:, ks], _NT, preferred_element_type=F32)
        a = jnp.where(tril, a, 0.0).astype(BF16)
        o = jnp.dot(a, v, preferred_element_type=F32)
        st = s_ref[hh]
        inter = []
        for cc in range(n_c):
            rs = slice(cc * HGRN_CHUNK, (cc + 1) * HGRN_CHUNK)
            inter.append(lax.dot_general(q_t[rs, ks], st.astype(BF16), _NT, preferred_element_type=F32))
            u = lax.dot_general(v[rs], k_e[rs, ks], _TN, preferred_element_type=F32)
            last = (cc + 1) * HGRN_CHUNK - 1
            st = dec[last:last + 1, ks] * st + u
        s_ref[hh] = st
        o = o + jnp.concatenate(inter, axis=0)
        o = o * lax.rsqrt(jnp.mean(o * o, axis=-1, keepdims=True) + LN_EPS) * ng
        gt = gate[:, vs]
        o_ref[:, vs] = o * (gt * jax.nn.sigmoid(gt))

    @pl.when(tb == pl.num_programs(1) - 1)
    def _():
        s_out_ref[0] = s_ref[...]


def _hgrn_prompt(h, b, t, lb_logits, norm_g):
    nt = t // HB
    return pl.pallas_call(
        _hgrn_prompt_kernel,
        grid=(b, nt),
        in_specs=[pl.BlockSpec((HB, Q_B), lambda i, j: (i * nt + j, C_QB // Q_B)),
                  pl.BlockSpec((HB, Q_B), lambda i, j: (i * nt + j, C_FB // Q_B)),
                  pl.BlockSpec((HB, I_B), lambda i, j: (i * nt + j, C_IB // I_B)),
                  pl.BlockSpec((HB, I_B), lambda i, j: (i * nt + j, C_GB // I_B)),
                  pl.BlockSpec((2, Q_B), lambda i, j: (0, 0)),
                  pl.BlockSpec((1, B_DV), lambda i, j: (0, 0))],
        out_specs=[pl.BlockSpec((HB, I_B), lambda i, j: (i * nt + j, 0)),
                   pl.BlockSpec((1, B_HEADS, B_DV, B_DK), lambda i, j: (i, 0, 0, 0))],
        out_shape=[jax.ShapeDtypeStruct((b * t, I_B), F32),
                   jax.ShapeDtypeStruct((b, B_HEADS, B_DV, B_DK), F32)],
        scratch_shapes=[pltpu.VMEM((B_HEADS, B_DV, B_DK), F32)],
        compiler_params=_cparams(("arbitrary", "arbitrary")),
        name="hgrn_prompt",
    )(h, h, h, h, lb_logits, norm_g)


def _prompt_layer(x, w_pad, w_kvt, cmpw, lb_logits, norm_g, w_a, w_b, w_o, ln1_g, ln1_b, w_up, w_down,
                  ln2_g, ln2_b):
    b, t, _ = x.shape
    x2d = x.reshape(b * t, D_MODEL)
    h, *kv_t = _project_prompt(x2d, w_pad, w_kvt, b, t)
    kcmp = _cmp_prompt(h, b, t, *cmpw)
    o_a = _nsa_prompt(h, kcmp, b, t)
    o_b, s_end = _hgrn_prompt(h, b, t, lb_logits, norm_g)
    x1 = _merge(x2d, o_a, o_b, h, w_a, w_b, w_o, ln1_g, ln1_b)
    y = _mlp(x1, w_up, w_down, ln2_g, ln2_b)
    kv_out = [jnp.moveaxis(a.reshape(b, 2, KV_HEADS, HEAD_DIM, t), -1, 1) for a in kv_t]
    return y.reshape(b, t, D_MODEL), kv_out, jnp.swapaxes(s_end, 2, 3)


def _kv_out(h, col, b, t):
    kv = h[:, col:col + 256].reshape(b, t, KV_HEADS, 2, HEAD_DIM)
    return jnp.swapaxes(kv, 2, 3)


LAND_PAGES = 4
CH_PER_PAGE = PAGE_SIZE // CMP_STRIDE
BLK_PER_PAGE = PAGE_SIZE // SEL_BLOCK
N_POOL_SEL = N_SELECT - 1


def _head_rows(q_row, g):
    rows = [q_row[:, g * 256 + r * 64:g * 256 + (r + 1) * 64] for r in range(GROUP)]
    return jnp.concatenate(rows + [jnp.zeros((8 - GROUP, 64), F32)], axis=0) * ATTN_SCALE


def _slope_col(g):
    row = lax.broadcasted_iota(jnp.int32, (8, 1), 0)
    col = jnp.zeros((8, 1), F32)
    for r in range(GROUP):
        col = jnp.where(row == r, 2.0 ** -(g * GROUP + r + 1), col)
    return col


def _cmp_sample_kernel(pt_ref, cache_ref, w_ref, posf_ref, wb_ref, q_ref, oc_ref, idx_ref,
                       buf_ref, xk_ref, xv_ref, sem, *, n_pages, past):
    b = pl.program_id(0)
    n_ch = n_pages * CH_PER_PAGE
    n_blk = past // SEL_BLOCK
    lanes = -(-n_blk // 128) * 128

    def page_copy(pg, page):
        return pltpu.make_async_copy(cache_ref.at[page], buf_ref.at[pg], sem.at[pg])

    def issue(pg, carry):
        page_copy(pg, pt_ref[b, pg]).start()
        return carry

    def land(i, carry):
        pages = [i * LAND_PAGES + u for u in range(LAND_PAGES)]
        for pg in pages:
            page_copy(pg, 0).wait()
        for pg in pages:
            r0 = pl.multiple_of(pg * PAGE_SIZE, PAGE_SIZE)
            xk_ref[pl.ds(r0, PAGE_SIZE), :] = jnp.transpose(buf_ref[pg, 0].reshape(128, PAGE_SIZE))
            xv_ref[pl.ds(r0, PAGE_SIZE), :] = jnp.transpose(buf_ref[pg, 1].reshape(128, PAGE_SIZE))
        return carry

    lax.fori_loop(0, n_pages, issue, 0)
    lax.fori_loop(0, n_pages // LAND_PAGES, land, 0)

    bias = jnp.dot(posf_ref[...].astype(BF16), wb_ref[...], preferred_element_type=F32)[0:1]
    blocks = []
    for c, x_ref in enumerate((xk_ref, xv_ref)):
        acc = jnp.zeros((n_ch, 256), F32)
        for l2 in range(CMP_STRIDE // 2):
            xl = jnp.concatenate([x_ref[pl.ds(2 * l2 + u, n_ch, stride=CMP_STRIDE), :] for u in range(2)],
                                 axis=1).astype(BF16)
            acc += jnp.dot(xl, w_ref[c, l2], preferred_element_type=F32)
        bias_c = jnp.concatenate([bias[:, c * 64:(c + 1) * 64]] * KV_HEADS, axis=1)
        blocks.append((acc[:, :128] + pltpu.roll(acc[:, 128:], n_ch - 1, 0) + bias_c).astype(BF16))
    kc_all, vc_all = blocks

    q_row = q_ref[0]
    n_ix = lax.broadcasted_iota(jnp.int32, (1, n_ch), 1)
    dist_i = past - (n_ix * CMP_STRIDE + (CMP_LEN - 1))
    valid = jnp.logical_and(dist_i >= 0, n_ix < n_ch - 1)
    dist = dist_i.astype(F32)
    ratio = SEL_BLOCK // CMP_STRIDE
    pool = (lax.broadcasted_iota(jnp.int32, (n_ch, lanes), 0) // ratio
            == lax.broadcasted_iota(jnp.int32, (n_ch, lanes), 1)).astype(BF16)
    row8 = lax.broadcasted_iota(jnp.int32, (8, lanes), 0)
    score = jnp.full((8, lanes), -FORCE, F32)
    lane_g = lax.broadcasted_iota(jnp.int32, (8, 128), 1) // HEAD_DIM
    o_c = jnp.zeros((8, 128), F32)
    for g in range(KV_HEADS):
        pieces = [jnp.zeros((8, 64), F32)] * KV_HEADS
        pieces[g] = _head_rows(q_row, g)
        qpad = jnp.concatenate(pieces, axis=1).astype(BF16)
        s = lax.dot_general(qpad, kc_all, _NT, preferred_element_type=F32)
        s = jnp.where(valid, s - _slope_col(g) * dist, NEG)
        m = jnp.max(s, axis=-1, keepdims=True)
        p = jnp.where(valid, jnp.exp(s - m), 0.0)
        p = p / jnp.maximum(jnp.sum(p, axis=-1, keepdims=True), 1e-30)
        o_c = jnp.where(lane_g == g, jnp.dot(p.astype(BF16), vc_all, preferred_element_type=F32), o_c)
        imp = p[0:1] + p[1:2] + p[2:3] + p[3:4]
        imp_blk = _x_dot01(jnp.broadcast_to(imp, (8, n_ch)), pool)
        score = jnp.where(row8 == g, imp_blk, score)
    lane = lax.broadcasted_iota(jnp.int32, (8, lanes), 1)
    forced = jnp.logical_or(lane == 0, lane == n_blk - 1)
    score = jnp.where(lane < n_blk, jnp.where(forced, FORCE, score), -jnp.inf)
    sc_t = jnp.transpose(jnp.concatenate([score, jnp.full((120, lanes), -jnp.inf, F32)], axis=0))
    blk = lax.broadcasted_iota(jnp.int32, (lanes, 128), 0)
    picks = []
    for r in range(N_POOL_SEL):
        m = jnp.max(sc_t, axis=0, keepdims=True)
        first = jnp.min(jnp.where(sc_t == m, blk, lanes), axis=0, keepdims=True)
        picks.append(first)
        sc_t = jnp.where(blk == first, -jnp.inf, sc_t)
    idx_ref[0] = jnp.concatenate(picks + [jnp.zeros((N_SELECT - N_POOL_SEL, 128), jnp.int32)], axis=0)
    oc_ref[0] = o_c


def _cmp_sample(page_table, cache5, w_s, posf, wb, q3, past):
    bsz, n_pages = page_table.shape
    grid_spec = pltpu.PrefetchScalarGridSpec(
        num_scalar_prefetch=1,
        grid=(bsz,),
        in_specs=[pl.BlockSpec(memory_space=pl.ANY),
                  pl.BlockSpec((2, CMP_STRIDE // 2, 256, 256), lambda i, pt: (0, 0, 0, 0)),
                  pl.BlockSpec((8, 4096), lambda i, pt: (0, 0)),
                  pl.BlockSpec((4096, 128), lambda i, pt: (0, 0)),
                  pl.BlockSpec((1, 1, Q_A), lambda i, pt: (i, 0, 0))],
        out_specs=[pl.BlockSpec((1, 8, 128), lambda i, pt: (i, 0, 0)),
                   pl.BlockSpec((1, N_SELECT, 128), lambda i, pt: (i, 0, 0))],
        scratch_shapes=[pltpu.VMEM((n_pages, 2, KV_HEADS, HEAD_DIM, PAGE_SIZE), F32),
                        pltpu.VMEM((n_pages * PAGE_SIZE, 128), F32),
                        pltpu.VMEM((n_pages * PAGE_SIZE, 128), F32),
                        pltpu.SemaphoreType.DMA((n_pages,))],
    )
    return pl.pallas_call(
        functools.partial(_cmp_sample_kernel, n_pages=n_pages, past=past),
        grid_spec=grid_spec,
        out_shape=[jax.ShapeDtypeStruct((bsz, 8, 128), F32),
                   jax.ShapeDtypeStruct((bsz, N_SELECT, 128), jnp.int32)],
        compiler_params=_cparams(("arbitrary",)),
        name="cmp_sample",
    )(page_table, cache5, w_s, posf, wb, q3)


def _pick_lane(mat, lane, target):
    return jnp.sum(jnp.where(lane == target, mat, 0.0), axis=-1, keepdims=True)


def _sel_win_sample_kernel(pt_ref, idx_ref, cache_ref, win_ref, q_ref, ks_ref, kw_ref, gl_ref, oc_ref, o_ref,
                           kbuf_ref, vbuf_ref, sem, *, past):
    b = pl.program_id(0)
    slot_lanes = N_SELECT * PAGE_SIZE

    def page_copies(g, k, page):
        dst = pl.ds(k * PAGE_SIZE, PAGE_SIZE)
        return (pltpu.make_async_copy(cache_ref.at[page, 0, g], kbuf_ref.at[g, :, dst], sem),
                pltpu.make_async_copy(cache_ref.at[page, 1, g], vbuf_ref.at[g, :, dst], sem))

    for g in range(KV_HEADS):
        for k in range(N_POOL_SEL):
            page = pt_ref[b, idx_ref[b, g * N_SELECT + k] // BLK_PER_PAGE]
            for cp in page_copies(g, k, page):
                cp.start()
        pad = pl.ds(N_POOL_SEL * PAGE_SIZE, PAGE_SIZE)
        kbuf_ref[g, :, pad] = jnp.zeros((HEAD_DIM, PAGE_SIZE), F32)
        vbuf_ref[g, :, pad] = jnp.zeros((HEAD_DIM, PAGE_SIZE), F32)

    q_row = q_ref[0]
    ks_new = ks_ref[0]
    kw_new = kw_ref[0]
    gl_all = jax.nn.sigmoid(gl_ref[0])
    lane128 = lax.broadcasted_iota(jnp.int32, (8, 128), 1)
    row8 = lax.broadcasted_iota(jnp.int32, (8, 1), 0)
    w_len = win_ref.shape[-1]

    def two_piece(qh, slope, k_t, v_t, dist_i, valid, k_new, v_new):
        s = jnp.dot(qh.astype(BF16), k_t.astype(BF16), preferred_element_type=F32)
        s = jnp.where(valid, s - slope * dist_i.astype(F32), NEG)
        s_n = jnp.sum(qh * k_new, axis=-1, keepdims=True)
        m = jnp.maximum(jnp.max(s, axis=-1, keepdims=True), s_n)
        p = jnp.where(valid, jnp.exp(s - m), 0.0)
        p_n = jnp.exp(s_n - m)
        l = jnp.sum(p, axis=-1, keepdims=True) + p_n
        o = lax.dot_general(p.astype(BF16), v_t.astype(BF16), _NT, preferred_element_type=F32) + p_n * v_new
        return o / l

    outs = []
    for g in range(KV_HEADS):
        qh = _head_rows(q_row, g)
        slope = _slope_col(g)
        j_ix = lax.broadcasted_iota(jnp.int32, (1, w_len), 1)
        dist_w = w_len - j_ix
        o_w = two_piece(qh, slope, win_ref[0, 0, g], win_ref[0, 1, g], dist_w, dist_w < WINDOW,
                        kw_new[:, g * 128:g * 128 + 64], kw_new[:, g * 128 + 64:(g + 1) * 128])
        outs.append((qh, slope, o_w))

    for g in range(KV_HEADS):
        for k in range(N_POOL_SEL):
            for cp in page_copies(g, k, 0):
                cp.wait()

    lane_s = lax.broadcasted_iota(jnp.int32, (1, slot_lanes), 1)
    o_all = []
    for g in range(KV_HEADS):
        qh, slope, o_w = outs[g]
        pos_k = jnp.full((1, slot_lanes), past + 1, jnp.int32)
        for k in range(N_POOL_SEL):
            blk = idx_ref[b, g * N_SELECT + k]
            r = lane_s % PAGE_SIZE
            in_blk = jnp.logical_and(lane_s // PAGE_SIZE == k, r // SEL_BLOCK == blk % BLK_PER_PAGE)
            pos_k = jnp.where(in_blk, (blk // BLK_PER_PAGE) * PAGE_SIZE + r, pos_k)
        dist_s = past - pos_k
        o_s = two_piece(qh, slope, kbuf_ref[g], vbuf_ref[g], dist_s, dist_s >= 0,
                        ks_new[:, g * 128:g * 128 + 64], ks_new[:, g * 128 + 64:(g + 1) * 128])
        o_c = oc_ref[0][:, g * 64:(g + 1) * 64]
        gates = jnp.broadcast_to(gl_all[:, g * 128:(g + 1) * 128], (8, 128))
        g_c = _pick_lane(gates, lane128, 3 * row8)
        g_s = _pick_lane(gates, lane128, 3 * row8 + 1)
        g_w = _pick_lane(gates, lane128, 3 * row8 + 2)
        o_all.append(g_c * o_c + g_s * o_s + g_w * o_w)
    o_ref[0] = jnp.concatenate(o_all, axis=1)


def _sel_win_sample(page_table, idx, cache_sel5, cache_win5, q3, ks3, kw3, gl3, o_c, past):
    bsz = page_table.shape[0]
    w_len = cache_win5.shape[-1]
    grid_spec = pltpu.PrefetchScalarGridSpec(
        num_scalar_prefetch=2,
        grid=(bsz,),
        in_specs=[pl.BlockSpec(memory_space=pl.ANY),
                  pl.BlockSpec((1, 2, KV_HEADS, HEAD_DIM, w_len), lambda i, pt, ix: (i, 0, 0, 0, 0)),
                  pl.BlockSpec((1, 1, Q_A), lambda i, pt, ix: (i, 0, 0)),
                  pl.BlockSpec((1, 1, 256), lambda i, pt, ix: (i, 0, 0)),
                  pl.BlockSpec((1, 1, 256), lambda i, pt, ix: (i, 0, 0)),
                  pl.BlockSpec((1, 1, 256), lambda i, pt, ix: (i, 0, 0)),
                  pl.BlockSpec((1, 8, 128), lambda i, pt, ix: (i, 0, 0))],
        out_specs=pl.BlockSpec((1, 8, 128), lambda i, pt, ix: (i, 0, 0)),
        scratch_shapes=[pltpu.VMEM((KV_HEADS, HEAD_DIM, N_SELECT * PAGE_SIZE), F32),
                        pltpu.VMEM((KV_HEADS, HEAD_DIM, N_SELECT * PAGE_SIZE), F32),
                        pltpu.SemaphoreType.DMA(())],
    )
    return pl.pallas_call(
        functools.partial(_sel_win_sample_kernel, past=past),
        grid_spec=grid_spec,
        out_shape=jax.ShapeDtypeStruct((bsz, 8, 128), F32),
        compiler_params=_cparams(("arbitrary",)),
        name="sel_win_sample",
    )(page_table, idx, cache_sel5, cache_win5, q3, ks3, kw3, gl3, o_c)


def _hgrn_sample_kernel(q_ref, f_ref, v_ref, g_ref, lb_ref, ng_ref, s_ref, o_ref, s_out_ref):
    z = lb_ref[...]
    e = jnp.exp(z - jnp.max(z, axis=0, keepdims=True))
    lb = e[0] / jnp.sum(e, axis=0)
    f = lb + (1.0 - lb) * jax.nn.sigmoid(f_ref[0])
    decay = jnp.exp(jnp.log(f))
    k = 1.0 - f
    q = q_ref[0]
    v = v_ref[0]
    gate = g_ref[0]
    ng = ng_ref[...]
    for hh in range(B_HEADS):
        hs = slice(hh, hh + 1)
        s_new = decay[hs] * s_ref[0, hh] + v[:, hs] * k[hs]
        s_out_ref[0, hh] = s_new
        o = jnp.sum(q[hs] * s_new, axis=1, keepdims=True)
        o = o * lax.rsqrt(jnp.mean(o * o, axis=0, keepdims=True) + LN_EPS) * ng
        gt = gate[:, hs]
        o_ref[0, :, hs] = o * (gt * jax.nn.sigmoid(gt))


def _hgrn_sample(q_hk, f_hk, v_vh, g_vh, lb_hk, ng_col, state_t):
    bsz = q_hk.shape[0]
    return pl.pallas_call(
        _hgrn_sample_kernel,
        grid=(bsz,),
        in_specs=[pl.BlockSpec((1, B_HEADS, B_DK), lambda i: (i, 0, 0)),
                  pl.BlockSpec((1, B_HEADS, B_DK), lambda i: (i, 0, 0)),
                  pl.BlockSpec((1, B_DV, B_HEADS), lambda i: (i, 0, 0)),
                  pl.BlockSpec((1, B_DV, B_HEADS), lambda i: (i, 0, 0)),
                  pl.BlockSpec((2, B_HEADS, B_DK), lambda i: (0, 0, 0)),
                  pl.BlockSpec((B_DV, 1), lambda i: (0, 0)),
                  pl.BlockSpec((1, B_HEADS, B_DV, B_DK), lambda i: (i, 0, 0, 0))],
        out_specs=[pl.BlockSpec((1, B_DV, B_HEADS), lambda i: (i, 0, 0)),
                   pl.BlockSpec((1, B_HEADS, B_DV, B_DK), lambda i: (i, 0, 0, 0))],
        out_shape=[jax.ShapeDtypeStruct((bsz, B_DV, B_HEADS), F32),
                   jax.ShapeDtypeStruct((bsz, B_HEADS, B_DV, B_DK), F32)],
        compiler_params=_cparams(("arbitrary",)),
        name="hgrn_sample",
    )(q_hk, f_hk, v_vh, g_vh, lb_hk, ng_col, state_t)


def _rows_last(cache):
    return jnp.moveaxis(cache, -4, -1)


def _sample_layer(x, cache_cmp, cache_sel, cache_win, state, page_table, w_pad, cmpw, w_s, lb_logits, norm_g,
                  w_a, w_b, w_o, ln1_g, ln1_b, w_up, w_down, ln2_g, ln2_b):
    bsz, t, _ = x.shape
    assert t == 1, "the sample group decodes one token per request"
    n_pages = page_table.shape[1]
    past = n_pages * PAGE_SIZE
    x2d = x.reshape(bsz, D_MODEL)
    h = _project(x2d, w_pad)
    h3 = h.reshape(bsz, 1, D_PAD)
    q3 = h3[:, :, C_QA:C_QA + Q_A]
    wl, wb, posf = cmpw
    o_c, idx = _cmp_sample(page_table, _rows_last(cache_cmp), w_s, posf, wb, q3, past)
    idx2 = jnp.swapaxes(idx[:, :, :KV_HEADS], 1, 2).reshape(bsz, KV_HEADS * N_SELECT)
    o_rd = _sel_win_sample(page_table, idx2, _rows_last(cache_sel), _rows_last(cache_win),
                           q3, h3[:, :, C_KVS:C_KVS + 256], h3[:, :, C_KVW:C_KVW + 256],
                           h3[:, :, C_GL:C_GL + 256], o_c, past)
    o_a = jnp.swapaxes(o_rd[:, :GROUP].reshape(bsz, GROUP, KV_HEADS, HEAD_DIM), 1, 2).reshape(bsz, Q_A)
    to_hk = lambda a: a.reshape(-1, B_HEADS, B_DK)
    to_vh = lambda a: jnp.swapaxes(a.reshape(-1, B_HEADS, B_DV), 1, 2)
    o_vh, s_t = _hgrn_sample(to_hk(h[:, C_QB:C_QB + Q_B]), to_hk(h[:, C_FB:C_FB + Q_B]),
                             to_vh(h[:, C_IB:C_IB + I_B]), to_vh(h[:, C_GB:C_GB + I_B]),
                             to_hk(lb_logits), norm_g.reshape(B_DV, 1), jnp.swapaxes(state, 2, 3))
    o_b = jnp.swapaxes(o_vh, 1, 2).reshape(bsz, I_B)
    x1 = _merge(x2d, o_a, o_b, h, w_a, w_b, w_o, ln1_g, ln1_b)
    y = _mlp(x1, w_up, w_down, ln2_g, ln2_b)
    return y.reshape(bsz, 1, D_MODEL), h, jnp.swapaxes(s_t, 2, 3)


def _cmp_sample_weights(cmp_w):
    w = cmp_w.reshape(2, 2, CMP_STRIDE, HEAD_DIM, HEAD_DIM)
    base = jnp.transpose(w, (0, 2, 3, 1, 4))
    ws = jnp.stack([_at_slot(base, g, 4) for g in range(2)], axis=2)
    return ws.reshape(2, CMP_STRIDE // 2, 256, 256).astype(BF16)


def kernel(x_prompt, x_sample, cache_cmp_kv, cache_sel_kv, cache_win_kv, state_hgrn, page_table,
           w_in, cmp_w, cmp_pos, hgrn_lb_logits, hgrn_norm_g, w_br_a, w_br_b, w_out,
           ln1_g, ln1_b, w_up, w_down, ln2_g, ln2_b):
    assert w_in.shape[0] == 1, "one layer"
    b, t, _ = x_prompt.shape
    bsz = x_sample.shape[0]
    assert t % HB == 0 and t % QB == 0 and WINDOW + QB <= t <= 128 * SEL_BLOCK
    w_pad = _reorder_w_in(w_in[0])
    cmpw = _cmp_weights(cmp_w[0], cmp_pos[0])
    w_s = _cmp_sample_weights(cmp_w[0])
    dense = (w_br_a[0].astype(BF16), w_br_b[0].astype(BF16), w_out[0].astype(BF16), ln1_g, ln1_b,
             w_up[0].astype(BF16), w_down[0].astype(BF16), ln2_g, ln2_b)

    o_kv = Q_A
    w_kvt = jnp.transpose(w_in[0][:, o_kv:o_kv + 3 * KV_A]).astype(BF16)
    y_p, (kvc_p, kvs_p, kvw_p), s_p = _prompt_layer(x_prompt, w_pad, w_kvt, cmpw, hgrn_lb_logits, hgrn_norm_g,
                                                   *dense)
    y_s, h_s, s_s = _sample_layer(x_sample, cache_cmp_kv[0], cache_sel_kv[0], cache_win_kv[0], state_hgrn[0],
                                  page_table, w_pad, cmpw, w_s, hgrn_lb_logits, hgrn_norm_g, *dense)

    win_p = min(WINDOW, t)
    kvw_s = _kv_out(h_s, C_KVW, bsz, 1)
    new_win_s = jnp.concatenate([cache_win_kv[0], kvw_s], axis=1)[:, -min(WINDOW, cache_win_kv.shape[2] + 1):]
    return (y_p, y_s,
            kvc_p[None], kvs_p[None], kvw_p[:, -win_p:][None], s_p[None],
            _kv_out(h_s, C_KVC, bsz, 1)[None], _kv_out(h_s, C_KVS, bsz, 1)[None], new_win_s[None], s_s[None])
```

```python
import functools

import numpy as np
import jax
import jax.numpy as jnp
from jax import lax
from jax.experimental import pallas as pl
from jax.experimental.pallas import tpu as pltpu

F32 = jnp.float32
BF16 = jnp.bfloat16

D_MODEL = 1024
HEAD_DIM = 64
A_HEADS = 8
KV_HEADS = 2
GROUP = A_HEADS // KV_HEADS
CMP_STRIDE = 16
CMP_LEN = 32
SEL_BLOCK = 64
N_SELECT = 16
WINDOW = 512
PAGE_SIZE = 128
B_HEADS = 8
B_DK = 128
B_DV = 64
HGRN_CHUNK = 32
D_FF = 4 * D_MODEL
DEEPNORM_ALPHA = 2.0 ** 0.25
LN_EPS = 1e-5
NEG = -1e30
FORCE = 1e6
ATTN_SCALE = HEAD_DIM ** -0.5
LOG2E = 1.4426950408889634
Q_A = A_HEADS * HEAD_DIM
KV_A = 2 * KV_HEADS * HEAD_DIM
GATE_A = 3 * A_HEADS
Q_B = B_HEADS * B_DK
I_B = B_HEADS * B_DV

C_MG = 0
C_QB = 2048
C_FB = 3072
C_QA = 4096
C_IB = 4608
C_GB = 5120
C_KVC = 5632
C_KVS = 5888
C_KVW = 6144
C_GL = 6400
D_PAD = 6656
PROJ_TN = 1664

VMEM_LIMIT = 56 * 1024 * 1024
UNSEL = float(2.0 ** 100)


def _cparams(sem):
    return pltpu.CompilerParams(dimension_semantics=sem, vmem_limit_bytes=VMEM_LIMIT)


def _proj_perm():
    perm = np.full((D_PAD,), -1, np.int64)
    o_qa, o_kvc, o_kvs, o_kvw = 0, Q_A, Q_A + KV_A, Q_A + 2 * KV_A
    o_gl = Q_A + 3 * KV_A
    o_qb = o_gl + GATE_A
    o_fb = o_qb + Q_B
    o_ib = o_fb + Q_B
    o_gb = o_ib + I_B
    o_mg = o_gb + I_B
    perm[C_MG:C_MG + 2 * D_MODEL] = o_mg + np.arange(2 * D_MODEL)
    perm[C_QB:C_QB + Q_B] = o_qb + np.arange(Q_B)
    perm[C_FB:C_FB + Q_B] = o_fb + np.arange(Q_B)
    perm[C_QA:C_QA + Q_A] = o_qa + np.arange(Q_A)
    perm[C_IB:C_IB + I_B] = o_ib + np.arange(I_B)
    perm[C_GB:C_GB + I_B] = o_gb + np.arange(I_B)
    for new, old in ((C_KVC, o_kvc), (C_KVS, o_kvs), (C_KVW, o_kvw)):
        for g in range(KV_HEADS):
            for c in range(2):
                dst = new + g * 128 + c * 64
                src = old + c * 128 + g * 64
                perm[dst:dst + 64] = src + np.arange(64)
    for g in range(KV_HEADS):
        perm[C_GL + g * 128:C_GL + g * 128 + 12] = o_gl + g * 12 + np.arange(12)
    return perm


def _perm_runs(perm):
    runs = []
    for col in perm:
        if runs and ((col < 0 and runs[-1][0] < 0) or (col >= 0 and runs[-1][0] >= 0
                                                      and runs[-1][0] + runs[-1][1] == col)):
            runs[-1][1] += 1
        else:
            runs.append([int(col), 1])
    return [tuple(r) for r in runs]


_PERM_RUNS = _perm_runs(_proj_perm())


def _reorder_w_in(w):
    pieces = [jnp.zeros((w.shape[0], n), w.dtype) if s < 0 else w[:, s:s + n] for s, n in _PERM_RUNS]
    return jnp.concatenate(pieces, axis=1).astype(BF16)


def _proj_kernel(x_ref, w_ref, o_ref):
    xb = x_ref[...].astype(BF16)
    for j in range(D_PAD // PROJ_TN):
        cols = slice(j * PROJ_TN, (j + 1) * PROJ_TN)
        o_ref[:, cols] = jnp.dot(xb, w_ref[:, cols], preferred_element_type=F32)


def _proj_kvt_kernel(x_ref, w_ref, wt_ref, o_ref, kc_ref, ks_ref, kw_ref):
    xb = x_ref[...].astype(BF16)
    for j in range(D_PAD // PROJ_TN):
        cols = slice(j * PROJ_TN, (j + 1) * PROJ_TN)
        o_ref[:, cols] = jnp.dot(xb, w_ref[:, cols], preferred_element_type=F32)
    for j, kt_ref in enumerate((kc_ref, ks_ref, kw_ref)):
        kt_ref[0] = lax.dot_general(wt_ref[j * KV_A:(j + 1) * KV_A, :], xb, _NT, preferred_element_type=F32)


def _project_prompt(x2d, w_pad, w_kvt, b, t):
    tm = 512
    nt = t // tm
    kvt = jax.ShapeDtypeStruct((b, KV_A, t), F32)
    kvt_spec = pl.BlockSpec((1, KV_A, tm), lambda i: (i // nt, 0, i % nt))
    return pl.pallas_call(
        _proj_kvt_kernel,
        grid=(b * nt,),
        in_specs=[pl.BlockSpec((tm, D_MODEL), lambda i: (i, 0)),
                  pl.BlockSpec((D_MODEL, D_PAD), lambda i: (0, 0), pipeline_mode=pl.Buffered(1)),
                  pl.BlockSpec((3 * KV_A, D_MODEL), lambda i: (0, 0), pipeline_mode=pl.Buffered(1))],
        out_specs=[pl.BlockSpec((tm, D_PAD), lambda i: (i, 0)), kvt_spec, kvt_spec, kvt_spec],
        out_shape=[jax.ShapeDtypeStruct((b * t, D_PAD), F32), kvt, kvt, kvt],
        compiler_params=_cparams(("arbitrary",)),
        name="proj_prompt",
    )(x2d, w_pad, w_kvt)


def _project(x2d, w_pad):
    n = x2d.shape[0]
    tm = min(512, n)
    return pl.pallas_call(
        _proj_kernel,
        grid=(n // tm,),
        in_specs=[pl.BlockSpec((tm, D_MODEL), lambda i: (i, 0)),
                  pl.BlockSpec((D_MODEL, D_PAD), lambda i: (0, 0), pipeline_mode=pl.Buffered(1))],
        out_specs=pl.BlockSpec((tm, D_PAD), lambda i: (i, 0)),
        out_shape=jax.ShapeDtypeStruct((n, D_PAD), F32),
        compiler_params=_cparams(("arbitrary",)),
        name="proj",
    )(x2d, w_pad)


def _layer_norm(v, g, b):
    mu = jnp.mean(v, axis=-1, keepdims=True)
    d = v - mu
    var = jnp.mean(d * d, axis=-1, keepdims=True)
    return d * lax.rsqrt(var + LN_EPS) * g + b


def _merge_kernel(x_ref, oa_ref, ob_ref, mga_ref, mgb_ref, wa_ref, wb_ref, wo_ref, g_ref, b_ref, o_ref):
    br_a = jnp.dot(oa_ref[...].astype(BF16), wa_ref[...], preferred_element_type=F32)
    br_b = jnp.dot(ob_ref[...].astype(BF16), wb_ref[...], preferred_element_type=F32)
    merged = jax.nn.sigmoid(mga_ref[...]) * br_a + jax.nn.sigmoid(mgb_ref[...]) * br_b
    mix = jnp.dot(merged.astype(BF16), wo_ref[...], preferred_element_type=F32)
    o_ref[...] = _layer_norm(DEEPNORM_ALPHA * x_ref[...] + mix, g_ref[...], b_ref[...])


def _merge(x2d, o_a, o_b, h, w_a, w_b, w_o, ln_g, ln_b):
    n = x2d.shape[0]
    tm = min(256, n)
    const = lambda i: (0, 0)
    return pl.pallas_call(
        _merge_kernel,
        grid=(n // tm,),
        in_specs=[pl.BlockSpec((tm, D_MODEL), lambda i: (i, 0)),
                  pl.BlockSpec((tm, Q_A), lambda i: (i, 0)),
                  pl.BlockSpec((tm, I_B), lambda i: (i, 0)),
                  pl.BlockSpec((tm, D_MODEL), lambda i: (i, C_MG // D_MODEL)),
                  pl.BlockSpec((tm, D_MODEL), lambda i: (i, C_MG // D_MODEL + 1)),
                  pl.BlockSpec((Q_A, D_MODEL), const),
                  pl.BlockSpec((I_B, D_MODEL), const),
                  pl.BlockSpec((D_MODEL, D_MODEL), const),
                  pl.BlockSpec((1, D_MODEL), const),
                  pl.BlockSpec((1, D_MODEL), const)],
        out_specs=pl.BlockSpec((tm, D_MODEL), lambda i: (i, 0)),
        out_shape=jax.ShapeDtypeStruct((n, D_MODEL), F32),
        compiler_params=_cparams(("arbitrary",)),
        name="merge_ln1",
    )(x2d, o_a, o_b, h, h, w_a, w_b, w_o, ln_g, ln_b)


def _mlp_kernel(x_ref, wu_ref, wd_ref, g_ref, b_ref, o_ref, xb_ref, acc_ref):
    j = pl.program_id(1)

    @pl.when(j == 0)
    def _():
        xb_ref[...] = x_ref[...].astype(BF16)
        acc_ref[...] = jnp.zeros_like(acc_ref)

    u = jnp.dot(xb_ref[...], wu_ref[...], preferred_element_type=F32)
    u = jnp.maximum(u, 0.0)
    acc_ref[...] += jnp.dot((u * u).astype(BF16), wd_ref[...], preferred_element_type=F32)

    @pl.when(j == pl.num_programs(1) - 1)
    def _():
        o_ref[...] = _layer_norm(DEEPNORM_ALPHA * x_ref[...] + acc_ref[...], g_ref[...], b_ref[...])


def _mlp(x1, w_up, w_down, ln_g, ln_b):
    n = x1.shape[0]
    tm = min(1024, n)
    tf = 1024
    return pl.pallas_call(
        _mlp_kernel,
        grid=(n // tm, D_FF // tf),
        in_specs=[pl.BlockSpec((tm, D_MODEL), lambda i, j: (i, 0)),
                  pl.BlockSpec((D_MODEL, tf), lambda i, j: (0, j)),
                  pl.BlockSpec((tf, D_MODEL), lambda i, j: (j, 0)),
                  pl.BlockSpec((1, D_MODEL), lambda i, j: (0, 0)),
                  pl.BlockSpec((1, D_MODEL), lambda i, j: (0, 0))],
        out_specs=pl.BlockSpec((tm, D_MODEL), lambda i, j: (i, 0)),
        out_shape=jax.ShapeDtypeStruct((n, D_MODEL), F32),
        scratch_shapes=[pltpu.VMEM((tm, D_MODEL), BF16), pltpu.VMEM((tm, D_MODEL), F32)],
        compiler_params=_cparams(("arbitrary", "arbitrary")),
        name="mlp_ln2",
    )(x1, w_up, w_down, ln_g, ln_b)


def _dot01(a01, x):
    hi = x.astype(BF16)
    r1 = x - hi.astype(F32)
    mid = r1.astype(BF16)
    lo = (r1 - mid.astype(F32)).astype(BF16)
    out = jnp.dot(a01, hi, preferred_element_type=F32)
    out += jnp.dot(a01, mid, preferred_element_type=F32)
    out += jnp.dot(a01, lo, preferred_element_type=F32)
    return out


def _x_dot01(x, b01):
    hi = x.astype(BF16)
    r1 = x - hi.astype(F32)
    mid = r1.astype(BF16)
    lo = (r1 - mid.astype(F32)).astype(BF16)
    out = jnp.dot(hi, b01, preferred_element_type=F32)
    out += jnp.dot(mid, b01, preferred_element_type=F32)
    out += jnp.dot(lo, b01, preferred_element_type=F32)
    return out


_NT = (((1,), (1,)), ((), ()))
_TN = (((0,), (0,)), ((), ()))


def _cmp_prompt_kernel(kv_ref, w_ref, posf_ref, wb_ref, o_ref, *, n_ch):
    acc = jnp.zeros((n_ch, 256), F32)
    for l in range(CMP_STRIDE):
        xl = kv_ref[pl.ds(l, n_ch, stride=CMP_STRIDE), :].astype(BF16)
        acc += jnp.dot(xl, w_ref[l], preferred_element_type=F32)
    bias = jnp.dot(posf_ref[...].astype(BF16), wb_ref[...], preferred_element_type=F32)[0:1]
    nxt = pltpu.roll(acc[:, 128:], n_ch - 1, 0)
    o_ref[0] = (acc[:, :128] + nxt + bias).astype(BF16)


def _at_slot(a, slot, axis):
    pads = [(0, 0)] * (a.ndim + 1)
    pads[axis] = (slot, 1 - slot)
    return jnp.pad(jnp.expand_dims(a, axis), pads)


def _cmp_weights(cmp_w, cmp_pos):
    w = cmp_w.reshape(2, 2, CMP_STRIDE, HEAD_DIM, HEAD_DIM)
    wl = jnp.stack([_at_slot(jnp.transpose(w[c], (1, 2, 0, 3)), c, 3) for c in range(2)], axis=1)
    wb = jnp.stack([_at_slot(cmp_w[c].reshape(CMP_LEN * HEAD_DIM, HEAD_DIM), c, 1) for c in range(2)])
    wl = wl.reshape(CMP_STRIDE, 128, 256).astype(BF16)
    wb = wb.reshape(2 * CMP_LEN * HEAD_DIM, 128).astype(BF16)
    posf = jnp.broadcast_to(cmp_pos.reshape(1, 2 * CMP_LEN * HEAD_DIM), (8, 2 * CMP_LEN * HEAD_DIM))
    return wl, wb, posf


def _cmp_prompt(h, b, t, wl, wb, posf):
    n_ch = t // CMP_STRIDE
    return pl.pallas_call(
        functools.partial(_cmp_prompt_kernel, n_ch=n_ch),
        grid=(b, KV_HEADS),
        in_specs=[pl.BlockSpec((t, 128), lambda i, g: (i, C_KVC // 128 + g)),
                  pl.BlockSpec((CMP_STRIDE, 128, 256), lambda i, g: (0, 0, 0)),
                  pl.BlockSpec((8, 4096), lambda i, g: (0, 0)),
                  pl.BlockSpec((4096, 128), lambda i, g: (0, 0))],
        out_specs=pl.BlockSpec((1, n_ch, 128), lambda i, g: (i, 0, g)),
        out_shape=jax.ShapeDtypeStruct((b, n_ch, 256), BF16),
        compiler_params=_cparams(("arbitrary", "arbitrary")),
        name="cmp_prompt",
    )(h, wl, posf, wb)


QB = 256
PIECE = 256
SEL_UNROLL = 2
KT = 256
WT = 128


ROWS = GROUP * QB
TL_COUNT = 64


def _head_major(q):
    return jnp.concatenate([q[:, r * 64:(r + 1) * 64] for r in range(GROUP)], axis=0) * ATTN_SCALE


def _head_slope(g, r):
    return jnp.where(g == 0, 2.0 ** -(r + 1), 2.0 ** -(r + 1 + GROUP)).astype(F32)


def _slope_row(g):
    r_ix = lax.broadcasted_iota(jnp.int32, (1, ROWS), 1) // QB
    row = jnp.zeros((1, ROWS), F32)
    for r in range(GROUP):
        row = jnp.where(r_ix == r, _head_slope(g, r), row)
    return row


def _nsa_select_kernel(q_ref, kc_ref, oc_ref, un_ref, tl_ref, *, n_ch):
    g = pl.program_id(1)
    c = pl.program_id(2)
    t0 = c * QB
    qpad = jnp.concatenate([_head_major(q_ref[...]), jnp.zeros((ROWS, 64), F32)], axis=1).astype(BF16)
    kc = kc_ref[0]
    s = lax.dot_general(kc, qpad, _NT, preferred_element_type=F32)
    pos_row = t0 + lax.broadcasted_iota(jnp.int32, (1, ROWS), 1) % QB
    n_col = lax.broadcasted_iota(jnp.int32, (n_ch, 1), 0)
    dist_i = pos_row - (n_col * CMP_STRIDE + (CMP_LEN - 1))
    valid = jnp.logical_and(dist_i >= 0, n_col < n_ch - 1)
    s = jnp.where(valid, s - _slope_row(g) * dist_i.astype(F32), NEG)
    m = jnp.max(s, axis=0, keepdims=True)
    p = jnp.exp(s - m)
    inv = jnp.where(m > 0.5 * NEG, 1.0 / jnp.maximum(jnp.sum(p, axis=0, keepdims=True), 1e-30), 0.0)
    p = p * inv
    o_c = lax.dot_general(p.astype(BF16), kc, _TN, preferred_element_type=F32)
    for r in range(GROUP):
        oc_ref[:, r * 64:(r + 1) * 64] = o_c[r * QB:(r + 1) * QB, 64:128]

    imp = p[:, 0:QB]
    for r in range(1, GROUP):
        imp = imp + p[:, r * QB:(r + 1) * QB]
    ratio = SEL_BLOCK // CMP_STRIDE
    pool_t = (lax.broadcasted_iota(jnp.int32, (128, n_ch), 1) // ratio
              == lax.broadcasted_iota(jnp.int32, (128, n_ch), 0)).astype(BF16)
    imp_blk = _dot01(pool_t, imp)
    blk = lax.broadcasted_iota(jnp.int32, (128, QB), 0)
    pos_q = t0 + lax.broadcasted_iota(jnp.int32, (1, QB), 1)
    cur = pos_q // SEL_BLOCK
    forced = jnp.logical_or(jnp.logical_or(blk == 0, blk == cur), blk == cur - 1)
    allowed = blk * SEL_BLOCK <= pos_q
    work = jnp.where(jnp.logical_and(allowed, jnp.logical_not(forced)), imp_blk, -jnp.inf)
    sel = forced
    for _ in range(N_SELECT - 3):
        mx = jnp.max(work, axis=0, keepdims=True)
        first = jnp.min(jnp.where(work == mx, blk, 128), axis=0, keepdims=True)
        pick = blk == first
        sel = jnp.logical_or(sel, pick)
        work = jnp.where(pick, -jnp.inf, work)
    sel = jnp.logical_and(sel, allowed)
    un_ref[...] = jnp.transpose(jnp.where(sel, 0.0, -UNSEL)).astype(BF16)

    tile_of = (lax.broadcasted_iota(jnp.int32, (128, 128), 1) // (KT // SEL_BLOCK)
               == lax.broadcasted_iota(jnp.int32, (128, 128), 0)).astype(BF16)
    sel_b = jnp.where(jnp.logical_and(sel, blk > 0), 1.0, 0.0).astype(BF16)
    cnt = jnp.sum(jnp.dot(tile_of, sel_b, preferred_element_type=F32), axis=1, keepdims=True)
    tile_col = lax.broadcasted_iota(jnp.int32, (128, 1), 0)
    flag = jnp.logical_and(cnt > 0.0, tile_col < t0 // KT)
    flag_m = jnp.where(jnp.broadcast_to(flag, (128, 128)), 1.0, 0.0).astype(BF16)
    row_i = lax.broadcasted_iota(jnp.int32, (128, 128), 0)
    lane_i = lax.broadcasted_iota(jnp.int32, (128, 128), 1)
    before = jnp.dot((lane_i < row_i).astype(BF16), flag_m, preferred_element_type=F32)
    slot = jnp.where(jnp.logical_and(flag, before == lane_i.astype(F32)), 1.0, 0.0).astype(BF16)
    j_rows = lax.broadcasted_iota(jnp.int32, (8, 128), 1).astype(BF16)
    listed = jnp.dot(j_rows, slot, preferred_element_type=F32)
    total = jnp.dot(jnp.ones((8, 128), BF16), flag_m, preferred_element_type=F32)
    lane8 = lax.broadcasted_iota(jnp.int32, (8, 128), 1)
    tl_ref[0] = jnp.where(lane8 == TL_COUNT, total, listed).astype(jnp.int32)


def _nsa_select(h, kcmp, b, t):
    n_ch = t // CMP_STRIDE
    nq = t // QB
    steps = b * KV_HEADS * nq
    return pl.pallas_call(
        functools.partial(_nsa_select_kernel, n_ch=n_ch),
        grid=(b, KV_HEADS, nq),
        in_specs=[pl.BlockSpec((QB, 256), lambda i, g, c: (i * nq + c, C_QA // 256 + g)),
                  pl.BlockSpec((1, n_ch, 128), lambda i, g, c: (i, 0, g))],
        out_specs=[pl.BlockSpec((QB, 256), lambda i, g, c: (i * nq + c, g)),
                   pl.BlockSpec((QB, 128), lambda i, g, c: (i * nq + c, g)),
                   pl.BlockSpec((1, 8, 128), lambda i, g, c: ((i * KV_HEADS + g) * nq + c, 0, 0))],
        out_shape=[jax.ShapeDtypeStruct((b * t, Q_A), F32),
                   jax.ShapeDtypeStruct((b * t, KV_HEADS * 128), BF16),
                   jax.ShapeDtypeStruct((steps, 8, 128), jnp.int32)],
        compiler_params=_cparams(("arbitrary", "arbitrary", "arbitrary")),
        name="nsa_select",
    )(h, kcmp)


def _nsa_attend_kernel(tl_ref, q_ref, un_ref, gl_ref, oc_ref, ks_ref, kw_ref, o_ref,
                       ksb_ref, ksa_ref, kwb_ref, kwa_ref, *, nq):
    i = pl.program_id(0)
    g = pl.program_id(1)
    c = pl.program_id(2)
    step = (i * KV_HEADS + g) * nq + c
    t0 = c * QB
    half = PIECE
    n_piece = ROWS // PIECE
    qs = _head_major(q_ref[...]) * LOG2E
    s2 = jnp.concatenate([jnp.broadcast_to(_head_slope(g, r) * LOG2E, (QB, 1)) for r in range(GROUP)], axis=0)
    s2_hi = s2.astype(BF16).astype(F32)
    s2_lo = s2 - s2_hi
    lane64 = lax.broadcasted_iota(jnp.int32, (ROWS, 64), 1)
    ali_q = jnp.where(lane64 == 0, -64.0 * s2_hi, jnp.where(lane64 == 1, -s2_hi,
                      jnp.where(lane64 == 2, -64.0 * s2_lo, jnp.where(lane64 == 3, -s2_lo, 0.0))))
    q_win = jnp.concatenate([qs, ali_q], axis=1).astype(BF16)
    first_blk = jnp.logical_and(lax.broadcasted_iota(jnp.int32, (QB, 128), 1) == 0, c > 0)
    un = jnp.where(first_blk, jnp.asarray(-UNSEL, BF16), un_ref[...])
    q_sel =jnp.concatenate([q_win, jnp.concatenate([un] * GROUP, axis=0)], axis=1)
    slope_row = _slope_row(g) * LOG2E

    @pl.when(c == 0)
    def _():
        def fill(j, carry):
            r0 = pl.multiple_of(j * KT, KT)
            row = lax.broadcasted_iota(jnp.int32, (KT, 128), 0)
            lane = lax.broadcasted_iota(jnp.int32, (KT, 128), 1)
            for src, dst_kv, dst_aug, tile in ((ks_ref, ksb_ref, ksa_ref, KT), (kw_ref, kwb_ref, kwa_ref, WT)):
                kv_b = src[pl.ds(r0, KT), :].astype(BF16)
                d = (QB - 1) - row % tile
                ali = jnp.where(lane // 4 == 16, jnp.where(lane % 2 == 0, d >> 6, d & 63), 0).astype(F32).astype(BF16)
                dst_kv[pl.ds(r0, KT), :] = kv_b
                dst_aug[pl.ds(r0, KT), 0:128] = jnp.where(lane < 64, kv_b, ali)
            ksa_ref[pl.ds(r0, KT), 128:256] = jnp.where((r0 + row) // SEL_BLOCK == lane, 1.0, 0.0).astype(BF16)
            return carry

        lax.fori_loop(0, ks_ref.shape[0] // KT, fill, 0)

    def update(carry, s, kv_b, offset):
        out = []
        for hh, ((m, l, acc), sh) in enumerate(zip(carry, s)):
            shift = slope_row[:, hh * half:(hh + 1) * half] * offset
            m_new = jnp.maximum(m, jnp.max(sh, axis=0, keepdims=True) - shift)
            alpha = jnp.exp2(m - m_new)
            p = jnp.exp2(sh - (m_new + shift))
            l = alpha * l + jnp.sum(p, axis=0, keepdims=True)
            acc = alpha * acc + lax.dot_general(kv_b, p.astype(BF16), _TN, preferred_element_type=F32)
            out.append((m_new, l, acc))
        return tuple(out)

    def init():
        return tuple((jnp.full((1, half), NEG, F32), jnp.zeros((1, half), F32), jnp.zeros((128, half), F32))
                     for _ in range(n_piece))

    def tile(carry, aug_ref, kvb_ref, q_side, k0, n_keys, keep):
        k_aug = aug_ref[pl.ds(k0, n_keys), :]
        s = []
        for hh in range(n_piece):
            sh = lax.dot_general(k_aug, q_side[hh * half:(hh + 1) * half], _NT, preferred_element_type=F32)
            if keep is not None:
                sh = jnp.where(jnp.concatenate([keep] * (half // QB), axis=1), sh, NEG)
            s.append(sh)
        return update(carry, s, kvb_ref[pl.ds(k0, n_keys), :], (t0 - k0).astype(F32))

    k_tail = pl.multiple_of((t0 // KT) * KT, KT)
    key_x = lax.broadcasted_iota(jnp.int32, (KT, QB), 0)
    q_x = lax.broadcasted_iota(jnp.int32, (KT, QB), 1)
    carry = tile(init(), ksa_ref, ksb_ref, q_sel, k_tail, KT, k_tail + key_x <= t0 + q_x)

    def sel_listed(n, carry):
        return tile(carry, ksa_ref, ksb_ref, q_sel, pl.multiple_of(tl_ref[step, n] * KT, KT), KT, None)

    n_listed = tl_ref[step, TL_COUNT]

    def sel_group(n, cr):
        for u in range(SEL_UNROLL):
            cr = sel_listed(SEL_UNROLL * n + u, cr)
        return cr

    carry = lax.fori_loop(0, n_listed // SEL_UNROLL, sel_group, carry)
    carry = lax.fori_loop(n_listed - n_listed % SEL_UNROLL, n_listed, sel_listed, carry)
    k_first = ksa_ref[0:SEL_BLOCK, 0:128]
    s_first = [jnp.where(c > 0, lax.dot_general(k_first, q_win[hh * half:(hh + 1) * half], _NT,
                                                 preferred_element_type=F32), NEG) for hh in range(n_piece)]
    carry = update(carry, s_first, ksb_ref[0:SEL_BLOCK, :], t0.astype(F32))
    o_sel = [acc * (1.0 / l) for (_, l, acc) in carry]

    w_keys = WINDOW + QB
    k0w = pl.multiple_of(jnp.maximum(t0 - WINDOW, 0), WT)
    k_aug = kwa_ref[pl.ds(k0w, w_keys), :]
    kv_w = kwb_ref[pl.ds(k0w, w_keys), :]
    dq = lax.broadcasted_iota(jnp.int32, (WT, QB), 1) - lax.broadcasted_iota(jnp.int32, (WT, QB), 0)
    offs = [t0 - (k0w + gi * WT) for gi in range(w_keys // WT)]
    keeps = [jnp.concatenate([jnp.logical_and(dq + off >= 0, dq + off < WINDOW)] * (half // QB), axis=1)
             for off in offs]
    o_win = []
    for hh in range(n_piece):
        sl = slope_row[:, hh * half:(hh + 1) * half]
        sh = lax.dot_general(k_aug, q_win[hh * half:(hh + 1) * half], _NT, preferred_element_type=F32)
        parts = [jnp.where(keep, sh[gi * WT:(gi + 1) * WT], NEG) for gi, keep in enumerate(keeps)]
        shifts = [sl * off.astype(F32) for off in offs]
        m_w = functools.reduce(jnp.maximum, [jnp.max(pt, axis=0, keepdims=True) - sf
                                             for pt, sf in zip(parts, shifts)])
        ps = [jnp.exp2(pt - (m_w + sf)) for pt, sf in zip(parts, shifts)]
        l_w = functools.reduce(jnp.add, [jnp.sum(p, axis=0, keepdims=True) for p in ps])
        acc = lax.dot_general(kv_w, jnp.concatenate(ps, axis=0).astype(BF16), _TN, preferred_element_type=F32)
        o_win.append(acc * (1.0 / l_w))

    gates = jax.nn.sigmoid(gl_ref[...])
    o_c = oc_ref[...]
    for r in range(GROUP):
        hh, cs = divmod(r * QB, half)
        o_s = jnp.transpose(o_sel[hh][:, cs:cs + QB])[:, 64:128]
        o_w = jnp.transpose(o_win[hh][:, cs:cs + QB])[:, 64:128]
        o_ref[:, r * 64:(r + 1) * 64] = (gates[:, 3 * r:3 * r + 1] * o_c[:, r * 64:(r + 1) * 64]
                                         + gates[:, 3 * r + 1:3 * r + 2] * o_s
                                         + gates[:, 3 * r + 2:3 * r + 3] * o_w)


def _nsa_attend(tiles, h, unsel, o_c, b, t):
    nq = t // QB
    grid_spec = pltpu.PrefetchScalarGridSpec(
        num_scalar_prefetch=1,
        grid=(b, KV_HEADS, nq),
        in_specs=[pl.BlockSpec((QB, 256), lambda i, g, c, tl: (i * nq + c, C_QA // 256 + g)),
                  pl.BlockSpec((QB, 128), lambda i, g, c, tl: (i * nq + c, g)),
                  pl.BlockSpec((QB, 128), lambda i, g, c, tl: (i * nq + c, C_GL // 128 + g)),
                  pl.BlockSpec((QB, 256), lambda i, g, c, tl: (i * nq + c, g)),
                  pl.BlockSpec((t, 128), lambda i, g, c, tl: (i, C_KVS // 128 + g)),
                  pl.BlockSpec((t, 128), lambda i, g, c, tl: (i, C_KVW // 128 + g))],
        out_specs=pl.BlockSpec((QB, 256), lambda i, g, c, tl: (i * nq + c, g)),
        scratch_shapes=[pltpu.VMEM((t, 128), BF16), pltpu.VMEM((t, 256), BF16),
                        pltpu.VMEM((t, 128), BF16), pltpu.VMEM((t, 128), BF16)],
    )
    return pl.pallas_call(
        functools.partial(_nsa_attend_kernel, nq=nq),
        grid_spec=grid_spec,
        out_shape=jax.ShapeDtypeStruct((b * t, Q_A), F32),
        compiler_params=_cparams(("arbitrary", "arbitrary", "arbitrary")),
        name="nsa_attend",
    )(tiles, h, unsel, h, o_c, h, h)


def _nsa_prompt(h, kcmp, b, t):
    o_c, unsel, tl = _nsa_select(h, kcmp, b, t)
    return _nsa_attend(tl[:, 0, :], h, unsel, o_c, b, t)


HB = 256


def _lower_bound(lb_ref):
    z = lb_ref[...]
    e = jnp.exp(z - jnp.max(z, axis=0, keepdims=True))
    return e[0:1] / jnp.sum(e, axis=0, keepdims=True)


def _hgrn_prompt_kernel(q_ref, f_ref, i_ref, g_ref, lb_ref, ng_ref, o_ref, s_out_ref, s_ref):
    tb = pl.program_id(1)

    @pl.when(tb == 0)
    def _():
        s_ref[...] = jnp.zeros_like(s_ref)

    n_c = HB // HGRN_CHUNK
    lb = _lower_bound(lb_ref)
    f = lb + (1.0 - lb) * jax.nn.sigmoid(f_ref[...])
    log_f = jnp.log(f)
    row = lax.broadcasted_iota(jnp.int32, (HB, HB), 0)
    col = lax.broadcasted_iota(jnp.int32, (HB, HB), 1)
    tril = jnp.logical_and(col <= row, row // HGRN_CHUNK == col // HGRN_CHUNK)
    lc = _dot01(tril.astype(BF16), log_f)
    lc3 = lc.reshape(n_c, HGRN_CHUNK, Q_B)
    dec8 = jnp.exp(lc3[:, HGRN_CHUNK - 1, :])
    dec = jnp.broadcast_to(dec8[:, None, :], lc3.shape).reshape(HB, Q_B)
    q_t = (q_ref[...] * jnp.exp(lc)).astype(BF16)
    k_f = (1.0 - f) * jnp.exp(-lc)
    k_t = k_f.astype(BF16)
    k_e = (k_f * dec).astype(BF16)
    v_all = i_ref[...]
    gate = g_ref[...]
    ng = ng_ref[...]
    for hh in range(B_HEADS):
        ks = slice(hh * B_DK, (hh + 1) * B_DK)
        vs = slice(hh * B_DV, (hh + 1) * B_DV)
        v = v_all[:, vs].astype(BF16)
        a = lax.dot_general(q_t[:, ks], k_t[:, ks], _NT, preferred_element_type=F32)
        a = jnp.where(tril, a, 0.0).astype(BF16)
        o = jnp.dot(a, v, preferred_element_type=F32)
        st = s_ref[hh]
        inter = []
        for cc in range(n_c):
            rs = slice(cc * HGRN_CHUNK, (cc + 1) * HGRN_CHUNK)
            inter.append(lax.dot_general(q_t[rs, ks], st.astype(BF16), _NT, preferred_element_type=F32))
            u = lax.dot_general(v[rs], k_e[rs, ks], _TN, preferred_element_type=F32)
            st = dec8[cc:cc + 1, ks] * st + u
        s_ref[hh] = st
        o = o + jnp.concatenate(inter, axis=0)
        o = o * lax.rsqrt(jnp.mean(o * o, axis=-1, keepdims=True) + LN_EPS) * ng
        gt = gate[:, vs]
        o_ref[:, vs] = o * (gt * jax.nn.sigmoid(gt))

    @pl.when(tb == pl.num_programs(1) - 1)
    def _():
        s_out_ref[0] = s_ref[...]


def _hgrn_prompt(h, b, t, lb_logits, norm_g):
    nt = t // HB
    return pl.pallas_call(
        _hgrn_prompt_kernel,
        grid=(b, nt),
        in_specs=[pl.BlockSpec((HB, Q_B), lambda i, j: (i * nt + j, C_QB // Q_B)),
                  pl.BlockSpec((HB, Q_B), lambda i, j: (i * nt + j, C_FB // Q_B)),
                  pl.BlockSpec((HB, I_B), lambda i, j: (i * nt + j, C_IB // I_B)),
                  pl.BlockSpec((HB, I_B), lambda i, j: (i * nt + j, C_GB // I_B)),
                  pl.BlockSpec((2, Q_B), lambda i, j: (0, 0)),
                  pl.BlockSpec((1, B_DV), lambda i, j: (0, 0))],
        out_specs=[pl.BlockSpec((HB, I_B), lambda i, j: (i * nt + j, 0)),
                   pl.BlockSpec((1, B_HEADS, B_DV, B_DK), lambda i, j: (i, 0, 0, 0))],
        out_shape=[jax.ShapeDtypeStruct((b * t, I_B), F32),
                   jax.ShapeDtypeStruct((b, B_HEADS, B_DV, B_DK), F32)],
        scratch_shapes=[pltpu.VMEM((B_HEADS, B_DV, B_DK), F32)],
        compiler_params=_cparams(("arbitrary", "arbitrary")),
        name="hgrn_prompt",
    )(h, h, h, h, lb_logits, norm_g)


def _prompt_layer(x, w_pad, w_kvt, cmpw, lb_logits, norm_g, w_a, w_b, w_o, ln1_g, ln1_b, w_up, w_down,
                  ln2_g, ln2_b):
    b, t, _ = x.shape
    x2d = x.reshape(b * t, D_MODEL)
    h, *kv_t = _project_prompt(x2d, w_pad, w_kvt, b, t)
    kcmp = _cmp_prompt(h, b, t, *cmpw)
    o_a = _nsa_prompt(h, kcmp, b, t)
    o_b, s_end = _hgrn_prompt(h, b, t, lb_logits, norm_g)
    x1 = _merge(x2d, o_a, o_b, h, w_a, w_b, w_o, ln1_g, ln1_b)
    y = _mlp(x1, w_up, w_down, ln2_g, ln2_b)
    kv_out = [jnp.moveaxis(a.reshape(b, 2, KV_HEADS, HEAD_DIM, t), -1, 1) for a in kv_t]
    return y.reshape(b, t, D_MODEL), kv_out, jnp.swapaxes(s_end, 2, 3)


def _kv_out(h, col, b, t):
    kv = h[:, col:col + 256].reshape(b, t, KV_HEADS, 2, HEAD_DIM)
    return jnp.swapaxes(kv, 2, 3)


LAND_PAGES = 4
CH_PER_PAGE = PAGE_SIZE // CMP_STRIDE
BLK_PER_PAGE = PAGE_SIZE // SEL_BLOCK
N_POOL_SEL = N_SELECT - 1


def _head_rows(q_row, g):
    rows = [q_row[:, g * 256 + r * 64:g * 256 + (r + 1) * 64] for r in range(GROUP)]
    return jnp.concatenate(rows + [jnp.zeros((8 - GROUP, 64), F32)], axis=0) * ATTN_SCALE


def _slope_col(g):
    row = lax.broadcasted_iota(jnp.int32, (8, 1), 0)
    col = jnp.zeros((8, 1), F32)
    for r in range(GROUP):
        col = jnp.where(row == r, 2.0 ** -(g * GROUP + r + 1), col)
    return col


def _cmp_sample_kernel(pt_ref, cache_ref, w_ref, posf_ref, wb_ref, q_ref, oc_ref, idx_ref,
                       buf_ref, xk_ref, xv_ref, bias_ref, sem, *, n_pages, past):
    b = pl.program_id(0)
    n_ch = n_pages * CH_PER_PAGE
    n_blk = past // SEL_BLOCK
    lanes = -(-n_blk // 128) * 128

    def page_copy(pg, page):
        return pltpu.make_async_copy(cache_ref.at[page], buf_ref.at[pg], sem.at[pg])

    def issue(pg, carry):
        page_copy(pg, pt_ref[b, pg]).start()
        return carry

    def land(i, carry):
        pages = [i * LAND_PAGES + u for u in range(LAND_PAGES)]
        for pg in pages:
            page_copy(pg, 0).wait()
        for pg in pages:
            r0 = pl.multiple_of(pg * PAGE_SIZE, PAGE_SIZE)
            xk_ref[pl.ds(r0, PAGE_SIZE), :] = jnp.transpose(buf_ref[pg, 0].reshape(128, PAGE_SIZE))
            xv_ref[pl.ds(r0, PAGE_SIZE), :] = jnp.transpose(buf_ref[pg, 1].reshape(128, PAGE_SIZE))
        return carry

    lax.fori_loop(0, n_pages, issue, 0)
    lax.fori_loop(0, n_pages // LAND_PAGES, land, 0)

    @pl.when(b == 0)
    def _():
        bias_ref[...] = jnp.dot(posf_ref[...].astype(BF16), wb_ref[...], preferred_element_type=F32)

    bias = bias_ref[0:1]
    blocks = []
    for c, x_ref in enumerate((xk_ref, xv_ref)):
        acc = jnp.zeros((n_ch, 256), F32)
        for l2 in range(CMP_STRIDE // 2):
            xl = jnp.concatenate([x_ref[pl.ds(2 * l2 + u, n_ch, stride=CMP_STRIDE), :] for u in range(2)],
                                 axis=1).astype(BF16)
            acc += jnp.dot(xl, w_ref[c, l2], preferred_element_type=F32)
        bias_c = jnp.concatenate([bias[:, c * 64:(c + 1) * 64]] * KV_HEADS, axis=1)
        blocks.append((acc[:, :128] + pltpu.roll(acc[:, 128:], n_ch - 1, 0) + bias_c).astype(BF16))
    kc_all, vc_all = blocks

    q_row = q_ref[0]
    n_ix = lax.broadcasted_iota(jnp.int32, (1, n_ch), 1)
    dist_i = past - (n_ix * CMP_STRIDE + (CMP_LEN - 1))
    valid = jnp.logical_and(dist_i >= 0, n_ix < n_ch - 1)
    dist = dist_i.astype(F32)
    ratio = SEL_BLOCK // CMP_STRIDE
    pool = (lax.broadcasted_iota(jnp.int32, (n_ch, lanes), 0) // ratio
            == lax.broadcasted_iota(jnp.int32, (n_ch, lanes), 1)).astype(BF16)
    row8 = lax.broadcasted_iota(jnp.int32, (8, lanes), 0)
    lane_g = lax.broadcasted_iota(jnp.int32, (8, 128), 1) // HEAD_DIM
    zeros4 = jnp.zeros((GROUP, 64), F32)
    q8 = jnp.concatenate([jnp.concatenate([_head_rows(q_row, 0)[0:GROUP], zeros4], axis=1),
                          jnp.concatenate([zeros4, _head_rows(q_row, 1)[0:GROUP]], axis=1)], axis=0).astype(BF16)
    s = lax.dot_general(q8, kc_all, _NT, preferred_element_type=F32)
    head = lax.broadcasted_iota(jnp.int32, (8, 1), 0)
    slope8 = jnp.zeros((8, 1), F32)
    for h in range(A_HEADS):
        slope8 = jnp.where(head == h, 2.0 ** -(h + 1), slope8)
    s = jnp.where(valid, s - slope8 * dist, NEG)
    m = jnp.max(s, axis=-1, keepdims=True)
    p = jnp.where(valid, jnp.exp(s - m), 0.0)
    p = p / jnp.maximum(jnp.sum(p, axis=-1, keepdims=True), 1e-30)
    o8 = jnp.dot(p.astype(BF16), vc_all, preferred_element_type=F32)
    o_c = jnp.where(lane_g == 0, o8, pltpu.roll(o8, GROUP, 0))
    imp8 = jnp.concatenate([p[0:1] + p[1:2] + p[2:3] + p[3:4], p[4:5] + p[5:6] + p[6:7] + p[7:8],
                            jnp.zeros((8 - KV_HEADS, n_ch), F32)], axis=0)
    score = jnp.where(row8 < KV_HEADS, _x_dot01(imp8, pool), -FORCE)
    lane = lax.broadcasted_iota(jnp.int32, (8, lanes), 1)
    forced = jnp.logical_or(lane == 0, lane == n_blk - 1)
    score = jnp.where(lane < n_blk, jnp.where(forced, FORCE, score), -jnp.inf)
    sc_t = jnp.transpose(jnp.concatenate([score, jnp.full((120, lanes), -jnp.inf, F32)], axis=0))
    blk = lax.broadcasted_iota(jnp.int32, (lanes, 128), 0)
    picks = []
    for r in range(N_POOL_SEL):
        m = jnp.max(sc_t, axis=0, keepdims=True)
        first = jnp.min(jnp.where(sc_t == m, blk, lanes), axis=0, keepdims=True)
        picks.append(first)
        sc_t = jnp.where(blk == first, -jnp.inf, sc_t)
    idx_ref[0] = jnp.concatenate(picks + [jnp.zeros((N_SELECT - N_POOL_SEL, 128), jnp.int32)], axis=0)
    oc_ref[0] = o_c


def _cmp_sample(page_table, cache5, w_s, posf, wb, q3, past):
    bsz, n_pages = page_table.shape
    grid_spec = pltpu.PrefetchScalarGridSpec(
        num_scalar_prefetch=1,
        grid=(bsz,),
        in_specs=[pl.BlockSpec(memory_space=pl.ANY),
                  pl.BlockSpec((2, CMP_STRIDE // 2, 256, 256), lambda i, pt: (0, 0, 0, 0)),
                  pl.BlockSpec((8, 4096), lambda i, pt: (0, 0)),
                  pl.BlockSpec((4096, 128), lambda i, pt: (0, 0)),
                  pl.BlockSpec((1, 1, Q_A), lambda i, pt: (i, 0, 0))],
        out_specs=[pl.BlockSpec((1, 8, 128), lambda i, pt: (i, 0, 0)),
                   pl.BlockSpec((1, N_SELECT, 128), lambda i, pt: (i, 0, 0))],
        scratch_shapes=[pltpu.VMEM((n_pages, 2, KV_HEADS, HEAD_DIM, PAGE_SIZE), F32),
                        pltpu.VMEM((n_pages * PAGE_SIZE, 128), F32),
                        pltpu.VMEM((n_pages * PAGE_SIZE, 128), F32),
                        pltpu.VMEM((8, 128), F32),
                        pltpu.SemaphoreType.DMA((n_pages,))],
    )
    return pl.pallas_call(
        functools.partial(_cmp_sample_kernel, n_pages=n_pages, past=past),
        grid_spec=grid_spec,
        out_shape=[jax.ShapeDtypeStruct((bsz, 8, 128), F32),
                   jax.ShapeDtypeStruct((bsz, N_SELECT, 128), jnp.int32)],
        compiler_params=_cparams(("arbitrary",)),
        name="cmp_sample",
    )(page_table, cache5, w_s, posf, wb, q3)


def _pick_lane(mat, lane, target):
    return jnp.sum(jnp.where(lane == target, mat, 0.0), axis=-1, keepdims=True)


def _sel_win_sample_kernel(pt_ref, idx_ref, cache_ref, win_ref, q_ref, ks_ref, kw_ref, gl_ref, oc_ref, o_ref,
                           kbuf_ref, vbuf_ref, sem, *, past):
    b = pl.program_id(0)
    slot_lanes = N_SELECT * PAGE_SIZE

    def page_copies(g, k, page):
        dst = pl.ds(k * PAGE_SIZE, PAGE_SIZE)
        return (pltpu.make_async_copy(cache_ref.at[page, 0, g], kbuf_ref.at[g, :, dst], sem),
                pltpu.make_async_copy(cache_ref.at[page, 1, g], vbuf_ref.at[g, :, dst], sem))

    for g in range(KV_HEADS):
        for k in range(N_POOL_SEL):
            page = pt_ref[b, idx_ref[b, g * N_SELECT + k] // BLK_PER_PAGE]
            for cp in page_copies(g, k, page):
                cp.start()
        pad = pl.ds(N_POOL_SEL * PAGE_SIZE, PAGE_SIZE)
        kbuf_ref[g, :, pad] = jnp.zeros((HEAD_DIM, PAGE_SIZE), F32)
        vbuf_ref[g, :, pad] = jnp.zeros((HEAD_DIM, PAGE_SIZE), F32)

    q_row = q_ref[0]
    ks_new = ks_ref[0]
    kw_new = kw_ref[0]
    gl_all = jax.nn.sigmoid(gl_ref[0])
    lane128 = lax.broadcasted_iota(jnp.int32, (8, 128), 1)
    row8 = lax.broadcasted_iota(jnp.int32, (8, 1), 0)
    w_len = win_ref.shape[-1]

    def two_piece(qh, slope, k_t, v_t, dist_i, valid, k_new, v_new):
        s = jnp.dot(qh.astype(BF16), k_t.astype(BF16), preferred_element_type=F32)
        s = jnp.where(valid, s - slope * dist_i.astype(F32), NEG)
        s_n = jnp.sum(qh * k_new, axis=-1, keepdims=True)
        m = jnp.maximum(jnp.max(s, axis=-1, keepdims=True), s_n)
        p = jnp.where(valid, jnp.exp(s - m), 0.0)
        p_n = jnp.exp(s_n - m)
        l = jnp.sum(p, axis=-1, keepdims=True) + p_n
        o = lax.dot_general(p.astype(BF16), v_t.astype(BF16), _NT, preferred_element_type=F32) + p_n * v_new
        return o / l

    outs = []
    for g in range(KV_HEADS):
        qh = _head_rows(q_row, g)
        slope = _slope_col(g)
        j_ix = lax.broadcasted_iota(jnp.int32, (1, w_len), 1)
        dist_w = w_len - j_ix
        o_w = two_piece(qh, slope, win_ref[0, 0, g], win_ref[0, 1, g], dist_w, dist_w < WINDOW,
                        kw_new[:, g * 128:g * 128 + 64], kw_new[:, g * 128 + 64:(g + 1) * 128])
        outs.append((qh, slope, o_w))

    for g in range(KV_HEADS):
        for k in range(N_POOL_SEL):
            for cp in page_copies(g, k, 0):
                cp.wait()

    lane_s = lax.broadcasted_iota(jnp.int32, (1, slot_lanes), 1)
    o_all = []
    for g in range(KV_HEADS):
        qh, slope, o_w = outs[g]
        pos_k = jnp.full((1, slot_lanes), past + 1, jnp.int32)
        for k in range(N_POOL_SEL):
            blk = idx_ref[b, g * N_SELECT + k]
            r = lane_s % PAGE_SIZE
            in_blk = jnp.logical_and(lane_s // PAGE_SIZE == k, r // SEL_BLOCK == blk % BLK_PER_PAGE)
            pos_k = jnp.where(in_blk, (blk // BLK_PER_PAGE) * PAGE_SIZE + r, pos_k)
        dist_s = past - pos_k
        o_s = two_piece(qh, slope, kbuf_ref[g], vbuf_ref[g], dist_s, dist_s >= 0,
                        ks_new[:, g * 128:g * 128 + 64], ks_new[:, g * 128 + 64:(g + 1) * 128])
        o_c = oc_ref[0][:, g * 64:(g + 1) * 64]
        gates = jnp.broadcast_to(gl_all[:, g * 128:(g + 1) * 128], (8, 128))
        g_c = _pick_lane(gates, lane128, 3 * row8)
        g_s = _pick_lane(gates, lane128, 3 * row8 + 1)
        g_w = _pick_lane(gates, lane128, 3 * row8 + 2)
        o_all.append(g_c * o_c + g_s * o_s + g_w * o_w)
    o_ref[0] = jnp.concatenate(o_all, axis=1)


def _sel_win_sample(page_table, idx, cache_sel5, cache_win5, q3, ks3, kw3, gl3, o_c, past):
    bsz = page_table.shape[0]
    w_len = cache_win5.shape[-1]
    grid_spec = pltpu.PrefetchScalarGridSpec(
        num_scalar_prefetch=2,
        grid=(bsz,),
        in_specs=[pl.BlockSpec(memory_space=pl.ANY),
                  pl.BlockSpec((1, 2, KV_HEADS, HEAD_DIM, w_len), lambda i, pt, ix: (i, 0, 0, 0, 0)),
                  pl.BlockSpec((1, 1, Q_A), lambda i, pt, ix: (i, 0, 0)),
                  pl.BlockSpec((1, 1, 256), lambda i, pt, ix: (i, 0, 0)),
                  pl.BlockSpec((1, 1, 256), lambda i, pt, ix: (i, 0, 0)),
                  pl.BlockSpec((1, 1, 256), lambda i, pt, ix: (i, 0, 0)),
                  pl.BlockSpec((1, 8, 128), lambda i, pt, ix: (i, 0, 0))],
        out_specs=pl.BlockSpec((1, 8, 128), lambda i, pt, ix: (i, 0, 0)),
        scratch_shapes=[pltpu.VMEM((KV_HEADS, HEAD_DIM, N_SELECT * PAGE_SIZE), F32),
                        pltpu.VMEM((KV_HEADS, HEAD_DIM, N_SELECT * PAGE_SIZE), F32),
                        pltpu.SemaphoreType.DMA(())],
    )
    return pl.pallas_call(
        functools.partial(_sel_win_sample_kernel, past=past),
        grid_spec=grid_spec,
        out_shape=jax.ShapeDtypeStruct((bsz, 8, 128), F32),
        compiler_params=_cparams(("arbitrary",)),
        name="sel_win_sample",
    )(page_table, idx, cache_sel5, cache_win5, q3, ks3, kw3, gl3, o_c)


def _hgrn_sample_kernel(q_ref, f_ref, v_ref, g_ref, lb_ref, ng_ref, s_ref, o_ref, s_out_ref):
    z = lb_ref[...]
    e = jnp.exp(z - jnp.max(z, axis=0, keepdims=True))
    lb = e[0] / jnp.sum(e, axis=0)
    f = lb + (1.0 - lb) * jax.nn.sigmoid(f_ref[0])
    decay = jnp.exp(jnp.log(f))
    k = 1.0 - f
    q = q_ref[0]
    v = v_ref[0]
    gate = g_ref[0]
    ng = ng_ref[...]
    for hh in range(B_HEADS):
        hs = slice(hh, hh + 1)
        s_new = decay[hs] * s_ref[0, hh] + v[:, hs] * k[hs]
        s_out_ref[0, hh] = s_new
        o = jnp.sum(q[hs] * s_new, axis=1, keepdims=True)
        o = o * lax.rsqrt(jnp.mean(o * o, axis=0, keepdims=True) + LN_EPS) * ng
        gt = gate[:, hs]
        o_ref[0, :, hs] = o * (gt * jax.nn.sigmoid(gt))


def _hgrn_sample(q_hk, f_hk, v_vh, g_vh, lb_hk, ng_col, state_t):
    bsz = q_hk.shape[0]
    return pl.pallas_call(
        _hgrn_sample_kernel,
        grid=(bsz,),
        in_specs=[pl.BlockSpec((1, B_HEADS, B_DK), lambda i: (i, 0, 0)),
                  pl.BlockSpec((1, B_HEADS, B_DK), lambda i: (i, 0, 0)),
                  pl.BlockSpec((1, B_DV, B_HEADS), lambda i: (i, 0, 0)),
                  pl.BlockSpec((1, B_DV, B_HEADS), lambda i: (i, 0, 0)),
                  pl.BlockSpec((2, B_HEADS, B_DK), lambda i: (0, 0, 0)),
                  pl.BlockSpec((B_DV, 1), lambda i: (0, 0)),
                  pl.BlockSpec((1, B_HEADS, B_DV, B_DK), lambda i: (i, 0, 0, 0))],
        out_specs=[pl.BlockSpec((1, B_DV, B_HEADS), lambda i: (i, 0, 0)),
                   pl.BlockSpec((1, B_HEADS, B_DV, B_DK), lambda i: (i, 0, 0, 0))],
        out_shape=[jax.ShapeDtypeStruct((bsz, B_DV, B_HEADS), F32),
                   jax.ShapeDtypeStruct((bsz, B_HEADS, B_DV, B_DK), F32)],
        compiler_params=_cparams(("arbitrary",)),
        name="hgrn_sample",
    )(q_hk, f_hk, v_vh, g_vh, lb_hk, ng_col, state_t)


def _rows_last(cache):
    return jnp.moveaxis(cache, -4, -1)


def _sample_layer(x, cache_cmp, cache_sel, cache_win, state, page_table, w_pad, cmpw, w_s, lb_logits, norm_g,
                  w_a, w_b, w_o, ln1_g, ln1_b, w_up, w_down, ln2_g, ln2_b):
    bsz, t, _ = x.shape
    assert t == 1, "the sample group decodes one token per request"
    n_pages = page_table.shape[1]
    past = n_pages * PAGE_SIZE
    x2d = x.reshape(bsz, D_MODEL)
    h = _project(x2d, w_pad)
    h3 = h.reshape(bsz, 1, D_PAD)
    q3 = h3[:, :, C_QA:C_QA + Q_A]
    wl, wb, posf = cmpw
    o_c, idx = _cmp_sample(page_table, _rows_last(cache_cmp), w_s, posf, wb, q3, past)
    idx2 = jnp.swapaxes(idx[:, :, :KV_HEADS], 1, 2).reshape(bsz, KV_HEADS * N_SELECT)
    o_rd = _sel_win_sample(page_table, idx2, _rows_last(cache_sel), _rows_last(cache_win),
                           q3, h3[:, :, C_KVS:C_KVS + 256], h3[:, :, C_KVW:C_KVW + 256],
                           h3[:, :, C_GL:C_GL + 256], o_c, past)
    o_a = jnp.swapaxes(o_rd[:, :GROUP].reshape(bsz, GROUP, KV_HEADS, HEAD_DIM), 1, 2).reshape(bsz, Q_A)
    to_hk = lambda a: a.reshape(-1, B_HEADS, B_DK)
    to_vh = lambda a: jnp.swapaxes(a.reshape(-1, B_HEADS, B_DV), 1, 2)
    o_vh, s_t = _hgrn_sample(to_hk(h[:, C_QB:C_QB + Q_B]), to_hk(h[:, C_FB:C_FB + Q_B]),
                             to_vh(h[:, C_IB:C_IB + I_B]), to_vh(h[:, C_GB:C_GB + I_B]),
                             to_hk(lb_logits), norm_g.reshape(B_DV, 1), jnp.swapaxes(state, 2, 3))
    o_b = jnp.swapaxes(o_vh, 1, 2).reshape(bsz, I_B)
    x1 = _merge(x2d, o_a, o_b, h, w_a, w_b, w_o, ln1_g, ln1_b)
    y = _mlp(x1, w_up, w_down, ln2_g, ln2_b)
    return y.reshape(bsz, 1, D_MODEL), h, jnp.swapaxes(s_t, 2, 3)


def _cmp_sample_weights(cmp_w):
    w = cmp_w.reshape(2, 2, CMP_STRIDE, HEAD_DIM, HEAD_DIM)
    base = jnp.transpose(w, (0, 2, 3, 1, 4))
    ws = jnp.stack([_at_slot(base, g, 4) for g in range(2)], axis=2)
    return ws.reshape(2, CMP_STRIDE // 2, 256, 256).astype(BF16)


def kernel(x_prompt, x_sample, cache_cmp_kv, cache_sel_kv, cache_win_kv, state_hgrn, page_table,
           w_in, cmp_w, cmp_pos, hgrn_lb_logits, hgrn_norm_g, w_br_a, w_br_b, w_out,
           ln1_g, ln1_b, w_up, w_down, ln2_g, ln2_b):
    assert w_in.shape[0] == 1, "one layer"
    b, t, _ = x_prompt.shape
    bsz = x_sample.shape[0]
    assert t % HB == 0 and t % QB == 0 and WINDOW + QB <= t <= 128 * SEL_BLOCK
    w_pad = _reorder_w_in(w_in[0])
    cmpw = _cmp_weights(cmp_w[0], cmp_pos[0])
    w_s = _cmp_sample_weights(cmp_w[0])
    dense = (w_br_a[0].astype(BF16), w_br_b[0].astype(BF16), w_out[0].astype(BF16), ln1_g, ln1_b,
             w_up[0].astype(BF16), w_down[0].astype(BF16), ln2_g, ln2_b)

    o_kv = Q_A
    w_kvt = jnp.transpose(w_in[0][:, o_kv:o_kv + 3 * KV_A]).astype(BF16)
    y_p, (kvc_p, kvs_p, kvw_p), s_p = _prompt_layer(x_prompt, w_pad, w_kvt, cmpw, hgrn_lb_logits, hgrn_norm_g,
                                                   *dense)
    y_s, h_s, s_s = _sample_layer(x_sample, cache_cmp_kv[0], cache_sel_kv[0], cache_win_kv[0], state_hgrn[0],
                                  page_table, w_pad, cmpw, w_s, hgrn_lb_logits, hgrn_norm_g, *dense)

    win_p = min(WINDOW, t)
    kvw_s = _kv_out(h_s, C_KVW, bsz, 1)
    new_win_s = jnp.concatenate([cache_win_kv[0], kvw_s], axis=1)[:, -min(WINDOW, cache_win_kv.shape[2] + 1):]
    return (y_p, y_s,
            kvc_p[None], kvs_p[None], kvw_p[:, -win_p:][None], s_p[None],
            _kv_out(h_s, C_KVC, bsz, 1)[None], _kv_out(h_s, C_KVS, bsz, 1)[None], new_win_s[None], s_s[None])
```

```python
import functools

import numpy as np
import jax
import jax.numpy as jnp
from jax import lax
from jax.experimental import pallas as pl
from jax.experimental.pallas import tpu as pltpu

F32 = jnp.float32
BF16 = jnp.bfloat16

D_MODEL = 1024
HEAD_DIM = 64
A_HEADS = 8
KV_HEADS = 2
GROUP = A_HEADS // KV_HEADS
CMP_STRIDE = 16
CMP_LEN = 32
SEL_BLOCK = 64
N_SELECT = 16
WINDOW = 512
PAGE_SIZE = 128
B_HEADS = 8
B_DK = 128
B_DV = 64
HGRN_CHUNK = 32
D_FF = 4 * D_MODEL
DEEPNORM_ALPHA = 2.0 ** 0.25
LN_EPS = 1e-5
NEG = -1e30
FORCE = 1e6
ATTN_SCALE = HEAD_DIM ** -0.5
LOG2E = 1.4426950408889634
Q_A = A_HEADS * HEAD_DIM
KV_A = 2 * KV_HEADS * HEAD_DIM
GATE_A = 3 * A_HEADS
Q_B = B_HEADS * B_DK
I_B = B_HEADS * B_DV

C_MG = 0
C_QB = 2048
C_FB = 3072
C_QA = 4096
C_IB = 4608
C_GB = 5120
C_KVC = 5632
C_KVS = 5888
C_KVW = 6144
C_GL = 6400
D_PAD = 6656
PROJ_TN = 1664

VMEM_LIMIT = 56 * 1024 * 1024
UNSEL = float(2.0 ** 100)


def _cparams(sem):
    return pltpu.CompilerParams(dimension_semantics=sem, vmem_limit_bytes=VMEM_LIMIT)


def _proj_perm():
    perm = np.full((D_PAD,), -1, np.int64)
    o_qa, o_kvc, o_kvs, o_kvw = 0, Q_A, Q_A + KV_A, Q_A + 2 * KV_A
    o_gl = Q_A + 3 * KV_A
    o_qb = o_gl + GATE_A
    o_fb = o_qb + Q_B
    o_ib = o_fb + Q_B
    o_gb = o_ib + I_B
    o_mg = o_gb + I_B
    perm[C_MG:C_MG + 2 * D_MODEL] = o_mg + np.arange(2 * D_MODEL)
    perm[C_QB:C_QB + Q_B] = o_qb + np.arange(Q_B)
    perm[C_FB:C_FB + Q_B] = o_fb + np.arange(Q_B)
    perm[C_QA:C_QA + Q_A] = o_qa + np.arange(Q_A)
    perm[C_IB:C_IB + I_B] = o_ib + np.arange(I_B)
    perm[C_GB:C_GB + I_B] = o_gb + np.arange(I_B)
    for new, old in ((C_KVC, o_kvc), (C_KVS, o_kvs), (C_KVW, o_kvw)):
        for g in range(KV_HEADS):
            for c in range(2):
                dst = new + g * 128 + c * 64
                src = old + c * 128 + g * 64
                perm[dst:dst + 64] = src + np.arange(64)
    for g in range(KV_HEADS):
        perm[C_GL + g * 128:C_GL + g * 128 + 12] = o_gl + g * 12 + np.arange(12)
    return perm


def _perm_runs(perm):
    runs = []
    for col in perm:
        if runs and ((col < 0 and runs[-1][0] < 0) or (col >= 0 and runs[-1][0] >= 0
                                                      and runs[-1][0] + runs[-1][1] == col)):
            runs[-1][1] += 1
        else:
            runs.append([int(col), 1])
    return [tuple(r) for r in runs]


_PERM_RUNS = _perm_runs(_proj_perm())


def _reorder_w_in(w):
    pieces = [jnp.zeros((w.shape[0], n), w.dtype) if s < 0 else w[:, s:s + n] for s, n in _PERM_RUNS]
    return jnp.concatenate(pieces, axis=1).astype(BF16)


def _proj_kernel(x_ref, w_ref, o_ref):
    xb = x_ref[...].astype(BF16)
    for j in range(D_PAD // PROJ_TN):
        cols = slice(j * PROJ_TN, (j + 1) * PROJ_TN)
        o_ref[:, cols] = jnp.dot(xb, w_ref[:, cols], preferred_element_type=F32)


def _proj_kvt_kernel(x_ref, w_ref, wt_ref, o_ref, kc_ref, ks_ref, kw_ref):
    xb = x_ref[...].astype(BF16)
    for j in range(D_PAD // PROJ_TN):
        cols = slice(j * PROJ_TN, (j + 1) * PROJ_TN)
        o_ref[:, cols] = jnp.dot(xb, w_ref[:, cols], preferred_element_type=F32)
    for j, kt_ref in enumerate((kc_ref, ks_ref, kw_ref)):
        kt_ref[0] = lax.dot_general(wt_ref[j * KV_A:(j + 1) * KV_A, :], xb, _NT, preferred_element_type=F32)


def _project_prompt(x2d, w_pad, w_kvt, b, t):
    tm = 512
    nt = t // tm
    kvt = jax.ShapeDtypeStruct((b, KV_A, t), F32)
    kvt_spec = pl.BlockSpec((1, KV_A, tm), lambda i: (i // nt, 0, i % nt))
    return pl.pallas_call(
        _proj_kvt_kernel,
        grid=(b * nt,),
        in_specs=[pl.BlockSpec((tm, D_MODEL), lambda i: (i, 0)),
                  pl.BlockSpec((D_MODEL, D_PAD), lambda i: (0, 0), pipeline_mode=pl.Buffered(1)),
                  pl.BlockSpec((3 * KV_A, D_MODEL), lambda i: (0, 0), pipeline_mode=pl.Buffered(1))],
        out_specs=[pl.BlockSpec((tm, D_PAD), lambda i: (i, 0)), kvt_spec, kvt_spec, kvt_spec],
        out_shape=[jax.ShapeDtypeStruct((b * t, D_PAD), F32), kvt, kvt, kvt],
        compiler_params=_cparams(("arbitrary",)),
        name="proj_prompt",
    )(x2d, w_pad, w_kvt)


def _project(x2d, w_pad):
    n = x2d.shape[0]
    tm = min(512, n)
    return pl.pallas_call(
        _proj_kernel,
        grid=(n // tm,),
        in_specs=[pl.BlockSpec((tm, D_MODEL), lambda i: (i, 0)),
                  pl.BlockSpec((D_MODEL, D_PAD), lambda i: (0, 0), pipeline_mode=pl.Buffered(1))],
        out_specs=pl.BlockSpec((tm, D_PAD), lambda i: (i, 0)),
        out_shape=jax.ShapeDtypeStruct((n, D_PAD), F32),
        compiler_params=_cparams(("arbitrary",)),
        name="proj",
    )(x2d, w_pad)


def _layer_norm(v, g, b):
    mu = jnp.mean(v, axis=-1, keepdims=True)
    d = v - mu
    var = jnp.mean(d * d, axis=-1, keepdims=True)
    return d * lax.rsqrt(var + LN_EPS) * g + b


FF_CHUNK = 1024


def _merge_mlp_kernel(x_ref, oa_ref, ob_ref, mga_ref, mgb_ref, wa_ref, wb_ref, wo_ref, g1_ref, b1_ref,
                      wu_ref, wd_ref, g2_ref, b2_ref, o_ref):
    br_a = jnp.dot(oa_ref[...].astype(BF16), wa_ref[...], preferred_element_type=F32)
    br_b = jnp.dot(ob_ref[...].astype(BF16), wb_ref[...], preferred_element_type=F32)
    merged = jax.nn.sigmoid(mga_ref[...]) * br_a + jax.nn.sigmoid(mgb_ref[...]) * br_b
    mix = jnp.dot(merged.astype(BF16), wo_ref[...], preferred_element_type=F32)
    x1 = _layer_norm(DEEPNORM_ALPHA * x_ref[...] + mix, g1_ref[...], b1_ref[...])
    x1b = x1.astype(BF16)
    acc = jnp.zeros(x1.shape, F32)
    for j in range(D_FF // FF_CHUNK):
        cols = slice(j * FF_CHUNK, (j + 1) * FF_CHUNK)
        u = jnp.maximum(jnp.dot(x1b, wu_ref[:, cols], preferred_element_type=F32), 0.0)
        acc += jnp.dot((u * u).astype(BF16), wd_ref[cols, :], preferred_element_type=F32)
    o_ref[...] = _layer_norm(DEEPNORM_ALPHA * x1 + acc, g2_ref[...], b2_ref[...])


def _merge_mlp(x2d, o_a, o_b, h, w_a, w_b, w_o, ln1_g, ln1_b, w_up, w_down, ln2_g, ln2_b):
    n = x2d.shape[0]
    tm = min(512, n)
    resident = lambda shape: pl.BlockSpec(shape, lambda i: (0, 0), pipeline_mode=pl.Buffered(1))
    return pl.pallas_call(
        _merge_mlp_kernel,
        grid=(n // tm,),
        in_specs=[pl.BlockSpec((tm, D_MODEL), lambda i: (i, 0)),
                  pl.BlockSpec((tm, Q_A), lambda i: (i, 0)),
                  pl.BlockSpec((tm, I_B), lambda i: (i, 0)),
                  pl.BlockSpec((tm, D_MODEL), lambda i: (i, C_MG // D_MODEL)),
                  pl.BlockSpec((tm, D_MODEL), lambda i: (i, C_MG // D_MODEL + 1)),
                  resident((Q_A, D_MODEL)), resident((I_B, D_MODEL)), resident((D_MODEL, D_MODEL)),
                  resident((1, D_MODEL)), resident((1, D_MODEL)),
                  resident((D_MODEL, D_FF)), resident((D_FF, D_MODEL)),
                  resident((1, D_MODEL)), resident((1, D_MODEL))],
        out_specs=pl.BlockSpec((tm, D_MODEL), lambda i: (i, 0)),
        out_shape=jax.ShapeDtypeStruct((n, D_MODEL), F32),
        compiler_params=_cparams(("arbitrary",)),
        name="merge_mlp",
    )(x2d, o_a, o_b, h, h, w_a, w_b, w_o, ln1_g, ln1_b, w_up, w_down, ln2_g, ln2_b)


def _dot01(a01, x):
    hi = x.astype(BF16)
    r1 = x - hi.astype(F32)
    mid = r1.astype(BF16)
    lo = (r1 - mid.astype(F32)).astype(BF16)
    out = jnp.dot(a01, hi, preferred_element_type=F32)
    out += jnp.dot(a01, mid, preferred_element_type=F32)
    out += jnp.dot(a01, lo, preferred_element_type=F32)
    return out


def _x_dot01(x, b01):
    hi = x.astype(BF16)
    r1 = x - hi.astype(F32)
    mid = r1.astype(BF16)
    lo = (r1 - mid.astype(F32)).astype(BF16)
    out = jnp.dot(hi, b01, preferred_element_type=F32)
    out += jnp.dot(mid, b01, preferred_element_type=F32)
    out += jnp.dot(lo, b01, preferred_element_type=F32)
    return out


_NT = (((1,), (1,)), ((), ()))
_TN = (((0,), (0,)), ((), ()))


def _cmp_prompt_kernel(kv_ref, w_ref, posf_ref, wb_ref, o_ref, *, n_ch):
    acc = jnp.zeros((n_ch, 256), F32)
    for l in range(CMP_STRIDE):
        xl = kv_ref[pl.ds(l, n_ch, stride=CMP_STRIDE), :].astype(BF16)
        acc += jnp.dot(xl, w_ref[l], preferred_element_type=F32)
    bias = jnp.dot(posf_ref[...].astype(BF16), wb_ref[...], preferred_element_type=F32)[0:1]
    nxt = pltpu.roll(acc[:, 128:], n_ch - 1, 0)
    o_ref[0] = (acc[:, :128] + nxt + bias).astype(BF16)


def _at_slot(a, slot, axis):
    pads = [(0, 0)] * (a.ndim + 1)
    pads[axis] = (slot, 1 - slot)
    return jnp.pad(jnp.expand_dims(a, axis), pads)


def _cmp_weights(cmp_w, cmp_pos):
    w = cmp_w.reshape(2, 2, CMP_STRIDE, HEAD_DIM, HEAD_DIM)
    wl = jnp.stack([_at_slot(jnp.transpose(w[c], (1, 2, 0, 3)), c, 3) for c in range(2)], axis=1)
    wb = jnp.stack([_at_slot(cmp_w[c].reshape(CMP_LEN * HEAD_DIM, HEAD_DIM), c, 1) for c in range(2)])
    wl = wl.reshape(CMP_STRIDE, 128, 256).astype(BF16)
    wb = wb.reshape(2 * CMP_LEN * HEAD_DIM, 128).astype(BF16)
    posf = jnp.broadcast_to(cmp_pos.reshape(1, 2 * CMP_LEN * HEAD_DIM), (8, 2 * CMP_LEN * HEAD_DIM))
    return wl, wb, posf


def _cmp_prompt(h, b, t, wl, wb, posf):
    n_ch = t // CMP_STRIDE
    return pl.pallas_call(
        functools.partial(_cmp_prompt_kernel, n_ch=n_ch),
        grid=(b, KV_HEADS),
        in_specs=[pl.BlockSpec((t, 128), lambda i, g: (i, C_KVC // 128 + g)),
                  pl.BlockSpec((CMP_STRIDE, 128, 256), lambda i, g: (0, 0, 0)),
                  pl.BlockSpec((8, 4096), lambda i, g: (0, 0)),
                  pl.BlockSpec((4096, 128), lambda i, g: (0, 0))],
        out_specs=pl.BlockSpec((1, n_ch, 128), lambda i, g: (i, 0, g)),
        out_shape=jax.ShapeDtypeStruct((b, n_ch, 256), BF16),
        compiler_params=_cparams(("arbitrary", "arbitrary")),
        name="cmp_prompt",
    )(h, wl, posf, wb)


QB = 256
PIECE = 256
SEL_UNROLL = 2
KT = 256
WT = 128


ROWS = GROUP * QB
TL_COUNT = 64


def _head_major(q):
    return jnp.concatenate([q[:, r * 64:(r + 1) * 64] for r in range(GROUP)], axis=0) * ATTN_SCALE


def _head_slope(g, r):
    return jnp.where(g == 0, 2.0 ** -(r + 1), 2.0 ** -(r + 1 + GROUP)).astype(F32)


def _slope_row(g):
    r_ix = lax.broadcasted_iota(jnp.int32, (1, ROWS), 1) // QB
    row = jnp.zeros((1, ROWS), F32)
    for r in range(GROUP):
        row = jnp.where(r_ix == r, _head_slope(g, r), row)
    return row


def _nsa_select_kernel(q_ref, kc_ref, oc_ref, un_ref, tl_ref, *, n_ch):
    g = pl.program_id(1)
    c = pl.program_id(2)
    t0 = c * QB
    qpad = jnp.concatenate([_head_major(q_ref[...]), jnp.zeros((ROWS, 64), F32)], axis=1).astype(BF16)
    kc = kc_ref[0]
    s = lax.dot_general(kc, qpad, _NT, preferred_element_type=F32)
    pos_row = t0 + lax.broadcasted_iota(jnp.int32, (1, ROWS), 1) % QB
    n_col = lax.broadcasted_iota(jnp.int32, (n_ch, 1), 0)
    dist_i = pos_row - (n_col * CMP_STRIDE + (CMP_LEN - 1))
    valid = jnp.logical_and(dist_i >= 0, n_col < n_ch - 1)
    s = jnp.where(valid, s - _slope_row(g) * dist_i.astype(F32), NEG)
    m = jnp.max(s, axis=0, keepdims=True)
    p = jnp.exp(s - m)
    inv = jnp.where(m > 0.5 * NEG, 1.0 / jnp.maximum(jnp.sum(p, axis=0, keepdims=True), 1e-30), 0.0)
    p = p * inv
    o_c = lax.dot_general(p.astype(BF16), kc, _TN, preferred_element_type=F32)
    for r in range(GROUP):
        oc_ref[:, r * 64:(r + 1) * 64] = o_c[r * QB:(r + 1) * QB, 64:128]

    imp = p[:, 0:QB]
    for r in range(1, GROUP):
        imp = imp + p[:, r * QB:(r + 1) * QB]
    ratio = SEL_BLOCK // CMP_STRIDE
    pool_t = (lax.broadcasted_iota(jnp.int32, (128, n_ch), 1) // ratio
              == lax.broadcasted_iota(jnp.int32, (128, n_ch), 0)).astype(BF16)
    imp_blk = _dot01(pool_t, imp)
    blk = lax.broadcasted_iota(jnp.int32, (128, QB), 0)
    pos_q = t0 + lax.broadcasted_iota(jnp.int32, (1, QB), 1)
    cur = pos_q // SEL_BLOCK
    forced = jnp.logical_or(jnp.logical_or(blk == 0, blk == cur), blk == cur - 1)
    allowed = blk * SEL_BLOCK <= pos_q
    work = jnp.where(jnp.logical_and(allowed, jnp.logical_not(forced)), imp_blk, -jnp.inf)
    sel = forced
    for _ in range(N_SELECT - 3):
        mx = jnp.max(work, axis=0, keepdims=True)
        first = jnp.min(jnp.where(work == mx, blk, 128), axis=0, keepdims=True)
        pick = blk == first
        sel = jnp.logical_or(sel, pick)
        work = jnp.where(pick, -jnp.inf, work)
    sel = jnp.logical_and(sel, allowed)
    un_ref[...] = jnp.transpose(jnp.where(sel, 0.0, -UNSEL)).astype(BF16)

    tile_of = (lax.broadcasted_iota(jnp.int32, (128, 128), 1) // (KT // SEL_BLOCK)
               == lax.broadcasted_iota(jnp.int32, (128, 128), 0)).astype(BF16)
    sel_b = jnp.where(jnp.logical_and(sel, blk > 0), 1.0, 0.0).astype(BF16)
    cnt = jnp.sum(jnp.dot(tile_of, sel_b, preferred_element_type=F32), axis=1, keepdims=True)
    tile_col = lax.broadcasted_iota(jnp.int32, (128, 1), 0)
    flag = jnp.logical_and(cnt > 0.0, tile_col < t0 // KT)
    flag_m = jnp.where(jnp.broadcast_to(flag, (128, 128)), 1.0, 0.0).astype(BF16)
    row_i = lax.broadcasted_iota(jnp.int32, (128, 128), 0)
    lane_i = lax.broadcasted_iota(jnp.int32, (128, 128), 1)
    before = jnp.dot((lane_i < row_i).astype(BF16), flag_m, preferred_element_type=F32)
    slot = jnp.where(jnp.logical_and(flag, before == lane_i.astype(F32)), 1.0, 0.0).astype(BF16)
    j_rows = lax.broadcasted_iota(jnp.int32, (8, 128), 1).astype(BF16)
    listed = jnp.dot(j_rows, slot, preferred_element_type=F32)
    total = jnp.dot(jnp.ones((8, 128), BF16), flag_m, preferred_element_type=F32)
    lane8 = lax.broadcasted_iota(jnp.int32, (8, 128), 1)
    tl_ref[0] = jnp.where(lane8 == TL_COUNT, total, listed).astype(jnp.int32)


def _nsa_select(h, kcmp, b, t):
    n_ch = t // CMP_STRIDE
    nq = t // QB
    steps = b * KV_HEADS * nq
    return pl.pallas_call(
        functools.partial(_nsa_select_kernel, n_ch=n_ch),
        grid=(b, KV_HEADS, nq),
        in_specs=[pl.BlockSpec((QB, 256), lambda i, g, c: (i * nq + c, C_QA // 256 + g)),
                  pl.BlockSpec((1, n_ch, 128), lambda i, g, c: (i, 0, g))],
        out_specs=[pl.BlockSpec((QB, 256), lambda i, g, c: (i * nq + c, g)),
                   pl.BlockSpec((QB, 128), lambda i, g, c: (i * nq + c, g)),
                   pl.BlockSpec((1, 8, 128), lambda i, g, c: ((i * KV_HEADS + g) * nq + c, 0, 0))],
        out_shape=[jax.ShapeDtypeStruct((b * t, Q_A), F32),
                   jax.ShapeDtypeStruct((b * t, KV_HEADS * 128), BF16),
                   jax.ShapeDtypeStruct((steps, 8, 128), jnp.int32)],
        compiler_params=_cparams(("arbitrary", "arbitrary", "arbitrary")),
        name="nsa_select",
    )(h, kcmp)


def _nsa_attend_kernel(tl_ref, q_ref, un_ref, gl_ref, oc_ref, ks_ref, kw_ref, o_ref,
                       ksb_ref, ksa_ref, kwb_ref, kwa_ref, *, nq):
    i = pl.program_id(0)
    g = pl.program_id(1)
    c = pl.program_id(2)
    step = (i * KV_HEADS + g) * nq + c
    t0 = c * QB
    half = PIECE
    n_piece = ROWS // PIECE
    qs = _head_major(q_ref[...]) * LOG2E
    s2 = jnp.concatenate([jnp.broadcast_to(_head_slope(g, r) * LOG2E, (QB, 1)) for r in range(GROUP)], axis=0)
    s2_hi = s2.astype(BF16).astype(F32)
    s2_lo = s2 - s2_hi
    lane64 = lax.broadcasted_iota(jnp.int32, (ROWS, 64), 1)
    ali_q = jnp.where(lane64 == 0, -64.0 * s2_hi, jnp.where(lane64 == 1, -s2_hi,
                      jnp.where(lane64 == 2, -64.0 * s2_lo, jnp.where(lane64 == 3, -s2_lo, 0.0))))
    q_win = jnp.concatenate([qs, ali_q], axis=1).astype(BF16)
    first_blk = jnp.logical_and(lax.broadcasted_iota(jnp.int32, (QB, 128), 1) == 0, c > 0)
    un = jnp.where(first_blk, jnp.asarray(-UNSEL, BF16), un_ref[...])
    q_sel =jnp.concatenate([q_win, jnp.concatenate([un] * GROUP, axis=0)], axis=1)
    slope_row = _slope_row(g) * LOG2E

    @pl.when(c == 0)
    def _():
        def fill(j, carry):
            r0 = pl.multiple_of(j * KT, KT)
            row = lax.broadcasted_iota(jnp.int32, (KT, 128), 0)
            lane = lax.broadcasted_iota(jnp.int32, (KT, 128), 1)
            for src, dst_kv, dst_aug, tile in ((ks_ref, ksb_ref, ksa_ref, KT), (kw_ref, kwb_ref, kwa_ref, WT)):
                kv_b = src[pl.ds(r0, KT), :].astype(BF16)
                d = (QB - 1) - row % tile
                ali = jnp.where(lane // 4 == 16, jnp.where(lane % 2 == 0, d >> 6, d & 63), 0).astype(F32).astype(BF16)
                dst_kv[pl.ds(r0, KT), :] = kv_b
                dst_aug[pl.ds(r0, KT), 0:128] = jnp.where(lane < 64, kv_b, ali)
            ksa_ref[pl.ds(r0, KT), 128:256] = jnp.where((r0 + row) // SEL_BLOCK == lane, 1.0, 0.0).astype(BF16)
            return carry

        lax.fori_loop(0, ks_ref.shape[0] // KT, fill, 0)

    def update(carry, s, kv_b, offset):
        out = []
        for hh, ((m, l, acc), sh) in enumerate(zip(carry, s)):
            shift = slope_row[:, hh * half:(hh + 1) * half] * offset
            m_new = jnp.maximum(m, jnp.max(sh, axis=0, keepdims=True) - shift)
            alpha = jnp.exp2(m - m_new)
            p = jnp.exp2(sh - (m_new + shift))
            l = alpha * l + jnp.sum(p, axis=0, keepdims=True)
            acc = alpha * acc + lax.dot_general(kv_b, p.astype(BF16), _TN, preferred_element_type=F32)
            out.append((m_new, l, acc))
        return tuple(out)

    def init():
        return tuple((jnp.full((1, half), NEG, F32), jnp.zeros((1, half), F32), jnp.zeros((128, half), F32))
                     for _ in range(n_piece))

    def tile(carry, aug_ref, kvb_ref, q_side, k0, n_keys, keep):
        k_aug = aug_ref[pl.ds(k0, n_keys), :]
        s = []
        for hh in range(n_piece):
            sh = lax.dot_general(k_aug, q_side[hh * half:(hh + 1) * half], _NT, preferred_element_type=F32)
            if keep is not None:
                sh = jnp.where(jnp.concatenate([keep] * (half // QB), axis=1), sh, NEG)
            s.append(sh)
        return update(carry, s, kvb_ref[pl.ds(k0, n_keys), :], (t0 - k0).astype(F32))

    k_tail = pl.multiple_of((t0 // KT) * KT, KT)
    key_x = lax.broadcasted_iota(jnp.int32, (KT, QB), 0)
    q_x = lax.broadcasted_iota(jnp.int32, (KT, QB), 1)
    carry = tile(init(), ksa_ref, ksb_ref, q_sel, k_tail, KT, k_tail + key_x <= t0 + q_x)

    def sel_listed(n, carry):
        return tile(carry, ksa_ref, ksb_ref, q_sel, pl.multiple_of(tl_ref[step, n] * KT, KT), KT, None)

    n_listed = tl_ref[step, TL_COUNT]

    def sel_group(n, cr):
        for u in range(SEL_UNROLL):
            cr = sel_listed(SEL_UNROLL * n + u, cr)
        return cr

    carry = lax.fori_loop(0, n_listed // SEL_UNROLL, sel_group, carry)
    carry = lax.fori_loop(n_listed - n_listed % SEL_UNROLL, n_listed, sel_listed, carry)
    k_first = ksa_ref[0:SEL_BLOCK, 0:128]
    s_first = [jnp.where(c > 0, lax.dot_general(k_first, q_win[hh * half:(hh + 1) * half], _NT,
                                                 preferred_element_type=F32), NEG) for hh in range(n_piece)]
    carry = update(carry, s_first, ksb_ref[0:SEL_BLOCK, :], t0.astype(F32))
    o_sel = [acc * (1.0 / l) for (_, l, acc) in carry]

    w_keys = WINDOW + QB
    k0w = pl.multiple_of(jnp.maximum(t0 - WINDOW, 0), WT)
    k_aug = kwa_ref[pl.ds(k0w, w_keys), :]
    kv_w = kwb_ref[pl.ds(k0w, w_keys), :]
    dq = lax.broadcasted_iota(jnp.int32, (WT, QB), 1) - lax.broadcasted_iota(jnp.int32, (WT, QB), 0)
    offs = [t0 - (k0w + gi * WT) for gi in range(w_keys // WT)]
    keeps = [jnp.concatenate([jnp.logical_and(dq + off >= 0, dq + off < WINDOW)] * (half // QB), axis=1)
             for off in offs]
    o_win = []
    for hh in range(n_piece):
        sl = slope_row[:, hh * half:(hh + 1) * half]
        sh = lax.dot_general(k_aug, q_win[hh * half:(hh + 1) * half], _NT, preferred_element_type=F32)
        parts = [jnp.where(keep, sh[gi * WT:(gi + 1) * WT], NEG) for gi, keep in enumerate(keeps)]
        shifts = [sl * off.astype(F32) for off in offs]
        m_w = functools.reduce(jnp.maximum, [jnp.max(pt, axis=0, keepdims=True) - sf
                                             for pt, sf in zip(parts, shifts)])
        ps = [jnp.exp2(pt - (m_w + sf)) for pt, sf in zip(parts, shifts)]
        l_w = functools.reduce(jnp.add, [jnp.sum(p, axis=0, keepdims=True) for p in ps])
        acc = lax.dot_general(kv_w, jnp.concatenate(ps, axis=0).astype(BF16), _TN, preferred_element_type=F32)
        o_win.append(acc * (1.0 / l_w))

    gates = jax.nn.sigmoid(gl_ref[...])
    o_c = oc_ref[...]
    for r in range(GROUP):
        hh, cs = divmod(r * QB, half)
        o_s = jnp.transpose(o_sel[hh][:, cs:cs + QB])[:, 64:128]
        o_w = jnp.transpose(o_win[hh][:, cs:cs + QB])[:, 64:128]
        o_ref[:, r * 64:(r + 1) * 64] = (gates[:, 3 * r:3 * r + 1] * o_c[:, r * 64:(r + 1) * 64]
                                         + gates[:, 3 * r + 1:3 * r + 2] * o_s
                                         + gates[:, 3 * r + 2:3 * r + 3] * o_w)


def _nsa_attend(tiles, h, unsel, o_c, b, t):
    nq = t // QB
    grid_spec = pltpu.PrefetchScalarGridSpec(
        num_scalar_prefetch=1,
        grid=(b, KV_HEADS, nq),
        in_specs=[pl.BlockSpec((QB, 256), lambda i, g, c, tl: (i * nq + c, C_QA // 256 + g)),
                  pl.BlockSpec((QB, 128), lambda i, g, c, tl: (i * nq + c, g)),
                  pl.BlockSpec((QB, 128), lambda i, g, c, tl: (i * nq + c, C_GL // 128 + g)),
                  pl.BlockSpec((QB, 256), lambda i, g, c, tl: (i * nq + c, g)),
                  pl.BlockSpec((t, 128), lambda i, g, c, tl: (i, C_KVS // 128 + g)),
                  pl.BlockSpec((t, 128), lambda i, g, c, tl: (i, C_KVW // 128 + g))],
        out_specs=pl.BlockSpec((QB, 256), lambda i, g, c, tl: (i * nq + c, g)),
        scratch_shapes=[pltpu.VMEM((t, 128), BF16), pltpu.VMEM((t, 256), BF16),
                        pltpu.VMEM((t, 128), BF16), pltpu.VMEM((t, 128), BF16)],
    )
    return pl.pallas_call(
        functools.partial(_nsa_attend_kernel, nq=nq),
        grid_spec=grid_spec,
        out_shape=jax.ShapeDtypeStruct((b * t, Q_A), F32),
        compiler_params=_cparams(("arbitrary", "arbitrary", "arbitrary")),
        name="nsa_attend",
    )(tiles, h, unsel, h, o_c, h, h)


def _nsa_prompt(h, kcmp, b, t):
    o_c, unsel, tl = _nsa_select(h, kcmp, b, t)
    return _nsa_attend(tl[:, 0, :], h, unsel, o_c, b, t)


HB = 256


def _lower_bound(lb_ref):
    z = lb_ref[...]
    e = jnp.exp(z - jnp.max(z, axis=0, keepdims=True))
    return e[0:1] / jnp.sum(e, axis=0, keepdims=True)


def _hgrn_prompt_kernel(q_ref, f_ref, i_ref, g_ref, lb_ref, ng_ref, o_ref, s_out_ref, s_ref):
    tb = pl.program_id(1)

    @pl.when(tb == 0)
    def _():
        s_ref[...] = jnp.zeros_like(s_ref)

    n_c = HB // HGRN_CHUNK
    lb = _lower_bound(lb_ref)
    f = lb + (1.0 - lb) * jax.nn.sigmoid(f_ref[...])
    log_f = jnp.log(f)
    row = lax.broadcasted_iota(jnp.int32, (HB, HB), 0)
    col = lax.broadcasted_iota(jnp.int32, (HB, HB), 1)
    tril = jnp.logical_and(col <= row, row // HGRN_CHUNK == col // HGRN_CHUNK)
    lc = _dot01(tril.astype(BF16), log_f)
    lc3 = lc.reshape(n_c, HGRN_CHUNK, Q_B)
    dec8 = jnp.exp(lc3[:, HGRN_CHUNK - 1, :])
    dec = jnp.broadcast_to(dec8[:, None, :], lc3.shape).reshape(HB, Q_B)
    q_t = (q_ref[...] * jnp.exp(lc)).astype(BF16)
    k_f = (1.0 - f) * jnp.exp(-lc)
    k_t = k_f.astype(BF16)
    k_e = (k_f * dec).astype(BF16)
    v_all = i_ref[...]
    gate = g_ref[...]
    ng = ng_ref[...]
    for hh in range(B_HEADS):
        ks = slice(hh * B_DK, (hh + 1) * B_DK)
        vs = slice(hh * B_DV, (hh + 1) * B_DV)
        v = v_all[:, vs].astype(BF16)
        a = lax.dot_general(q_t[:, ks], k_t[:, ks], _NT, preferred_element_type=F32)
        a = jnp.where(tril, a, 0.0).astype(BF16)
        o = jnp.dot(a, v, preferred_element_type=F32)
        st = s_ref[hh]
        inter = []
        for cc in range(n_c):
            rs = slice(cc * HGRN_CHUNK, (cc + 1) * HGRN_CHUNK)
            inter.append(lax.dot_general(q_t[rs, ks], st.astype(BF16), _NT, preferred_element_type=F32))
            u = lax.dot_general(v[rs], k_e[rs, ks], _TN, preferred_element_type=F32)
            st = dec8[cc:cc + 1, ks] * st + u
        s_ref[hh] = st
        o = o + jnp.concatenate(inter, axis=0)
        o = o * lax.rsqrt(jnp.mean(o * o, axis=-1, keepdims=True) + LN_EPS) * ng
        gt = gate[:, vs]
        o_ref[:, vs] = o * (gt * jax.nn.sigmoid(gt))

    @pl.when(tb == pl.num_programs(1) - 1)
    def _():
        s_out_ref[0] = s_ref[...]


def _hgrn_prompt(h, b, t, lb_logits, norm_g):
    nt = t // HB
    return pl.pallas_call(
        _hgrn_prompt_kernel,
        grid=(b, nt),
        in_specs=[pl.BlockSpec((HB, Q_B), lambda i, j: (i * nt + j, C_QB // Q_B)),
                  pl.BlockSpec((HB, Q_B), lambda i, j: (i * nt + j, C_FB // Q_B)),
                  pl.BlockSpec((HB, I_B), lambda i, j: (i * nt + j, C_IB // I_B)),
                  pl.BlockSpec((HB, I_B), lambda i, j: (i * nt + j, C_GB // I_B)),
                  pl.BlockSpec((2, Q_B), lambda i, j: (0, 0)),
                  pl.BlockSpec((1, B_DV), lambda i, j: (0, 0))],
        out_specs=[pl.BlockSpec((HB, I_B), lambda i, j: (i * nt + j, 0)),
                   pl.BlockSpec((1, B_HEADS, B_DV, B_DK), lambda i, j: (i, 0, 0, 0))],
        out_shape=[jax.ShapeDtypeStruct((b * t, I_B), F32),
                   jax.ShapeDtypeStruct((b, B_HEADS, B_DV, B_DK), F32)],
        scratch_shapes=[pltpu.VMEM((B_HEADS, B_DV, B_DK), F32)],
        compiler_params=_cparams(("arbitrary", "arbitrary")),
        name="hgrn_prompt",
    )(h, h, h, h, lb_logits, norm_g)


def _prompt_layer(x, w_pad, w_kvt, cmpw, lb_logits, norm_g, w_a, w_b, w_o, ln1_g, ln1_b, w_up, w_down,
                  ln2_g, ln2_b):
    b, t, _ = x.shape
    x2d = x.reshape(b * t, D_MODEL)
    h, *kv_t = _project_prompt(x2d, w_pad, w_kvt, b, t)
    kcmp = _cmp_prompt(h, b, t, *cmpw)
    o_a = _nsa_prompt(h, kcmp, b, t)
    o_b, s_end = _hgrn_prompt(h, b, t, lb_logits, norm_g)
    y = _merge_mlp(x2d, o_a, o_b, h, w_a, w_b, w_o, ln1_g, ln1_b, w_up, w_down, ln2_g, ln2_b)
    kv_out = [jnp.moveaxis(a.reshape(b, 2, KV_HEADS, HEAD_DIM, t), -1, 1) for a in kv_t]
    return y.reshape(b, t, D_MODEL), kv_out, jnp.swapaxes(s_end, 2, 3)


def _kv_out(h, col, b, t):
    kv = h[:, col:col + 256].reshape(b, t, KV_HEADS, 2, HEAD_DIM)
    return jnp.swapaxes(kv, 2, 3)


LAND_PAGES = 8
CH_PER_PAGE = PAGE_SIZE // CMP_STRIDE
BLK_PER_PAGE = PAGE_SIZE // SEL_BLOCK
N_POOL_SEL = N_SELECT - 1


def _head_rows(q_row, g):
    rows = [q_row[:, g * 256 + r * 64:g * 256 + (r + 1) * 64] for r in range(GROUP)]
    return jnp.concatenate(rows + [jnp.zeros((8 - GROUP, 64), F32)], axis=0) * ATTN_SCALE


def _slope_col(g):
    row = lax.broadcasted_iota(jnp.int32, (8, 1), 0)
    col = jnp.zeros((8, 1), F32)
    for r in range(GROUP):
        col = jnp.where(row == r, 2.0 ** -(g * GROUP + r + 1), col)
    return col


def _cmp_sample_kernel(pt_ref, cache_ref, w_ref, posf_ref, wb_ref, q_ref, oc_ref, idx_ref,
                       buf_ref, xk_ref, xv_ref, bias_ref, sem, *, n_pages, past):
    b = pl.program_id(0)
    n_ch = n_pages * CH_PER_PAGE
    n_blk = past // SEL_BLOCK
    lanes = -(-n_blk // 128) * 128

    def page_copy(pg, page):
        return pltpu.make_async_copy(cache_ref.at[page], buf_ref.at[pg], sem.at[pg])

    def issue(pg, carry):
        page_copy(pg, pt_ref[b, pg]).start()
        return carry

    def land(i, carry):
        pages = [i * LAND_PAGES + u for u in range(LAND_PAGES)]
        for pg in pages:
            page_copy(pg, 0).wait()
        for pg in pages:
            r0 = pl.multiple_of(pg * PAGE_SIZE, PAGE_SIZE)
            xk_ref[pl.ds(r0, PAGE_SIZE), :] = jnp.transpose(buf_ref[pg, 0].reshape(128, PAGE_SIZE))
            xv_ref[pl.ds(r0, PAGE_SIZE), :] = jnp.transpose(buf_ref[pg, 1].reshape(128, PAGE_SIZE))
        return carry

    lax.fori_loop(0, n_pages, issue, 0)
    lax.fori_loop(0, n_pages // LAND_PAGES, land, 0)

    @pl.when(b == 0)
    def _():
        bias_ref[...] = jnp.dot(posf_ref[...].astype(BF16), wb_ref[...], preferred_element_type=F32)

    bias = bias_ref[0:1]
    blocks = []
    for c, x_ref in enumerate((xk_ref, xv_ref)):
        acc = jnp.zeros((n_ch, 256), F32)
        for l2 in range(CMP_STRIDE // 2):
            xl = jnp.concatenate([x_ref[pl.ds(2 * l2 + u, n_ch, stride=CMP_STRIDE), :] for u in range(2)],
                                 axis=1).astype(BF16)
            acc += jnp.dot(xl, w_ref[c, l2], preferred_element_type=F32)
        bias_c = jnp.concatenate([bias[:, c * 64:(c + 1) * 64]] * KV_HEADS, axis=1)
        blocks.append((acc[:, :128] + pltpu.roll(acc[:, 128:], n_ch - 1, 0) + bias_c).astype(BF16))
    kc_all, vc_all = blocks

    q_row = q_ref[0]
    n_ix = lax.broadcasted_iota(jnp.int32, (1, n_ch), 1)
    dist_i = past - (n_ix * CMP_STRIDE + (CMP_LEN - 1))
    valid = jnp.logical_and(dist_i >= 0, n_ix < n_ch - 1)
    dist = dist_i.astype(F32)
    ratio = SEL_BLOCK // CMP_STRIDE
    pool = (lax.broadcasted_iota(jnp.int32, (n_ch, lanes), 0) // ratio
            == lax.broadcasted_iota(jnp.int32, (n_ch, lanes), 1)).astype(BF16)
    row8 = lax.broadcasted_iota(jnp.int32, (8, lanes), 0)
    lane_g = lax.broadcasted_iota(jnp.int32, (8, 128), 1) // HEAD_DIM
    zeros4 = jnp.zeros((GROUP, 64), F32)
    q8 = jnp.concatenate([jnp.concatenate([_head_rows(q_row, 0)[0:GROUP], zeros4], axis=1),
                          jnp.concatenate([zeros4, _head_rows(q_row, 1)[0:GROUP]], axis=1)], axis=0).astype(BF16)
    s = lax.dot_general(q8, kc_all, _NT, preferred_element_type=F32)
    head = lax.broadcasted_iota(jnp.int32, (8, 1), 0)
    slope8 = jnp.zeros((8, 1), F32)
    for h in range(A_HEADS):
        slope8 = jnp.where(head == h, 2.0 ** -(h + 1), slope8)
    s = jnp.where(valid, s - slope8 * dist, NEG)
    m = jnp.max(s, axis=-1, keepdims=True)
    p = jnp.where(valid, jnp.exp(s - m), 0.0)
    p = p / jnp.maximum(jnp.sum(p, axis=-1, keepdims=True), 1e-30)
    o8 = jnp.dot(p.astype(BF16), vc_all, preferred_element_type=F32)
    o_c = jnp.where(lane_g == 0, o8, pltpu.roll(o8, GROUP, 0))
    imp8 = jnp.concatenate([p[0:1] + p[1:2] + p[2:3] + p[3:4], p[4:5] + p[5:6] + p[6:7] + p[7:8],
                            jnp.zeros((8 - KV_HEADS, n_ch), F32)], axis=0)
    score = jnp.where(row8 < KV_HEADS, _x_dot01(imp8, pool), -FORCE)
    lane = lax.broadcasted_iota(jnp.int32, (8, lanes), 1)
    forced = jnp.logical_or(lane == 0, lane == n_blk - 1)
    score = jnp.where(lane < n_blk, jnp.where(forced, FORCE, score), -jnp.inf)
    sc_t = jnp.transpose(jnp.concatenate([score, jnp.full((120, lanes), -jnp.inf, F32)], axis=0))
    blk = lax.broadcasted_iota(jnp.int32, (lanes, 128), 0)
    picks = []
    for r in range(N_POOL_SEL):
        m = jnp.max(sc_t, axis=0, keepdims=True)
        first = jnp.min(jnp.where(sc_t == m, blk, lanes), axis=0, keepdims=True)
        picks.append(first)
        sc_t = jnp.where(blk == first, -jnp.inf, sc_t)
    idx_ref[0] = jnp.concatenate(picks + [jnp.zeros((N_SELECT - N_POOL_SEL, 128), jnp.int32)], axis=0)
    oc_ref[0] = o_c


def _cmp_sample(page_table, cache5, w_s, posf, wb, q3, past):
    bsz, n_pages = page_table.shape
    grid_spec = pltpu.PrefetchScalarGridSpec(
        num_scalar_prefetch=1,
        grid=(bsz,),
        in_specs=[pl.BlockSpec(memory_space=pl.ANY),
                  pl.BlockSpec((2, CMP_STRIDE // 2, 256, 256), lambda i, pt: (0, 0, 0, 0)),
                  pl.BlockSpec((8, 4096), lambda i, pt: (0, 0)),
                  pl.BlockSpec((4096, 128), lambda i, pt: (0, 0)),
                  pl.BlockSpec((1, 1, Q_A), lambda i, pt: (i, 0, 0))],
        out_specs=[pl.BlockSpec((1, 8, 128), lambda i, pt: (i, 0, 0)),
                   pl.BlockSpec((1, N_SELECT, 128), lambda i, pt: (i, 0, 0))],
        scratch_shapes=[pltpu.VMEM((n_pages, 2, KV_HEADS, HEAD_DIM, PAGE_SIZE), F32),
                        pltpu.VMEM((n_pages * PAGE_SIZE, 128), F32),
                        pltpu.VMEM((n_pages * PAGE_SIZE, 128), F32),
                        pltpu.VMEM((8, 128), F32),
                        pltpu.SemaphoreType.DMA((n_pages,))],
    )
    return pl.pallas_call(
        functools.partial(_cmp_sample_kernel, n_pages=n_pages, past=past),
        grid_spec=grid_spec,
        out_shape=[jax.ShapeDtypeStruct((bsz, 8, 128), F32),
                   jax.ShapeDtypeStruct((bsz, N_SELECT, 128), jnp.int32)],
        compiler_params=_cparams(("arbitrary",)),
        name="cmp_sample",
    )(page_table, cache5, w_s, posf, wb, q3)


def _pick_lane(mat, lane, target):
    return jnp.sum(jnp.where(lane == target, mat, 0.0), axis=-1, keepdims=True)


def _sel_win_sample_kernel(pt_ref, idx_ref, cache_ref, win_ref, q_ref, ks_ref, kw_ref, gl_ref, oc_ref, o_ref,
                           kbuf_ref, vbuf_ref, sem, *, past):
    b = pl.program_id(0)
    slot_lanes = N_SELECT * PAGE_SIZE

    def page_copies(g, k, page):
        dst = pl.ds(k * PAGE_SIZE, PAGE_SIZE)
        return (pltpu.make_async_copy(cache_ref.at[page, 0, g], kbuf_ref.at[g, :, dst], sem),
                pltpu.make_async_copy(cache_ref.at[page, 1, g], vbuf_ref.at[g, :, dst], sem))

    for g in range(KV_HEADS):
        for k in range(N_POOL_SEL):
            page = pt_ref[b, idx_ref[b, g * N_SELECT + k] // BLK_PER_PAGE]
            for cp in page_copies(g, k, page):
                cp.start()
        pad = pl.ds(N_POOL_SEL * PAGE_SIZE, PAGE_SIZE)
        kbuf_ref[g, :, pad] = jnp.zeros((HEAD_DIM, PAGE_SIZE), F32)
        vbuf_ref[g, :, pad] = jnp.zeros((HEAD_DIM, PAGE_SIZE), F32)

    q_row = q_ref[0]
    ks_new = ks_ref[0]
    kw_new = kw_ref[0]
    gl_all = jax.nn.sigmoid(gl_ref[0])
    lane128 = lax.broadcasted_iota(jnp.int32, (8, 128), 1)
    row8 = lax.broadcasted_iota(jnp.int32, (8, 1), 0)
    w_len = win_ref.shape[-1]

    def two_piece(qh, slope, k_t, v_t, dist_i, valid, k_new, v_new):
        s = jnp.dot(qh.astype(BF16), k_t.astype(BF16), preferred_element_type=F32)
        s = jnp.where(valid, s - slope * dist_i.astype(F32), NEG)
        s_n = jnp.sum(qh * k_new, axis=-1, keepdims=True)
        m = jnp.maximum(jnp.max(s, axis=-1, keepdims=True), s_n)
        p = jnp.where(valid, jnp.exp(s - m), 0.0)
        p_n = jnp.exp(s_n - m)
        l = jnp.sum(p, axis=-1, keepdims=True) + p_n
        o = lax.dot_general(p.astype(BF16), v_t.astype(BF16), _NT, preferred_element_type=F32) + p_n * v_new
        return o / l

    outs = []
    for g in range(KV_HEADS):
        qh = _head_rows(q_row, g)
        slope = _slope_col(g)
        j_ix = lax.broadcasted_iota(jnp.int32, (1, w_len), 1)
        dist_w = w_len - j_ix
        o_w = two_piece(qh, slope, win_ref[0, 0, g], win_ref[0, 1, g], dist_w, dist_w < WINDOW,
                        kw_new[:, g * 128:g * 128 + 64], kw_new[:, g * 128 + 64:(g + 1) * 128])
        outs.append((qh, slope, o_w))

    for g in range(KV_HEADS):
        for k in range(N_POOL_SEL):
            for cp in page_copies(g, k, 0):
                cp.wait()

    lane_s = lax.broadcasted_iota(jnp.int32, (1, slot_lanes), 1)
    o_all = []
    for g in range(KV_HEADS):
        qh, slope, o_w = outs[g]
        pos_k = jnp.full((1, slot_lanes), past + 1, jnp.int32)
        for k in range(N_POOL_SEL):
            blk = idx_ref[b, g * N_SELECT + k]
            r = lane_s % PAGE_SIZE
            in_blk = jnp.logical_and(lane_s // PAGE_SIZE == k, r // SEL_BLOCK == blk % BLK_PER_PAGE)
            pos_k = jnp.where(in_blk, (blk // BLK_PER_PAGE) * PAGE_SIZE + r, pos_k)
        dist_s = past - pos_k
        o_s = two_piece(qh, slope, kbuf_ref[g], vbuf_ref[g], dist_s, dist_s >= 0,
                        ks_new[:, g * 128:g * 128 + 64], ks_new[:, g * 128 + 64:(g + 1) * 128])
        o_c = oc_ref[0][:, g * 64:(g + 1) * 64]
        gates = jnp.broadcast_to(gl_all[:, g * 128:(g + 1) * 128], (8, 128))
        g_c = _pick_lane(gates, lane128, 3 * row8)
        g_s = _pick_lane(gates, lane128, 3 * row8 + 1)
        g_w = _pick_lane(gates, lane128, 3 * row8 + 2)
        o_all.append(g_c * o_c + g_s * o_s + g_w * o_w)
    o_ref[0] = jnp.concatenate(o_all, axis=1)


def _sel_win_sample(page_table, idx, cache_sel5, cache_win5, q3, ks3, kw3, gl3, o_c, past):
    bsz = page_table.shape[0]
    w_len = cache_win5.shape[-1]
    grid_spec = pltpu.PrefetchScalarGridSpec(
        num_scalar_prefetch=2,
        grid=(bsz,),
        in_specs=[pl.BlockSpec(memory_space=pl.ANY),
                  pl.BlockSpec((1, 2, KV_HEADS, HEAD_DIM, w_len), lambda i, pt, ix: (i, 0, 0, 0, 0)),
                  pl.BlockSpec((1, 1, Q_A), lambda i, pt, ix: (i, 0, 0)),
                  pl.BlockSpec((1, 1, 256), lambda i, pt, ix: (i, 0, 0)),
                  pl.BlockSpec((1, 1, 256), lambda i, pt, ix: (i, 0, 0)),
                  pl.BlockSpec((1, 1, 256), lambda i, pt, ix: (i, 0, 0)),
                  pl.BlockSpec((1, 8, 128), lambda i, pt, ix: (i, 0, 0))],
        out_specs=pl.BlockSpec((1, 8, 128), lambda i, pt, ix: (i, 0, 0)),
        scratch_shapes=[pltpu.VMEM((KV_HEADS, HEAD_DIM, N_SELECT * PAGE_SIZE), F32),
                        pltpu.VMEM((KV_HEADS, HEAD_DIM, N_SELECT * PAGE_SIZE), F32),
                        pltpu.SemaphoreType.DMA(())],
    )
    return pl.pallas_call(
        functools.partial(_sel_win_sample_kernel, past=past),
        grid_spec=grid_spec,
        out_shape=jax.ShapeDtypeStruct((bsz, 8, 128), F32),
        compiler_params=_cparams(("arbitrary",)),
        name="sel_win_sample",
    )(page_table, idx, cache_sel5, cache_win5, q3, ks3, kw3, gl3, o_c)


def _hgrn_sample_kernel(q_ref, f_ref, v_ref, g_ref, lb_ref, ng_ref, s_ref, o_ref, s_out_ref):
    z = lb_ref[...]
    e = jnp.exp(z - jnp.max(z, axis=0, keepdims=True))
    lb = e[0] / jnp.sum(e, axis=0)
    f = lb + (1.0 - lb) * jax.nn.sigmoid(f_ref[0])
    decay = jnp.exp(jnp.log(f))
    k = 1.0 - f
    q = q_ref[0]
    v = v_ref[0]
    gate = g_ref[0]
    ng = ng_ref[...]
    for hh in range(B_HEADS):
        hs = slice(hh, hh + 1)
        s_new = decay[hs] * s_ref[0, hh] + v[:, hs] * k[hs]
        s_out_ref[0, hh] = s_new
        o = jnp.sum(q[hs] * s_new, axis=1, keepdims=True)
        o = o * lax.rsqrt(jnp.mean(o * o, axis=0, keepdims=True) + LN_EPS) * ng
        gt = gate[:, hs]
        o_ref[0, :, hs] = o * (gt * jax.nn.sigmoid(gt))


def _hgrn_sample(q_hk, f_hk, v_vh, g_vh, lb_hk, ng_col, state_t):
    bsz = q_hk.shape[0]
    return pl.pallas_call(
        _hgrn_sample_kernel,
        grid=(bsz,),
        in_specs=[pl.BlockSpec((1, B_HEADS, B_DK), lambda i: (i, 0, 0)),
                  pl.BlockSpec((1, B_HEADS, B_DK), lambda i: (i, 0, 0)),
                  pl.BlockSpec((1, B_DV, B_HEADS), lambda i: (i, 0, 0)),
                  pl.BlockSpec((1, B_DV, B_HEADS), lambda i: (i, 0, 0)),
                  pl.BlockSpec((2, B_HEADS, B_DK), lambda i: (0, 0, 0)),
                  pl.BlockSpec((B_DV, 1), lambda i: (0, 0)),
                  pl.BlockSpec((1, B_HEADS, B_DV, B_DK), lambda i: (i, 0, 0, 0))],
        out_specs=[pl.BlockSpec((1, B_DV, B_HEADS), lambda i: (i, 0, 0)),
                   pl.BlockSpec((1, B_HEADS, B_DV, B_DK), lambda i: (i, 0, 0, 0))],
        out_shape=[jax.ShapeDtypeStruct((bsz, B_DV, B_HEADS), F32),
                   jax.ShapeDtypeStruct((bsz, B_HEADS, B_DV, B_DK), F32)],
        compiler_params=_cparams(("arbitrary",)),
        name="hgrn_sample",
    )(q_hk, f_hk, v_vh, g_vh, lb_hk, ng_col, state_t)


def _rows_last(cache):
    return jnp.moveaxis(cache, -4, -1)


def _sample_layer(x, cache_cmp, cache_sel, cache_win, state, page_table, w_pad, cmpw, w_s, lb_logits, norm_g,
                  w_a, w_b, w_o, ln1_g, ln1_b, w_up, w_down, ln2_g, ln2_b):
    bsz, t, _ = x.shape
    assert t == 1, "the sample group decodes one token per request"
    n_pages = page_table.shape[1]
    past = n_pages * PAGE_SIZE
    x2d = x.reshape(bsz, D_MODEL)
    h = _project(x2d, w_pad)
    h3 = h.reshape(bsz, 1, D_PAD)
    q3 = h3[:, :, C_QA:C_QA + Q_A]
    wl, wb, posf = cmpw
    o_c, idx = _cmp_sample(page_table, _rows_last(cache_cmp), w_s, posf, wb, q3, past)
    idx2 = jnp.swapaxes(idx[:, :, :KV_HEADS], 1, 2).reshape(bsz, KV_HEADS * N_SELECT)
    o_rd = _sel_win_sample(page_table, idx2, _rows_last(cache_sel), _rows_last(cache_win),
                           q3, h3[:, :, C_KVS:C_KVS + 256], h3[:, :, C_KVW:C_KVW + 256],
                           h3[:, :, C_GL:C_GL + 256], o_c, past)
    o_a = jnp.swapaxes(o_rd[:, :GROUP].reshape(bsz, GROUP, KV_HEADS, HEAD_DIM), 1, 2).reshape(bsz, Q_A)
    to_hk = lambda a: a.reshape(-1, B_HEADS, B_DK)
    to_vh = lambda a: jnp.swapaxes(a.reshape(-1, B_HEADS, B_DV), 1, 2)
    o_vh, s_t = _hgrn_sample(to_hk(h[:, C_QB:C_QB + Q_B]), to_hk(h[:, C_FB:C_FB + Q_B]),
                             to_vh(h[:, C_IB:C_IB + I_B]), to_vh(h[:, C_GB:C_GB + I_B]),
                             to_hk(lb_logits), norm_g.reshape(B_DV, 1), jnp.swapaxes(state, 2, 3))
    o_b = jnp.swapaxes(o_vh, 1, 2).reshape(bsz, I_B)
    y = _merge_mlp(x2d, o_a, o_b, h, w_a, w_b, w_o, ln1_g, ln1_b, w_up, w_down, ln2_g, ln2_b)
    return y.reshape(bsz, 1, D_MODEL), h, jnp.swapaxes(s_t, 2, 3)


def _cmp_sample_weights(cmp_w):
    w = cmp_w.reshape(2, 2, CMP_STRIDE, HEAD_DIM, HEAD_DIM)
    base = jnp.transpose(w, (0, 2, 3, 1, 4))
    ws = jnp.stack([_at_slot(base, g, 4) for g in range(2)], axis=2)
    return ws.reshape(2, CMP_STRIDE // 2, 256, 256).astype(BF16)


def kernel(x_prompt, x_sample, cache_cmp_kv, cache_sel_kv, cache_win_kv, state_hgrn, page_table,
           w_in, cmp_w, cmp_pos, hgrn_lb_logits, hgrn_norm_g, w_br_a, w_br_b, w_out,
           ln1_g, ln1_b, w_up, w_down, ln2_g, ln2_b):
    assert w_in.shape[0] == 1, "one layer"
    b, t, _ = x_prompt.shape
    bsz = x_sample.shape[0]
    assert t % HB == 0 and t % QB == 0 and WINDOW + QB <= t <= 128 * SEL_BLOCK
    w_pad = _reorder_w_in(w_in[0])
    cmpw = _cmp_weights(cmp_w[0], cmp_pos[0])
    w_s = _cmp_sample_weights(cmp_w[0])
    dense = (w_br_a[0].astype(BF16), w_br_b[0].astype(BF16), w_out[0].astype(BF16), ln1_g, ln1_b,
             w_up[0].astype(BF16), w_down[0].astype(BF16), ln2_g, ln2_b)

    o_kv = Q_A
    w_kvt = jnp.transpose(w_in[0][:, o_kv:o_kv + 3 * KV_A]).astype(BF16)
    y_p, (kvc_p, kvs_p, kvw_p), s_p = _prompt_layer(x_prompt, w_pad, w_kvt, cmpw, hgrn_lb_logits, hgrn_norm_g,
                                                   *dense)
    y_s, h_s, s_s = _sample_layer(x_sample, cache_cmp_kv[0], cache_sel_kv[0], cache_win_kv[0], state_hgrn[0],
                                  page_table, w_pad, cmpw, w_s, hgrn_lb_logits, hgrn_norm_g, *dense)

    win_p = min(WINDOW, t)
    kvw_s = _kv_out(h_s, C_KVW, bsz, 1)
    new_win_s = jnp.concatenate([cache_win_kv[0], kvw_s], axis=1)[:, -min(WINDOW, cache_win_kv.shape[2] + 1):]
    return (y_p, y_s,
            kvc_p[None], kvs_p[None], kvw_p[:, -win_p:][None], s_p[None],
            _kv_out(h_s, C_KVC, bsz, 1)[None], _kv_out(h_s, C_KVS, bsz, 1)[None], new_win_s[None], s_s[None])
```

```python
import functools

import numpy as np
import jax
import jax.numpy as jnp
from jax import lax
from jax.experimental import pallas as pl
from jax.experimental.pallas import tpu as pltpu

F32 = jnp.float32
BF16 = jnp.bfloat16

D_MODEL = 1024
HEAD_DIM = 64
A_HEADS = 8
KV_HEADS = 2
GROUP = A_HEADS // KV_HEADS
CMP_STRIDE = 16
CMP_LEN = 32
SEL_BLOCK = 64
N_SELECT = 16
WINDOW = 512
PAGE_SIZE = 128
B_HEADS = 8
B_DK = 128
B_DV = 64
HGRN_CHUNK = 32
D_FF = 4 * D_MODEL
DEEPNORM_ALPHA = 2.0 ** 0.25
LN_EPS = 1e-5
NEG = -1e30
FORCE = 1e6
ATTN_SCALE = HEAD_DIM ** -0.5
LOG2E = 1.4426950408889634
Q_A = A_HEADS * HEAD_DIM
KV_A = 2 * KV_HEADS * HEAD_DIM
GATE_A = 3 * A_HEADS
Q_B = B_HEADS * B_DK
I_B = B_HEADS * B_DV

C_MG = 0
C_QB = 2048
C_FB = 3072
C_QA = 4096
C_IB = 4608
C_GB = 5120
C_KVC = 5632
C_KVS = 5888
C_KVW = 6144
C_GL = 6400
D_PAD = 6656
PROJ_TN = 1664

VMEM_LIMIT = 56 * 1024 * 1024
UNSEL = float(2.0 ** 100)


def _cparams(sem):
    return pltpu.CompilerParams(dimension_semantics=sem, vmem_limit_bytes=VMEM_LIMIT)


def _proj_perm():
    perm = np.full((D_PAD,), -1, np.int64)
    o_qa, o_kvc, o_kvs, o_kvw = 0, Q_A, Q_A + KV_A, Q_A + 2 * KV_A
    o_gl = Q_A + 3 * KV_A
    o_qb = o_gl + GATE_A
    o_fb = o_qb + Q_B
    o_ib = o_fb + Q_B
    o_gb = o_ib + I_B
    o_mg = o_gb + I_B
    perm[C_MG:C_MG + 2 * D_MODEL] = o_mg + np.arange(2 * D_MODEL)
    perm[C_QB:C_QB + Q_B] = o_qb + np.arange(Q_B)
    perm[C_FB:C_FB + Q_B] = o_fb + np.arange(Q_B)
    perm[C_QA:C_QA + Q_A] = o_qa + np.arange(Q_A)
    perm[C_IB:C_IB + I_B] = o_ib + np.arange(I_B)
    perm[C_GB:C_GB + I_B] = o_gb + np.arange(I_B)
    for new, old in ((C_KVC, o_kvc), (C_KVS, o_kvs), (C_KVW, o_kvw)):
        for g in range(KV_HEADS):
            for c in range(2):
                dst = new + g * 128 + c * 64
                src = old + c * 128 + g * 64
                perm[dst:dst + 64] = src + np.arange(64)
    for g in range(KV_HEADS):
        perm[C_GL + g * 128:C_GL + g * 128 + 12] = o_gl + g * 12 + np.arange(12)
    return perm


def _perm_runs(perm):
    runs = []
    for col in perm:
        if runs and ((col < 0 and runs[-1][0] < 0) or (col >= 0 and runs[-1][0] >= 0
                                                      and runs[-1][0] + runs[-1][1] == col)):
            runs[-1][1] += 1
        else:
            runs.append([int(col), 1])
    return [tuple(r) for r in runs]


_PERM_RUNS = _perm_runs(_proj_perm())


def _reorder_w_in(w):
    pieces = [jnp.zeros((w.shape[0], n), w.dtype) if s < 0 else w[:, s:s + n] for s, n in _PERM_RUNS]
    return jnp.concatenate(pieces, axis=1).astype(BF16)


def _proj_kernel(x_ref, w_ref, o_ref):
    xb = x_ref[...].astype(BF16)
    for j in range(D_PAD // PROJ_TN):
        cols = slice(j * PROJ_TN, (j + 1) * PROJ_TN)
        o_ref[:, cols] = jnp.dot(xb, w_ref[:, cols], preferred_element_type=F32)


def _proj_kvt_kernel(x_ref, w_ref, wt_ref, o_ref, kc_ref, ks_ref, kw_ref):
    xb = x_ref[...].astype(BF16)
    for j in range(D_PAD // PROJ_TN):
        cols = slice(j * PROJ_TN, (j + 1) * PROJ_TN)
        o_ref[:, cols] = jnp.dot(xb, w_ref[:, cols], preferred_element_type=F32)
    for j, kt_ref in enumerate((kc_ref, ks_ref, kw_ref)):
        kt_ref[0] = lax.dot_general(wt_ref[j * KV_A:(j + 1) * KV_A, :], xb, _NT, preferred_element_type=F32)


def _project_prompt(x2d, w_pad, w_kvt, b, t):
    tm = 512
    nt = t // tm
    kvt = jax.ShapeDtypeStruct((b, KV_A, t), F32)
    kvt_spec = pl.BlockSpec((1, KV_A, tm), lambda i: (i // nt, 0, i % nt))
    return pl.pallas_call(
        _proj_kvt_kernel,
        grid=(b * nt,),
        in_specs=[pl.BlockSpec((tm, D_MODEL), lambda i: (i, 0)),
                  pl.BlockSpec((D_MODEL, D_PAD), lambda i: (0, 0), pipeline_mode=pl.Buffered(1)),
                  pl.BlockSpec((3 * KV_A, D_MODEL), lambda i: (0, 0), pipeline_mode=pl.Buffered(1))],
        out_specs=[pl.BlockSpec((tm, D_PAD), lambda i: (i, 0)), kvt_spec, kvt_spec, kvt_spec],
        out_shape=[jax.ShapeDtypeStruct((b * t, D_PAD), F32), kvt, kvt, kvt],
        compiler_params=_cparams(("arbitrary",)),
        name="proj_prompt",
    )(x2d, w_pad, w_kvt)


def _project(x2d, w_pad):
    n = x2d.shape[0]
    tm = min(512, n)
    return pl.pallas_call(
        _proj_kernel,
        grid=(n // tm,),
        in_specs=[pl.BlockSpec((tm, D_MODEL), lambda i: (i, 0)),
                  pl.BlockSpec((D_MODEL, D_PAD), lambda i: (0, 0), pipeline_mode=pl.Buffered(1))],
        out_specs=pl.BlockSpec((tm, D_PAD), lambda i: (i, 0)),
        out_shape=jax.ShapeDtypeStruct((n, D_PAD), F32),
        compiler_params=_cparams(("arbitrary",)),
        name="proj",
    )(x2d, w_pad)


def _layer_norm(v, g, b):
    mu = jnp.mean(v, axis=-1, keepdims=True)
    d = v - mu
    var = jnp.mean(d * d, axis=-1, keepdims=True)
    return d * lax.rsqrt(var + LN_EPS) * g + b


FF_CHUNK = 1024


def _merge_mlp_kernel(x_ref, oa_ref, ob_ref, mga_ref, mgb_ref, wa_ref, wb_ref, wo_ref, g1_ref, b1_ref,
                      wu_ref, wd_ref, g2_ref, b2_ref, o_ref):
    br_a = jnp.dot(oa_ref[...].astype(BF16), wa_ref[...], preferred_element_type=F32)
    br_b = jnp.dot(ob_ref[...].astype(BF16), wb_ref[...], preferred_element_type=F32)
    merged = jax.nn.sigmoid(mga_ref[...]) * br_a + jax.nn.sigmoid(mgb_ref[...]) * br_b
    mix = jnp.dot(merged.astype(BF16), wo_ref[...], preferred_element_type=F32)
    x1 = _layer_norm(DEEPNORM_ALPHA * x_ref[...] + mix, g1_ref[...], b1_ref[...])
    x1b = x1.astype(BF16)
    acc = jnp.zeros(x1.shape, F32)
    for j in range(D_FF // FF_CHUNK):
        cols = slice(j * FF_CHUNK, (j + 1) * FF_CHUNK)
        u = jnp.maximum(jnp.dot(x1b, wu_ref[:, cols], preferred_element_type=F32), 0.0)
        acc += jnp.dot((u * u).astype(BF16), wd_ref[cols, :], preferred_element_type=F32)
    o_ref[...] = _layer_norm(DEEPNORM_ALPHA * x1 + acc, g2_ref[...], b2_ref[...])


def _merge_mlp(x2d, o_a, o_b, h, w_a, w_b, w_o, ln1_g, ln1_b, w_up, w_down, ln2_g, ln2_b):
    n = x2d.shape[0]
    tm = min(512, n)
    resident = lambda shape: pl.BlockSpec(shape, lambda i: (0, 0), pipeline_mode=pl.Buffered(1))
    return pl.pallas_call(
        _merge_mlp_kernel,
        grid=(n // tm,),
        in_specs=[pl.BlockSpec((tm, D_MODEL), lambda i: (i, 0)),
                  pl.BlockSpec((tm, Q_A), lambda i: (i, 0)),
                  pl.BlockSpec((tm, I_B), lambda i: (i, 0)),
                  pl.BlockSpec((tm, D_MODEL), lambda i: (i, C_MG // D_MODEL)),
                  pl.BlockSpec((tm, D_MODEL), lambda i: (i, C_MG // D_MODEL + 1)),
                  resident((Q_A, D_MODEL)), resident((I_B, D_MODEL)), resident((D_MODEL, D_MODEL)),
                  resident((1, D_MODEL)), resident((1, D_MODEL)),
                  resident((D_MODEL, D_FF)), resident((D_FF, D_MODEL)),
                  resident((1, D_MODEL)), resident((1, D_MODEL))],
        out_specs=pl.BlockSpec((tm, D_MODEL), lambda i: (i, 0)),
        out_shape=jax.ShapeDtypeStruct((n, D_MODEL), F32),
        compiler_params=_cparams(("arbitrary",)),
        name="merge_mlp",
    )(x2d, o_a, o_b, h, h, w_a, w_b, w_o, ln1_g, ln1_b, w_up, w_down, ln2_g, ln2_b)


def _dot01(a01, x):
    hi = x.astype(BF16)
    r1 = x - hi.astype(F32)
    mid = r1.astype(BF16)
    lo = (r1 - mid.astype(F32)).astype(BF16)
    out = jnp.dot(a01, hi, preferred_element_type=F32)
    out += jnp.dot(a01, mid, preferred_element_type=F32)
    out += jnp.dot(a01, lo, preferred_element_type=F32)
    return out


def _x_dot01(x, b01):
    hi = x.astype(BF16)
    r1 = x - hi.astype(F32)
    mid = r1.astype(BF16)
    lo = (r1 - mid.astype(F32)).astype(BF16)
    out = jnp.dot(hi, b01, preferred_element_type=F32)
    out += jnp.dot(mid, b01, preferred_element_type=F32)
    out += jnp.dot(lo, b01, preferred_element_type=F32)
    return out


_NT = (((1,), (1,)), ((), ()))
_TN = (((0,), (0,)), ((), ()))


def _cmp_prompt_kernel(kv_ref, w_ref, posf_ref, wb_ref, o_ref, *, n_ch):
    acc = jnp.zeros((n_ch, 256), F32)
    for l in range(CMP_STRIDE):
        xl = kv_ref[pl.ds(l, n_ch, stride=CMP_STRIDE), :].astype(BF16)
        acc += jnp.dot(xl, w_ref[l], preferred_element_type=F32)
    bias = jnp.dot(posf_ref[...].astype(BF16), wb_ref[...], preferred_element_type=F32)[0:1]
    nxt = pltpu.roll(acc[:, 128:], n_ch - 1, 0)
    o_ref[0] = (acc[:, :128] + nxt + bias).astype(BF16)


def _cmp_weights(cmp_w, cmp_pos):
    w = cmp_w.reshape(2, 2, CMP_STRIDE, HEAD_DIM, HEAD_DIM)
    eye = jnp.eye(2, dtype=F32)
    wl = jnp.transpose(w, (2, 0, 3, 1, 4))[:, :, :, :, None, :] * eye[None, :, None, None, :, None]
    wb = cmp_w.reshape(2, CMP_LEN * HEAD_DIM, 1, HEAD_DIM) * eye[:, None, :, None]
    wl = wl.reshape(CMP_STRIDE, 128, 256).astype(BF16)
    wb = wb.reshape(2 * CMP_LEN * HEAD_DIM, 128).astype(BF16)
    posf = jnp.broadcast_to(cmp_pos.reshape(1, 2 * CMP_LEN * HEAD_DIM), (8, 2 * CMP_LEN * HEAD_DIM))
    return wl, wb, posf


def _cmp_prompt(h, b, t, wl, wb, posf):
    n_ch = t // CMP_STRIDE
    return pl.pallas_call(
        functools.partial(_cmp_prompt_kernel, n_ch=n_ch),
        grid=(b, KV_HEADS),
        in_specs=[pl.BlockSpec((t, 128), lambda i, g: (i, C_KVC // 128 + g)),
                  pl.BlockSpec((CMP_STRIDE, 128, 256), lambda i, g: (0, 0, 0)),
                  pl.BlockSpec((8, 4096), lambda i, g: (0, 0)),
                  pl.BlockSpec((4096, 128), lambda i, g: (0, 0))],
        out_specs=pl.BlockSpec((1, n_ch, 128), lambda i, g: (i, 0, g)),
        out_shape=jax.ShapeDtypeStruct((b, n_ch, 256), BF16),
        compiler_params=_cparams(("arbitrary", "arbitrary")),
        name="cmp_prompt",
    )(h, wl, posf, wb)


QB = 256
PIECE = 256
SEL_UNROLL = 2
KT = 256
WT = 128


ROWS = GROUP * QB
TL_COUNT = 64


def _head_major(q):
    return jnp.concatenate([q[:, r * 64:(r + 1) * 64] for r in range(GROUP)], axis=0) * ATTN_SCALE


def _head_slope(g, r):
    return jnp.where(g == 0, 2.0 ** -(r + 1), 2.0 ** -(r + 1 + GROUP)).astype(F32)


def _slope_row(g):
    r_ix = lax.broadcasted_iota(jnp.int32, (1, ROWS), 1) // QB
    row = jnp.zeros((1, ROWS), F32)
    for r in range(GROUP):
        row = jnp.where(r_ix == r, _head_slope(g, r), row)
    return row


def _nsa_select_kernel(q_ref, kc_ref, oc_ref, un_ref, tl_ref, *, n_ch):
    g = pl.program_id(1)
    c = pl.program_id(2)
    t0 = c * QB
    qpad = jnp.concatenate([_head_major(q_ref[...]), jnp.zeros((ROWS, 64), F32)], axis=1).astype(BF16)
    kc = kc_ref[0]
    s = lax.dot_general(kc, qpad, _NT, preferred_element_type=F32)
    pos_row = t0 + lax.broadcasted_iota(jnp.int32, (1, ROWS), 1) % QB
    n_col = lax.broadcasted_iota(jnp.int32, (n_ch, 1), 0)
    dist_i = pos_row - (n_col * CMP_STRIDE + (CMP_LEN - 1))
    valid = jnp.logical_and(dist_i >= 0, n_col < n_ch - 1)
    s = jnp.where(valid, s - _slope_row(g) * dist_i.astype(F32), NEG)
    m = jnp.max(s, axis=0, keepdims=True)
    p = jnp.exp(s - m)
    inv = jnp.where(m > 0.5 * NEG, 1.0 / jnp.maximum(jnp.sum(p, axis=0, keepdims=True), 1e-30), 0.0)
    p = p * inv
    o_c = lax.dot_general(p.astype(BF16), kc, _TN, preferred_element_type=F32)
    for r in range(GROUP):
        oc_ref[:, r * 64:(r + 1) * 64] = o_c[r * QB:(r + 1) * QB, 64:128]

    imp = p[:, 0:QB]
    for r in range(1, GROUP):
        imp = imp + p[:, r * QB:(r + 1) * QB]
    ratio = SEL_BLOCK // CMP_STRIDE
    pool_t = (lax.broadcasted_iota(jnp.int32, (128, n_ch), 1) // ratio
              == lax.broadcasted_iota(jnp.int32, (128, n_ch), 0)).astype(BF16)
    imp_blk = _dot01(pool_t, imp)
    blk = lax.broadcasted_iota(jnp.int32, (128, QB), 0)
    pos_q = t0 + lax.broadcasted_iota(jnp.int32, (1, QB), 1)
    cur = pos_q // SEL_BLOCK
    forced = jnp.logical_or(jnp.logical_or(blk == 0, blk == cur), blk == cur - 1)
    allowed = blk * SEL_BLOCK <= pos_q
    work = jnp.where(jnp.logical_and(allowed, jnp.logical_not(forced)), imp_blk, -jnp.inf)
    sel = forced
    for _ in range(N_SELECT - 3):
        mx = jnp.max(work, axis=0, keepdims=True)
        first = jnp.min(jnp.where(work == mx, blk, 128), axis=0, keepdims=True)
        pick = blk == first
        sel = jnp.logical_or(sel, pick)
        work = jnp.where(pick, -jnp.inf, work)
    sel = jnp.logical_and(sel, allowed)
    un_ref[...] = jnp.transpose(jnp.where(sel, 0.0, -UNSEL)).astype(BF16)

    tile_of = (lax.broadcasted_iota(jnp.int32, (128, 128), 1) // (KT // SEL_BLOCK)
               == lax.broadcasted_iota(jnp.int32, (128, 128), 0)).astype(BF16)
    sel_b = jnp.where(jnp.logical_and(sel, blk > 0), 1.0, 0.0).astype(BF16)
    cnt = jnp.sum(jnp.dot(tile_of, sel_b, preferred_element_type=F32), axis=1, keepdims=True)
    tile_col = lax.broadcasted_iota(jnp.int32, (128, 1), 0)
    flag = jnp.logical_and(cnt > 0.0, tile_col < t0 // KT)
    flag_m = jnp.where(jnp.broadcast_to(flag, (128, 128)), 1.0, 0.0).astype(BF16)
    row_i = lax.broadcasted_iota(jnp.int32, (128, 128), 0)
    lane_i = lax.broadcasted_iota(jnp.int32, (128, 128), 1)
    before = jnp.dot((lane_i < row_i).astype(BF16), flag_m, preferred_element_type=F32)
    slot = jnp.where(jnp.logical_and(flag, before == lane_i.astype(F32)), 1.0, 0.0).astype(BF16)
    j_rows = lax.broadcasted_iota(jnp.int32, (8, 128), 1).astype(BF16)
    listed = jnp.dot(j_rows, slot, preferred_element_type=F32)
    total = jnp.dot(jnp.ones((8, 128), BF16), flag_m, preferred_element_type=F32)
    lane8 = lax.broadcasted_iota(jnp.int32, (8, 128), 1)
    tl_ref[0] = jnp.where(lane8 == TL_COUNT, total, listed).astype(jnp.int32)


def _nsa_select(h, kcmp, b, t):
    n_ch = t // CMP_STRIDE
    nq = t // QB
    steps = b * KV_HEADS * nq
    return pl.pallas_call(
        functools.partial(_nsa_select_kernel, n_ch=n_ch),
        grid=(b, KV_HEADS, nq),
        in_specs=[pl.BlockSpec((QB, 256), lambda i, g, c: (i * nq + c, C_QA // 256 + g)),
                  pl.BlockSpec((1, n_ch, 128), lambda i, g, c: (i, 0, g))],
        out_specs=[pl.BlockSpec((QB, 256), lambda i, g, c: (i * nq + c, g)),
                   pl.BlockSpec((QB, 128), lambda i, g, c: (i * nq + c, g)),
                   pl.BlockSpec((1, 8, 128), lambda i, g, c: ((i * KV_HEADS + g) * nq + c, 0, 0))],
        out_shape=[jax.ShapeDtypeStruct((b * t, Q_A), F32),
                   jax.ShapeDtypeStruct((b * t, KV_HEADS * 128), BF16),
                   jax.ShapeDtypeStruct((steps, 8, 128), jnp.int32)],
        compiler_params=_cparams(("arbitrary", "arbitrary", "arbitrary")),
        name="nsa_select",
    )(h, kcmp)


def _nsa_attend_kernel(tl_ref, q_ref, un_ref, gl_ref, oc_ref, ks_ref, kw_ref, o_ref,
                       ksb_ref, ksa_ref, kwb_ref, kwa_ref, *, nq):
    i = pl.program_id(0)
    g = pl.program_id(1)
    c = pl.program_id(2)
    step = (i * KV_HEADS + g) * nq + c
    t0 = c * QB
    half = PIECE
    n_piece = ROWS // PIECE
    qs = _head_major(q_ref[...]) * LOG2E
    s2 = jnp.concatenate([jnp.broadcast_to(_head_slope(g, r) * LOG2E, (QB, 1)) for r in range(GROUP)], axis=0)
    s2_hi = s2.astype(BF16).astype(F32)
    s2_lo = s2 - s2_hi
    lane64 = lax.broadcasted_iota(jnp.int32, (ROWS, 64), 1)
    ali_q = jnp.where(lane64 == 0, -64.0 * s2_hi, jnp.where(lane64 == 1, -s2_hi,
                      jnp.where(lane64 == 2, -64.0 * s2_lo, jnp.where(lane64 == 3, -s2_lo, 0.0))))
    q_win = jnp.concatenate([qs, ali_q], axis=1).astype(BF16)
    first_blk = jnp.logical_and(lax.broadcasted_iota(jnp.int32, (QB, 128), 1) == 0, c > 0)
    un = jnp.where(first_blk, jnp.asarray(-UNSEL, BF16), un_ref[...])
    q_sel =jnp.concatenate([q_win, jnp.concatenate([un] * GROUP, axis=0)], axis=1)
    slope_row = _slope_row(g) * LOG2E

    @pl.when(c == 0)
    def _():
        def fill(j, carry):
            r0 = pl.multiple_of(j * KT, KT)
            row = lax.broadcasted_iota(jnp.int32, (KT, 128), 0)
            lane = lax.broadcasted_iota(jnp.int32, (KT, 128), 1)
            for src, dst_kv, dst_aug, tile in ((ks_ref, ksb_ref, ksa_ref, KT), (kw_ref, kwb_ref, kwa_ref, WT)):
                kv_b = src[pl.ds(r0, KT), :].astype(BF16)
                d = (QB - 1) - row % tile
                ali = jnp.where(lane // 4 == 16, jnp.where(lane % 2 == 0, d >> 6, d & 63), 0).astype(F32).astype(BF16)
                dst_kv[pl.ds(r0, KT), :] = kv_b
                dst_aug[pl.ds(r0, KT), 0:128] = jnp.where(lane < 64, kv_b, ali)
            ksa_ref[pl.ds(r0, KT), 128:256] = jnp.where((r0 + row) // SEL_BLOCK == lane, 1.0, 0.0).astype(BF16)
            return carry

        lax.fori_loop(0, ks_ref.shape[0] // KT, fill, 0)

    def update(carry, s, kv_b, offset):
        out = []
        for hh, ((m, l, acc), sh) in enumerate(zip(carry, s)):
            shift = slope_row[:, hh * half:(hh + 1) * half] * offset
            m_new = jnp.maximum(m, jnp.max(sh, axis=0, keepdims=True) - shift)
            alpha = jnp.exp2(m - m_new)
            p = jnp.exp2(sh - (m_new + shift))
            l = alpha * l + jnp.sum(p, axis=0, keepdims=True)
            acc = alpha * acc + lax.dot_general(kv_b, p.astype(BF16), _TN, preferred_element_type=F32)
            out.append((m_new, l, acc))
        return tuple(out)

    def init():
        return tuple((jnp.full((1, half), NEG, F32), jnp.zeros((1, half), F32), jnp.zeros((128, half), F32))
                     for _ in range(n_piece))

    def tile(carry, aug_ref, kvb_ref, q_side, k0, n_keys, keep):
        k_aug = aug_ref[pl.ds(k0, n_keys), :]
        s = []
        for hh in range(n_piece):
            sh = lax.dot_general(k_aug, q_side[hh * half:(hh + 1) * half], _NT, preferred_element_type=F32)
            if keep is not None:
                sh = jnp.where(jnp.concatenate([keep] * (half // QB), axis=1), sh, NEG)
            s.append(sh)
        return update(carry, s, kvb_ref[pl.ds(k0, n_keys), :], (t0 - k0).astype(F32))

    k_tail = pl.multiple_of((t0 // KT) * KT, KT)
    key_x = lax.broadcasted_iota(jnp.int32, (KT, QB), 0)
    q_x = lax.broadcasted_iota(jnp.int32, (KT, QB), 1)
    carry = tile(init(), ksa_ref, ksb_ref, q_sel, k_tail, KT, k_tail + key_x <= t0 + q_x)

    def sel_listed(n, carry):
        return tile(carry, ksa_ref, ksb_ref, q_sel, pl.multiple_of(tl_ref[step, n] * KT, KT), KT, None)

    n_listed = tl_ref[step, TL_COUNT]

    def sel_group(n, cr):
        for u in range(SEL_UNROLL):
            cr = sel_listed(SEL_UNROLL * n + u, cr)
        return cr

    carry = lax.fori_loop(0, n_listed // SEL_UNROLL, sel_group, carry)
    carry = lax.fori_loop(n_listed - n_listed % SEL_UNROLL, n_listed, sel_listed, carry)
    k_first = ksa_ref[0:SEL_BLOCK, 0:128]
    s_first = [jnp.where(c > 0, lax.dot_general(k_first, q_win[hh * half:(hh + 1) * half], _NT,
                                                 preferred_element_type=F32), NEG) for hh in range(n_piece)]
    carry = update(carry, s_first, ksb_ref[0:SEL_BLOCK, :], t0.astype(F32))
    o_sel = [acc * (1.0 / l) for (_, l, acc) in carry]

    w_keys = WINDOW + QB
    k0w = pl.multiple_of(jnp.maximum(t0 - WINDOW, 0), WT)
    k_aug = kwa_ref[pl.ds(k0w, w_keys), :]
    kv_w = kwb_ref[pl.ds(k0w, w_keys), :]
    dq = lax.broadcasted_iota(jnp.int32, (WT, QB), 1) - lax.broadcasted_iota(jnp.int32, (WT, QB), 0)
    offs = [t0 - (k0w + gi * WT) for gi in range(w_keys // WT)]
    keeps = [jnp.concatenate([jnp.logical_and(dq + off >= 0, dq + off < WINDOW)] * (half // QB), axis=1)
             for off in offs]
    o_win = []
    for hh in range(n_piece):
        sl = slope_row[:, hh * half:(hh + 1) * half]
        sh = lax.dot_general(k_aug, q_win[hh * half:(hh + 1) * half], _NT, preferred_element_type=F32)
        parts = [jnp.where(keep, sh[gi * WT:(gi + 1) * WT], NEG) for gi, keep in enumerate(keeps)]
        shifts = [sl * off.astype(F32) for off in offs]
        m_w = functools.reduce(jnp.maximum, [jnp.max(pt, axis=0, keepdims=True) - sf
                                             for pt, sf in zip(parts, shifts)])
        ps = [jnp.exp2(pt - (m_w + sf)) for pt, sf in zip(parts, shifts)]
        l_w = functools.reduce(jnp.add, [jnp.sum(p, axis=0, keepdims=True) for p in ps])
        acc = lax.dot_general(kv_w, jnp.concatenate(ps, axis=0).astype(BF16), _TN, preferred_element_type=F32)
        o_win.append(acc * (1.0 / l_w))

    gates = jax.nn.sigmoid(gl_ref[...])
    o_c = oc_ref[...]
    for r in range(GROUP):
        hh, cs = divmod(r * QB, half)
        o_s = jnp.transpose(o_sel[hh][:, cs:cs + QB])[:, 64:128]
        o_w = jnp.transpose(o_win[hh][:, cs:cs + QB])[:, 64:128]
        o_ref[:, r * 64:(r + 1) * 64] = (gates[:, 3 * r:3 * r + 1] * o_c[:, r * 64:(r + 1) * 64]
                                         + gates[:, 3 * r + 1:3 * r + 2] * o_s
                                         + gates[:, 3 * r + 2:3 * r + 3] * o_w)


def _nsa_attend(tiles, h, unsel, o_c, b, t):
    nq = t // QB
    grid_spec = pltpu.PrefetchScalarGridSpec(
        num_scalar_prefetch=1,
        grid=(b, KV_HEADS, nq),
        in_specs=[pl.BlockSpec((QB, 256), lambda i, g, c, tl: (i * nq + c, C_QA // 256 + g)),
                  pl.BlockSpec((QB, 128), lambda i, g, c, tl: (i * nq + c, g)),
                  pl.BlockSpec((QB, 128), lambda i, g, c, tl: (i * nq + c, C_GL // 128 + g)),
                  pl.BlockSpec((QB, 256), lambda i, g, c, tl: (i * nq + c, g)),
                  pl.BlockSpec((t, 128), lambda i, g, c, tl: (i, C_KVS // 128 + g)),
                  pl.BlockSpec((t, 128), lambda i, g, c, tl: (i, C_KVW // 128 + g))],
        out_specs=pl.BlockSpec((QB, 256), lambda i, g, c, tl: (i * nq + c, g)),
        scratch_shapes=[pltpu.VMEM((t, 128), BF16), pltpu.VMEM((t, 256), BF16),
                        pltpu.VMEM((t, 128), BF16), pltpu.VMEM((t, 128), BF16)],
    )
    return pl.pallas_call(
        functools.partial(_nsa_attend_kernel, nq=nq),
        grid_spec=grid_spec,
        out_shape=jax.ShapeDtypeStruct((b * t, Q_A), F32),
        compiler_params=_cparams(("arbitrary", "arbitrary", "arbitrary")),
        name="nsa_attend",
    )(tiles, h, unsel, h, o_c, h, h)


def _nsa_prompt(h, kcmp, b, t):
    o_c, unsel, tl = _nsa_select(h, kcmp, b, t)
    return _nsa_attend(tl[:, 0, :], h, unsel, o_c, b, t)


HB = 256


def _lower_bound(lb_ref):
    z = lb_ref[...]
    e = jnp.exp(z - jnp.max(z, axis=0, keepdims=True))
    return e[0:1] / jnp.sum(e, axis=0, keepdims=True)


def _hgrn_prompt_kernel(q_ref, f_ref, i_ref, g_ref, lb_ref, ng_ref, o_ref, s_out_ref, s_ref):
    tb = pl.program_id(1)

    @pl.when(tb == 0)
    def _():
        s_ref[...] = jnp.zeros_like(s_ref)

    n_c = HB // HGRN_CHUNK
    lb = _lower_bound(lb_ref)
    f = lb + (1.0 - lb) * jax.nn.sigmoid(f_ref[...])
    log_f = jnp.log(f)
    row = lax.broadcasted_iota(jnp.int32, (HB, HB), 0)
    col = lax.broadcasted_iota(jnp.int32, (HB, HB), 1)
    tril = jnp.logical_and(col <= row, row // HGRN_CHUNK == col // HGRN_CHUNK)
    lc = _dot01(tril.astype(BF16), log_f)
    lc3 = lc.reshape(n_c, HGRN_CHUNK, Q_B)
    dec8 = jnp.exp(lc3[:, HGRN_CHUNK - 1, :])
    dec = jnp.broadcast_to(dec8[:, None, :], lc3.shape).reshape(HB, Q_B)
    q_t = (q_ref[...] * jnp.exp(lc)).astype(BF16)
    k_f = (1.0 - f) * jnp.exp(-lc)
    k_t = k_f.astype(BF16)
    k_e = (k_f * dec).astype(BF16)
    v_all = i_ref[...]
    gate = g_ref[...]
    ng = ng_ref[...]
    for hh in range(B_HEADS):
        ks = slice(hh * B_DK, (hh + 1) * B_DK)
        vs = slice(hh * B_DV, (hh + 1) * B_DV)
        v = v_all[:, vs].astype(BF16)
        a = lax.dot_general(q_t[:, ks], k_t[:, ks], _NT, preferred_element_type=F32)
        a = jnp.where(tril, a, 0.0).astype(BF16)
        o = jnp.dot(a, v, preferred_element_type=F32)
        st = s_ref[hh]
        inter = []
        for cc in range(n_c):
            rs = slice(cc * HGRN_CHUNK, (cc + 1) * HGRN_CHUNK)
            inter.append(lax.dot_general(q_t[rs, ks], st.astype(BF16), _NT, preferred_element_type=F32))
            u = lax.dot_general(v[rs], k_e[rs, ks], _TN, preferred_element_type=F32)
            st = dec8[cc:cc + 1, ks] * st + u
        s_ref[hh] = st
        o = o + jnp.concatenate(inter, axis=0)
        o = o * lax.rsqrt(jnp.mean(o * o, axis=-1, keepdims=True) + LN_EPS) * ng
        gt = gate[:, vs]
        o_ref[:, vs] = o * (gt * jax.nn.sigmoid(gt))

    @pl.when(tb == pl.num_programs(1) - 1)
    def _():
        s_out_ref[0] = s_ref[...]


def _hgrn_prompt(h, b, t, lb_logits, norm_g):
    nt = t // HB
    return pl.pallas_call(
        _hgrn_prompt_kernel,
        grid=(b, nt),
        in_specs=[pl.BlockSpec((HB, Q_B), lambda i, j: (i * nt + j, C_QB // Q_B)),
                  pl.BlockSpec((HB, Q_B), lambda i, j: (i * nt + j, C_FB // Q_B)),
                  pl.BlockSpec((HB, I_B), lambda i, j: (i * nt + j, C_IB // I_B)),
                  pl.BlockSpec((HB, I_B), lambda i, j: (i * nt + j, C_GB // I_B)),
                  pl.BlockSpec((2, Q_B), lambda i, j: (0, 0)),
                  pl.BlockSpec((1, B_DV), lambda i, j: (0, 0))],
        out_specs=[pl.BlockSpec((HB, I_B), lambda i, j: (i * nt + j, 0)),
                   pl.BlockSpec((1, B_HEADS, B_DV, B_DK), lambda i, j: (i, 0, 0, 0))],
        out_shape=[jax.ShapeDtypeStruct((b * t, I_B), F32),
                   jax.ShapeDtypeStruct((b, B_HEADS, B_DV, B_DK), F32)],
        scratch_shapes=[pltpu.VMEM((B_HEADS, B_DV, B_DK), F32)],
        compiler_params=_cparams(("arbitrary", "arbitrary")),
        name="hgrn_prompt",
    )(h, h, h, h, lb_logits, norm_g)


def _prompt_layer(x, w_pad, w_kvt, cmpw, lb_logits, norm_g, w_a, w_b, w_o, ln1_g, ln1_b, w_up, w_down,
                  ln2_g, ln2_b):
    b, t, _ = x.shape
    x2d = x.reshape(b * t, D_MODEL)
    h, *kv_t = _project_prompt(x2d, w_pad, w_kvt, b, t)
    kcmp = _cmp_prompt(h, b, t, *cmpw)
    o_a = _nsa_prompt(h, kcmp, b, t)
    o_b, s_end = _hgrn_prompt(h, b, t, lb_logits, norm_g)
    y = _merge_mlp(x2d, o_a, o_b, h, w_a, w_b, w_o, ln1_g, ln1_b, w_up, w_down, ln2_g, ln2_b)
    kv_out = [jnp.moveaxis(a.reshape(b, 2, KV_HEADS, HEAD_DIM, t), -1, 1) for a in kv_t]
    return y.reshape(b, t, D_MODEL), kv_out, jnp.swapaxes(s_end, 2, 3)


def _kv_out(h, col, b, t):
    kv = h[:, col:col + 256].reshape(b, t, KV_HEADS, 2, HEAD_DIM)
    return jnp.swapaxes(kv, 2, 3)


LAND_PAGES = 8
CH_PER_PAGE = PAGE_SIZE // CMP_STRIDE
BLK_PER_PAGE = PAGE_SIZE // SEL_BLOCK
N_POOL_SEL = N_SELECT - 1


def _head_rows(q_row, g):
    rows = [q_row[:, g * 256 + r * 64:g * 256 + (r + 1) * 64] for r in range(GROUP)]
    return jnp.concatenate(rows + [jnp.zeros((8 - GROUP, 64), F32)], axis=0) * ATTN_SCALE


def _slope_col(g):
    row = lax.broadcasted_iota(jnp.int32, (8, 1), 0)
    col = jnp.zeros((8, 1), F32)
    for r in range(GROUP):
        col = jnp.where(row == r, 2.0 ** -(g * GROUP + r + 1), col)
    return col


def _cmp_sample_kernel(pt_ref, cache_ref, w_ref, posf_ref, wb_ref, q_ref, oc_ref, idx_ref,
                       buf_ref, xk_ref, xv_ref, bias_ref, sem, *, n_pages, past):
    b = pl.program_id(0)
    n_ch = n_pages * CH_PER_PAGE
    n_blk = past // SEL_BLOCK
    lanes = -(-n_blk // 128) * 128

    def page_copy(pg, page):
        return pltpu.make_async_copy(cache_ref.at[page], buf_ref.at[pg], sem.at[pg])

    def issue_all(req):
        def issue(pg, carry):
            page_copy(pg, pt_ref[req, pg]).start()
            return carry

        lax.fori_loop(0, n_pages, issue, 0)

    def land(i, carry):
        pages = [i * LAND_PAGES + u for u in range(LAND_PAGES)]
        for pg in pages:
            page_copy(pg, 0).wait()
        for pg in pages:
            r0 = pl.multiple_of(pg * PAGE_SIZE, PAGE_SIZE)
            xk_ref[pl.ds(r0, PAGE_SIZE), :] = jnp.transpose(buf_ref[pg, 0].reshape(128, PAGE_SIZE))
            xv_ref[pl.ds(r0, PAGE_SIZE), :] = jnp.transpose(buf_ref[pg, 1].reshape(128, PAGE_SIZE))
        return carry

    @pl.when(b == 0)
    def _():
        issue_all(b)

    lax.fori_loop(0, n_pages // LAND_PAGES, land, 0)

    @pl.when(b + 1 < pl.num_programs(0))
    def _():
        issue_all(b + 1)

    @pl.when(b == 0)
    def _():
        bias_ref[...] = jnp.dot(posf_ref[...].astype(BF16), wb_ref[...], preferred_element_type=F32)

    bias = bias_ref[0:1]
    blocks = []
    for c, x_ref in enumerate((xk_ref, xv_ref)):
        acc = jnp.zeros((n_ch, 256), F32)
        for l2 in range(CMP_STRIDE // 2):
            xl = jnp.concatenate([x_ref[pl.ds(2 * l2 + u, n_ch, stride=CMP_STRIDE), :] for u in range(2)],
                                 axis=1).astype(BF16)
            acc += jnp.dot(xl, w_ref[c, l2], preferred_element_type=F32)
        bias_c = jnp.concatenate([bias[:, c * 64:(c + 1) * 64]] * KV_HEADS, axis=1)
        blocks.append((acc[:, :128] + pltpu.roll(acc[:, 128:], n_ch - 1, 0) + bias_c).astype(BF16))
    kc_all, vc_all = blocks

    q_row = q_ref[0]
    n_ix = lax.broadcasted_iota(jnp.int32, (1, n_ch), 1)
    dist_i = past - (n_ix * CMP_STRIDE + (CMP_LEN - 1))
    valid = jnp.logical_and(dist_i >= 0, n_ix < n_ch - 1)
    dist = dist_i.astype(F32)
    ratio = SEL_BLOCK // CMP_STRIDE
    pool = (lax.broadcasted_iota(jnp.int32, (n_ch, lanes), 0) // ratio
            == lax.broadcasted_iota(jnp.int32, (n_ch, lanes), 1)).astype(BF16)
    row8 = lax.broadcasted_iota(jnp.int32, (8, lanes), 0)
    lane_g = lax.broadcasted_iota(jnp.int32, (8, 128), 1) // HEAD_DIM
    zeros4 = jnp.zeros((GROUP, 64), F32)
    q8 = jnp.concatenate([jnp.concatenate([_head_rows(q_row, 0)[0:GROUP], zeros4], axis=1),
                          jnp.concatenate([zeros4, _head_rows(q_row, 1)[0:GROUP]], axis=1)], axis=0).astype(BF16)
    s = lax.dot_general(q8, kc_all, _NT, preferred_element_type=F32)
    head = lax.broadcasted_iota(jnp.int32, (8, 1), 0)
    slope8 = jnp.zeros((8, 1), F32)
    for h in range(A_HEADS):
        slope8 = jnp.where(head == h, 2.0 ** -(h + 1), slope8)
    s = jnp.where(valid, s - slope8 * dist, NEG)
    m = jnp.max(s, axis=-1, keepdims=True)
    p = jnp.where(valid, jnp.exp(s - m), 0.0)
    p = p / jnp.maximum(jnp.sum(p, axis=-1, keepdims=True), 1e-30)
    o8 = jnp.dot(p.astype(BF16), vc_all, preferred_element_type=F32)
    o_c = jnp.where(lane_g == 0, o8, pltpu.roll(o8, GROUP, 0))
    imp8 = jnp.concatenate([p[0:1] + p[1:2] + p[2:3] + p[3:4], p[4:5] + p[5:6] + p[6:7] + p[7:8],
                            jnp.zeros((8 - KV_HEADS, n_ch), F32)], axis=0)
    score = jnp.where(row8 < KV_HEADS, _x_dot01(imp8, pool), -FORCE)
    lane = lax.broadcasted_iota(jnp.int32, (8, lanes), 1)
    forced = jnp.logical_or(lane == 0, lane == n_blk - 1)
    score = jnp.where(lane < n_blk, jnp.where(forced, FORCE, score), -jnp.inf)
    sc_t = jnp.transpose(jnp.concatenate([score, jnp.full((120, lanes), -jnp.inf, F32)], axis=0))
    blk = lax.broadcasted_iota(jnp.int32, (lanes, 128), 0)
    picks = []
    for r in range(N_POOL_SEL):
        m = jnp.max(sc_t, axis=0, keepdims=True)
        first = jnp.min(jnp.where(sc_t == m, blk, lanes), axis=0, keepdims=True)
        picks.append(first)
        sc_t = jnp.where(blk == first, -jnp.inf, sc_t)
    idx_ref[0] = jnp.concatenate(picks + [jnp.zeros((N_SELECT - N_POOL_SEL, 128), jnp.int32)], axis=0)
    oc_ref[0] = o_c


def _cmp_sample(page_table, cache5, w_s, posf, wb, q3, past):
    bsz, n_pages = page_table.shape
    grid_spec = pltpu.PrefetchScalarGridSpec(
        num_scalar_prefetch=1,
        grid=(bsz,),
        in_specs=[pl.BlockSpec(memory_space=pl.ANY),
                  pl.BlockSpec((2, CMP_STRIDE // 2, 256, 256), lambda i, pt: (0, 0, 0, 0)),
                  pl.BlockSpec((8, 4096), lambda i, pt: (0, 0)),
                  pl.BlockSpec((4096, 128), lambda i, pt: (0, 0)),
                  pl.BlockSpec((1, 1, Q_A), lambda i, pt: (i, 0, 0))],
        out_specs=[pl.BlockSpec((1, 8, 128), lambda i, pt: (i, 0, 0)),
                   pl.BlockSpec((1, N_SELECT, 128), lambda i, pt: (i, 0, 0))],
        scratch_shapes=[pltpu.VMEM((n_pages, 2, KV_HEADS, HEAD_DIM, PAGE_SIZE), F32),
                        pltpu.VMEM((n_pages * PAGE_SIZE, 128), F32),
                        pltpu.VMEM((n_pages * PAGE_SIZE, 128), F32),
                        pltpu.VMEM((8, 128), F32),
                        pltpu.SemaphoreType.DMA((n_pages,))],
    )
    return pl.pallas_call(
        functools.partial(_cmp_sample_kernel, n_pages=n_pages, past=past),
        grid_spec=grid_spec,
        out_shape=[jax.ShapeDtypeStruct((bsz, 8, 128), F32),
                   jax.ShapeDtypeStruct((bsz, N_SELECT, 128), jnp.int32)],
        compiler_params=_cparams(("arbitrary",)),
        name="cmp_sample",
    )(page_table, cache5, w_s, posf, wb, q3)


def _pick_lane(mat, lane, target):
    return jnp.sum(jnp.where(lane == target, mat, 0.0), axis=-1, keepdims=True)


def _sel_win_sample_kernel(pt_ref, idx_ref, cache_ref, win_ref, q_ref, ks_ref, kw_ref, gl_ref, oc_ref, o_ref,
                           kbuf_ref, vbuf_ref, sem, *, past):
    b = pl.program_id(0)
    slot_lanes = N_SELECT * PAGE_SIZE

    def page_copies(g, k, page):
        dst = pl.ds(k * PAGE_SIZE, PAGE_SIZE)
        return (pltpu.make_async_copy(cache_ref.at[page, 0, g], kbuf_ref.at[g, :, dst], sem),
                pltpu.make_async_copy(cache_ref.at[page, 1, g], vbuf_ref.at[g, :, dst], sem))

    for g in range(KV_HEADS):
        for k in range(N_POOL_SEL):
            page = pt_ref[b, idx_ref[b, g * N_SELECT + k] // BLK_PER_PAGE]
            for cp in page_copies(g, k, page):
                cp.start()
        pad = pl.ds(N_POOL_SEL * PAGE_SIZE, PAGE_SIZE)
        kbuf_ref[g, :, pad] = jnp.zeros((HEAD_DIM, PAGE_SIZE), F32)
        vbuf_ref[g, :, pad] = jnp.zeros((HEAD_DIM, PAGE_SIZE), F32)

    q_row = q_ref[0]
    ks_new = ks_ref[0]
    kw_new = kw_ref[0]
    gl_all = jax.nn.sigmoid(gl_ref[0])
    lane128 = lax.broadcasted_iota(jnp.int32, (8, 128), 1)
    row8 = lax.broadcasted_iota(jnp.int32, (8, 1), 0)
    w_len = win_ref.shape[-1]

    def two_piece(qh, slope, k_t, v_t, dist_i, valid, k_new, v_new):
        s = jnp.dot(qh.astype(BF16), k_t.astype(BF16), preferred_element_type=F32)
        s = jnp.where(valid, s - slope * dist_i.astype(F32), NEG)
        s_n = jnp.sum(qh * k_new, axis=-1, keepdims=True)
        m = jnp.maximum(jnp.max(s, axis=-1, keepdims=True), s_n)
        p = jnp.where(valid, jnp.exp(s - m), 0.0)
        p_n = jnp.exp(s_n - m)
        l = jnp.sum(p, axis=-1, keepdims=True) + p_n
        o = lax.dot_general(p.astype(BF16), v_t.astype(BF16), _NT, preferred_element_type=F32) + p_n * v_new
        return o / l

    outs = []
    for g in range(KV_HEADS):
        qh = _head_rows(q_row, g)
        slope = _slope_col(g)
        j_ix = lax.broadcasted_iota(jnp.int32, (1, w_len), 1)
        dist_w = w_len - j_ix
        o_w = two_piece(qh, slope, win_ref[0, 0, g], win_ref[0, 1, g], dist_w, dist_w < WINDOW,
                        kw_new[:, g * 128:g * 128 + 64], kw_new[:, g * 128 + 64:(g + 1) * 128])
        outs.append((qh, slope, o_w))

    for g in range(KV_HEADS):
        for k in range(N_POOL_SEL):
            for cp in page_copies(g, k, 0):
                cp.wait()

    lane_s = lax.broadcasted_iota(jnp.int32, (1, slot_lanes), 1)
    o_all = []
    for g in range(KV_HEADS):
        qh, slope, o_w = outs[g]
        pos_k = jnp.full((1, slot_lanes), past + 1, jnp.int32)
        for k in range(N_POOL_SEL):
            blk = idx_ref[b, g * N_SELECT + k]
            r = lane_s % PAGE_SIZE
            in_blk = jnp.logical_and(lane_s // PAGE_SIZE == k, r // SEL_BLOCK == blk % BLK_PER_PAGE)
            pos_k = jnp.where(in_blk, (blk // BLK_PER_PAGE) * PAGE_SIZE + r, pos_k)
        dist_s = past - pos_k
        o_s = two_piece(qh, slope, kbuf_ref[g], vbuf_ref[g], dist_s, dist_s >= 0,
                        ks_new[:, g * 128:g * 128 + 64], ks_new[:, g * 128 + 64:(g + 1) * 128])
        o_c = oc_ref[0][:, g * 64:(g + 1) * 64]
        gates = jnp.broadcast_to(gl_all[:, g * 128:(g + 1) * 128], (8, 128))
        g_c = _pick_lane(gates, lane128, 3 * row8)
        g_s = _pick_lane(gates, lane128, 3 * row8 + 1)
        g_w = _pick_lane(gates, lane128, 3 * row8 + 2)
        o_all.append(g_c * o_c + g_s * o_s + g_w * o_w)
    o_ref[0] = jnp.concatenate(o_all, axis=1)


def _sel_win_sample(page_table, idx, cache_sel5, cache_win5, q3, ks3, kw3, gl3, o_c, past):
    bsz = page_table.shape[0]
    w_len = cache_win5.shape[-1]
    grid_spec = pltpu.PrefetchScalarGridSpec(
        num_scalar_prefetch=2,
        grid=(bsz,),
        in_specs=[pl.BlockSpec(memory_space=pl.ANY),
                  pl.BlockSpec((1, 2, KV_HEADS, HEAD_DIM, w_len), lambda i, pt, ix: (i, 0, 0, 0, 0)),
                  pl.BlockSpec((1, 1, Q_A), lambda i, pt, ix: (i, 0, 0)),
                  pl.BlockSpec((1, 1, 256), lambda i, pt, ix: (i, 0, 0)),
                  pl.BlockSpec((1, 1, 256), lambda i, pt, ix: (i, 0, 0)),
                  pl.BlockSpec((1, 1, 256), lambda i, pt, ix: (i, 0, 0)),
                  pl.BlockSpec((1, 8, 128), lambda i, pt, ix: (i, 0, 0))],
        out_specs=pl.BlockSpec((1, 8, 128), lambda i, pt, ix: (i, 0, 0)),
        scratch_shapes=[pltpu.VMEM((KV_HEADS, HEAD_DIM, N_SELECT * PAGE_SIZE), F32),
                        pltpu.VMEM((KV_HEADS, HEAD_DIM, N_SELECT * PAGE_SIZE), F32),
                        pltpu.SemaphoreType.DMA(())],
    )
    return pl.pallas_call(
        functools.partial(_sel_win_sample_kernel, past=past),
        grid_spec=grid_spec,
        out_shape=jax.ShapeDtypeStruct((bsz, 8, 128), F32),
        compiler_params=_cparams(("arbitrary",)),
        name="sel_win_sample",
    )(page_table, idx, cache_sel5, cache_win5, q3, ks3, kw3, gl3, o_c)


def _hgrn_sample_kernel(q_ref, f_ref, v_ref, g_ref, lb_ref, ng_ref, s_ref, o_ref, s_out_ref):
    z = lb_ref[...]
    e = jnp.exp(z - jnp.max(z, axis=0, keepdims=True))
    lb = e[0] / jnp.sum(e, axis=0)
    f = lb + (1.0 - lb) * jax.nn.sigmoid(f_ref[0])
    decay = jnp.exp(jnp.log(f))
    k = 1.0 - f
    q = q_ref[0]
    v = v_ref[0]
    gate = g_ref[0]
    ng = ng_ref[...]
    for hh in range(B_HEADS):
        hs = slice(hh, hh + 1)
        s_new = decay[hs] * s_ref[0, hh] + v[:, hs] * k[hs]
        s_out_ref[0, hh] = s_new
        o = jnp.sum(q[hs] * s_new, axis=1, keepdims=True)
        o = o * lax.rsqrt(jnp.mean(o * o, axis=0, keepdims=True) + LN_EPS) * ng
        gt = gate[:, hs]
        o_ref[0, :, hs] = o * (gt * jax.nn.sigmoid(gt))


def _hgrn_sample(q_hk, f_hk, v_vh, g_vh, lb_hk, ng_col, state_t):
    bsz = q_hk.shape[0]
    return pl.pallas_call(
        _hgrn_sample_kernel,
        grid=(bsz,),
        in_specs=[pl.BlockSpec((1, B_HEADS, B_DK), lambda i: (i, 0, 0)),
                  pl.BlockSpec((1, B_HEADS, B_DK), lambda i: (i, 0, 0)),
                  pl.BlockSpec((1, B_DV, B_HEADS), lambda i: (i, 0, 0)),
                  pl.BlockSpec((1, B_DV, B_HEADS), lambda i: (i, 0, 0)),
                  pl.BlockSpec((2, B_HEADS, B_DK), lambda i: (0, 0, 0)),
                  pl.BlockSpec((B_DV, 1), lambda i: (0, 0)),
                  pl.BlockSpec((1, B_HEADS, B_DV, B_DK), lambda i: (i, 0, 0, 0))],
        out_specs=[pl.BlockSpec((1, B_DV, B_HEADS), lambda i: (i, 0, 0)),
                   pl.BlockSpec((1, B_HEADS, B_DV, B_DK), lambda i: (i, 0, 0, 0))],
        out_shape=[jax.ShapeDtypeStruct((bsz, B_DV, B_HEADS), F32),
                   jax.ShapeDtypeStruct((bsz, B_HEADS, B_DV, B_DK), F32)],
        compiler_params=_cparams(("arbitrary",)),
        name="hgrn_sample",
    )(q_hk, f_hk, v_vh, g_vh, lb_hk, ng_col, state_t)


def _rows_last(cache):
    return jnp.moveaxis(cache, -4, -1)


def _sample_layer(x, cache_cmp, cache_sel, cache_win, state, page_table, w_pad, cmpw, w_s, lb_logits, norm_g,
                  w_a, w_b, w_o, ln1_g, ln1_b, w_up, w_down, ln2_g, ln2_b):
    bsz, t, _ = x.shape
    assert t == 1, "the sample group decodes one token per request"
    n_pages = page_table.shape[1]
    past = n_pages * PAGE_SIZE
    x2d = x.reshape(bsz, D_MODEL)
    h = _project(x2d, w_pad)
    h3 = h.reshape(bsz, 1, D_PAD)
    q3 = h3[:, :, C_QA:C_QA + Q_A]
    wl, wb, posf = cmpw
    o_c, idx = _cmp_sample(page_table, _rows_last(cache_cmp), w_s, posf, wb, q3, past)
    idx2 = jnp.swapaxes(idx[:, :, :KV_HEADS], 1, 2).reshape(bsz, KV_HEADS * N_SELECT)
    o_rd = _sel_win_sample(page_table, idx2, _rows_last(cache_sel), _rows_last(cache_win),
                           q3, h3[:, :, C_KVS:C_KVS + 256], h3[:, :, C_KVW:C_KVW + 256],
                           h3[:, :, C_GL:C_GL + 256], o_c, past)
    o_a = jnp.swapaxes(o_rd[:, :GROUP].reshape(bsz, GROUP, KV_HEADS, HEAD_DIM), 1, 2).reshape(bsz, Q_A)
    to_hk = lambda a: a.reshape(-1, B_HEADS, B_DK)
    to_vh = lambda a: jnp.swapaxes(a.reshape(-1, B_HEADS, B_DV), 1, 2)
    o_vh, s_t = _hgrn_sample(to_hk(h[:, C_QB:C_QB + Q_B]), to_hk(h[:, C_FB:C_FB + Q_B]),
                             to_vh(h[:, C_IB:C_IB + I_B]), to_vh(h[:, C_GB:C_GB + I_B]),
                             to_hk(lb_logits), norm_g.reshape(B_DV, 1), jnp.swapaxes(state, 2, 3))
    o_b = jnp.swapaxes(o_vh, 1, 2).reshape(bsz, I_B)
    y = _merge_mlp(x2d, o_a, o_b, h, w_a, w_b, w_o, ln1_g, ln1_b, w_up, w_down, ln2_g, ln2_b)
    return y.reshape(bsz, 1, D_MODEL), h, jnp.swapaxes(s_t, 2, 3)


def _cmp_sample_weights(cmp_w):
    w = cmp_w.reshape(2, 2, CMP_STRIDE, HEAD_DIM, HEAD_DIM)
    base = jnp.transpose(w, (0, 2, 3, 1, 4))
    eye = jnp.eye(2, dtype=F32)
    ws = base[:, :, None, :, :, None, :] * eye[None, None, :, None, None, :, None]
    return ws.reshape(2, CMP_STRIDE // 2, 256, 256).astype(BF16)


def kernel(x_prompt, x_sample, cache_cmp_kv, cache_sel_kv, cache_win_kv, state_hgrn, page_table,
           w_in, cmp_w, cmp_pos, hgrn_lb_logits, hgrn_norm_g, w_br_a, w_br_b, w_out,
           ln1_g, ln1_b, w_up, w_down, ln2_g, ln2_b):
    assert w_in.shape[0] == 1, "one layer"
    b, t, _ = x_prompt.shape
    bsz = x_sample.shape[0]
    assert t % HB == 0 and t % QB == 0 and WINDOW + QB <= t <= 128 * SEL_BLOCK
    w_pad = _reorder_w_in(w_in[0])
    cmpw = _cmp_weights(cmp_w[0], cmp_pos[0])
    w_s = _cmp_sample_weights(cmp_w[0])
    dense = (w_br_a[0].astype(BF16), w_br_b[0].astype(BF16), w_out[0].astype(BF16), ln1_g, ln1_b,
             w_up[0].astype(BF16), w_down[0].astype(BF16), ln2_g, ln2_b)

    o_kv = Q_A
    w_kvt = jnp.transpose(w_in[0][:, o_kv:o_kv + 3 * KV_A]).astype(BF16)
    y_p, (kvc_p, kvs_p, kvw_p), s_p = _prompt_layer(x_prompt, w_pad, w_kvt, cmpw, hgrn_lb_logits, hgrn_norm_g,
                                                   *dense)
    y_s, h_s, s_s = _sample_layer(x_sample, cache_cmp_kv[0], cache_sel_kv[0], cache_win_kv[0], state_hgrn[0],
                                  page_table, w_pad, cmpw, w_s, hgrn_lb_logits, hgrn_norm_g, *dense)

    win_p = min(WINDOW, t)
    kvw_s = _kv_out(h_s, C_KVW, bsz, 1)
    new_win_s = jnp.concatenate([cache_win_kv[0], kvw_s], axis=1)[:, -min(WINDOW, cache_win_kv.shape[2] + 1):]
    return (y_p, y_s,
            kvc_p[None], kvs_p[None], kvw_p[:, -win_p:][None], s_p[None],
            _kv_out(h_s, C_KVC, bsz, 1)[None], _kv_out(h_s, C_KVS, bsz, 1)[None], new_win_s[None], s_s[None])
```

```python
import functools

import numpy as np
import jax
import jax.numpy as jnp
from jax import lax
from jax.experimental import pallas as pl
from jax.experimental.pallas import tpu as pltpu

F32 = jnp.float32
BF16 = jnp.bfloat16

D_MODEL = 1024
HEAD_DIM = 64
A_HEADS = 8
KV_HEADS = 2
GROUP = A_HEADS // KV_HEADS
CMP_STRIDE = 16
CMP_LEN = 32
SEL_BLOCK = 64
N_SELECT = 16
WINDOW = 512
PAGE_SIZE = 128
B_HEADS = 8
B_DK = 128
B_DV = 64
HGRN_CHUNK = 32
D_FF = 4 * D_MODEL
DEEPNORM_ALPHA = 2.0 ** 0.25
LN_EPS = 1e-5
NEG = -1e30
FORCE = 1e6
ATTN_SCALE = HEAD_DIM ** -0.5
LOG2E = 1.4426950408889634
Q_A = A_HEADS * HEAD_DIM
KV_A = 2 * KV_HEADS * HEAD_DIM
GATE_A = 3 * A_HEADS
Q_B = B_HEADS * B_DK
I_B = B_HEADS * B_DV

C_MG = 0
C_QB = 2048
C_FB = 3072
C_QA = 4096
C_IB = 4608
C_GB = 5120
C_KVC = 5632
C_KVS = 5888
C_KVW = 6144
C_GL = 6400
D_PAD = 6656
PROJ_TN = 1664

VMEM_LIMIT = 56 * 1024 * 1024
UNSEL = float(2.0 ** 100)


def _cparams(sem):
    return pltpu.CompilerParams(dimension_semantics=sem, vmem_limit_bytes=VMEM_LIMIT)


def _proj_perm():
    perm = np.full((D_PAD,), -1, np.int64)
    o_qa, o_kvc, o_kvs, o_kvw = 0, Q_A, Q_A + KV_A, Q_A + 2 * KV_A
    o_gl = Q_A + 3 * KV_A
    o_qb = o_gl + GATE_A
    o_fb = o_qb + Q_B
    o_ib = o_fb + Q_B
    o_gb = o_ib + I_B
    o_mg = o_gb + I_B
    perm[C_MG:C_MG + 2 * D_MODEL] = o_mg + np.arange(2 * D_MODEL)
    perm[C_QB:C_QB + Q_B] = o_qb + np.arange(Q_B)
    perm[C_FB:C_FB + Q_B] = o_fb + np.arange(Q_B)
    perm[C_QA:C_QA + Q_A] = o_qa + np.arange(Q_A)
    perm[C_IB:C_IB + I_B] = o_ib + np.arange(I_B)
    perm[C_GB:C_GB + I_B] = o_gb + np.arange(I_B)
    for new, old in ((C_KVC, o_kvc), (C_KVS, o_kvs), (C_KVW, o_kvw)):
        for g in range(KV_HEADS):
            for c in range(2):
                dst = new + g * 128 + c * 64
                src = old + c * 128 + g * 64
                perm[dst:dst + 64] = src + np.arange(64)
    for g in range(KV_HEADS):
        perm[C_GL + g * 128:C_GL + g * 128 + 12] = o_gl + g * 12 + np.arange(12)
    return perm


def _perm_runs(perm):
    runs = []
    for col in perm:
        if runs and ((col < 0 and runs[-1][0] < 0) or (col >= 0 and runs[-1][0] >= 0
                                                      and runs[-1][0] + runs[-1][1] == col)):
            runs[-1][1] += 1
        else:
            runs.append([int(col), 1])
    return [tuple(r) for r in runs]


_PERM_RUNS = _perm_runs(_proj_perm())


def _reorder_w_in(w):
    pieces = [jnp.zeros((w.shape[0], n), w.dtype) if s < 0 else w[:, s:s + n] for s, n in _PERM_RUNS]
    return jnp.concatenate(pieces, axis=1).astype(BF16)


def _proj_kernel(x_ref, w_ref, o_ref):
    xb = x_ref[...].astype(BF16)
    for j in range(D_PAD // PROJ_TN):
        cols = slice(j * PROJ_TN, (j + 1) * PROJ_TN)
        o_ref[:, cols] = jnp.dot(xb, w_ref[:, cols], preferred_element_type=F32)


def _proj_kvt_kernel(x_ref, w_ref, wt_ref, o_ref, kc_ref, ks_ref, kw_ref):
    xb = x_ref[...].astype(BF16)
    for j in range(D_PAD // PROJ_TN):
        cols = slice(j * PROJ_TN, (j + 1) * PROJ_TN)
        o_ref[:, cols] = jnp.dot(xb, w_ref[:, cols], preferred_element_type=F32)
    for j, kt_ref in enumerate((kc_ref, ks_ref, kw_ref)):
        kt_ref[0] = lax.dot_general(wt_ref[j * KV_A:(j + 1) * KV_A, :], xb, _NT, preferred_element_type=F32)


def _project_prompt(x2d, w_pad, w_kvt, b, t):
    tm = 512
    nt = t // tm
    kvt = jax.ShapeDtypeStruct((b, KV_A, t), F32)
    kvt_spec = pl.BlockSpec((1, KV_A, tm), lambda i: (i // nt, 0, i % nt))
    return pl.pallas_call(
        _proj_kvt_kernel,
        grid=(b * nt,),
        in_specs=[pl.BlockSpec((tm, D_MODEL), lambda i: (i, 0)),
                  pl.BlockSpec((D_MODEL, D_PAD), lambda i: (0, 0), pipeline_mode=pl.Buffered(1)),
                  pl.BlockSpec((3 * KV_A, D_MODEL), lambda i: (0, 0), pipeline_mode=pl.Buffered(1))],
        out_specs=[pl.BlockSpec((tm, D_PAD), lambda i: (i, 0)), kvt_spec, kvt_spec, kvt_spec],
        out_shape=[jax.ShapeDtypeStruct((b * t, D_PAD), F32), kvt, kvt, kvt],
        compiler_params=_cparams(("arbitrary",)),
        name="proj_prompt",
    )(x2d, w_pad, w_kvt)


def _project(x2d, w_pad):
    n = x2d.shape[0]
    tm = min(512, n)
    return pl.pallas_call(
        _proj_kernel,
        grid=(n // tm,),
        in_specs=[pl.BlockSpec((tm, D_MODEL), lambda i: (i, 0)),
                  pl.BlockSpec((D_MODEL, D_PAD), lambda i: (0, 0), pipeline_mode=pl.Buffered(1))],
        out_specs=pl.BlockSpec((tm, D_PAD), lambda i: (i, 0)),
        out_shape=jax.ShapeDtypeStruct((n, D_PAD), F32),
        compiler_params=_cparams(("arbitrary",)),
        name="proj",
    )(x2d, w_pad)


def _layer_norm(v, g, b):
    mu = jnp.mean(v, axis=-1, keepdims=True)
    d = v - mu
    var = jnp.mean(d * d, axis=-1, keepdims=True)
    return d * lax.rsqrt(var + LN_EPS) * g + b


FF_CHUNK = 1024


def _merge_mlp_kernel(x_ref, oa_ref, ob_ref, mga_ref, mgb_ref, wa_ref, wb_ref, wo_ref, g1_ref, b1_ref,
                      wu_ref, wd_ref, g2_ref, b2_ref, o_ref):
    br_a = jnp.dot(oa_ref[...].astype(BF16), wa_ref[...], preferred_element_type=F32)
    br_b = jnp.dot(ob_ref[...].astype(BF16), wb_ref[...], preferred_element_type=F32)
    merged = jax.nn.sigmoid(mga_ref[...]) * br_a + jax.nn.sigmoid(mgb_ref[...]) * br_b
    mix = jnp.dot(merged.astype(BF16), wo_ref[...], preferred_element_type=F32)
    x1 = _layer_norm(DEEPNORM_ALPHA * x_ref[...] + mix, g1_ref[...], b1_ref[...])
    x1b = x1.astype(BF16)
    acc = jnp.zeros(x1.shape, F32)
    for j in range(D_FF // FF_CHUNK):
        cols = slice(j * FF_CHUNK, (j + 1) * FF_CHUNK)
        u = jnp.maximum(jnp.dot(x1b, wu_ref[:, cols], preferred_element_type=F32), 0.0)
        acc += jnp.dot((u * u).astype(BF16), wd_ref[cols, :], preferred_element_type=F32)
    o_ref[...] = _layer_norm(DEEPNORM_ALPHA * x1 + acc, g2_ref[...], b2_ref[...])


def _merge_mlp(x2d, o_a, o_b, h, w_a, w_b, w_o, ln1_g, ln1_b, w_up, w_down, ln2_g, ln2_b):
    n = x2d.shape[0]
    tm = min(512, n)
    resident = lambda shape: pl.BlockSpec(shape, lambda i: (0, 0), pipeline_mode=pl.Buffered(1))
    return pl.pallas_call(
        _merge_mlp_kernel,
        grid=(n // tm,),
        in_specs=[pl.BlockSpec((tm, D_MODEL), lambda i: (i, 0)),
                  pl.BlockSpec((tm, Q_A), lambda i: (i, 0)),
                  pl.BlockSpec((tm, I_B), lambda i: (i, 0)),
                  pl.BlockSpec((tm, D_MODEL), lambda i: (i, C_MG // D_MODEL)),
                  pl.BlockSpec((tm, D_MODEL), lambda i: (i, C_MG // D_MODEL + 1)),
                  resident((Q_A, D_MODEL)), resident((I_B, D_MODEL)), resident((D_MODEL, D_MODEL)),
                  resident((1, D_MODEL)), resident((1, D_MODEL)),
                  resident((D_MODEL, D_FF)), resident((D_FF, D_MODEL)),
                  resident((1, D_MODEL)), resident((1, D_MODEL))],
        out_specs=pl.BlockSpec((tm, D_MODEL), lambda i: (i, 0)),
        out_shape=jax.ShapeDtypeStruct((n, D_MODEL), F32),
        compiler_params=_cparams(("arbitrary",)),
        name="merge_mlp",
    )(x2d, o_a, o_b, h, h, w_a, w_b, w_o, ln1_g, ln1_b, w_up, w_down, ln2_g, ln2_b)


def _dot01(a01, x):
    hi = x.astype(BF16)
    r1 = x - hi.astype(F32)
    mid = r1.astype(BF16)
    lo = (r1 - mid.astype(F32)).astype(BF16)
    out = jnp.dot(a01, hi, preferred_element_type=F32)
    out += jnp.dot(a01, mid, preferred_element_type=F32)
    out += jnp.dot(a01, lo, preferred_element_type=F32)
    return out


def _x_dot01(x, b01):
    hi = x.astype(BF16)
    r1 = x - hi.astype(F32)
    mid = r1.astype(BF16)
    lo = (r1 - mid.astype(F32)).astype(BF16)
    out = jnp.dot(hi, b01, preferred_element_type=F32)
    out += jnp.dot(mid, b01, preferred_element_type=F32)
    out += jnp.dot(lo, b01, preferred_element_type=F32)
    return out


_NT = (((1,), (1,)), ((), ()))
_TN = (((0,), (0,)), ((), ()))


def _cmp_prompt_kernel(kv_ref, w_ref, posf_ref, wb_ref, o_ref, *, n_ch):
    acc = jnp.zeros((n_ch, 256), F32)
    for l in range(CMP_STRIDE):
        xl = kv_ref[pl.ds(l, n_ch, stride=CMP_STRIDE), :].astype(BF16)
        acc += jnp.dot(xl, w_ref[l], preferred_element_type=F32)
    bias = jnp.dot(posf_ref[...].astype(BF16), wb_ref[...], preferred_element_type=F32)[0:1]
    nxt = pltpu.roll(acc[:, 128:], n_ch - 1, 0)
    o_ref[0] = (acc[:, :128] + nxt + bias).astype(BF16)


def _cmp_weights(cmp_w, cmp_pos):
    w = cmp_w.reshape(2, 2, CMP_STRIDE, HEAD_DIM, HEAD_DIM)
    eye = jnp.eye(2, dtype=F32)
    wl = jnp.transpose(w, (2, 0, 3, 1, 4))[:, :, :, :, None, :] * eye[None, :, None, None, :, None]
    wb = cmp_w.reshape(2, CMP_LEN * HEAD_DIM, 1, HEAD_DIM) * eye[:, None, :, None]
    wl = wl.reshape(CMP_STRIDE, 128, 256).astype(BF16)
    wb = wb.reshape(2 * CMP_LEN * HEAD_DIM, 128).astype(BF16)
    posf = jnp.broadcast_to(cmp_pos.reshape(1, 2 * CMP_LEN * HEAD_DIM), (8, 2 * CMP_LEN * HEAD_DIM))
    return wl, wb, posf


def _cmp_prompt(h, b, t, wl, wb, posf):
    n_ch = t // CMP_STRIDE
    return pl.pallas_call(
        functools.partial(_cmp_prompt_kernel, n_ch=n_ch),
        grid=(b, KV_HEADS),
        in_specs=[pl.BlockSpec((t, 128), lambda i, g: (i, C_KVC // 128 + g)),
                  pl.BlockSpec((CMP_STRIDE, 128, 256), lambda i, g: (0, 0, 0)),
                  pl.BlockSpec((8, 4096), lambda i, g: (0, 0)),
                  pl.BlockSpec((4096, 128), lambda i, g: (0, 0))],
        out_specs=pl.BlockSpec((1, n_ch, 128), lambda i, g: (i, 0, g)),
        out_shape=jax.ShapeDtypeStruct((b, n_ch, 256), BF16),
        compiler_params=_cparams(("arbitrary", "arbitrary")),
        name="cmp_prompt",
    )(h, wl, posf, wb)


QB = 256
PIECE = 256
SEL_UNROLL = 2
KT = 256
WT = 128


ROWS = GROUP * QB
TL_COUNT = 64


def _head_major(q):
    return jnp.concatenate([q[:, r * 64:(r + 1) * 64] for r in range(GROUP)], axis=0) * ATTN_SCALE


def _head_slope(g, r):
    return jnp.where(g == 0, 2.0 ** -(r + 1), 2.0 ** -(r + 1 + GROUP)).astype(F32)


def _slope_row(g):
    r_ix = lax.broadcasted_iota(jnp.int32, (1, ROWS), 1) // QB
    row = jnp.zeros((1, ROWS), F32)
    for r in range(GROUP):
        row = jnp.where(r_ix == r, _head_slope(g, r), row)
    return row


def _nsa_select_kernel(q_ref, kc_ref, oc_ref, un_ref, tl_ref, *, n_ch):
    g = pl.program_id(1)
    c = pl.program_id(2)
    t0 = c * QB
    qpad = jnp.concatenate([_head_major(q_ref[...]), jnp.zeros((ROWS, 64), F32)], axis=1).astype(BF16)
    kc = kc_ref[0]
    s = lax.dot_general(kc, qpad, _NT, preferred_element_type=F32)
    pos_row = t0 + lax.broadcasted_iota(jnp.int32, (1, ROWS), 1) % QB
    n_col = lax.broadcasted_iota(jnp.int32, (n_ch, 1), 0)
    dist_i = pos_row - (n_col * CMP_STRIDE + (CMP_LEN - 1))
    valid = jnp.logical_and(dist_i >= 0, n_col < n_ch - 1)
    s = jnp.where(valid, s - _slope_row(g) * dist_i.astype(F32), NEG)
    m = jnp.max(s, axis=0, keepdims=True)
    p = jnp.exp(s - m)
    inv = jnp.where(m > 0.5 * NEG, 1.0 / jnp.maximum(jnp.sum(p, axis=0, keepdims=True), 1e-30), 0.0)
    p = p * inv
    o_c = lax.dot_general(kc, p.astype(BF16), _TN, preferred_element_type=F32)
    oc_ref[0] = o_c[64:128, :]

    imp = p[:, 0:QB]
    for r in range(1, GROUP):
        imp = imp + p[:, r * QB:(r + 1) * QB]
    ratio = SEL_BLOCK // CMP_STRIDE
    pool_t = (lax.broadcasted_iota(jnp.int32, (128, n_ch), 1) // ratio
              == lax.broadcasted_iota(jnp.int32, (128, n_ch), 0)).astype(BF16)
    imp_blk = _dot01(pool_t, imp)
    blk = lax.broadcasted_iota(jnp.int32, (128, QB), 0)
    pos_q = t0 + lax.broadcasted_iota(jnp.int32, (1, QB), 1)
    cur = pos_q // SEL_BLOCK
    forced = jnp.logical_or(jnp.logical_or(blk == 0, blk == cur), blk == cur - 1)
    allowed = blk * SEL_BLOCK <= pos_q
    work = jnp.where(jnp.logical_and(allowed, jnp.logical_not(forced)), imp_blk, -jnp.inf)
    sel = forced
    for _ in range(N_SELECT - 3):
        mx = jnp.max(work, axis=0, keepdims=True)
        first = jnp.min(jnp.where(work == mx, blk, 128), axis=0, keepdims=True)
        pick = blk == first
        sel = jnp.logical_or(sel, pick)
        work = jnp.where(pick, -jnp.inf, work)
    sel = jnp.logical_and(sel, allowed)
    un_ref[...] = jnp.transpose(jnp.where(sel, 0.0, -UNSEL)).astype(BF16)

    tile_of = (lax.broadcasted_iota(jnp.int32, (128, 128), 1) // (KT // SEL_BLOCK)
               == lax.broadcasted_iota(jnp.int32, (128, 128), 0)).astype(BF16)
    sel_b = jnp.where(jnp.logical_and(sel, blk > 0), 1.0, 0.0).astype(BF16)
    cnt = jnp.sum(jnp.dot(tile_of, sel_b, preferred_element_type=F32), axis=1, keepdims=True)
    tile_col = lax.broadcasted_iota(jnp.int32, (128, 1), 0)
    flag = jnp.logical_and(cnt > 0.0, tile_col < t0 // KT)
    flag_m = jnp.where(jnp.broadcast_to(flag, (128, 128)), 1.0, 0.0).astype(BF16)
    row_i = lax.broadcasted_iota(jnp.int32, (128, 128), 0)
    lane_i = lax.broadcasted_iota(jnp.int32, (128, 128), 1)
    before = jnp.dot((lane_i < row_i).astype(BF16), flag_m, preferred_element_type=F32)
    slot = jnp.where(jnp.logical_and(flag, before == lane_i.astype(F32)), 1.0, 0.0).astype(BF16)
    j_rows = lax.broadcasted_iota(jnp.int32, (8, 128), 1).astype(BF16)
    listed = jnp.dot(j_rows, slot, preferred_element_type=F32)
    total = jnp.dot(jnp.ones((8, 128), BF16), flag_m, preferred_element_type=F32)
    lane8 = lax.broadcasted_iota(jnp.int32, (8, 128), 1)
    tl_ref[0] = jnp.where(lane8 == TL_COUNT, total, listed).astype(jnp.int32)


def _nsa_select(h, kcmp, b, t):
    n_ch = t // CMP_STRIDE
    nq = t // QB
    steps = b * KV_HEADS * nq
    return pl.pallas_call(
        functools.partial(_nsa_select_kernel, n_ch=n_ch),
        grid=(b, KV_HEADS, nq),
        in_specs=[pl.BlockSpec((QB, 256), lambda i, g, c: (i * nq + c, C_QA // 256 + g)),
                  pl.BlockSpec((1, n_ch, 128), lambda i, g, c: (i, 0, g))],
        out_specs=[pl.BlockSpec((1, HEAD_DIM, ROWS), lambda i, g, c: ((i * KV_HEADS + g) * nq + c, 0, 0)),
                   pl.BlockSpec((QB, 128), lambda i, g, c: (i * nq + c, g)),
                   pl.BlockSpec((1, 8, 128), lambda i, g, c: ((i * KV_HEADS + g) * nq + c, 0, 0))],
        out_shape=[jax.ShapeDtypeStruct((steps, HEAD_DIM, ROWS), F32),
                   jax.ShapeDtypeStruct((b * t, KV_HEADS * 128), BF16),
                   jax.ShapeDtypeStruct((steps, 8, 128), jnp.int32)],
        compiler_params=_cparams(("arbitrary", "arbitrary", "arbitrary")),
        name="nsa_select",
    )(h, kcmp)


def _nsa_attend_kernel(tl_ref, q_ref, un_ref, gl_ref, oc_ref, ks_ref, kw_ref, o_ref,
                       ksb_ref, ksa_ref, kwb_ref, kwa_ref, *, nq):
    i = pl.program_id(0)
    g = pl.program_id(1)
    c = pl.program_id(2)
    step = (i * KV_HEADS + g) * nq + c
    t0 = c * QB
    half = PIECE
    n_piece = ROWS // PIECE
    qs = _head_major(q_ref[...]) * LOG2E
    s2 = jnp.concatenate([jnp.broadcast_to(_head_slope(g, r) * LOG2E, (QB, 1)) for r in range(GROUP)], axis=0)
    s2_hi = s2.astype(BF16).astype(F32)
    s2_lo = s2 - s2_hi
    lane64 = lax.broadcasted_iota(jnp.int32, (ROWS, 64), 1)
    ali_q = jnp.where(lane64 == 0, -64.0 * s2_hi, jnp.where(lane64 == 1, -s2_hi,
                      jnp.where(lane64 == 2, -64.0 * s2_lo, jnp.where(lane64 == 3, -s2_lo, 0.0))))
    q_win = jnp.concatenate([qs, ali_q], axis=1).astype(BF16)
    first_blk = jnp.logical_and(lax.broadcasted_iota(jnp.int32, (QB, 128), 1) == 0, c > 0)
    un = jnp.where(first_blk, jnp.asarray(-UNSEL, BF16), un_ref[...])
    q_sel =jnp.concatenate([q_win, jnp.concatenate([un] * GROUP, axis=0)], axis=1)
    slope_row = _slope_row(g) * LOG2E

    @pl.when(c == 0)
    def _():
        def fill(j, carry):
            r0 = pl.multiple_of(j * KT, KT)
            row = lax.broadcasted_iota(jnp.int32, (KT, 128), 0)
            lane = lax.broadcasted_iota(jnp.int32, (KT, 128), 1)
            for src, dst_kv, dst_aug, tile in ((ks_ref, ksb_ref, ksa_ref, KT), (kw_ref, kwb_ref, kwa_ref, WT)):
                kv_b = src[pl.ds(r0, KT), :].astype(BF16)
                d = (QB - 1) - row % tile
                ali = jnp.where(lane // 4 == 16, jnp.where(lane % 2 == 0, d >> 6, d & 63), 0).astype(F32).astype(BF16)
                dst_kv[pl.ds(r0, KT), :] = kv_b
                dst_aug[pl.ds(r0, KT), 0:128] = jnp.where(lane < 64, kv_b, ali)
            ksa_ref[pl.ds(r0, KT), 128:256] = jnp.where((r0 + row) // SEL_BLOCK == lane, 1.0, 0.0).astype(BF16)
            return carry

        lax.fori_loop(0, ks_ref.shape[0] // KT, fill, 0)

    def update(carry, s, kv_b, offset):
        out = []
        for hh, ((m, l, acc), sh) in enumerate(zip(carry, s)):
            shift = slope_row[:, hh * half:(hh + 1) * half] * offset
            m_new = jnp.maximum(m, jnp.max(sh, axis=0, keepdims=True) - shift)
            alpha = jnp.exp2(m - m_new)
            p = jnp.exp2(sh - (m_new + shift))
            l = alpha * l + jnp.sum(p, axis=0, keepdims=True)
            acc = alpha * acc + lax.dot_general(kv_b, p.astype(BF16), _TN, preferred_element_type=F32)
            out.append((m_new, l, acc))
        return tuple(out)

    def init():
        return tuple((jnp.full((1, half), NEG, F32), jnp.zeros((1, half), F32), jnp.zeros((128, half), F32))
                     for _ in range(n_piece))

    def tile(carry, aug_ref, kvb_ref, q_side, k0, n_keys, keep):
        k_aug = aug_ref[pl.ds(k0, n_keys), :]
        s = []
        for hh in range(n_piece):
            sh = lax.dot_general(k_aug, q_side[hh * half:(hh + 1) * half], _NT, preferred_element_type=F32)
            if keep is not None:
                sh = jnp.where(jnp.concatenate([keep] * (half // QB), axis=1), sh, NEG)
            s.append(sh)
        return update(carry, s, kvb_ref[pl.ds(k0, n_keys), :], (t0 - k0).astype(F32))

    k_tail = pl.multiple_of((t0 // KT) * KT, KT)
    key_x = lax.broadcasted_iota(jnp.int32, (KT, QB), 0)
    q_x = lax.broadcasted_iota(jnp.int32, (KT, QB), 1)
    carry = tile(init(), ksa_ref, ksb_ref, q_sel, k_tail, KT, k_tail + key_x <= t0 + q_x)

    def sel_listed(n, carry):
        return tile(carry, ksa_ref, ksb_ref, q_sel, pl.multiple_of(tl_ref[step, n] * KT, KT), KT, None)

    n_listed = tl_ref[step, TL_COUNT]

    def sel_group(n, cr):
        for u in range(SEL_UNROLL):
            cr = sel_listed(SEL_UNROLL * n + u, cr)
        return cr

    carry = lax.fori_loop(0, n_listed // SEL_UNROLL, sel_group, carry)
    carry = lax.fori_loop(n_listed - n_listed % SEL_UNROLL, n_listed, sel_listed, carry)
    k_first = ksa_ref[0:SEL_BLOCK, 0:128]
    s_first = [jnp.where(c > 0, lax.dot_general(k_first, q_win[hh * half:(hh + 1) * half], _NT,
                                                 preferred_element_type=F32), NEG) for hh in range(n_piece)]
    carry = update(carry, s_first, ksb_ref[0:SEL_BLOCK, :], t0.astype(F32))
    o_sel = [acc * (1.0 / l) for (_, l, acc) in carry]

    w_keys = WINDOW + WT
    dq = lax.broadcasted_iota(jnp.int32, (WT, WT), 1) - lax.broadcasted_iota(jnp.int32, (WT, WT), 0)
    o_win = []
    for part in range(QB // WT):
        q0 = part * WT
        k0w = pl.multiple_of(jnp.maximum(t0 + q0 - WINDOW, 0), WT)
        k_aug = kwa_ref[pl.ds(k0w, w_keys), :]
        kv_w = kwb_ref[pl.ds(k0w, w_keys), :]
        offs = [t0 - (k0w + gi * WT) for gi in range(w_keys // WT)]
        keeps = [jnp.concatenate([jnp.logical_and(dq + (off + q0) >= 0, dq + (off + q0) < WINDOW)] * (PIECE // WT),
                                 axis=1) for off in offs]
        heads = []
        for r2 in range(0, GROUP, PIECE // WT):
            rs = range(r2, r2 + PIECE // WT)
            sl = jnp.concatenate([slope_row[:, r * QB + q0:r * QB + q0 + WT] for r in rs], axis=1)
            q_part = jnp.concatenate([q_win[r * QB + q0:r * QB + q0 + WT] for r in rs], axis=0)
            sh = lax.dot_general(k_aug, q_part, _NT, preferred_element_type=F32)
            parts = [jnp.where(keep, sh[gi * WT:(gi + 1) * WT], NEG) for gi, keep in enumerate(keeps)]
            shifts = [sl * off.astype(F32) for off in offs]
            m_w = functools.reduce(jnp.maximum, [jnp.max(pt, axis=0, keepdims=True) - sf
                                                 for pt, sf in zip(parts, shifts)])
            ps = [jnp.exp2(pt - (m_w + sf)) for pt, sf in zip(parts, shifts)]
            l_w = functools.reduce(jnp.add, [jnp.sum(p, axis=0, keepdims=True) for p in ps])
            acc = lax.dot_general(kv_w, jnp.concatenate(ps, axis=0).astype(BF16), _TN,
                                  preferred_element_type=F32) * (1.0 / l_w)
            heads += [acc[:, u * WT:(u + 1) * WT] for u in range(PIECE // WT)]
        o_win.append(heads)

    gates = jnp.transpose(jax.nn.sigmoid(gl_ref[...]))
    o_c = oc_ref[0]
    for r2 in range(0, GROUP, 2):
        mixed = []
        for r in (r2, r2 + 1):
            hh, cs = divmod(r * QB, half)
            o_s = o_sel[hh][HEAD_DIM:, cs:cs + QB]
            o_w = jnp.concatenate([o_win[part][r][HEAD_DIM:, :] for part in range(QB // WT)], axis=1)
            mixed.append(gates[3 * r:3 * r + 1] * o_c[:, r * QB:(r + 1) * QB]
                         + gates[3 * r + 1:3 * r + 2] * o_s + gates[3 * r + 2:3 * r + 3] * o_w)
        o_ref[:, r2 * 64:(r2 + 2) * 64] = jnp.transpose(jnp.concatenate(mixed, axis=0))


def _nsa_attend(tiles, h, unsel, o_c, b, t):
    nq = t // QB
    grid_spec = pltpu.PrefetchScalarGridSpec(
        num_scalar_prefetch=1,
        grid=(b, KV_HEADS, nq),
        in_specs=[pl.BlockSpec((QB, 256), lambda i, g, c, tl: (i * nq + c, C_QA // 256 + g)),
                  pl.BlockSpec((QB, 128), lambda i, g, c, tl: (i * nq + c, g)),
                  pl.BlockSpec((QB, 128), lambda i, g, c, tl: (i * nq + c, C_GL // 128 + g)),
                  pl.BlockSpec((1, HEAD_DIM, ROWS), lambda i, g, c, tl: ((i * KV_HEADS + g) * nq + c, 0, 0)),
                  pl.BlockSpec((t, 128), lambda i, g, c, tl: (i, C_KVS // 128 + g)),
                  pl.BlockSpec((t, 128), lambda i, g, c, tl: (i, C_KVW // 128 + g))],
        out_specs=pl.BlockSpec((QB, 256), lambda i, g, c, tl: (i * nq + c, g)),
        scratch_shapes=[pltpu.VMEM((t, 128), BF16), pltpu.VMEM((t, 256), BF16),
                        pltpu.VMEM((t, 128), BF16), pltpu.VMEM((t, 128), BF16)],
    )
    return pl.pallas_call(
        functools.partial(_nsa_attend_kernel, nq=nq),
        grid_spec=grid_spec,
        out_shape=jax.ShapeDtypeStruct((b * t, Q_A), F32),
        compiler_params=_cparams(("arbitrary", "arbitrary", "arbitrary")),
        name="nsa_attend",
    )(tiles, h, unsel, h, o_c, h, h)


def _nsa_prompt(h, kcmp, b, t):
    o_c, unsel, tl = _nsa_select(h, kcmp, b, t)
    return _nsa_attend(tl[:, 0, :], h, unsel, o_c, b, t)


HB = 256


def _lower_bound(lb_ref):
    z = lb_ref[...]
    e = jnp.exp(z - jnp.max(z, axis=0, keepdims=True))
    return e[0:1] / jnp.sum(e, axis=0, keepdims=True)


def _hgrn_prompt_kernel(q_ref, f_ref, i_ref, g_ref, lb_ref, ng_ref, o_ref, s_out_ref, s_ref):
    tb = pl.program_id(1)

    @pl.when(tb == 0)
    def _():
        s_ref[...] = jnp.zeros_like(s_ref)

    n_c = HB // HGRN_CHUNK
    lb = _lower_bound(lb_ref)
    f = lb + (1.0 - lb) * jax.nn.sigmoid(f_ref[...])
    log_f = jnp.log(f)
    row = lax.broadcasted_iota(jnp.int32, (HB, HB), 0)
    col = lax.broadcasted_iota(jnp.int32, (HB, HB), 1)
    tril = jnp.logical_and(col <= row, row // HGRN_CHUNK == col // HGRN_CHUNK)
    lc = _dot01(tril.astype(BF16), log_f)
    lc3 = lc.reshape(n_c, HGRN_CHUNK, Q_B)
    dec8 = jnp.exp(lc3[:, HGRN_CHUNK - 1, :])
    dec = jnp.broadcast_to(dec8[:, None, :], lc3.shape).reshape(HB, Q_B)
    q_t = (q_ref[...] * jnp.exp(lc)).astype(BF16)
    k_f = (1.0 - f) * jnp.exp(-lc)
    k_t = k_f.astype(BF16)
    k_e = (k_f * dec).astype(BF16)
    v_all = i_ref[...]
    gate = g_ref[...]
    ng = ng_ref[...]
    for hh in range(B_HEADS):
        ks = slice(hh * B_DK, (hh + 1) * B_DK)
        vs = slice(hh * B_DV, (hh + 1) * B_DV)
        v = v_all[:, vs].astype(BF16)
        a = lax.dot_general(q_t[:, ks], k_t[:, ks], _NT, preferred_element_type=F32)
        a = jnp.where(tril, a, 0.0).astype(BF16)
        o = jnp.dot(a, v, preferred_element_type=F32)
        st = s_ref[hh]
        inter = []
        for cc in range(n_c):
            rs = slice(cc * HGRN_CHUNK, (cc + 1) * HGRN_CHUNK)
            inter.append(lax.dot_general(q_t[rs, ks], st.astype(BF16), _NT, preferred_element_type=F32))
            u = lax.dot_general(v[rs], k_e[rs, ks], _TN, preferred_element_type=F32)
            st = dec8[cc:cc + 1, ks] * st + u
        s_ref[hh] = st
        o = o + jnp.concatenate(inter, axis=0)
        o = o * lax.rsqrt(jnp.mean(o * o, axis=-1, keepdims=True) + LN_EPS) * ng
        gt = gate[:, vs]
        o_ref[:, vs] = o * (gt * jax.nn.sigmoid(gt))

    @pl.when(tb == pl.num_programs(1) - 1)
    def _():
        s_out_ref[0] = s_ref[...]


def _hgrn_prompt(h, b, t, lb_logits, norm_g):
    nt = t // HB
    return pl.pallas_call(
        _hgrn_prompt_kernel,
        grid=(b, nt),
        in_specs=[pl.BlockSpec((HB, Q_B), lambda i, j: (i * nt + j, C_QB // Q_B)),
                  pl.BlockSpec((HB, Q_B), lambda i, j: (i * nt + j, C_FB // Q_B)),
                  pl.BlockSpec((HB, I_B), lambda i, j: (i * nt + j, C_IB // I_B)),
                  pl.BlockSpec((HB, I_B), lambda i, j: (i * nt + j, C_GB // I_B)),
                  pl.BlockSpec((2, Q_B), lambda i, j: (0, 0)),
                  pl.BlockSpec((1, B_DV), lambda i, j: (0, 0))],
        out_specs=[pl.BlockSpec((HB, I_B), lambda i, j: (i * nt + j, 0)),
                   pl.BlockSpec((1, B_HEADS, B_DV, B_DK), lambda i, j: (i, 0, 0, 0))],
        out_shape=[jax.ShapeDtypeStruct((b * t, I_B), F32),
                   jax.ShapeDtypeStruct((b, B_HEADS, B_DV, B_DK), F32)],
        scratch_shapes=[pltpu.VMEM((B_HEADS, B_DV, B_DK), F32)],
        compiler_params=_cparams(("arbitrary", "arbitrary")),
        name="hgrn_prompt",
    )(h, h, h, h, lb_logits, norm_g)


def _prompt_layer(x, w_pad, w_kvt, cmpw, lb_logits, norm_g, w_a, w_b, w_o, ln1_g, ln1_b, w_up, w_down,
                  ln2_g, ln2_b):
    b, t, _ = x.shape
    x2d = x.reshape(b * t, D_MODEL)
    h, *kv_t = _project_prompt(x2d, w_pad, w_kvt, b, t)
    kcmp = _cmp_prompt(h, b, t, *cmpw)
    o_a = _nsa_prompt(h, kcmp, b, t)
    o_b, s_end = _hgrn_prompt(h, b, t, lb_logits, norm_g)
    y = _merge_mlp(x2d, o_a, o_b, h, w_a, w_b, w_o, ln1_g, ln1_b, w_up, w_down, ln2_g, ln2_b)
    kv_out = [jnp.moveaxis(a.reshape(b, 2, KV_HEADS, HEAD_DIM, t), -1, 1) for a in kv_t]
    return y.reshape(b, t, D_MODEL), kv_out, jnp.swapaxes(s_end, 2, 3)


def _kv_out(h, col, b, t):
    kv = h[:, col:col + 256].reshape(b, t, KV_HEADS, 2, HEAD_DIM)
    return jnp.swapaxes(kv, 2, 3)


LAND_PAGES = 8
CH_PER_PAGE = PAGE_SIZE // CMP_STRIDE
BLK_PER_PAGE = PAGE_SIZE // SEL_BLOCK
N_POOL_SEL = N_SELECT - 1


def _head_rows(q_row, g):
    rows = [q_row[:, g * 256 + r * 64:g * 256 + (r + 1) * 64] for r in range(GROUP)]
    return jnp.concatenate(rows + [jnp.zeros((8 - GROUP, 64), F32)], axis=0) * ATTN_SCALE


def _slope_col(g):
    row = lax.broadcasted_iota(jnp.int32, (8, 1), 0)
    col = jnp.zeros((8, 1), F32)
    for r in range(GROUP):
        col = jnp.where(row == r, 2.0 ** -(g * GROUP + r + 1), col)
    return col


def _cmp_sample_kernel(pt_ref, cache_ref, w_ref, posf_ref, wb_ref, q_ref, oc_ref, idx_ref,
                       buf_ref, xk_ref, xv_ref, bias_ref, sem, *, n_pages, past):
    b = pl.program_id(0)
    n_ch = n_pages * CH_PER_PAGE
    n_blk = past // SEL_BLOCK
    lanes = -(-n_blk // 128) * 128

    def page_copy(pg, page):
        return pltpu.make_async_copy(cache_ref.at[page], buf_ref.at[pg], sem.at[pg])

    def issue_all(req):
        def issue(pg, carry):
            page_copy(pg, pt_ref[req, pg]).start()
            return carry

        lax.fori_loop(0, n_pages, issue, 0)

    def land(i, carry):
        pages = [i * LAND_PAGES + u for u in range(LAND_PAGES)]
        for pg in pages:
            page_copy(pg, 0).wait()
        for pg in pages:
            r0 = pl.multiple_of(pg * PAGE_SIZE, PAGE_SIZE)
            xk_ref[pl.ds(r0, PAGE_SIZE), :] = jnp.transpose(buf_ref[pg, 0].reshape(128, PAGE_SIZE))
            xv_ref[pl.ds(r0, PAGE_SIZE), :] = jnp.transpose(buf_ref[pg, 1].reshape(128, PAGE_SIZE))
        return carry

    @pl.when(b == 0)
    def _():
        issue_all(b)

    lax.fori_loop(0, n_pages // LAND_PAGES, land, 0)

    @pl.when(b + 1 < pl.num_programs(0))
    def _():
        issue_all(b + 1)

    @pl.when(b == 0)
    def _():
        bias_ref[...] = jnp.dot(posf_ref[...].astype(BF16), wb_ref[...], preferred_element_type=F32)

    bias = bias_ref[0:1]
    blocks = []
    for c, x_ref in enumerate((xk_ref, xv_ref)):
        acc = jnp.zeros((n_ch, 256), F32)
        for l2 in range(CMP_STRIDE // 2):
            xl = jnp.concatenate([x_ref[pl.ds(2 * l2 + u, n_ch, stride=CMP_STRIDE), :] for u in range(2)],
                                 axis=1).astype(BF16)
            acc += jnp.dot(xl, w_ref[c, l2], preferred_element_type=F32)
        bias_c = jnp.concatenate([bias[:, c * 64:(c + 1) * 64]] * KV_HEADS, axis=1)
        blocks.append((acc[:, :128] + pltpu.roll(acc[:, 128:], n_ch - 1, 0) + bias_c).astype(BF16))
    kc_all, vc_all = blocks

    q_row = q_ref[0]
    n_ix = lax.broadcasted_iota(jnp.int32, (1, n_ch), 1)
    dist_i = past - (n_ix * CMP_STRIDE + (CMP_LEN - 1))
    valid = jnp.logical_and(dist_i >= 0, n_ix < n_ch - 1)
    dist = dist_i.astype(F32)
    ratio = SEL_BLOCK // CMP_STRIDE
    pool = (lax.broadcasted_iota(jnp.int32, (n_ch, lanes), 0) // ratio
            == lax.broadcasted_iota(jnp.int32, (n_ch, lanes), 1)).astype(BF16)
    row8 = lax.broadcasted_iota(jnp.int32, (8, lanes), 0)
    lane_g = lax.broadcasted_iota(jnp.int32, (8, 128), 1) // HEAD_DIM
    zeros4 = jnp.zeros((GROUP, 64), F32)
    q8 = jnp.concatenate([jnp.concatenate([_head_rows(q_row, 0)[0:GROUP], zeros4], axis=1),
                          jnp.concatenate([zeros4, _head_rows(q_row, 1)[0:GROUP]], axis=1)], axis=0).astype(BF16)
    s = lax.dot_general(q8, kc_all, _NT, preferred_element_type=F32)
    head = lax.broadcasted_iota(jnp.int32, (8, 1), 0)
    slope8 = jnp.zeros((8, 1), F32)
    for h in range(A_HEADS):
        slope8 = jnp.where(head == h, 2.0 ** -(h + 1), slope8)
    s = jnp.where(valid, s - slope8 * dist, NEG)
    m = jnp.max(s, axis=-1, keepdims=True)
    p = jnp.where(valid, jnp.exp(s - m), 0.0)
    p = p / jnp.maximum(jnp.sum(p, axis=-1, keepdims=True), 1e-30)
    o8 = jnp.dot(p.astype(BF16), vc_all, preferred_element_type=F32)
    o_c = jnp.where(lane_g == 0, o8, pltpu.roll(o8, GROUP, 0))
    imp8 = jnp.concatenate([p[0:1] + p[1:2] + p[2:3] + p[3:4], p[4:5] + p[5:6] + p[6:7] + p[7:8],
                            jnp.zeros((8 - KV_HEADS, n_ch), F32)], axis=0)
    score = jnp.where(row8 < KV_HEADS, _x_dot01(imp8, pool), -FORCE)
    lane = lax.broadcasted_iota(jnp.int32, (8, lanes), 1)
    forced = jnp.logical_or(lane == 0, lane == n_blk - 1)
    score = jnp.where(lane < n_blk, jnp.where(forced, FORCE, score), -jnp.inf)
    sc_t = jnp.transpose(jnp.concatenate([score, jnp.full((120, lanes), -jnp.inf, F32)], axis=0))
    blk = lax.broadcasted_iota(jnp.int32, (lanes, 128), 0)
    picks = []
    for r in range(N_POOL_SEL):
        m = jnp.max(sc_t, axis=0, keepdims=True)
        first = jnp.min(jnp.where(sc_t == m, blk, lanes), axis=0, keepdims=True)
        picks.append(first)
        sc_t = jnp.where(blk == first, -jnp.inf, sc_t)
    idx_ref[0] = jnp.concatenate(picks + [jnp.zeros((N_SELECT - N_POOL_SEL, 128), jnp.int32)], axis=0)
    oc_ref[0] = o_c


def _cmp_sample(page_table, cache5, w_s, posf, wb, q3, past):
    bsz, n_pages = page_table.shape
    grid_spec = pltpu.PrefetchScalarGridSpec(
        num_scalar_prefetch=1,
        grid=(bsz,),
        in_specs=[pl.BlockSpec(memory_space=pl.ANY),
                  pl.BlockSpec((2, CMP_STRIDE // 2, 256, 256), lambda i, pt: (0, 0, 0, 0)),
                  pl.BlockSpec((8, 4096), lambda i, pt: (0, 0)),
                  pl.BlockSpec((4096, 128), lambda i, pt: (0, 0)),
                  pl.BlockSpec((1, 1, Q_A), lambda i, pt: (i, 0, 0))],
        out_specs=[pl.BlockSpec((1, 8, 128), lambda i, pt: (i, 0, 0)),
                   pl.BlockSpec((1, N_SELECT, 128), lambda i, pt: (i, 0, 0))],
        scratch_shapes=[pltpu.VMEM((n_pages, 2, KV_HEADS, HEAD_DIM, PAGE_SIZE), F32),
                        pltpu.VMEM((n_pages * PAGE_SIZE, 128), F32),
                        pltpu.VMEM((n_pages * PAGE_SIZE, 128), F32),
                        pltpu.VMEM((8, 128), F32),
                        pltpu.SemaphoreType.DMA((n_pages,))],
    )
    return pl.pallas_call(
        functools.partial(_cmp_sample_kernel, n_pages=n_pages, past=past),
        grid_spec=grid_spec,
        out_shape=[jax.ShapeDtypeStruct((bsz, 8, 128), F32),
                   jax.ShapeDtypeStruct((bsz, N_SELECT, 128), jnp.int32)],
        compiler_params=_cparams(("arbitrary",)),
        name="cmp_sample",
    )(page_table, cache5, w_s, posf, wb, q3)


def _pick_lane(mat, lane, target):
    return jnp.sum(jnp.where(lane == target, mat, 0.0), axis=-1, keepdims=True)


def _sel_win_sample_kernel(pt_ref, idx_ref, cache_ref, win_ref, q_ref, ks_ref, kw_ref, gl_ref, oc_ref, o_ref,
                           kbuf_ref, vbuf_ref, sem, *, past):
    b = pl.program_id(0)
    slot_lanes = N_SELECT * PAGE_SIZE

    def page_copies(g, k, page):
        dst = pl.ds(k * PAGE_SIZE, PAGE_SIZE)
        return (pltpu.make_async_copy(cache_ref.at[page, 0, g], kbuf_ref.at[g, :, dst], sem),
                pltpu.make_async_copy(cache_ref.at[page, 1, g], vbuf_ref.at[g, :, dst], sem))

    for g in range(KV_HEADS):
        for k in range(N_POOL_SEL):
            page = pt_ref[b, idx_ref[b, g * N_SELECT + k] // BLK_PER_PAGE]
            for cp in page_copies(g, k, page):
                cp.start()
        pad = pl.ds(N_POOL_SEL * PAGE_SIZE, PAGE_SIZE)
        kbuf_ref[g, :, pad] = jnp.zeros((HEAD_DIM, PAGE_SIZE), F32)
        vbuf_ref[g, :, pad] = jnp.zeros((HEAD_DIM, PAGE_SIZE), F32)

    q_row = q_ref[0]
    ks_new = ks_ref[0]
    kw_new = kw_ref[0]
    gl_all = jax.nn.sigmoid(gl_ref[0])
    lane128 = lax.broadcasted_iota(jnp.int32, (8, 128), 1)
    row8 = lax.broadcasted_iota(jnp.int32, (8, 1), 0)
    w_len = win_ref.shape[-1]

    def two_piece(qh, slope, k_t, v_t, dist_i, valid, k_new, v_new):
        s = jnp.dot(qh.astype(BF16), k_t.astype(BF16), preferred_element_type=F32)
        s = jnp.where(valid, s - slope * dist_i.astype(F32), NEG)
        s_n = jnp.sum(qh * k_new, axis=-1, keepdims=True)
        m = jnp.maximum(jnp.max(s, axis=-1, keepdims=True), s_n)
        p = jnp.where(valid, jnp.exp(s - m), 0.0)
        p_n = jnp.exp(s_n - m)
        l = jnp.sum(p, axis=-1, keepdims=True) + p_n
        o = lax.dot_general(p.astype(BF16), v_t.astype(BF16), _NT, preferred_element_type=F32) + p_n * v_new
        return o / l

    outs = []
    for g in range(KV_HEADS):
        qh = _head_rows(q_row, g)
        slope = _slope_col(g)
        j_ix = lax.broadcasted_iota(jnp.int32, (1, w_len), 1)
        dist_w = w_len - j_ix
        o_w = two_piece(qh, slope, win_ref[0, 0, g], win_ref[0, 1, g], dist_w, dist_w < WINDOW,
                        kw_new[:, g * 128:g * 128 + 64], kw_new[:, g * 128 + 64:(g + 1) * 128])
        outs.append((qh, slope, o_w))

    for g in range(KV_HEADS):
        for k in range(N_POOL_SEL):
            for cp in page_copies(g, k, 0):
                cp.wait()

    lane_s = lax.broadcasted_iota(jnp.int32, (1, slot_lanes), 1)
    o_all = []
    for g in range(KV_HEADS):
        qh, slope, o_w = outs[g]
        pos_k = jnp.full((1, slot_lanes), past + 1, jnp.int32)
        for k in range(N_POOL_SEL):
            blk = idx_ref[b, g * N_SELECT + k]
            r = lane_s % PAGE_SIZE
            in_blk = jnp.logical_and(lane_s // PAGE_SIZE == k, r // SEL_BLOCK == blk % BLK_PER_PAGE)
            pos_k = jnp.where(in_blk, (blk // BLK_PER_PAGE) * PAGE_SIZE + r, pos_k)
        dist_s = past - pos_k
        o_s = two_piece(qh, slope, kbuf_ref[g], vbuf_ref[g], dist_s, dist_s >= 0,
                        ks_new[:, g * 128:g * 128 + 64], ks_new[:, g * 128 + 64:(g + 1) * 128])
        o_c = oc_ref[0][:, g * 64:(g + 1) * 64]
        gates = jnp.broadcast_to(gl_all[:, g * 128:(g + 1) * 128], (8, 128))
        g_c = _pick_lane(gates, lane128, 3 * row8)
        g_s = _pick_lane(gates, lane128, 3 * row8 + 1)
        g_w = _pick_lane(gates, lane128, 3 * row8 + 2)
        o_all.append(g_c * o_c + g_s * o_s + g_w * o_w)
    o_ref[0] = jnp.concatenate(o_all, axis=1)


def _sel_win_sample(page_table, idx, cache_sel5, cache_win5, q3, ks3, kw3, gl3, o_c, past):
    bsz = page_table.shape[0]
    w_len = cache_win5.shape[-1]
    grid_spec = pltpu.PrefetchScalarGridSpec(
        num_scalar_prefetch=2,
        grid=(bsz,),
        in_specs=[pl.BlockSpec(memory_space=pl.ANY),
                  pl.BlockSpec((1, 2, KV_HEADS, HEAD_DIM, w_len), lambda i, pt, ix: (i, 0, 0, 0, 0)),
                  pl.BlockSpec((1, 1, Q_A), lambda i, pt, ix: (i, 0, 0)),
                  pl.BlockSpec((1, 1, 256), lambda i, pt, ix: (i, 0, 0)),
                  pl.BlockSpec((1, 1, 256), lambda i, pt, ix: (i, 0, 0)),
                  pl.BlockSpec((1, 1, 256), lambda i, pt, ix: (i, 0, 0)),
                  pl.BlockSpec((1, 8, 128), lambda i, pt, ix: (i, 0, 0))],
        out_specs=pl.BlockSpec((1, 8, 128), lambda i, pt, ix: (i, 0, 0)),
        scratch_shapes=[pltpu.VMEM((KV_HEADS, HEAD_DIM, N_SELECT * PAGE_SIZE), F32),
                        pltpu.VMEM((KV_HEADS, HEAD_DIM, N_SELECT * PAGE_SIZE), F32),
                        pltpu.SemaphoreType.DMA(())],
    )
    return pl.pallas_call(
        functools.partial(_sel_win_sample_kernel, past=past),
        grid_spec=grid_spec,
        out_shape=jax.ShapeDtypeStruct((bsz, 8, 128), F32),
        compiler_params=_cparams(("arbitrary",)),
        name="sel_win_sample",
    )(page_table, idx, cache_sel5, cache_win5, q3, ks3, kw3, gl3, o_c)


def _hgrn_sample_kernel(q_ref, f_ref, v_ref, g_ref, lb_ref, ng_ref, s_ref, o_ref, s_out_ref):
    z = lb_ref[...]
    e = jnp.exp(z - jnp.max(z, axis=0, keepdims=True))
    lb = e[0] / jnp.sum(e, axis=0)
    f = lb + (1.0 - lb) * jax.nn.sigmoid(f_ref[0])
    decay = jnp.exp(jnp.log(f))
    k = 1.0 - f
    q = q_ref[0]
    v = v_ref[0]
    gate = g_ref[0]
    ng = ng_ref[...]
    for hh in range(B_HEADS):
        hs = slice(hh, hh + 1)
        s_new = decay[hs] * s_ref[0, hh] + v[:, hs] * k[hs]
        s_out_ref[0, hh] = s_new
        o = jnp.sum(q[hs] * s_new, axis=1, keepdims=True)
        o = o * lax.rsqrt(jnp.mean(o * o, axis=0, keepdims=True) + LN_EPS) * ng
        gt = gate[:, hs]
        o_ref[0, :, hs] = o * (gt * jax.nn.sigmoid(gt))


def _hgrn_sample(q_hk, f_hk, v_vh, g_vh, lb_hk, ng_col, state_t):
    bsz = q_hk.shape[0]
    return pl.pallas_call(
        _hgrn_sample_kernel,
        grid=(bsz,),
        in_specs=[pl.BlockSpec((1, B_HEADS, B_DK), lambda i: (i, 0, 0)),
                  pl.BlockSpec((1, B_HEADS, B_DK), lambda i: (i, 0, 0)),
                  pl.BlockSpec((1, B_DV, B_HEADS), lambda i: (i, 0, 0)),
                  pl.BlockSpec((1, B_DV, B_HEADS), lambda i: (i, 0, 0)),
                  pl.BlockSpec((2, B_HEADS, B_DK), lambda i: (0, 0, 0)),
                  pl.BlockSpec((B_DV, 1), lambda i: (0, 0)),
                  pl.BlockSpec((1, B_HEADS, B_DV, B_DK), lambda i: (i, 0, 0, 0))],
        out_specs=[pl.BlockSpec((1, B_DV, B_HEADS), lambda i: (i, 0, 0)),
                   pl.BlockSpec((1, B_HEADS, B_DV, B_DK), lambda i: (i, 0, 0, 0))],
        out_shape=[jax.ShapeDtypeStruct((bsz, B_DV, B_HEADS), F32),
                   jax.ShapeDtypeStruct((bsz, B_HEADS, B_DV, B_DK), F32)],
        compiler_params=_cparams(("arbitrary",)),
        name="hgrn_sample",
    )(q_hk, f_hk, v_vh, g_vh, lb_hk, ng_col, state_t)


def _rows_last(cache):
    return jnp.moveaxis(cache, -4, -1)


def _sample_layer(x, cache_cmp, cache_sel, cache_win, state, page_table, w_pad, cmpw, w_s, lb_logits, norm_g,
                  w_a, w_b, w_o, ln1_g, ln1_b, w_up, w_down, ln2_g, ln2_b):
    bsz, t, _ = x.shape
    assert t == 1, "the sample group decodes one token per request"
    n_pages = page_table.shape[1]
    past = n_pages * PAGE_SIZE
    x2d = x.reshape(bsz, D_MODEL)
    h = _project(x2d, w_pad)
    h3 = h.reshape(bsz, 1, D_PAD)
    q3 = h3[:, :, C_QA:C_QA + Q_A]
    wl, wb, posf = cmpw
    o_c, idx = _cmp_sample(page_table, _rows_last(cache_cmp), w_s, posf, wb, q3, past)
    idx2 = jnp.swapaxes(idx[:, :, :KV_HEADS], 1, 2).reshape(bsz, KV_HEADS * N_SELECT)
    o_rd = _sel_win_sample(page_table, idx2, _rows_last(cache_sel), _rows_last(cache_win),
                           q3, h3[:, :, C_KVS:C_KVS + 256], h3[:, :, C_KVW:C_KVW + 256],
                           h3[:, :, C_GL:C_GL + 256], o_c, past)
    o_a = jnp.swapaxes(o_rd[:, :GROUP].reshape(bsz, GROUP, KV_HEADS, HEAD_DIM), 1, 2).reshape(bsz, Q_A)
    to_hk = lambda a: a.reshape(-1, B_HEADS, B_DK)
    to_vh = lambda a: jnp.swapaxes(a.reshape(-1, B_HEADS, B_DV), 1, 2)
    o_vh, s_t = _hgrn_sample(to_hk(h[:, C_QB:C_QB + Q_B]), to_hk(h[:, C_FB:C_FB + Q_B]),
                             to_vh(h[:, C_IB:C_IB + I_B]), to_vh(h[:, C_GB:C_GB + I_B]),
                             to_hk(lb_logits), norm_g.reshape(B_DV, 1), jnp.swapaxes(state, 2, 3))
    o_b = jnp.swapaxes(o_vh, 1, 2).reshape(bsz, I_B)
    y = _merge_mlp(x2d, o_a, o_b, h, w_a, w_b, w_o, ln1_g, ln1_b, w_up, w_down, ln2_g, ln2_b)
    return y.reshape(bsz, 1, D_MODEL), h, jnp.swapaxes(s_t, 2, 3)


def _cmp_sample_weights(cmp_w):
    w = cmp_w.reshape(2, 2, CMP_STRIDE, HEAD_DIM, HEAD_DIM)
    base = jnp.transpose(w, (0, 2, 3, 1, 4))
    eye = jnp.eye(2, dtype=F32)
    ws = base[:, :, None, :, :, None, :] * eye[None, None, :, None, None, :, None]
    return ws.reshape(2, CMP_STRIDE // 2, 256, 256).astype(BF16)


def kernel(x_prompt, x_sample, cache_cmp_kv, cache_sel_kv, cache_win_kv, state_hgrn, page_table,
           w_in, cmp_w, cmp_pos, hgrn_lb_logits, hgrn_norm_g, w_br_a, w_br_b, w_out,
           ln1_g, ln1_b, w_up, w_down, ln2_g, ln2_b):
    assert w_in.shape[0] == 1, "one layer"
    b, t, _ = x_prompt.shape
    bsz = x_sample.shape[0]
    assert t % HB == 0 and t % QB == 0 and WINDOW + QB <= t <= 128 * SEL_BLOCK
    w_pad = _reorder_w_in(w_in[0])
    cmpw = _cmp_weights(cmp_w[0], cmp_pos[0])
    w_s = _cmp_sample_weights(cmp_w[0])
    dense = (w_br_a[0].astype(BF16), w_br_b[0].astype(BF16), w_out[0].astype(BF16), ln1_g, ln1_b,
             w_up[0].astype(BF16), w_down[0].astype(BF16), ln2_g, ln2_b)

    o_kv = Q_A
    w_kvt = jnp.transpose(w_in[0][:, o_kv:o_kv + 3 * KV_A]).astype(BF16)
    y_p, (kvc_p, kvs_p, kvw_p), s_p = _prompt_layer(x_prompt, w_pad, w_kvt, cmpw, hgrn_lb_logits, hgrn_norm_g,
                                                   *dense)
    y_s, h_s, s_s = _sample_layer(x_sample, cache_cmp_kv[0], cache_sel_kv[0], cache_win_kv[0], state_hgrn[0],
                                  page_table, w_pad, cmpw, w_s, hgrn_lb_logits, hgrn_norm_g, *dense)

    win_p = min(WINDOW, t)
    kvw_s = _kv_out(h_s, C_KVW, bsz, 1)
    new_win_s = jnp.concatenate([cache_win_kv[0], kvw_s], axis=1)[:, -min(WINDOW, cache_win_kv.shape[2] + 1):]
    return (y_p, y_s,
            kvc_p[None], kvs_p[None], kvw_p[:, -win_p:][None], s_p[None],
            _kv_out(h_s, C_KVC, bsz, 1)[None], _kv_out(h_s, C_KVS, bsz, 1)[None], new_win_s[None], s_s[None])
```
